```python
import math
import jax, jax.numpy as jnp
from jax import lax
import numpy as np

D_MODEL = 1024
BATCH = 8
SEQ = 8192
DEPTH = 2

BRANCH_WIDTH = D_MODEL
N_BRANCHES = 3
CHUNK = 128
GMLP_GROUPS = 8
GMLP_GROUP_DIM = BRANCH_WIDTH // GMLP_GROUPS
CONV_WIDTH = 31
XATTN_HEADS = 4
XATTN_HEAD_DIM = BRANCH_WIDTH // XATTN_HEADS
MEM_LEN = 256
OFF_A_U = 0
OFF_A_V = OFF_A_U + BRANCH_WIDTH
OFF_A_G = OFF_A_V + BRANCH_WIDTH
OFF_B_A = OFF_A_G + BRANCH_WIDTH
OFF_B_B = OFF_B_A + BRANCH_WIDTH
OFF_B_G = OFF_B_B + BRANCH_WIDTH
OFF_C_Q = OFF_B_G + BRANCH_WIDTH
OFF_C_G = OFF_C_Q + BRANCH_WIDTH
OFF_MERGE = OFF_C_G + BRANCH_WIDTH
N_IN = OFF_MERGE + N_BRANCHES * D_MODEL
RMS_EPS = 1e-6
LN_EPS = 1e-5

kernel_name = "hybrid_gmlp_conformer_xattn_gated_merge"


def rms_norm(x, g):
    xf = x.astype(jnp.float32)
    y = xf * lax.rsqrt(jnp.mean(xf * xf, axis=-1, keepdims=True) + RMS_EPS)
    return (y * g.astype(jnp.float32)).astype(x.dtype)


def layer_norm(x, g, b):
    xf = x.astype(jnp.float32)
    mu = jnp.mean(xf, axis=-1, keepdims=True)
    xc = xf - mu
    var = jnp.mean(xc * xc, axis=-1, keepdims=True)
    y = xc * lax.rsqrt(var + LN_EPS)
    return (y * g.astype(jnp.float32) + b.astype(jnp.float32)).astype(x.dtype)


def gmlp_spatial_gate(u, v, w_s, b_s):
    bsz, seq, width = v.shape
    n_chunks = seq // CHUNK
    mask = jnp.tril(jnp.ones((CHUNK, CHUNK), dtype=bool))
    ws = jnp.where(mask[None], w_s, 0.0).astype(v.dtype)
    vr = v.reshape(bsz, n_chunks, CHUNK, GMLP_GROUPS, GMLP_GROUP_DIM)
    sv = jnp.einsum('gts,bcsgd->bctgd', ws, vr) + b_s.T.astype(v.dtype)[None, None, :, :, None]
    return u * sv.reshape(bsz, seq, width)


def causal_depthwise_conv(x, w, b):
    k = w.astype(x.dtype)[:, None, :]
    y = lax.conv_general_dilated(
        x, k, window_strides=(1,), padding=[(CONV_WIDTH - 1, 0)],
        dimension_numbers=('NWC', 'WIO', 'NWC'), feature_group_count=x.shape[-1])
    return y + b.astype(x.dtype)


def memory_cross_attention(q, mem_n, w_kv):
    bsz, seq, _ = q.shape
    kv = jnp.einsum('bmd,de->bme', mem_n, w_kv).reshape(bsz, MEM_LEN, 2, XATTN_HEADS, XATTN_HEAD_DIM)
    k, v = kv[:, :, 0], kv[:, :, 1]
    qh = q.reshape(bsz, seq, XATTN_HEADS, XATTN_HEAD_DIM)
    scores = jnp.einsum('bshd,bmhd->bhsm', qh.astype(jnp.float32), k.astype(jnp.float32))
    probs = jax.nn.softmax(scores * (1.0 / math.sqrt(XATTN_HEAD_DIM)), axis=-1).astype(v.dtype)
    out = jnp.einsum('bhsm,bmhd->bshd', probs, v)
    return out.reshape(bsz, seq, BRANCH_WIDTH)


def _fwd_setup_inputs(seed: int = 0) -> dict:
    key = jax.random.key(seed)
    ks = jax.random.split(key, 20)
    f32 = jnp.float32
    nrm = lambda k, shape, scale: jax.random.normal(k, shape, f32) * scale
    return {
        "x": nrm(ks[0], (BATCH, SEQ, D_MODEL), 1.0),
        "mem": nrm(ks[1], (BATCH, MEM_LEN, D_MODEL), 1.0),
        "norm_g": 1.0 + nrm(ks[2], (DEPTH, D_MODEL), 0.02),
        "mem_norm_g": 1.0 + nrm(ks[3], (DEPTH, D_MODEL), 0.02),
        "w_in": nrm(ks[4], (DEPTH, D_MODEL, N_IN), D_MODEL ** -0.5),
        "gmlp_ln_g": 1.0 + nrm(ks[5], (DEPTH, BRANCH_WIDTH), 0.02),
        "gmlp_ln_b": nrm(ks[6], (DEPTH, BRANCH_WIDTH), 0.02),
        "w_s": nrm(ks[7], (DEPTH, GMLP_GROUPS, CHUNK, CHUNK), CHUNK ** -0.5),
        "b_s": 1.0 + nrm(ks[8], (DEPTH, GMLP_GROUPS, CHUNK), 0.02),
        "conv_w": nrm(ks[9], (DEPTH, CONV_WIDTH, BRANCH_WIDTH), CONV_WIDTH ** -0.5),
        "conv_b": nrm(ks[10], (DEPTH, BRANCH_WIDTH), 0.02),
        "conv_ln_g": 1.0 + nrm(ks[11], (DEPTH, BRANCH_WIDTH), 0.02),
        "conv_ln_b": nrm(ks[12], (DEPTH, BRANCH_WIDTH), 0.02),
        "w_kv": nrm(ks[13], (DEPTH, D_MODEL, 2 * BRANCH_WIDTH), D_MODEL ** -0.5),
        "w_branch": nrm(ks[14], (DEPTH, N_BRANCHES, BRANCH_WIDTH, D_MODEL), BRANCH_WIDTH ** -0.5),
        "w_out": nrm(ks[15], (DEPTH, D_MODEL, D_MODEL), D_MODEL ** -0.5),
        "final_norm_g": 1.0 + nrm(ks[16], (D_MODEL,), 0.02),
    }


def _fwd_reference(x, mem, norm_g, mem_norm_g, w_in, gmlp_ln_g, gmlp_ln_b, w_s, b_s,
              conv_w, conv_b, conv_ln_g, conv_ln_b, w_kv, w_branch, w_out, final_norm_g):
    W = BRANCH_WIDTH
    for l in range(DEPTH):
        h = rms_norm(x, norm_g[l])
        z = jnp.einsum('bsd,de->bse', h, w_in[l])

        u = jax.nn.gelu(z[..., OFF_A_U:OFF_A_U + W])
        v = layer_norm(jax.nn.gelu(z[..., OFF_A_V:OFF_A_V + W]), gmlp_ln_g[l], gmlp_ln_b[l])
        br_a = gmlp_spatial_gate(u, v, w_s[l], b_s[l]) * jax.nn.silu(z[..., OFF_A_G:OFF_A_G + W])

        glu = z[..., OFF_B_A:OFF_B_A + W] * jax.nn.sigmoid(z[..., OFF_B_B:OFF_B_B + W])
        c = causal_depthwise_conv(glu, conv_w[l], conv_b[l])
        c = jax.nn.silu(layer_norm(c, conv_ln_g[l], conv_ln_b[l]))
        br_b = c * jax.nn.silu(z[..., OFF_B_G:OFF_B_G + W])

        mem_n = rms_norm(mem, mem_norm_g[l])
        att = memory_cross_attention(z[..., OFF_C_Q:OFF_C_Q + W], mem_n, w_kv[l])
        br_c = att * jax.nn.silu(z[..., OFF_C_G:OFF_C_G + W])

        branches = jnp.stack([br_a, br_b, br_c], axis=2)
        proj = jnp.einsum('bsnw,nwd->bsnd', branches, w_branch[l])
        gates = jax.nn.sigmoid(z[..., OFF_MERGE:OFF_MERGE + N_BRANCHES * D_MODEL]).reshape(
            x.shape[0], x.shape[1], N_BRANCHES, D_MODEL)
        merged = jnp.einsum('bsnd,bsnd->bsd', gates, proj)
        x = x + jnp.einsum('bsd,de->bse', merged, w_out[l])
    return rms_norm(x, final_norm_g)


import jax as _jax
import jax.numpy as _jnp

TWIN_FORMAT = 'train_step'
FWD_PARAMS = ['x', 'mem', 'norm_g', 'mem_norm_g', 'w_in', 'gmlp_ln_g', 'gmlp_ln_b', 'w_s', 'b_s', 'conv_w', 'conv_b', 'conv_ln_g', 'conv_ln_b', 'w_kv', 'w_branch', 'w_out', 'final_norm_g']
TWIN_WEIGHTS = ['norm_g', 'mem_norm_g', 'w_in', 'gmlp_ln_g', 'gmlp_ln_b', 'w_s', 'b_s', 'conv_w', 'conv_b', 'conv_ln_g', 'conv_ln_b', 'w_kv', 'w_branch', 'w_out', 'final_norm_g']
TWIN_DIFF_INPUT = 'x'
TWIN_INPUTS = ['x', 'mem', 'norm_g', 'mem_norm_g', 'w_in', 'gmlp_ln_g', 'gmlp_ln_b', 'w_s', 'b_s', 'conv_w', 'conv_b', 'conv_ln_g', 'conv_ln_b', 'w_kv', 'w_branch', 'w_out', 'final_norm_g', 'loss_target', 'm_norm_g', 'm_mem_norm_g', 'm_w_in', 'm_gmlp_ln_g', 'm_gmlp_ln_b', 'm_w_s', 'm_b_s', 'm_conv_w', 'm_conv_b', 'm_conv_ln_g', 'm_conv_ln_b', 'm_w_kv', 'm_w_branch', 'm_w_out', 'm_final_norm_g', 'v_norm_g', 'v_mem_norm_g', 'v_w_in', 'v_gmlp_ln_g', 'v_gmlp_ln_b', 'v_w_s', 'v_b_s', 'v_conv_w', 'v_conv_b', 'v_conv_ln_g', 'v_conv_ln_b', 'v_w_kv', 'v_w_branch', 'v_w_out', 'v_final_norm_g']
TWIN_OUTPUTS = ['loss', 'grad_x', 'grad_norm_g', 'grad_mem_norm_g', 'grad_w_in', 'grad_gmlp_ln_g', 'grad_gmlp_ln_b', 'grad_w_s', 'grad_b_s', 'grad_conv_w', 'grad_conv_b', 'grad_conv_ln_g', 'grad_conv_ln_b', 'grad_w_kv', 'grad_w_branch', 'grad_w_out', 'grad_final_norm_g', 'delta_norm_g', 'delta_mem_norm_g', 'delta_w_in', 'delta_gmlp_ln_g', 'delta_gmlp_ln_b', 'delta_w_s', 'delta_b_s', 'delta_conv_w', 'delta_conv_b', 'delta_conv_ln_g', 'delta_conv_ln_b', 'delta_w_kv', 'delta_w_branch', 'delta_w_out', 'delta_final_norm_g', 'new_m_norm_g', 'new_m_mem_norm_g', 'new_m_w_in', 'new_m_gmlp_ln_g', 'new_m_gmlp_ln_b', 'new_m_w_s', 'new_m_b_s', 'new_m_conv_w', 'new_m_conv_b', 'new_m_conv_ln_g', 'new_m_conv_ln_b', 'new_m_w_kv', 'new_m_w_branch', 'new_m_w_out', 'new_m_final_norm_g', 'new_v_norm_g', 'new_v_mem_norm_g', 'new_v_w_in', 'new_v_gmlp_ln_g', 'new_v_gmlp_ln_b', 'new_v_w_s', 'new_v_b_s', 'new_v_conv_w', 'new_v_conv_b', 'new_v_conv_ln_g', 'new_v_conv_ln_b', 'new_v_w_kv', 'new_v_w_branch', 'new_v_w_out', 'new_v_final_norm_g']
TWIN_LEAF_KINDS = {'loss': 'loss', 'grad_x': 'grad_x', 'grad_norm_g': 'grad_w', 'grad_mem_norm_g': 'grad_w', 'grad_w_in': 'grad_w', 'grad_gmlp_ln_g': 'grad_w', 'grad_gmlp_ln_b': 'grad_w', 'grad_w_s': 'grad_w', 'grad_b_s': 'grad_w', 'grad_conv_w': 'grad_w', 'grad_conv_b': 'grad_w', 'grad_conv_ln_g': 'grad_w', 'grad_conv_ln_b': 'grad_w', 'grad_w_kv': 'grad_w', 'grad_w_branch': 'grad_w', 'grad_w_out': 'grad_w', 'grad_final_norm_g': 'grad_w', 'delta_norm_g': 'delta_w', 'delta_mem_norm_g': 'delta_w', 'delta_w_in': 'delta_w', 'delta_gmlp_ln_g': 'delta_w', 'delta_gmlp_ln_b': 'delta_w', 'delta_w_s': 'delta_w', 'delta_b_s': 'delta_w', 'delta_conv_w': 'delta_w', 'delta_conv_b': 'delta_w', 'delta_conv_ln_g': 'delta_w', 'delta_conv_ln_b': 'delta_w', 'delta_w_kv': 'delta_w', 'delta_w_branch': 'delta_w', 'delta_w_out': 'delta_w', 'delta_final_norm_g': 'delta_w', 'new_m_norm_g': 'new_m', 'new_m_mem_norm_g': 'new_m', 'new_m_w_in': 'new_m', 'new_m_gmlp_ln_g': 'new_m', 'new_m_gmlp_ln_b': 'new_m', 'new_m_w_s': 'new_m', 'new_m_b_s': 'new_m', 'new_m_conv_w': 'new_m', 'new_m_conv_b': 'new_m', 'new_m_conv_ln_g': 'new_m', 'new_m_conv_ln_b': 'new_m', 'new_m_w_kv': 'new_m', 'new_m_w_branch': 'new_m', 'new_m_w_out': 'new_m', 'new_m_final_norm_g': 'new_m', 'new_v_norm_g': 'new_v', 'new_v_mem_norm_g': 'new_v', 'new_v_w_in': 'new_v', 'new_v_gmlp_ln_g': 'new_v', 'new_v_gmlp_ln_b': 'new_v', 'new_v_w_s': 'new_v', 'new_v_b_s': 'new_v', 'new_v_conv_w': 'new_v', 'new_v_conv_b': 'new_v', 'new_v_conv_ln_g': 'new_v', 'new_v_conv_ln_b': 'new_v', 'new_v_w_kv': 'new_v', 'new_v_w_branch': 'new_v', 'new_v_w_out': 'new_v', 'new_v_final_norm_g': 'new_v'}


def _forward(args):
    return _fwd_reference(*[args[k] for k in FWD_PARAMS])


def _output_shape():
    def fwd():
        inp = _fwd_setup_inputs(0)
        return _fwd_reference(*[inp[k] for k in FWD_PARAMS])
    out = _jax.eval_shape(fwd)
    return out.shape, out.dtype

N_MICROBATCH = 1
ADAM_LR = 0.001
ADAM_B1 = 0.9
ADAM_B2 = 0.999
ADAM_EPS = 1e-08
ADAM_WD = 0.01
ADAM_STEP = 10
PER_EXAMPLE_BATCH_AXIS = {'x': 0, 'mem': 0, 'loss_target': 0}
SHARED_INPUTS = []
_WEIGHT_DTYPES = {'norm_g': _jnp.float32, 'mem_norm_g': _jnp.float32, 'w_in': _jnp.float32, 'gmlp_ln_g': _jnp.float32, 'gmlp_ln_b': _jnp.float32, 'w_s': _jnp.float32, 'b_s': _jnp.float32, 'conv_w': _jnp.float32, 'conv_b': _jnp.float32, 'conv_ln_g': _jnp.float32, 'conv_ln_b': _jnp.float32, 'w_kv': _jnp.float32, 'w_branch': _jnp.float32, 'w_out': _jnp.float32, 'final_norm_g': _jnp.float32}
MOMENT_SCALE = {'norm_g': 1.295719e-01, 'mem_norm_g': 1.238331e-02, 'w_in': 3.840703e-02, 'gmlp_ln_g': 3.613268e-02, 'gmlp_ln_b': 3.781671e-02, 'w_s': 3.609885e-02, 'b_s': 5.228281e-02, 'conv_w': 4.878408e-02, 'conv_b': 1.033280e-01, 'conv_ln_g': 5.704407e-02, 'conv_ln_b': 4.823267e-02, 'w_kv': 8.258092e-03, 'w_branch': 4.558351e-02, 'w_out': 7.906708e-02, 'final_norm_g': 6.403376e+01}


def _to_microbatches(a, axis):
    t = _jnp.moveaxis(a, axis, 0)
    t = t.reshape((N_MICROBATCH, t.shape[0] // N_MICROBATCH) + t.shape[1:])
    return _jnp.moveaxis(t, 1, axis + 1)


def setup_inputs(seed: int = 0) -> dict:
    inp = _fwd_setup_inputs(seed)
    key = _jax.random.fold_in(_jax.random.key(seed), 7919)
    shape, _ = _output_shape()
    out = dict(inp)
    out["loss_target"] = _jax.random.normal(_jax.random.fold_in(key, 0), shape, _jnp.float32)
    for i, name in enumerate(TWIN_WEIGHTS):
        w = inp[name].astype(_jnp.float32)
        if MOMENT_SCALE is None:
            s = _jnp.sqrt(_jnp.mean(_jnp.square(w)) + 1e-30)
        else:
            s = MOMENT_SCALE[name]
        km, kv = _jax.random.split(_jax.random.fold_in(key, i + 1))
        out[name] = w
        out["m_" + name] = s * _jax.random.normal(km, w.shape, _jnp.float32)
        out["v_" + name] = (s * s) * _jax.random.uniform(kv, w.shape, _jnp.float32, 0.5, 1.5)
    if N_MICROBATCH > 1:
        for name, axis in PER_EXAMPLE_BATCH_AXIS.items():
            out[name] = _to_microbatches(out[name], axis)
    return {'x': out['x'], 'mem': out['mem'], 'norm_g': out['norm_g'], 'mem_norm_g': out['mem_norm_g'], 'w_in': out['w_in'], 'gmlp_ln_g': out['gmlp_ln_g'], 'gmlp_ln_b': out['gmlp_ln_b'], 'w_s': out['w_s'], 'b_s': out['b_s'], 'conv_w': out['conv_w'], 'conv_b': out['conv_b'], 'conv_ln_g': out['conv_ln_g'], 'conv_ln_b': out['conv_ln_b'], 'w_kv': out['w_kv'], 'w_branch': out['w_branch'], 'w_out': out['w_out'], 'final_norm_g': out['final_norm_g'], 'loss_target': out['loss_target'], 'm_norm_g': out['m_norm_g'], 'm_mem_norm_g': out['m_mem_norm_g'], 'm_w_in': out['m_w_in'], 'm_gmlp_ln_g': out['m_gmlp_ln_g'], 'm_gmlp_ln_b': out['m_gmlp_ln_b'], 'm_w_s': out['m_w_s'], 'm_b_s': out['m_b_s'], 'm_conv_w': out['m_conv_w'], 'm_conv_b': out['m_conv_b'], 'm_conv_ln_g': out['m_conv_ln_g'], 'm_conv_ln_b': out['m_conv_ln_b'], 'm_w_kv': out['m_w_kv'], 'm_w_branch': out['m_w_branch'], 'm_w_out': out['m_w_out'], 'm_final_norm_g': out['m_final_norm_g'], 'v_norm_g': out['v_norm_g'], 'v_mem_norm_g': out['v_mem_norm_g'], 'v_w_in': out['v_w_in'], 'v_gmlp_ln_g': out['v_gmlp_ln_g'], 'v_gmlp_ln_b': out['v_gmlp_ln_b'], 'v_w_s': out['v_w_s'], 'v_b_s': out['v_b_s'], 'v_conv_w': out['v_conv_w'], 'v_conv_b': out['v_conv_b'], 'v_conv_ln_g': out['v_conv_ln_g'], 'v_conv_ln_b': out['v_conv_ln_b'], 'v_w_kv': out['v_w_kv'], 'v_w_branch': out['v_w_branch'], 'v_w_out': out['v_w_out'], 'v_final_norm_g': out['v_final_norm_g']}


def _loss(weights, diff, rest, loss_target):
    with _jax.named_scope("forward"):
        args = {**rest, TWIN_DIFF_INPUT: diff, **{k: w.astype(_WEIGHT_DTYPES[k]) for k, w in weights.items()}}
        y = _forward(args)
    with _jax.named_scope("loss_head"):
        err = _jnp.square(y.astype(_jnp.float32) - loss_target)
        return 0.5 * _jnp.sum(_jnp.mean(err, axis=-1)) if err.ndim else 0.5 * err


def _adamw(w, g, m, v):
    m = ADAM_B1 * m + (1.0 - ADAM_B1) * g
    v = ADAM_B2 * v + (1.0 - ADAM_B2) * _jnp.square(g)
    m_hat = m / (1.0 - ADAM_B1 ** ADAM_STEP)
    v_hat = v / (1.0 - ADAM_B2 ** ADAM_STEP)
    delta = -ADAM_LR * (m_hat / (_jnp.sqrt(v_hat) + ADAM_EPS) + ADAM_WD * w)
    return delta, m, v


def reference(x, mem, norm_g, mem_norm_g, w_in, gmlp_ln_g, gmlp_ln_b, w_s, b_s, conv_w, conv_b, conv_ln_g, conv_ln_b, w_kv, w_branch, w_out, final_norm_g, loss_target, m_norm_g, m_mem_norm_g, m_w_in, m_gmlp_ln_g, m_gmlp_ln_b, m_w_s, m_b_s, m_conv_w, m_conv_b, m_conv_ln_g, m_conv_ln_b, m_w_kv, m_w_branch, m_w_out, m_final_norm_g, v_norm_g, v_mem_norm_g, v_w_in, v_gmlp_ln_g, v_gmlp_ln_b, v_w_s, v_b_s, v_conv_w, v_conv_b, v_conv_ln_g, v_conv_ln_b, v_w_kv, v_w_branch, v_w_out, v_final_norm_g):
    given = dict(x=x, mem=mem, norm_g=norm_g, mem_norm_g=mem_norm_g, w_in=w_in, gmlp_ln_g=gmlp_ln_g, gmlp_ln_b=gmlp_ln_b, w_s=w_s, b_s=b_s, conv_w=conv_w, conv_b=conv_b, conv_ln_g=conv_ln_g, conv_ln_b=conv_ln_b, w_kv=w_kv, w_branch=w_branch, w_out=w_out, final_norm_g=final_norm_g, loss_target=loss_target, m_norm_g=m_norm_g, m_mem_norm_g=m_mem_norm_g, m_w_in=m_w_in, m_gmlp_ln_g=m_gmlp_ln_g, m_gmlp_ln_b=m_gmlp_ln_b, m_w_s=m_w_s, m_b_s=m_b_s, m_conv_w=m_conv_w, m_conv_b=m_conv_b, m_conv_ln_g=m_conv_ln_g, m_conv_ln_b=m_conv_ln_b, m_w_kv=m_w_kv, m_w_branch=m_w_branch, m_w_out=m_w_out, m_final_norm_g=m_final_norm_g, v_norm_g=v_norm_g, v_mem_norm_g=v_mem_norm_g, v_w_in=v_w_in, v_gmlp_ln_g=v_gmlp_ln_g, v_gmlp_ln_b=v_gmlp_ln_b, v_w_s=v_w_s, v_b_s=v_b_s, v_conv_w=v_conv_w, v_conv_b=v_conv_b, v_conv_ln_g=v_conv_ln_g, v_conv_ln_b=v_conv_ln_b, v_w_kv=v_w_kv, v_w_branch=v_w_branch, v_w_out=v_w_out, v_final_norm_g=v_final_norm_g)
    weights = {n: given[n] for n in TWIN_WEIGHTS}
    shared = {n: given[n] for n in SHARED_INPUTS}
    per_example = {n: given[n] for n in ['x', 'mem']}
    grad_fn = _jax.value_and_grad(_loss, argnums=(0, 1))

    def one_microbatch(ex, loss_target):
        ex = dict(ex)
        diff = ex.pop(TWIN_DIFF_INPUT)
        return grad_fn(weights, diff, {**shared, **ex}, loss_target)

    if N_MICROBATCH == 1:
        loss, (grad_w, grad_x) = one_microbatch(per_example, given["loss_target"])
    else:
        def body(carry, xs):
            loss_sum, grad_sum = carry
            l_k, (gw_k, gx_k) = one_microbatch(xs[0], xs[1])
            with _jax.named_scope("update"):
                return (loss_sum + l_k, _jax.tree.map(_jnp.add, grad_sum, gw_k)), gx_k

        init = (_jnp.zeros((), _jnp.float32), _jax.tree.map(_jnp.zeros_like, weights))
        (loss, grad_w), grad_x = _jax.lax.scan(body, init, (per_example, given["loss_target"]))
    with _jax.named_scope("update"):
        delta_w, new_m, new_v = {}, {}, {}
        for n in TWIN_WEIGHTS:
            delta_w[n], new_m[n], new_v[n] = _adamw(weights[n], grad_w[n], given["m_" + n], given["v_" + n])
    return (loss, grad_x, *[grad_w[n] for n in TWIN_WEIGHTS], *[delta_w[n] for n in TWIN_WEIGHTS],
            *[new_m[n] for n in TWIN_WEIGHTS], *[new_v[n] for n in TWIN_WEIGHTS])
```

```python
import functools
import math

import jax
import jax.numpy as jnp
from jax import lax
from jax.experimental import pallas as pl
from jax.experimental.pallas import tpu as pltpu

F32 = jnp.float32
MXU_DTYPE = jnp.bfloat16
ACT_DTYPE = jnp.bfloat16

D = 1024
N_SEG = 11
N_IN = N_SEG * D
NDEV = 8
CHUNK = 128
N_GROUPS = 8
CONV_K = 31
HALO = 32
HEADS = 4
HEAD_DIM = D // HEADS
RMS_EPS = 1e-6
LN_EPS = 1e-5
ADAM_LR, ADAM_B1, ADAM_B2, ADAM_EPS, ADAM_WD, ADAM_STEP = 0.001, 0.9, 0.999, 1e-08, 0.01, 10
SEG_AU, SEG_AV, SEG_AG, SEG_BA, SEG_BB, SEG_BG, SEG_CQ, SEG_CG, SEG_M = 0, 1, 2, 3, 4, 5, 6, 7, 8

VMEM_LIMIT = 60 * 1024 * 1024
MESH = pl.DeviceIdType.MESH
NT_DIMS = (((1,), (1,)), ((), ()))
TN_DIMS = (((0,), (0,)), ((), ()))


def _params(*sem):
    return pltpu.CompilerParams(dimension_semantics=sem, vmem_limit_bytes=VMEM_LIMIT)


def _tile(n, want):
    t = min(n, want)
    assert n % t == 0, (n, want)
    return t


def _mx(v):
    return v.astype(MXU_DTYPE)


def _gelu(x):
    t = jnp.tanh(0.7978845608028654 * (x + 0.044715 * x * x * x))
    return 0.5 * x * (1.0 + t), t


def _gelu_grad(x, t):
    return 0.5 * (1.0 + t) + 0.5 * x * (1.0 - t * t) * 0.7978845608028654 * (1.0 + 3.0 * 0.044715 * x * x)


def _silu_grad(x, s):
    return s * (1.0 + x * (1.0 - s))


def _ln_stats(v):
    mu = jnp.mean(v, axis=-1, keepdims=True)
    vc = v - mu
    rstd = lax.rsqrt(jnp.mean(vc * vc, axis=-1, keepdims=True) + LN_EPS)
    return vc * rstd, rstd


def _ln_grad(dy, g, vhat, rstd):
    dvh = dy * g
    return rstd * (dvh - jnp.mean(dvh, axis=-1, keepdims=True)
                   - vhat * jnp.mean(dvh * vhat, axis=-1, keepdims=True))


def _rowsum(v):
    return jnp.sum(v, axis=0, keepdims=True)


def _exchange(arrays, gather, name):
    n = len(arrays)
    out_shape = [jax.ShapeDtypeStruct(((NDEV,) + a.shape) if g else a.shape, a.dtype)
                 for a, g in zip(arrays, gather)]

    def body(*refs):
        ins, outs = refs[:n], refs[n:2 * n]
        send_sems, recv_sems, local_sems = refs[2 * n:]
        x, y, c = lax.axis_index("x"), lax.axis_index("y"), lax.axis_index("c")
        me = 4 * x + 2 * y + c

        def src(i, blk):
            return ins[i] if gather[i] else ins[i].at[blk]

        local = []
        for i in range(n):
            cp = pltpu.make_async_copy(src(i, me), outs[i].at[me], local_sems.at[i])
            cp.start()
            local.append(cp)
        sent = []
        for d in range(1, NDEV):
            px = 1 - x if d & 4 else x
            py = 1 - y if d & 2 else y
            pc = 1 - c if d & 1 else c
            peer = 4 * px + 2 * py + pc
            for i in range(n):
                cp = pltpu.make_async_remote_copy(
                    src_ref=src(i, peer), dst_ref=outs[i].at[me],
                    send_sem=send_sems.at[i, d - 1], recv_sem=recv_sems.at[i, d - 1],
                    device_id=(px, py, pc), device_id_type=MESH)
                cp.start()
                landing = pltpu.make_async_remote_copy(
                    src_ref=src(i, peer), dst_ref=outs[i].at[peer],
                    send_sem=send_sems.at[i, d - 1], recv_sem=recv_sems.at[i, d - 1],
                    device_id=(px, py, pc), device_id_type=MESH)
                sent.append((cp, landing))
        for cp in local:
            cp.wait()
        for cp, landing in sent:
            landing.wait_recv()
        for cp, landing in sent:
            cp.wait_send()

    any_spec = pl.BlockSpec(memory_space=pl.ANY)
    return pl.pallas_call(
        body, name=name, out_shape=out_shape,
        in_specs=[any_spec] * n, out_specs=[any_spec] * n,
        scratch_shapes=[pltpu.SemaphoreType.DMA((n, NDEV - 1)), pltpu.SemaphoreType.DMA((n, NDEV - 1)),
                        pltpu.SemaphoreType.DMA((n,))],
    )(*arrays)


def _rms_matmul(x, g, w, name):
    T = x.shape[0]
    nb = w.shape[2]
    tT = _tile(T, 1024)

    def body(x_ref, g_ref, w_ref, z_ref, h_ref):
        @pl.when(pl.program_id(1) == 0)
        def _():
            xf = x_ref[...]
            r = lax.rsqrt(jnp.mean(xf * xf, axis=-1, keepdims=True) + RMS_EPS)
            h_ref[...] = (xf * r * g_ref[...]).astype(h_ref.dtype)

        z_ref[...] = jnp.dot(h_ref[...], w_ref[...], preferred_element_type=F32).astype(z_ref.dtype)

    return pl.pallas_call(
        body, name=name, grid=(T // tT, NDEV),
        in_specs=[pl.BlockSpec((tT, D), lambda t, n: (t, 0)),
                  pl.BlockSpec((1, D), lambda t, n: (0, 0)),
                  pl.BlockSpec((None, D, nb), lambda t, n: (n, 0, 0))],
        out_specs=[pl.BlockSpec((tT, nb), lambda t, n: (t, n)),
                   pl.BlockSpec((tT, D), lambda t, n: (t, 0))],
        out_shape=[jax.ShapeDtypeStruct((T, NDEV * nb), ACT_DTYPE), jax.ShapeDtypeStruct((T, D), MXU_DTYPE)],
        compiler_params=_params("parallel", "arbitrary"),
    )(x, g.reshape(1, D), w)


def _rms_matmul_bwd(dz, w, x, g, dxo, name):
    T = x.shape[0]
    nb = w.shape[2]
    tT = _tile(T, 1024)

    def body(dz_ref, w_ref, x_ref, g_ref, dxo_ref, dx_ref, dg_ref, acc):
        t, n = pl.program_id(0), pl.program_id(1)

        @pl.when(n == 0)
        def _():
            acc[...] = jnp.zeros_like(acc)

        @pl.when((n == 0) & (t == 0))
        def _():
            dg_ref[...] = jnp.zeros_like(dg_ref)

        acc[...] += lax.dot_general(_mx(dz_ref[...]), w_ref[...], NT_DIMS, preferred_element_type=F32)

        @pl.when(n == NDEV - 1)
        def _():
            xf = x_ref[...]
            r = lax.rsqrt(jnp.mean(xf * xf, axis=-1, keepdims=True) + RMS_EPS)
            xh = xf * r
            dh = acc[...]
            dxh = dh * g_ref[...]
            dx_ref[...] = dxo_ref[...] + r * (dxh - xh * jnp.mean(dxh * xh, axis=-1, keepdims=True))
            dg_ref[...] += _rowsum(dh * xh)

    return pl.pallas_call(
        body, name=name, grid=(T // tT, NDEV),
        in_specs=[pl.BlockSpec((tT, nb), lambda t, n: (t, n)),
                  pl.BlockSpec((None, D, nb), lambda t, n: (n, 0, 0)),
                  pl.BlockSpec((tT, D), lambda t, n: (t, 0)),
                  pl.BlockSpec((1, D), lambda t, n: (0, 0)),
                  pl.BlockSpec((tT, D), lambda t, n: (t, 0))],
        out_specs=[pl.BlockSpec((tT, D), lambda t, n: (t, 0)),
                   pl.BlockSpec((1, D), lambda t, n: (0, 0))],
        out_shape=[jax.ShapeDtypeStruct((T, D), F32), jax.ShapeDtypeStruct((1, D), F32)],
        scratch_shapes=[pltpu.VMEM((tT, D), F32)],
        compiler_params=_params("arbitrary", "arbitrary"),
    )(dz, w, x, g.reshape(1, D), dxo)


def _atb(a, ai, b, bi, nblk, name):
    T, M = a.shape[1:]
    N = b.shape[2]
    nb = N // nblk
    tk = _tile(T, 512)

    def body(a_ref, b_ref, o_ref):
        @pl.when(pl.program_id(1) == 0)
        def _():
            o_ref[...] = jnp.zeros_like(o_ref)

        o_ref[...] += lax.dot_general(_mx(a_ref[...]), _mx(b_ref[...]), TN_DIMS, preferred_element_type=F32)

    return pl.pallas_call(
        body, name=name, grid=(nblk, T // tk),
        in_specs=[pl.BlockSpec((None, tk, M), lambda n, k: (ai, k, 0)),
                  pl.BlockSpec((None, tk, nb), lambda n, k: (bi, k, n))],
        out_specs=pl.BlockSpec((None, M, nb), lambda n, k: (n, 0, 0)),
        out_shape=jax.ShapeDtypeStruct((nblk, M, nb), F32),
        compiler_params=_params("parallel", "arbitrary"),
    )(a, b)


def _loss_head(x, tgt, g, name):
    T = x.shape[0]
    tT = _tile(T, 512)

    def body(x_ref, t_ref, g_ref, loss_ref, dx_ref, dg_ref):
        @pl.when(pl.program_id(0) == 0)
        def _():
            loss_ref[...] = jnp.zeros_like(loss_ref)
            dg_ref[...] = jnp.zeros_like(dg_ref)

        xf = x_ref[...]
        r = lax.rsqrt(jnp.mean(xf * xf, axis=-1, keepdims=True) + RMS_EPS)
        xh = xf * r
        err = xh * g_ref[...] - t_ref[...]
        loss_ref[...] += 0.5 * jnp.sum(jnp.mean(err * err, axis=-1, keepdims=True), axis=0, keepdims=True)
        dy = err * (1.0 / D)
        dxh = dy * g_ref[...]
        dx_ref[...] = r * (dxh - xh * jnp.mean(dxh * xh, axis=-1, keepdims=True))
        dg_ref[...] += _rowsum(dy * xh)

    return pl.pallas_call(
        body, name=name, grid=(T // tT,),
        in_specs=[pl.BlockSpec((tT, D), lambda t: (t, 0)),
                  pl.BlockSpec((tT, D), lambda t: (t, 0)),
                  pl.BlockSpec((1, D), lambda t: (0, 0))],
        out_specs=[pl.BlockSpec((1, 1), lambda t: (0, 0)),
                   pl.BlockSpec((tT, D), lambda t: (t, 0)),
                   pl.BlockSpec((1, D), lambda t: (0, 0))],
        out_shape=[jax.ShapeDtypeStruct((1, 1), F32), jax.ShapeDtypeStruct((T, D), F32),
                   jax.ShapeDtypeStruct((1, D), F32)],
        compiler_params=_params("arbitrary"),
    )(x, tgt, g.reshape(1, D))


def _spatial_gate(wm_ref, bst_ref, vb_ref, sv_ref, n_chunks):
    for c in range(n_chunks):
        rows = slice(c * CHUNK, (c + 1) * CHUNK)
        for g in range(N_GROUPS):
            cols = slice(g * CHUNK, (g + 1) * CHUNK)
            sv_ref[rows, cols] = (jnp.dot(wm_ref[g], vb_ref[rows, cols], preferred_element_type=F32)
                                  + bst_ref[:, g:g + 1])


def _softmax_rows(s):
    e = jnp.exp(s - jnp.max(s, axis=-1, keepdims=True))
    return e / jnp.sum(e, axis=-1, keepdims=True)


def _branch_fwd(z, kv, wm, bst, ln_a, cw, cvec, name):
    T = z.shape[0]
    tT = _tile(T, 256)
    n_chunks = tT // CHUNK

    def body(z_ref, kv_ref, wm_ref, bst_ref, lna_ref, cw_ref, cvec_ref, br_ref, c_ref, gbuf, vb, sv):
        def seg(s):
            return z_ref[:, s * D:(s + 1) * D].astype(F32)

        u, _ = _gelu(seg(SEG_AU))
        gv, _ = _gelu(seg(SEG_AV))
        vhat, _ = _ln_stats(gv)
        vb[...] = _mx(vhat * lna_ref[0:1, :] + lna_ref[1:2, :])
        _spatial_gate(wm_ref, bst_ref, vb, sv, n_chunks)
        zg = seg(SEG_AG)
        br_ref[0] = (u * sv[...] * (zg * jax.nn.sigmoid(zg))).astype(br_ref.dtype)

        @pl.when(pl.program_id(0) == 0)
        def _():
            gbuf[0:HALO, :] = jnp.zeros((HALO, D), F32)

        gbuf[HALO:HALO + tT, :] = seg(SEG_BA) * jax.nn.sigmoid(seg(SEG_BB))
        acc = jnp.broadcast_to(cvec_ref[0:1, :], (tT, D))
        for k in range(CONV_K):
            acc = acc + cw_ref[k:k + 1, :] * gbuf[pl.ds(k + HALO - CONV_K + 1, tT), :]
        c_ref[...] = acc
        gbuf[0:HALO, :] = gbuf[tT:tT + HALO, :]
        chat, _ = _ln_stats(acc)
        cl = chat * cvec_ref[1:2, :] + cvec_ref[2:3, :]
        zg = seg(SEG_BG)
        br_ref[1] = (cl * jax.nn.sigmoid(cl) * (zg * jax.nn.sigmoid(zg))).astype(br_ref.dtype)

        for h in range(HEADS):
            cols = slice(h * HEAD_DIM, (h + 1) * HEAD_DIM)
            q = _mx(z_ref[:, SEG_CQ * D + h * HEAD_DIM:SEG_CQ * D + (h + 1) * HEAD_DIM])
            s = lax.dot_general(q, kv_ref[:, cols], NT_DIMS, preferred_element_type=F32)
            p = _softmax_rows(s * (1.0 / math.sqrt(HEAD_DIM)))
            att = jnp.dot(_mx(p), kv_ref[:, D + h * HEAD_DIM:D + (h + 1) * HEAD_DIM], preferred_element_type=F32)
            zg = z_ref[:, SEG_CG * D + h * HEAD_DIM:SEG_CG * D + (h + 1) * HEAD_DIM].astype(F32)
            br_ref[2, :, cols] = (att * (zg * jax.nn.sigmoid(zg))).astype(br_ref.dtype)

    full = lambda shape: pl.BlockSpec(shape, lambda t: (0,) * len(shape))
    return pl.pallas_call(
        body, name=name, grid=(T // tT,),
        in_specs=[pl.BlockSpec((tT, SEG_M * D), lambda t: (t, 0)),
                  full(kv.shape), full(wm.shape), full(bst.shape), full(ln_a.shape), full(cw.shape),
                  full(cvec.shape)],
        out_specs=[pl.BlockSpec((3, tT, D), lambda t: (0, t, 0)),
                   pl.BlockSpec((tT, D), lambda t: (t, 0))],
        out_shape=[jax.ShapeDtypeStruct((3, T, D), MXU_DTYPE), jax.ShapeDtypeStruct((T, D), F32)],
        scratch_shapes=[pltpu.VMEM((tT + HALO, D), F32), pltpu.VMEM((tT, D), MXU_DTYPE),
                        pltpu.VMEM((tT, D), F32)],
        compiler_params=_params("arbitrary"),
    )(z, kv, wm, bst, ln_a, cw, cvec)


def _branch_bwd(z, dbr, c, dzm, kv, wm, wmt, bst, ln_a, cw, cvec, name):
    T = z.shape[0]
    M = kv.shape[0]
    tT = _tile(T, 256)
    nT = T // tT
    n_chunks = tT // CHUNK
    halo_blocks = tT // HALO

    def body(z_ref, zha_ref, zhb_ref, dbr_ref, c_ref, dzm_ref, kv_ref, wm_ref, wmt_ref, bst_ref, lna_ref,
             cw_ref, cvec_ref, dz_ref, vecg_ref, dbst_ref, dws_ref, dcw_ref, dkv_ref,
             gbuf, dcbuf, vb, dsvb, sv, dvbuf):
        i = pl.program_id(0)

        @pl.when(i == 0)
        def _():
            vecg_ref[...] = jnp.zeros_like(vecg_ref)
            dbst_ref[...] = jnp.zeros_like(dbst_ref)
            dws_ref[...] = jnp.zeros_like(dws_ref)
            dcw_ref[...] = jnp.zeros_like(dcw_ref)
            dkv_ref[...] = jnp.zeros_like(dkv_ref)
            dcbuf[tT:tT + HALO, :] = jnp.zeros((HALO, D), F32)

        def seg(s):
            return z_ref[:, s * D:(s + 1) * D].astype(F32)

        def put(s, val):
            dz_ref[:, s * D:(s + 1) * D] = val.astype(dz_ref.dtype)

        zu, zv, zg = seg(SEG_AU), seg(SEG_AV), seg(SEG_AG)
        u, tu = _gelu(zu)
        gv, tv = _gelu(zv)
        vhat, rstd = _ln_stats(gv)
        vb[...] = _mx(vhat * lna_ref[0:1, :] + lna_ref[1:2, :])
        _spatial_gate(wm_ref, bst_ref, vb, sv, n_chunks)
        sg = jax.nn.sigmoid(zg)
        d_a = dbr_ref[0].astype(F32)
        put(SEG_AU, d_a * sv[...] * (zg * sg) * _gelu_grad(zu, tu))
        put(SEG_AG, d_a * u * sv[...] * _silu_grad(zg, sg))
        dsv = d_a * u * (zg * sg)
        dsvb[...] = _mx(dsv)
        tril = (lax.broadcasted_iota(jnp.int32, (CHUNK, CHUNK), 0)
                >= lax.broadcasted_iota(jnp.int32, (CHUNK, CHUNK), 1))
        for g in range(N_GROUPS):
            cols = slice(g * CHUNK, (g + 1) * CHUNK)
            rs = jnp.sum(dsv[:, cols], axis=-1, keepdims=True)
            tot = rs[0:CHUNK]
            for cc in range(1, n_chunks):
                tot = tot + rs[cc * CHUNK:(cc + 1) * CHUNK]
            dbst_ref[:, g:g + 1] += tot
            for cc in range(n_chunks):
                rows = slice(cc * CHUNK, (cc + 1) * CHUNK)
                dws = lax.dot_general(dsvb[rows, cols], vb[rows, cols], NT_DIMS, preferred_element_type=F32)
                dws_ref[g] += jnp.where(tril, dws, 0.0)
                dvbuf[rows, cols] = jnp.dot(wmt_ref[g], dsvb[rows, cols], preferred_element_type=F32)
        dv = dvbuf[...]
        vecg_ref[0:1, :] += _rowsum(dv * vhat)
        vecg_ref[1:2, :] += _rowsum(dv)
        put(SEG_AV, _ln_grad(dv, lna_ref[0:1, :], vhat, rstd) * _gelu_grad(zv, tv))

        za, zb, zg = seg(SEG_BA), seg(SEG_BB), seg(SEG_BG)
        sgb = jax.nn.sigmoid(zb)
        halo = zha_ref[...].astype(F32) * jax.nn.sigmoid(zhb_ref[...].astype(F32))
        gbuf[0:HALO, :] = jnp.where(i < nT - 1, halo, 0.0)
        gbuf[HALO:HALO + tT, :] = za * sgb
        chat, crstd = _ln_stats(c_ref[...])
        cl = chat * cvec_ref[1:2, :] + cvec_ref[2:3, :]
        scl = jax.nn.sigmoid(cl)
        sg = jax.nn.sigmoid(zg)
        d_b = dbr_ref[1].astype(F32)
        put(SEG_BG, d_b * (cl * scl) * _silu_grad(zg, sg))
        dcl = d_b * (zg * sg) * _silu_grad(cl, scl)
        vecg_ref[3:4, :] += _rowsum(dcl * chat)
        vecg_ref[4:5, :] += _rowsum(dcl)
        dc = _ln_grad(dcl, cvec_ref[1:2, :], chat, crstd)
        vecg_ref[2:3, :] += _rowsum(dc)
        dcbuf[0:tT, :] = dc
        dglu = jnp.zeros((tT, D), F32)
        for k in range(CONV_K):
            dglu = dglu + cw_ref[k:k + 1, :] * dcbuf[pl.ds(CONV_K - 1 - k, tT), :]
            dcw_ref[k:k + 1, :] += _rowsum(dc * gbuf[pl.ds(k + HALO - CONV_K + 1, tT), :])
        dcbuf[tT:tT + HALO, :] = dcbuf[0:HALO, :]
        put(SEG_BA, dglu * sgb)
        put(SEG_BB, dglu * za * sgb * (1.0 - sgb))

        scale = 1.0 / math.sqrt(HEAD_DIM)
        for h in range(HEADS):
            cols = slice(h * HEAD_DIM, (h + 1) * HEAD_DIM)
            qcols = slice(SEG_CQ * D + h * HEAD_DIM, SEG_CQ * D + (h + 1) * HEAD_DIM)
            gcols = slice(SEG_CG * D + h * HEAD_DIM, SEG_CG * D + (h + 1) * HEAD_DIM)
            vcols = slice(D + h * HEAD_DIM, D + (h + 1) * HEAD_DIM)
            q = _mx(z_ref[:, qcols])
            kh, vh = kv_ref[:, cols], kv_ref[:, vcols]
            p = _softmax_rows(lax.dot_general(q, kh, NT_DIMS, preferred_element_type=F32) * scale)
            pb = _mx(p)
            att = jnp.dot(pb, vh, preferred_element_type=F32)
            zg = z_ref[:, gcols].astype(F32)
            sg = jax.nn.sigmoid(zg)
            d_c = dbr_ref[2, :, cols].astype(F32)
            dz_ref[:, gcols] = (d_c * att * _silu_grad(zg, sg)).astype(dz_ref.dtype)
            datt = _mx(d_c * (zg * sg))
            dp = lax.dot_general(datt, vh, NT_DIMS, preferred_element_type=F32)
            dkv_ref[:, vcols] += lax.dot_general(pb, datt, TN_DIMS, preferred_element_type=F32)
            ds = _mx(p * (dp - jnp.sum(dp * p, axis=-1, keepdims=True)) * scale)
            dz_ref[:, qcols] = jnp.dot(ds, kh, preferred_element_type=F32).astype(dz_ref.dtype)
            dkv_ref[:, cols] += lax.dot_general(ds, q, TN_DIMS, preferred_element_type=F32)

        dz_ref[:, SEG_M * D:] = dzm_ref[...].astype(dz_ref.dtype)

    rev = lambda i: nT - 1 - i
    halo_row = lambda i: jnp.maximum(rev(i) * halo_blocks - 1, 0)
    full = lambda shape: pl.BlockSpec(shape, lambda i: (0,) * len(shape))
    return pl.pallas_call(
        body, name=name, grid=(nT,),
        in_specs=[pl.BlockSpec((tT, SEG_M * D), lambda i: (rev(i), 0)),
                  pl.BlockSpec((HALO, D), lambda i: (halo_row(i), SEG_BA)),
                  pl.BlockSpec((HALO, D), lambda i: (halo_row(i), SEG_BB)),
                  pl.BlockSpec((3, tT, D), lambda i: (0, rev(i), 0)),
                  pl.BlockSpec((tT, D), lambda i: (rev(i), 0)),
                  pl.BlockSpec((tT, 3 * D), lambda i: (rev(i), 0)),
                  full(kv.shape), full(wm.shape), full(wmt.shape), full(bst.shape), full(ln_a.shape),
                  full(cw.shape), full(cvec.shape)],
        out_specs=[pl.BlockSpec((tT, N_IN), lambda i: (rev(i), 0)),
                   full((8, D)), full((CHUNK, N_GROUPS)), full((N_GROUPS, CHUNK, CHUNK)), full((HALO, D)),
                   full((M, 2 * D))],
        out_shape=[jax.ShapeDtypeStruct((T, N_IN), MXU_DTYPE), jax.ShapeDtypeStruct((8, D), F32),
                   jax.ShapeDtypeStruct((CHUNK, N_GROUPS), F32),
                   jax.ShapeDtypeStruct((N_GROUPS, CHUNK, CHUNK), F32),
                   jax.ShapeDtypeStruct((HALO, D), F32), jax.ShapeDtypeStruct((M, 2 * D), F32)],
        scratch_shapes=[pltpu.VMEM((tT + HALO, D), F32), pltpu.VMEM((tT + HALO, D), F32),
                        pltpu.VMEM((tT, D), MXU_DTYPE), pltpu.VMEM((tT, D), MXU_DTYPE),
                        pltpu.VMEM((tT, D), F32), pltpu.VMEM((tT, D), F32)],
        compiler_params=_params("arbitrary"),
    )(z, z, z, dbr, c, dzm, kv, wm, wmt, bst, ln_a, cw, cvec)


def _merge_fwd(br, z, x, wb, wo, name):
    T = x.shape[0]
    tT = _tile(T, 512)

    def body(br_ref, z0, z1, z2, x_ref, wb_ref, wo_ref, xn_ref, mg_ref, pj_ref):
        merged = jnp.zeros((tT, D), F32)
        for n, zm in enumerate((z0, z1, z2)):
            proj = jnp.dot(br_ref[n], wb_ref[:, n].reshape(D, D), preferred_element_type=F32)
            pj_ref[n] = proj.astype(pj_ref.dtype)
            merged = merged + jax.nn.sigmoid(zm[...].astype(F32)) * proj
        mg_ref[...] = merged.astype(mg_ref.dtype)
        xn_ref[...] = x_ref[...] + jnp.dot(_mx(merged), wo_ref[...].reshape(D, D), preferred_element_type=F32)

    zspec = lambda n: pl.BlockSpec((tT, D), lambda t: (t, SEG_M + n))
    return pl.pallas_call(
        body, name=name, grid=(T // tT,),
        in_specs=[pl.BlockSpec((3, tT, D), lambda t: (0, t, 0)), zspec(0), zspec(1), zspec(2),
                  pl.BlockSpec((tT, D), lambda t: (t, 0)),
                  pl.BlockSpec(wb.shape, lambda t: (0, 0, 0, 0)),
                  pl.BlockSpec(wo.shape, lambda t: (0, 0, 0))],
        out_specs=[pl.BlockSpec((tT, D), lambda t: (t, 0)),
                   pl.BlockSpec((tT, D), lambda t: (t, 0)),
                   pl.BlockSpec((3, tT, D), lambda t: (0, t, 0))],
        out_shape=[jax.ShapeDtypeStruct((T, D), F32), jax.ShapeDtypeStruct((T, D), MXU_DTYPE),
                   jax.ShapeDtypeStruct((3, T, D), ACT_DTYPE)],
        compiler_params=_params("parallel"),
    )(br, z, z, z, x, wb, wo)


def _merge_bwd(dxo, proj, z, wb, wo, name):
    T = dxo.shape[0]
    tT = _tile(T, 512)

    def body(dxo_ref, pj_ref, z0, z1, z2, wb_ref, wo_ref, dpj_ref, dbr_ref, dzm_ref):
        dmerged = lax.dot_general(_mx(dxo_ref[...]), wo_ref[...].reshape(D, D), NT_DIMS,
                                  preferred_element_type=F32)
        for n, zm in enumerate((z0, z1, z2)):
            gate = jax.nn.sigmoid(zm[...].astype(F32))
            dproj = _mx(gate * dmerged)
            dpj_ref[n] = dproj
            dzm_ref[:, n * D:(n + 1) * D] = (pj_ref[n].astype(F32) * dmerged * gate * (1.0 - gate)
                                             ).astype(dzm_ref.dtype)
            dbr_ref[n] = lax.dot_general(dproj, wb_ref[:, n].reshape(D, D), NT_DIMS,
                                         preferred_element_type=F32).astype(dbr_ref.dtype)

    zspec = lambda n: pl.BlockSpec((tT, D), lambda t: (t, SEG_M + n))
    return pl.pallas_call(
        body, name=name, grid=(T // tT,),
        in_specs=[pl.BlockSpec((tT, D), lambda t: (t, 0)),
                  pl.BlockSpec((3, tT, D), lambda t: (0, t, 0)), zspec(0), zspec(1), zspec(2),
                  pl.BlockSpec(wb.shape, lambda t: (0, 0, 0, 0)),
                  pl.BlockSpec(wo.shape, lambda t: (0, 0, 0))],
        out_specs=[pl.BlockSpec((3, tT, D), lambda t: (0, t, 0)),
                   pl.BlockSpec((3, tT, D), lambda t: (0, t, 0)),
                   pl.BlockSpec((tT, 3 * D), lambda t: (t, 0))],
        out_shape=[jax.ShapeDtypeStruct((3, T, D), MXU_DTYPE), jax.ShapeDtypeStruct((3, T, D), F32),
                   jax.ShapeDtypeStruct((T, 3 * D), MXU_DTYPE)],
        compiler_params=_params("parallel"),
    )(dxo, proj, z, z, z, wb, wo)


def _adamw(parts, w, m, v, idx, name):
    R, C = parts.shape[1:]
    tr = 128 if R % 128 == 0 else R
    c1 = 1.0 / (1.0 - ADAM_B1 ** ADAM_STEP)
    c2 = 1.0 / (1.0 - ADAM_B2 ** ADAM_STEP)

    def body(p_ref, w_ref, m_ref, v_ref, g_out, d_out, m_out, v_out):
        g = p_ref[0]
        for p in range(1, NDEV):
            g = g + p_ref[p]
        mn = ADAM_B1 * m_ref[...] + (1.0 - ADAM_B1) * g
        vn = ADAM_B2 * v_ref[...] + (1.0 - ADAM_B2) * (g * g)
        g_out[...] = g
        m_out[...] = mn
        v_out[...] = vn
        d_out[...] = -ADAM_LR * ((mn * c1) / (jnp.sqrt(vn * c2) + ADAM_EPS) + ADAM_WD * w_ref[...])

    wspec = pl.BlockSpec((None, tr, C), lambda r: (idx, r, 0))
    ospec = pl.BlockSpec((tr, C), lambda r: (r, 0))
    return pl.pallas_call(
        body, name=name, grid=(R // tr,),
        in_specs=[pl.BlockSpec((NDEV, tr, C), lambda r: (0, r, 0)), wspec, wspec, wspec],
        out_specs=[ospec] * 4,
        out_shape=[jax.ShapeDtypeStruct((R, C), F32)] * 4,
        compiler_params=_params("parallel"),
    )(parts, w, m, v)


def kernel(x, mem, norm_g, mem_norm_g, w_in, gmlp_ln_g, gmlp_ln_b, w_s, b_s, conv_w, conv_b, conv_ln_g, conv_ln_b, w_kv, w_branch, w_out, final_norm_g, loss_target, m_norm_g, m_mem_norm_g, m_w_in, m_gmlp_ln_g, m_gmlp_ln_b, m_w_s, m_b_s, m_conv_w, m_conv_b, m_conv_ln_g, m_conv_ln_b, m_w_kv, m_w_branch, m_w_out, m_final_norm_g, v_norm_g, v_mem_norm_g, v_w_in, v_gmlp_ln_g, v_gmlp_ln_b, v_w_s, v_b_s, v_conv_w, v_conv_b, v_conv_ln_g, v_conv_ln_b, v_w_kv, v_w_branch, v_w_out, v_final_norm_g):
    L = w_in.shape[0]
    x0, mem0, tgt = x[0], mem[0], loss_target[0]
    T, M = x0.shape[0], mem0.shape[0]
    nbi, nbk, nbc = w_in.shape[2], w_kv.shape[2], conv_w.shape[2]

    shards = []
    for l in range(L):
        shards += [_mx(w_in[l]), _mx(w_kv[l]), _mx(w_branch[l]), _mx(w_out[l]), conv_w[l]]
    full = _exchange(shards, [True] * len(shards), "gather_weights")
    win = [full[5 * l + 0] for l in range(L)]
    wkv = [full[5 * l + 1] for l in range(L)]
    wbr = [full[5 * l + 2] for l in range(L)]
    wou = [full[5 * l + 3] for l in range(L)]
    cwf = [jnp.pad(full[5 * l + 4].transpose(1, 0, 2).reshape(CONV_K, D), ((0, HALO - CONV_K), (0, 0)))
           for l in range(L)]

    tril = jnp.tril(jnp.ones((CHUNK, CHUNK), bool))
    wm = [_mx(jnp.where(tril[None], w_s[l], 0.0)) for l in range(L)]
    wmt = [w.transpose(0, 2, 1) for w in wm]
    bst = [b_s[l].T for l in range(L)]
    ln_a = [jnp.stack([gmlp_ln_g[l], gmlp_ln_b[l]]) for l in range(L)]
    cvec = [jnp.stack([conv_b[l], conv_ln_g[l], conv_ln_b[l]]) for l in range(L)]

    memn, kvs, xs, saved = [], [], [x0], []
    for l in range(L):
        kv, mn = _rms_matmul(mem0, mem_norm_g[l], wkv[l], f"kv_fwd{l}")
        kvs.append(_mx(kv))
        memn.append(mn)
    for l in range(L):
        z, h = _rms_matmul(xs[l], norm_g[l], win[l], f"inproj_fwd{l}")
        br, cpre = _branch_fwd(z, kvs[l], wm[l], bst[l], ln_a[l], cwf[l], cvec[l], f"branch_fwd{l}")
        xn, merged, proj = _merge_fwd(br, z, xs[l], wbr[l], wou[l], f"merge_fwd{l}")
        xs.append(xn)
        saved.append((z, h, br, cpre, merged, proj))
    loss_part, dx, dfg = _loss_head(xs[L], tgt, final_norm_g, "loss_head")

    big = [None] * L
    small = [None] * L
    dws_all = [None] * L
    for l in reversed(range(L)):
        z, h, br, cpre, merged, proj = saved[l]
        dproj, dbr, dzm = _merge_bwd(dx, proj, z, wbr[l], wou[l], f"merge_bwd{l}")
        dwb = [_atb(br, n, dproj, n, 1, f"dwbranch{l}_{n}").reshape(NDEV, D // NDEV, D) for n in range(3)]
        dwo = _atb(merged[None], 0, dx[None], 0, 1, f"dwout{l}").reshape(NDEV, D // NDEV, D)
        dz, vecg, dbst, dws, dcw, dkv = _branch_bwd(z, dbr, cpre, dzm, kvs[l], wm[l], wmt[l], bst[l], ln_a[l],
                                                   cwf[l], cvec[l], f"branch_bwd{l}")
        dwi = _atb(h[None], 0, dz[None], 0, NDEV, f"dwin{l}")
        dx, dng = _rms_matmul_bwd(dz, win[l], xs[l], norm_g[l], dx, f"inproj_bwd{l}")
        dwk = _atb(memn[l][None], 0, dkv[None], 0, NDEV, f"dwkv{l}")
        _, dmg = _rms_matmul_bwd(dkv, wkv[l], mem0, mem_norm_g[l], jnp.zeros((M, D), F32), f"kv_bwd{l}")
        dcw_blocks = dcw[:CONV_K].reshape(CONV_K, NDEV, nbc).transpose(1, 0, 2)
        big[l] = [dwi, dwk] + dwb + [dwo, dcw_blocks]
        small[l] = jnp.concatenate([dng, dmg, vecg[0:2], vecg[2:5], dbst.T.reshape(1, D)], axis=0)
        dws_all[l] = dws.reshape(N_GROUPS * CHUNK, CHUNK)
    grad_x = dx[None]

    def pack(p):
        rows = []
        for l in range(L):
            rows += [p["norm_g"][l], p["mem_norm_g"][l], p["gmlp_ln_g"][l], p["gmlp_ln_b"][l], p["conv_b"][l],
                     p["conv_ln_g"][l], p["conv_ln_b"][l], p["b_s"][l].reshape(D)]
        return jnp.stack(rows + [p["final_norm_g"]])[None]

    names = ["norm_g", "mem_norm_g", "gmlp_ln_g", "gmlp_ln_b", "conv_b", "conv_ln_g", "conv_ln_b", "b_s",
             "final_norm_g"]
    w_small = pack(dict(zip(names, [norm_g, mem_norm_g, gmlp_ln_g, gmlp_ln_b, conv_b, conv_ln_g, conv_ln_b,
                                    b_s, final_norm_g])))
    m_small = pack(dict(zip(names, [m_norm_g, m_mem_norm_g, m_gmlp_ln_g, m_gmlp_ln_b, m_conv_b, m_conv_ln_g,
                                    m_conv_ln_b, m_b_s, m_final_norm_g])))
    v_small = pack(dict(zip(names, [v_norm_g, v_mem_norm_g, v_gmlp_ln_g, v_gmlp_ln_b, v_conv_b, v_conv_ln_g,
                                    v_conv_ln_b, v_b_s, v_final_norm_g])))
    g_small = jnp.concatenate(small + [dfg], axis=0)
    g_ws = jnp.concatenate(dws_all, axis=0)

    sends = [a for l in range(L) for a in big[l]] + [g_small, g_ws]
    per = len(big[0])
    recv = _exchange(sends, [False] * (L * per) + [True, True], "scatter_grads")

    outs = {}

    def run(key, parts, w, m, v, idx):
        outs[key] = _adamw(parts, w, m, v, idx, "adamw_" + key)

    for l in range(L):
        r = recv[l * per:(l + 1) * per]
        run(f"w_in{l}", r[0], w_in, m_w_in, v_w_in, l)
        run(f"w_kv{l}", r[1], w_kv, m_w_kv, v_w_kv, l)
        for n in range(3):
            sh = (L * 3, D // NDEV, D)
            run(f"w_branch{l}_{n}", r[2 + n], w_branch.reshape(sh), m_w_branch.reshape(sh),
                v_w_branch.reshape(sh), l * 3 + n)
        run(f"w_out{l}", r[5], w_out, m_w_out, v_w_out, l)
        run(f"conv_w{l}", r[6], conv_w, m_conv_w, v_conv_w, l)
    run("small", recv[L * per], w_small, m_small, v_small, 0)
    ws_shape = (1, L * N_GROUPS * CHUNK, CHUNK)
    run("w_s", recv[L * per + 1], w_s.reshape(ws_shape), m_w_s.reshape(ws_shape), v_w_s.reshape(ws_shape), 0)

    def leaf(name, k):
        if name in ("w_in", "w_kv", "w_out", "conv_w"):
            return jnp.stack([outs[f"{name}{l}"][k] for l in range(L)])
        if name == "w_branch":
            return jnp.stack([jnp.stack([outs[f"w_branch{l}_{n}"][k] for n in range(3)]) for l in range(L)])
        if name == "w_s":
            return outs["w_s"][k].reshape(L, N_GROUPS, CHUNK, CHUNK)
        sm = outs["small"][k]
        if name == "final_norm_g":
            return sm[8 * L]
        j = names.index(name)
        rows = jnp.stack([sm[8 * l + j] for l in range(L)])
        return rows.reshape(L, N_GROUPS, CHUNK) if name == "b_s" else rows

    order = ["norm_g", "mem_norm_g", "w_in", "gmlp_ln_g", "gmlp_ln_b", "w_s", "b_s", "conv_w", "conv_b",
             "conv_ln_g", "conv_ln_b", "w_kv", "w_branch", "w_out", "final_norm_g"]
    loss = lax.psum(loss_part[0, 0], ("x", "y", "c"))
    return (loss, grad_x, *[leaf(nm, k) for k in range(4) for nm in order])
```

```python
import functools
import math

import jax
import jax.numpy as jnp
from jax import lax
from jax.experimental import pallas as pl
from jax.experimental.pallas import tpu as pltpu

F32 = jnp.float32
MXU_DTYPE = jnp.bfloat16
ACT_DTYPE = jnp.bfloat16
GRAD_DTYPE = jnp.bfloat16

D = 1024
N_SEG = 11
N_IN = N_SEG * D
NDEV = 8
CHUNK = 128
N_GROUPS = 8
CONV_K = 31
HALO = 32
HEADS = 4
HEAD_DIM = D // HEADS
RMS_EPS = 1e-6
LN_EPS = 1e-5
ADAM_LR, ADAM_B1, ADAM_B2, ADAM_EPS, ADAM_WD, ADAM_STEP = 0.001, 0.9, 0.999, 1e-08, 0.01, 10
SEG_AU, SEG_AV, SEG_AG, SEG_BA, SEG_BB, SEG_BG, SEG_CQ, SEG_CG, SEG_M = 0, 1, 2, 3, 4, 5, 6, 7, 8

VMEM_LIMIT = 60 * 1024 * 1024
MESH = pl.DeviceIdType.MESH
NT_DIMS = (((1,), (1,)), ((), ()))
TN_DIMS = (((0,), (0,)), ((), ()))


def _params(*sem):
    return pltpu.CompilerParams(dimension_semantics=sem, vmem_limit_bytes=VMEM_LIMIT)


def _tile(n, want):
    t = min(n, want)
    assert n % t == 0, (n, want)
    return t


def _mx(v):
    return v.astype(MXU_DTYPE)


def _gelu(x):
    t = jnp.tanh(0.7978845608028654 * (x + 0.044715 * x * x * x))
    return 0.5 * x * (1.0 + t), t


def _gelu_grad(x, t):
    return 0.5 * (1.0 + t) + 0.5 * x * (1.0 - t * t) * 0.7978845608028654 * (1.0 + 3.0 * 0.044715 * x * x)


def _silu_grad(x, s):
    return s * (1.0 + x * (1.0 - s))


def _ln_stats(v):
    mu = jnp.mean(v, axis=-1, keepdims=True)
    vc = v - mu
    rstd = lax.rsqrt(jnp.mean(vc * vc, axis=-1, keepdims=True) + LN_EPS)
    return vc * rstd, rstd


def _ln_grad(dy, g, vhat, rstd):
    dvh = dy * g
    return rstd * (dvh - jnp.mean(dvh, axis=-1, keepdims=True)
                   - vhat * jnp.mean(dvh * vhat, axis=-1, keepdims=True))


def _rowsum(v):
    return jnp.sum(v, axis=0, keepdims=True)


def _coords():
    return lax.axis_index("x"), lax.axis_index("y"), lax.axis_index("c")


def _flip(pos, d):
    x, y, c = pos
    return (1 - x if d & 4 else x, 1 - y if d & 2 else y, 1 - c if d & 1 else c)


def _slot(pos):
    return 4 * pos[0] + 2 * pos[1] + pos[2]


CHIP_FLIPS = (4, 2, 6)


class _Gather:
    def __init__(self, arrays, mid_frac=0.8):
        self.arrays = list(arrays)
        self.n = n = len(arrays)
        self.mid_frac = mid_frac
        self.out_shape = [jax.ShapeDtypeStruct((NDEV,) + a.shape, a.dtype) for a in arrays]
        self.scratch = [pltpu.SemaphoreType.DMA((n, 7)), pltpu.SemaphoreType.DMA((n, 7)),
                        pltpu.SemaphoreType.DMA((n,))]

    def _copy(self, refs, i, k, block, to, own=False):
        ins, outs, (send, recv, _) = refs
        slot = outs[i].at[_slot(block)]
        return pltpu.make_async_remote_copy(
            src_ref=ins[i] if own else slot, dst_ref=slot, send_sem=send.at[i, k], recv_sem=recv.at[i, k],
            device_id=to, device_id_type=MESH)

    def _local(self, refs, i):
        ins, outs, (_, _, loc) = refs
        return pltpu.make_async_copy(ins[i], outs[i].at[_slot(_coords())], loc.at[i])

    def _first(self, refs, i, k):
        me = _coords()
        return self._copy(refs, i, k, me, _flip(me, ((1,) + CHIP_FLIPS)[k]), own=True)

    def _passed(self, refs, i, j):
        me = _coords()
        return self._copy(refs, i, 4 + j, _flip(me, CHIP_FLIPS[j]), _flip(me, 1))

    def start(self, refs):
        for i in range(self.n):
            self._local(refs, i).start()
        for k in range(4):
            for i in range(self.n):
                self._first(refs, i, k).start()

    def forward(self, refs):
        me = _coords()
        for j, d in enumerate(CHIP_FLIPS):
            for i in range(self.n):
                self._copy(refs, i, 1 + j, _flip(me, d), me).wait_recv()
                self._passed(refs, i, j).start()

    def finish(self, refs):
        me = _coords()
        sib = _flip(me, 1)
        for i in range(self.n):
            self._copy(refs, i, 0, sib, me).wait_recv()
        for j, d in enumerate(CHIP_FLIPS):
            for i in range(self.n):
                self._copy(refs, i, 4 + j, _flip(sib, d), me).wait_recv()
        for i in range(self.n):
            for k in range(4):
                self._first(refs, i, k).wait_send()
            for j in range(3):
                self._passed(refs, i, j).wait_send()
            self._local(refs, i).wait()


class _Scatter:
    def __init__(self, arrays):
        self.arrays = list(arrays)
        self.n = n = len(arrays)
        self.mid_frac = None
        self.out_shape = [jax.ShapeDtypeStruct(a.shape, a.dtype) for a in arrays]
        self.scratch = [pltpu.SemaphoreType.DMA((n, 7)), pltpu.SemaphoreType.DMA((n, 7)),
                        pltpu.SemaphoreType.DMA((n,))]

    def _copy(self, refs, i, d, landing):
        ins, outs, (send, recv, _) = refs
        me = _coords()
        peer = _flip(me, d)
        return pltpu.make_async_remote_copy(
            src_ref=ins[i].at[_slot(peer)], dst_ref=outs[i].at[_slot(peer) if landing else _slot(me)],
            send_sem=send.at[i, d - 1], recv_sem=recv.at[i, d - 1], device_id=peer, device_id_type=MESH)

    def _local(self, refs, i):
        ins, outs, (_, _, loc) = refs
        me = _slot(_coords())
        return pltpu.make_async_copy(ins[i].at[me], outs[i].at[me], loc.at[i])

    def start(self, refs):
        for i in range(self.n):
            self._local(refs, i).start()
        for d in range(1, NDEV):
            for i in range(self.n):
                self._copy(refs, i, d, False).start()

    def forward(self, refs):
        pass

    def finish(self, refs):
        for d in range(1, NDEV):
            for i in range(self.n):
                self._copy(refs, i, d, True).wait_recv()
        for d in range(1, NDEV):
            for i in range(self.n):
                self._copy(refs, i, d, False).wait_send()
        for i in range(self.n):
            self._local(refs, i).wait()


def _call(body, name, grid, in_specs, out_specs, out_shape, scratch, args, comm=None):
    params = _params(*(["arbitrary"] * len(grid)))
    if comm is None:
        outs = pl.pallas_call(
            body, name=name, grid=grid, in_specs=in_specs, out_specs=out_specs, out_shape=out_shape,
            scratch_shapes=scratch, compiler_params=params)(*args)
        return list(outs), []
    n_in, n_out, n_scr, k = len(in_specs), len(out_specs), len(scratch), comm.n
    nsteps = math.prod(grid) if grid else 1
    mid = min(nsteps - 1, int(nsteps * comm.mid_frac)) if comm.mid_frac is not None else None

    def hosted(*refs):
        ins, refs = refs[:n_in], refs[n_in:]
        cins, refs = refs[:k], refs[k:]
        outs, refs = refs[:n_out], refs[n_out:]
        couts, refs = refs[:k], refs[k:]
        scr, sems = refs[:n_scr], refs[n_scr:]
        crefs = (cins, couts, sems)
        if nsteps == 1:
            comm.start(crefs)
            body(*ins, *outs, *scr)
            comm.forward(crefs)
            comm.finish(crefs)
            return
        step = pl.program_id(0)
        for a in range(1, len(grid)):
            step = step * grid[a] + pl.program_id(a)
        pl.when(step == 0)(lambda: comm.start(crefs))
        if mid is not None:
            pl.when(step == mid)(lambda: comm.forward(crefs))
        body(*ins, *outs, *scr)
        pl.when(step == nsteps - 1)(lambda: comm.finish(crefs))

    any_spec = pl.BlockSpec(memory_space=pl.ANY)
    outs = pl.pallas_call(
        hosted, name=name, grid=grid,
        in_specs=list(in_specs) + [any_spec] * k, out_specs=list(out_specs) + [any_spec] * k,
        out_shape=list(out_shape) + comm.out_shape, scratch_shapes=list(scratch) + comm.scratch,
        compiler_params=params)(*args, *comm.arrays)
    return list(outs[:n_out]), list(outs[n_out:])


def _exchange(comm, name):
    return _call(lambda: None, name, (), [], [], [], [], [], comm)[1]


def _rms_matmul(x, g, w, name, comm=None):
    T = x.shape[0]
    nb = w.shape[2]
    tT = _tile(T, 1024)

    def body(x_ref, g_ref, w_ref, z_ref, h_ref):
        @pl.when(pl.program_id(1) == 0)
        def _():
            xf = x_ref[...]
            r = lax.rsqrt(jnp.mean(xf * xf, axis=-1, keepdims=True) + RMS_EPS)
            h_ref[...] = (xf * r * g_ref[...]).astype(h_ref.dtype)

        z_ref[...] = jnp.dot(h_ref[...], w_ref[...], preferred_element_type=F32).astype(z_ref.dtype)

    return _call(
        body, name, (T // tT, NDEV),
        [pl.BlockSpec((tT, D), lambda t, n: (t, 0)),
         pl.BlockSpec((1, D), lambda t, n: (0, 0)),
         pl.BlockSpec((None, D, nb), lambda t, n: (n, 0, 0))],
        [pl.BlockSpec((tT, nb), lambda t, n: (t, n)),
         pl.BlockSpec((tT, D), lambda t, n: (t, 0))],
        [jax.ShapeDtypeStruct((T, NDEV * nb), ACT_DTYPE), jax.ShapeDtypeStruct((T, D), MXU_DTYPE)],
        [], (x, g.reshape(1, D), w), comm)


def _rms_matmul_bwd(dz, w, x, g, dxo, name, comm=None):
    T = x.shape[0]
    nb = w.shape[2]
    tT = _tile(T, 1024)

    def body(dz_ref, w_ref, x_ref, g_ref, dxo_ref, dx_ref, dg_ref, acc):
        t, n = pl.program_id(0), pl.program_id(1)

        @pl.when(n == 0)
        def _():
            acc[...] = jnp.zeros_like(acc)

        @pl.when((n == 0) & (t == 0))
        def _():
            dg_ref[...] = jnp.zeros_like(dg_ref)

        acc[...] += lax.dot_general(_mx(dz_ref[...]), w_ref[...], NT_DIMS, preferred_element_type=F32)

        @pl.when(n == NDEV - 1)
        def _():
            xf = x_ref[...]
            r = lax.rsqrt(jnp.mean(xf * xf, axis=-1, keepdims=True) + RMS_EPS)
            xh = xf * r
            dh = acc[...]
            dxh = dh * g_ref[...]
            dx_ref[...] = dxo_ref[...] + r * (dxh - xh * jnp.mean(dxh * xh, axis=-1, keepdims=True))
            dg_ref[...] += _rowsum(dh * xh)

    return _call(
        body, name, (T // tT, NDEV),
        [pl.BlockSpec((tT, nb), lambda t, n: (t, n)),
         pl.BlockSpec((None, D, nb), lambda t, n: (n, 0, 0)),
         pl.BlockSpec((tT, D), lambda t, n: (t, 0)),
         pl.BlockSpec((1, D), lambda t, n: (0, 0)),
         pl.BlockSpec((tT, D), lambda t, n: (t, 0))],
        [pl.BlockSpec((tT, D), lambda t, n: (t, 0)),
         pl.BlockSpec((1, D), lambda t, n: (0, 0))],
        [jax.ShapeDtypeStruct((T, D), F32), jax.ShapeDtypeStruct((1, D), F32)],
        [pltpu.VMEM((tT, D), F32)], (dz, w, x, g.reshape(1, D), dxo), comm)


def _atb(a, ai, b, bi, nblk, name, comm=None):
    T, M = a.shape[1:]
    N = b.shape[2]
    nb = N // nblk
    tk = _tile(T, 512)
    nk = T // tk

    def body(a_ref, b_ref, o_ref, acc):
        k = pl.program_id(1)

        @pl.when(k == 0)
        def _():
            acc[...] = jnp.zeros_like(acc)

        acc[...] += lax.dot_general(_mx(a_ref[...]), _mx(b_ref[...]), TN_DIMS, preferred_element_type=F32)

        @pl.when(k == nk - 1)
        def _():
            o_ref[...] = acc[...].astype(o_ref.dtype)

    outs, couts = _call(
        body, name, (nblk, nk),
        [pl.BlockSpec((None, tk, M), lambda n, k: (ai, k, 0)),
         pl.BlockSpec((None, tk, nb), lambda n, k: (bi, k, n))],
        [pl.BlockSpec((None, M, nb), lambda n, k: (n, 0, 0))],
        [jax.ShapeDtypeStruct((nblk, M, nb), GRAD_DTYPE)],
        [pltpu.VMEM((M, nb), F32)], (a, b), comm)
    return outs[0], couts


def _loss_head(x, tgt, g, name):
    T = x.shape[0]
    tT = _tile(T, 512)

    def body(x_ref, t_ref, g_ref, loss_ref, dx_ref, dg_ref):
        @pl.when(pl.program_id(0) == 0)
        def _():
            loss_ref[...] = jnp.zeros_like(loss_ref)
            dg_ref[...] = jnp.zeros_like(dg_ref)

        xf = x_ref[...]
        r = lax.rsqrt(jnp.mean(xf * xf, axis=-1, keepdims=True) + RMS_EPS)
        xh = xf * r
        err = xh * g_ref[...] - t_ref[...]
        loss_ref[...] += 0.5 * jnp.sum(jnp.mean(err * err, axis=-1, keepdims=True), axis=0, keepdims=True)
        dy = err * (1.0 / D)
        dxh = dy * g_ref[...]
        dx_ref[...] = r * (dxh - xh * jnp.mean(dxh * xh, axis=-1, keepdims=True))
        dg_ref[...] += _rowsum(dy * xh)

    return pl.pallas_call(
        body, name=name, grid=(T // tT,),
        in_specs=[pl.BlockSpec((tT, D), lambda t: (t, 0)),
                  pl.BlockSpec((tT, D), lambda t: (t, 0)),
                  pl.BlockSpec((1, D), lambda t: (0, 0))],
        out_specs=[pl.BlockSpec((1, 1), lambda t: (0, 0)),
                   pl.BlockSpec((tT, D), lambda t: (t, 0)),
                   pl.BlockSpec((1, D), lambda t: (0, 0))],
        out_shape=[jax.ShapeDtypeStruct((1, 1), F32), jax.ShapeDtypeStruct((T, D), F32),
                   jax.ShapeDtypeStruct((1, D), F32)],
        compiler_params=_params("arbitrary"),
    )(x, tgt, g.reshape(1, D))


def _spatial_gate(wm_ref, bst_ref, vb_ref, sv_ref, n_chunks):
    for c in range(n_chunks):
        rows = slice(c * CHUNK, (c + 1) * CHUNK)
        for g in range(N_GROUPS):
            cols = slice(g * CHUNK, (g + 1) * CHUNK)
            sv_ref[rows, cols] = (jnp.dot(wm_ref[g], vb_ref[rows, cols], preferred_element_type=F32)
                                  + bst_ref[:, g:g + 1])


def _softmax_rows(s):
    e = jnp.exp(s - jnp.max(s, axis=-1, keepdims=True))
    return e / jnp.sum(e, axis=-1, keepdims=True)


def _branch_fwd(z, kv, wm, bst, ln_a, cw, cvec, name):
    T = z.shape[0]
    tT = _tile(T, 256)
    n_chunks = tT // CHUNK

    def body(z_ref, kv_ref, wm_ref, bst_ref, lna_ref, cw_ref, cvec_ref, br_ref, c_ref, gbuf, vb, sv):
        def seg(s):
            return z_ref[:, s * D:(s + 1) * D].astype(F32)

        u, _ = _gelu(seg(SEG_AU))
        gv, _ = _gelu(seg(SEG_AV))
        vhat, _ = _ln_stats(gv)
        vb[...] = _mx(vhat * lna_ref[0:1, :] + lna_ref[1:2, :])
        _spatial_gate(wm_ref, bst_ref, vb, sv, n_chunks)
        zg = seg(SEG_AG)
        br_ref[0] = (u * sv[...] * (zg * jax.nn.sigmoid(zg))).astype(br_ref.dtype)

        @pl.when(pl.program_id(0) == 0)
        def _():
            gbuf[0:HALO, :] = jnp.zeros((HALO, D), F32)

        gbuf[HALO:HALO + tT, :] = seg(SEG_BA) * jax.nn.sigmoid(seg(SEG_BB))
        acc = jnp.broadcast_to(cvec_ref[0:1, :], (tT, D))
        for k in range(CONV_K):
            acc = acc + cw_ref[k:k + 1, :] * gbuf[pl.ds(k + HALO - CONV_K + 1, tT), :]
        c_ref[...] = acc
        gbuf[0:HALO, :] = gbuf[tT:tT + HALO, :]
        chat, _ = _ln_stats(acc)
        cl = chat * cvec_ref[1:2, :] + cvec_ref[2:3, :]
        zg = seg(SEG_BG)
        br_ref[1] = (cl * jax.nn.sigmoid(cl) * (zg * jax.nn.sigmoid(zg))).astype(br_ref.dtype)

        for h in range(HEADS):
            cols = slice(h * HEAD_DIM, (h + 1) * HEAD_DIM)
            q = _mx(z_ref[:, SEG_CQ * D + h * HEAD_DIM:SEG_CQ * D + (h + 1) * HEAD_DIM])
            s = lax.dot_general(q, kv_ref[:, cols], NT_DIMS, preferred_element_type=F32)
            p = _softmax_rows(s * (1.0 / math.sqrt(HEAD_DIM)))
            att = jnp.dot(_mx(p), kv_ref[:, D + h * HEAD_DIM:D + (h + 1) * HEAD_DIM], preferred_element_type=F32)
            zg = z_ref[:, SEG_CG * D + h * HEAD_DIM:SEG_CG * D + (h + 1) * HEAD_DIM].astype(F32)
            br_ref[2, :, cols] = (att * (zg * jax.nn.sigmoid(zg))).astype(br_ref.dtype)

    full = lambda shape: pl.BlockSpec(shape, lambda t: (0,) * len(shape))
    return pl.pallas_call(
        body, name=name, grid=(T // tT,),
        in_specs=[pl.BlockSpec((tT, SEG_M * D), lambda t: (t, 0)),
                  full(kv.shape), full(wm.shape), full(bst.shape), full(ln_a.shape), full(cw.shape),
                  full(cvec.shape)],
        out_specs=[pl.BlockSpec((3, tT, D), lambda t: (0, t, 0)),
                   pl.BlockSpec((tT, D), lambda t: (t, 0))],
        out_shape=[jax.ShapeDtypeStruct((3, T, D), MXU_DTYPE), jax.ShapeDtypeStruct((T, D), F32)],
        scratch_shapes=[pltpu.VMEM((tT + HALO, D), F32), pltpu.VMEM((tT, D), MXU_DTYPE),
                        pltpu.VMEM((tT, D), F32)],
        compiler_params=_params("arbitrary"),
    )(z, kv, wm, bst, ln_a, cw, cvec)


def _branch_bwd(z, dbr, c, dzm, kv, wm, wmt, bst, ln_a, cw, cvec, name, comm=None):
    T = z.shape[0]
    M = kv.shape[0]
    tT = _tile(T, 128)
    nT = T // tT
    n_chunks = tT // CHUNK
    halo_blocks = tT // HALO

    def body(z_ref, zha_ref, zhb_ref, dbr_ref, c_ref, dzm_ref, kv_ref, wm_ref, wmt_ref, bst_ref, lna_ref,
             cw_ref, cvec_ref, dz_ref, vecg_ref, dbst_ref, dws_ref, dcw_ref, dkv_ref,
             gbuf, dcbuf, vb, dsvb, sv, dvbuf):
        i = pl.program_id(0)

        @pl.when(i == 0)
        def _():
            vecg_ref[...] = jnp.zeros_like(vecg_ref)
            dbst_ref[...] = jnp.zeros_like(dbst_ref)
            dws_ref[...] = jnp.zeros_like(dws_ref)
            dcw_ref[...] = jnp.zeros_like(dcw_ref)
            dkv_ref[...] = jnp.zeros_like(dkv_ref)
            dcbuf[tT:tT + HALO, :] = jnp.zeros((HALO, D), F32)

        def seg(s):
            return z_ref[:, s * D:(s + 1) * D].astype(F32)

        def put(s, val):
            dz_ref[:, s * D:(s + 1) * D] = val.astype(dz_ref.dtype)

        zu, zv, zg = seg(SEG_AU), seg(SEG_AV), seg(SEG_AG)
        u, tu = _gelu(zu)
        gv, tv = _gelu(zv)
        vhat, rstd = _ln_stats(gv)
        vb[...] = _mx(vhat * lna_ref[0:1, :] + lna_ref[1:2, :])
        _spatial_gate(wm_ref, bst_ref, vb, sv, n_chunks)
        sg = jax.nn.sigmoid(zg)
        d_a = dbr_ref[0].astype(F32)
        put(SEG_AU, d_a * sv[...] * (zg * sg) * _gelu_grad(zu, tu))
        put(SEG_AG, d_a * u * sv[...] * _silu_grad(zg, sg))
        dsv = d_a * u * (zg * sg)
        dsvb[...] = _mx(dsv)
        tril = (lax.broadcasted_iota(jnp.int32, (CHUNK, CHUNK), 0)
                >= lax.broadcasted_iota(jnp.int32, (CHUNK, CHUNK), 1))
        for g in range(N_GROUPS):
            cols = slice(g * CHUNK, (g + 1) * CHUNK)
            rs = jnp.sum(dsv[:, cols], axis=-1, keepdims=True)
            tot = rs[0:CHUNK]
            for cc in range(1, n_chunks):
                tot = tot + rs[cc * CHUNK:(cc + 1) * CHUNK]
            dbst_ref[:, g:g + 1] += tot
            for cc in range(n_chunks):
                rows = slice(cc * CHUNK, (cc + 1) * CHUNK)
                dws = lax.dot_general(dsvb[rows, cols], vb[rows, cols], NT_DIMS, preferred_element_type=F32)
                dws_ref[g] += jnp.where(tril, dws, 0.0)
                dvbuf[rows, cols] = jnp.dot(wmt_ref[g], dsvb[rows, cols], preferred_element_type=F32)
        dv = dvbuf[...]
        vecg_ref[0:1, :] += _rowsum(dv * vhat)
        vecg_ref[1:2, :] += _rowsum(dv)
        put(SEG_AV, _ln_grad(dv, lna_ref[0:1, :], vhat, rstd) * _gelu_grad(zv, tv))

        za, zb, zg = seg(SEG_BA), seg(SEG_BB), seg(SEG_BG)
        sgb = jax.nn.sigmoid(zb)
        halo = zha_ref[...].astype(F32) * jax.nn.sigmoid(zhb_ref[...].astype(F32))
        gbuf[0:HALO, :] = jnp.where(i < nT - 1, halo, 0.0)
        gbuf[HALO:HALO + tT, :] = za * sgb
        chat, crstd = _ln_stats(c_ref[...])
        cl = chat * cvec_ref[1:2, :] + cvec_ref[2:3, :]
        scl = jax.nn.sigmoid(cl)
        sg = jax.nn.sigmoid(zg)
        d_b = dbr_ref[1].astype(F32)
        put(SEG_BG, d_b * (cl * scl) * _silu_grad(zg, sg))
        dcl = d_b * (zg * sg) * _silu_grad(cl, scl)
        vecg_ref[3:4, :] += _rowsum(dcl * chat)
        vecg_ref[4:5, :] += _rowsum(dcl)
        dc = _ln_grad(dcl, cvec_ref[1:2, :], chat, crstd)
        vecg_ref[2:3, :] += _rowsum(dc)
        dcbuf[0:tT, :] = dc
        dglu = jnp.zeros((tT, D), F32)
        for k in range(CONV_K):
            dglu = dglu + cw_ref[k:k + 1, :] * dcbuf[pl.ds(CONV_K - 1 - k, tT), :]
            dcw_ref[k:k + 1, :] += _rowsum(dc * gbuf[pl.ds(k + HALO - CONV_K + 1, tT), :])
        dcbuf[tT:tT + HALO, :] = dcbuf[0:HALO, :]
        put(SEG_BA, dglu * sgb)
        put(SEG_BB, dglu * za * sgb * (1.0 - sgb))

        scale = 1.0 / math.sqrt(HEAD_DIM)
        for h in range(HEADS):
            cols = slice(h * HEAD_DIM, (h + 1) * HEAD_DIM)
            qcols = slice(SEG_CQ * D + h * HEAD_DIM, SEG_CQ * D + (h + 1) * HEAD_DIM)
            gcols = slice(SEG_CG * D + h * HEAD_DIM, SEG_CG * D + (h + 1) * HEAD_DIM)
            vcols = slice(D + h * HEAD_DIM, D + (h + 1) * HEAD_DIM)
            q = _mx(z_ref[:, qcols])
            kh, vh = kv_ref[:, cols], kv_ref[:, vcols]
            p = _softmax_rows(lax.dot_general(q, kh, NT_DIMS, preferred_element_type=F32) * scale)
            pb = _mx(p)
            att = jnp.dot(pb, vh, preferred_element_type=F32)
            zg = z_ref[:, gcols].astype(F32)
            sg = jax.nn.sigmoid(zg)
            d_c = dbr_ref[2, :, cols].astype(F32)
            dz_ref[:, gcols] = (d_c * att * _silu_grad(zg, sg)).astype(dz_ref.dtype)
            datt = _mx(d_c * (zg * sg))
            dp = lax.dot_general(datt, vh, NT_DIMS, preferred_element_type=F32)
            dkv_ref[:, vcols] += lax.dot_general(pb, datt, TN_DIMS, preferred_element_type=F32)
            ds = _mx(p * (dp - jnp.sum(dp * p, axis=-1, keepdims=True)) * scale)
            dz_ref[:, qcols] = jnp.dot(ds, kh, preferred_element_type=F32).astype(dz_ref.dtype)
            dkv_ref[:, cols] += lax.dot_general(ds, q, TN_DIMS, preferred_element_type=F32)

        dz_ref[:, SEG_M * D:] = dzm_ref[...].astype(dz_ref.dtype)

    rev = lambda i: nT - 1 - i
    halo_row = lambda i: jnp.maximum(rev(i) * halo_blocks - 1, 0)
    full = lambda shape: pl.BlockSpec(shape, lambda i: (0,) * len(shape))
    return _call(
        body, name, (nT,),
        [pl.BlockSpec((tT, SEG_M * D), lambda i: (rev(i), 0)),
         pl.BlockSpec((HALO, D), lambda i: (halo_row(i), SEG_BA)),
         pl.BlockSpec((HALO, D), lambda i: (halo_row(i), SEG_BB)),
         pl.BlockSpec((3, tT, D), lambda i: (0, rev(i), 0)),
         pl.BlockSpec((tT, D), lambda i: (rev(i), 0)),
         pl.BlockSpec((tT, 3 * D), lambda i: (rev(i), 0)),
         full(kv.shape), full(wm.shape), full(wmt.shape), full(bst.shape), full(ln_a.shape),
         full(cw.shape), full(cvec.shape)],
        [pl.BlockSpec((tT, N_IN), lambda i: (rev(i), 0)),
         full((8, D)), full((CHUNK, N_GROUPS)), full((N_GROUPS, CHUNK, CHUNK)), full((HALO, D)),
         full((M, 2 * D))],
        [jax.ShapeDtypeStruct((T, N_IN), MXU_DTYPE), jax.ShapeDtypeStruct((8, D), F32),
         jax.ShapeDtypeStruct((CHUNK, N_GROUPS), F32),
         jax.ShapeDtypeStruct((N_GROUPS, CHUNK, CHUNK), F32),
         jax.ShapeDtypeStruct((HALO, D), F32), jax.ShapeDtypeStruct((M, 2 * D), F32)],
        [pltpu.VMEM((tT + HALO, D), F32), pltpu.VMEM((tT + HALO, D), F32),
         pltpu.VMEM((tT, D), MXU_DTYPE), pltpu.VMEM((tT, D), MXU_DTYPE),
         pltpu.VMEM((tT, D), F32), pltpu.VMEM((tT, D), F32)],
        (z, z, z, dbr, c, dzm, kv, wm, wmt, bst, ln_a, cw, cvec), comm)


def _merge_fwd(br, z, x, wb, wo, name):
    T = x.shape[0]
    tT = _tile(T, 512)

    def body(br_ref, z0, z1, z2, x_ref, wb_ref, wo_ref, xn_ref, mg_ref, pj_ref):
        merged = jnp.zeros((tT, D), F32)
        for n, zm in enumerate((z0, z1, z2)):
            proj = jnp.dot(br_ref[n], wb_ref[:, n].reshape(D, D), preferred_element_type=F32)
            pj_ref[n] = proj.astype(pj_ref.dtype)
            merged = merged + jax.nn.sigmoid(zm[...].astype(F32)) * proj
        mg_ref[...] = merged.astype(mg_ref.dtype)
        xn_ref[...] = x_ref[...] + jnp.dot(_mx(merged), wo_ref[...].reshape(D, D), preferred_element_type=F32)

    zspec = lambda n: pl.BlockSpec((tT, D), lambda t: (t, SEG_M + n))
    return pl.pallas_call(
        body, name=name, grid=(T // tT,),
        in_specs=[pl.BlockSpec((3, tT, D), lambda t: (0, t, 0)), zspec(0), zspec(1), zspec(2),
                  pl.BlockSpec((tT, D), lambda t: (t, 0)),
                  pl.BlockSpec(wb.shape, lambda t: (0, 0, 0, 0)),
                  pl.BlockSpec(wo.shape, lambda t: (0, 0, 0))],
        out_specs=[pl.BlockSpec((tT, D), lambda t: (t, 0)),
                   pl.BlockSpec((tT, D), lambda t: (t, 0)),
                   pl.BlockSpec((3, tT, D), lambda t: (0, t, 0))],
        out_shape=[jax.ShapeDtypeStruct((T, D), F32), jax.ShapeDtypeStruct((T, D), MXU_DTYPE),
                   jax.ShapeDtypeStruct((3, T, D), ACT_DTYPE)],
        compiler_params=_params("parallel"),
    )(br, z, z, z, x, wb, wo)


def _merge_bwd(dxo, proj, z, wb, wo, name):
    T = dxo.shape[0]
    tT = _tile(T, 512)

    def body(dxo_ref, pj_ref, z0, z1, z2, wb_ref, wo_ref, dpj_ref, dbr_ref, dzm_ref):
        dmerged = lax.dot_general(_mx(dxo_ref[...]), wo_ref[...].reshape(D, D), NT_DIMS,
                                  preferred_element_type=F32)
        for n, zm in enumerate((z0, z1, z2)):
            gate = jax.nn.sigmoid(zm[...].astype(F32))
            dproj = _mx(gate * dmerged)
            dpj_ref[n] = dproj
            dzm_ref[:, n * D:(n + 1) * D] = (pj_ref[n].astype(F32) * dmerged * gate * (1.0 - gate)
                                             ).astype(dzm_ref.dtype)
            dbr_ref[n] = lax.dot_general(dproj, wb_ref[:, n].reshape(D, D), NT_DIMS,
                                         preferred_element_type=F32).astype(dbr_ref.dtype)

    zspec = lambda n: pl.BlockSpec((tT, D), lambda t: (t, SEG_M + n))
    return pl.pallas_call(
        body, name=name, grid=(T // tT,),
        in_specs=[pl.BlockSpec((tT, D), lambda t: (t, 0)),
                  pl.BlockSpec((3, tT, D), lambda t: (0, t, 0)), zspec(0), zspec(1), zspec(2),
                  pl.BlockSpec(wb.shape, lambda t: (0, 0, 0, 0)),
                  pl.BlockSpec(wo.shape, lambda t: (0, 0, 0))],
        out_specs=[pl.BlockSpec((3, tT, D), lambda t: (0, t, 0)),
                   pl.BlockSpec((3, tT, D), lambda t: (0, t, 0)),
                   pl.BlockSpec((tT, 3 * D), lambda t: (t, 0))],
        out_shape=[jax.ShapeDtypeStruct((3, T, D), MXU_DTYPE), jax.ShapeDtypeStruct((3, T, D), F32),
                   jax.ShapeDtypeStruct((T, 3 * D), MXU_DTYPE)],
        compiler_params=_params("parallel"),
    )(dxo, proj, z, z, z, wb, wo)


def _adamw(parts, w, m, v, idx, name):
    R, C = parts.shape[1:]
    tr = 128 if R % 128 == 0 else R
    c1 = 1.0 / (1.0 - ADAM_B1 ** ADAM_STEP)
    c2 = 1.0 / (1.0 - ADAM_B2 ** ADAM_STEP)

    def body(p_ref, w_ref, m_ref, v_ref, g_out, d_out, m_out, v_out):
        g = p_ref[0].astype(F32)
        for p in range(1, NDEV):
            g = g + p_ref[p].astype(F32)
        mn = ADAM_B1 * m_ref[...] + (1.0 - ADAM_B1) * g
        vn = ADAM_B2 * v_ref[...] + (1.0 - ADAM_B2) * (g * g)
        g_out[...] = g
        m_out[...] = mn
        v_out[...] = vn
        d_out[...] = -ADAM_LR * ((mn * c1) / (jnp.sqrt(vn * c2) + ADAM_EPS) + ADAM_WD * w_ref[...])

    wspec = pl.BlockSpec((None, tr, C), lambda r: (idx, r, 0))
    ospec = pl.BlockSpec((tr, C), lambda r: (r, 0))
    return pl.pallas_call(
        body, name=name, grid=(R // tr,),
        in_specs=[pl.BlockSpec((NDEV, tr, C), lambda r: (0, r, 0)), wspec, wspec, wspec],
        out_specs=[ospec] * 4,
        out_shape=[jax.ShapeDtypeStruct((R, C), F32)] * 4,
        compiler_params=_params("parallel"),
    )(parts, w, m, v)


def kernel(x, mem, norm_g, mem_norm_g, w_in, gmlp_ln_g, gmlp_ln_b, w_s, b_s, conv_w, conv_b, conv_ln_g, conv_ln_b, w_kv, w_branch, w_out, final_norm_g, loss_target, m_norm_g, m_mem_norm_g, m_w_in, m_gmlp_ln_g, m_gmlp_ln_b, m_w_s, m_b_s, m_conv_w, m_conv_b, m_conv_ln_g, m_conv_ln_b, m_w_kv, m_w_branch, m_w_out, m_final_norm_g, v_norm_g, v_mem_norm_g, v_w_in, v_gmlp_ln_g, v_gmlp_ln_b, v_w_s, v_b_s, v_conv_w, v_conv_b, v_conv_ln_g, v_conv_ln_b, v_w_kv, v_w_branch, v_w_out, v_final_norm_g):
    L = w_in.shape[0]
    x0, mem0, tgt = x[0], mem[0], loss_target[0]
    T, M = x0.shape[0], mem0.shape[0]
    nbi, nbk, nbc = w_in.shape[2], w_kv.shape[2], conv_w.shape[2]

    first = _exchange(_Gather([_mx(w_in[0]), _mx(w_kv[0])]), "gather_first")
    later = [_mx(w_branch[0]), _mx(w_out[0]), conv_w[0]]
    for l in range(1, L):
        later += [_mx(w_in[l]), _mx(w_kv[l]), _mx(w_branch[l]), _mx(w_out[l]), conv_w[l]]
    gather_later = _Gather(later)

    tril = jnp.tril(jnp.ones((CHUNK, CHUNK), bool))
    wm = [_mx(jnp.where(tril[None], w_s[l], 0.0)) for l in range(L)]
    wmt = [w.transpose(0, 2, 1) for w in wm]
    bst = [b_s[l].T for l in range(L)]
    ln_a = [jnp.stack([gmlp_ln_g[l], gmlp_ln_b[l]]) for l in range(L)]
    cvec = [jnp.stack([conv_b[l], conv_ln_g[l], conv_ln_b[l]]) for l in range(L)]

    win, wkv = [first[0]], [first[1]]
    memn, kvs, xs, saved = [], [], [x0], []
    for l in range(L):
        (z, h), full = _rms_matmul(xs[l], norm_g[l], win[l], f"inproj_fwd{l}", gather_later if l == 0 else None)
        if l == 0:
            wbr, wou, cwg = [full[0]], [full[1]], [full[2]]
            for k in range(1, L):
                f = full[3 + 5 * (k - 1):3 + 5 * k]
                win.append(f[0]); wkv.append(f[1]); wbr.append(f[2]); wou.append(f[3]); cwg.append(f[4])
            cwf = [jnp.pad(c.transpose(1, 0, 2).reshape(CONV_K, D), ((0, HALO - CONV_K), (0, 0))) for c in cwg]
        (kv, mn), _ = _rms_matmul(mem0, mem_norm_g[l], wkv[l], f"kv_fwd{l}")
        kvs.append(_mx(kv))
        memn.append(mn)
        br, cpre = _branch_fwd(z, kvs[l], wm[l], bst[l], ln_a[l], cwf[l], cvec[l], f"branch_fwd{l}")
        xn, merged, proj = _merge_fwd(br, z, xs[l], wbr[l], wou[l], f"merge_fwd{l}")
        xs.append(xn)
        saved.append((z, h, br, cpre, merged, proj))
    loss_part, dx, dfg = _loss_head(xs[L], tgt, final_norm_g, "loss_head")

    small = [None] * L
    dws_all = [None] * L
    pending, recv = [], {}

    def flush():
        sent = list(pending)
        pending.clear()
        return [k for k, _ in sent], _Scatter([a for _, a in sent])

    def landed(keys, arrays):
        recv.update(zip(keys, arrays))

    for l in reversed(range(L)):
        z, h, br, cpre, merged, proj = saved[l]
        dproj, dbr, dzm = _merge_bwd(dx, proj, z, wbr[l], wou[l], f"merge_bwd{l}")
        for n in range(3):
            dwb, _ = _atb(br, n, dproj, n, 1, f"dwbranch{l}_{n}")
            pending.append((f"w_branch{l}_{n}", dwb.reshape(NDEV, D // NDEV, D)))
        dwo, _ = _atb(merged[None], 0, dx[None], 0, 1, f"dwout{l}")
        pending.append((f"w_out{l}", dwo.reshape(NDEV, D // NDEV, D)))
        keys, comm = flush() if l == 0 else (None, None)
        (dz, vecg, dbst, dws, dcw, dkv), got = _branch_bwd(
            z, dbr, cpre, dzm, kvs[l], wm[l], wmt[l], bst[l], ln_a[l], cwf[l], cvec[l], f"branch_bwd{l}", comm)
        if l == 0:
            landed(keys, got)
        dwk, _ = _atb(memn[l][None], 0, dkv[None], 0, NDEV, f"dwkv{l}")
        (_, dmg), _ = _rms_matmul_bwd(dkv, wkv[l], mem0, mem_norm_g[l], jnp.zeros((M, D), F32), f"kv_bwd{l}")
        pending += [(f"w_kv{l}", dwk),
                    (f"conv_w{l}", dcw[:CONV_K].reshape(CONV_K, NDEV, nbc).transpose(1, 0, 2))]
        keys, comm = flush() if l == 0 else (None, None)
        dwi, got = _atb(h[None], 0, dz[None], 0, NDEV, f"dwin{l}", comm)
        if l == 0:
            landed(keys, got)
        pending.append((f"w_in{l}", dwi))
        keys, comm = flush() if l == 0 else (None, None)
        (dx, dng), got = _rms_matmul_bwd(dz, win[l], xs[l], norm_g[l], dx, f"inproj_bwd{l}", comm)
        if l == 0:
            landed(keys, got)
        small[l] = jnp.concatenate([dng, dmg, vecg[0:2], vecg[2:5], dbst.T.reshape(1, D)], axis=0)
        dws_all[l] = dws.reshape(N_GROUPS * CHUNK, CHUNK)
    grad_x = dx[None]

    def pack(p):
        rows = []
        for l in range(L):
            rows += [p["norm_g"][l], p["mem_norm_g"][l], p["gmlp_ln_g"][l], p["gmlp_ln_b"][l], p["conv_b"][l],
                     p["conv_ln_g"][l], p["conv_ln_b"][l], p["b_s"][l].reshape(D)]
        return jnp.stack(rows + [p["final_norm_g"]])[None]

    names = ["norm_g", "mem_norm_g", "gmlp_ln_g", "gmlp_ln_b", "conv_b", "conv_ln_g", "conv_ln_b", "b_s",
             "final_norm_g"]
    w_small = pack(dict(zip(names, [norm_g, mem_norm_g, gmlp_ln_g, gmlp_ln_b, conv_b, conv_ln_g, conv_ln_b,
                                    b_s, final_norm_g])))
    m_small = pack(dict(zip(names, [m_norm_g, m_mem_norm_g, m_gmlp_ln_g, m_gmlp_ln_b, m_conv_b, m_conv_ln_g,
                                    m_conv_ln_b, m_b_s, m_final_norm_g])))
    v_small = pack(dict(zip(names, [v_norm_g, v_mem_norm_g, v_gmlp_ln_g, v_gmlp_ln_b, v_conv_b, v_conv_ln_g,
                                    v_conv_ln_b, v_b_s, v_final_norm_g])))
    g_small = jnp.concatenate(small + [dfg], axis=0)
    g_ws = jnp.concatenate(dws_all, axis=0)

    parts_small, parts_ws = _exchange(_Gather([g_small, g_ws]), "gather_small_grads")

    outs = {}

    def run(key, parts, w, m, v, idx):
        outs[key] = _adamw(parts, w, m, v, idx, "adamw_" + key)

    for l in range(L):
        run(f"w_in{l}", recv[f"w_in{l}"], w_in, m_w_in, v_w_in, l)
        run(f"w_kv{l}", recv[f"w_kv{l}"], w_kv, m_w_kv, v_w_kv, l)
        for n in range(3):
            sh = (L * 3, D // NDEV, D)
            run(f"w_branch{l}_{n}", recv[f"w_branch{l}_{n}"], w_branch.reshape(sh), m_w_branch.reshape(sh),
                v_w_branch.reshape(sh), l * 3 + n)
        run(f"w_out{l}", recv[f"w_out{l}"], w_out, m_w_out, v_w_out, l)
        run(f"conv_w{l}", recv[f"conv_w{l}"], conv_w, m_conv_w, v_conv_w, l)
    run("small", parts_small, w_small, m_small, v_small, 0)
    ws_shape = (1, L * N_GROUPS * CHUNK, CHUNK)
    run("w_s", parts_ws, w_s.reshape(ws_shape), m_w_s.reshape(ws_shape), v_w_s.reshape(ws_shape), 0)

    def leaf(name, k):
        if name in ("w_in", "w_kv", "w_out", "conv_w"):
            return jnp.stack([outs[f"{name}{l}"][k] for l in range(L)])
        if name == "w_branch":
            return jnp.stack([jnp.stack([outs[f"w_branch{l}_{n}"][k] for n in range(3)]) for l in range(L)])
        if name == "w_s":
            return outs["w_s"][k].reshape(L, N_GROUPS, CHUNK, CHUNK)
        sm = outs["small"][k]
        if name == "final_norm_g":
            return sm[8 * L]
        j = names.index(name)
        rows = jnp.stack([sm[8 * l + j] for l in range(L)])
        return rows.reshape(L, N_GROUPS, CHUNK) if name == "b_s" else rows

    order = ["norm_g", "mem_norm_g", "w_in", "gmlp_ln_g", "gmlp_ln_b", "w_s", "b_s", "conv_w", "conv_b",
             "conv_ln_g", "conv_ln_b", "w_kv", "w_branch", "w_out", "final_norm_g"]
    loss = lax.psum(loss_part[0, 0], ("x", "y", "c"))
    return (loss, grad_x, *[leaf(nm, k) for k in range(4) for nm in order])
```

```python
import functools
import math

import jax
import jax.numpy as jnp
from jax import lax
from jax.experimental import pallas as pl
from jax.experimental.pallas import tpu as pltpu

F32 = jnp.float32
MXU_DTYPE = jnp.bfloat16
ACT_DTYPE = jnp.bfloat16
GRAD_DTYPE = jnp.bfloat16

D = 1024
N_SEG = 11
N_IN = N_SEG * D
NDEV = 8
CHUNK = 128
N_GROUPS = 8
CONV_K = 31
HALO = 32
LANES = 128
STRIP = 32
HEADS = 4
HEAD_DIM = D // HEADS
RMS_EPS = 1e-6
LN_EPS = 1e-5
ADAM_LR, ADAM_B1, ADAM_B2, ADAM_EPS, ADAM_WD, ADAM_STEP = 0.001, 0.9, 0.999, 1e-08, 0.01, 10
SEG_AU, SEG_AV, SEG_AG, SEG_BA, SEG_BB, SEG_BG, SEG_CQ, SEG_CG, SEG_M = 0, 1, 2, 3, 4, 5, 6, 7, 8

VMEM_LIMIT = 60 * 1024 * 1024
MESH = pl.DeviceIdType.MESH
NT_DIMS = (((1,), (1,)), ((), ()))
TN_DIMS = (((0,), (0,)), ((), ()))


def _params(*sem):
    return pltpu.CompilerParams(dimension_semantics=sem, vmem_limit_bytes=VMEM_LIMIT)


def _tile(n, want):
    t = min(n, want)
    assert n % t == 0, (n, want)
    return t


def _mx(v):
    return v.astype(MXU_DTYPE)


def _gelu(x):
    t = jnp.tanh(0.7978845608028654 * (x + 0.044715 * x * x * x))
    return 0.5 * x * (1.0 + t), t


def _gelu_grad(x, t):
    return 0.5 * (1.0 + t) + 0.5 * x * (1.0 - t * t) * 0.7978845608028654 * (1.0 + 3.0 * 0.044715 * x * x)


def _silu_grad(x, s):
    return s * (1.0 + x * (1.0 - s))


def _ln_stats(v):
    mu = jnp.mean(v, axis=-1, keepdims=True)
    vc = v - mu
    rstd = lax.rsqrt(jnp.mean(vc * vc, axis=-1, keepdims=True) + LN_EPS)
    return vc * rstd, rstd


def _ln_grad(dy, g, vhat, rstd):
    dvh = dy * g
    return rstd * (dvh - jnp.mean(dvh, axis=-1, keepdims=True)
                   - vhat * jnp.mean(dvh * vhat, axis=-1, keepdims=True))


def _rowsum(v):
    return jnp.sum(v, axis=0, keepdims=True)


def _coords():
    return lax.axis_index("x"), lax.axis_index("y"), lax.axis_index("c")


def _flip(pos, d):
    x, y, c = pos
    return (1 - x if d & 4 else x, 1 - y if d & 2 else y, 1 - c if d & 1 else c)


def _slot(pos):
    return 4 * pos[0] + 2 * pos[1] + pos[2]


CHIP_FLIPS = (4, 2, 6)


class _Gather:
    def __init__(self, arrays, mid_frac=0.8):
        self.arrays = list(arrays)
        self.n = n = len(arrays)
        self.mid_frac = mid_frac
        self.out_shape = [jax.ShapeDtypeStruct((NDEV,) + a.shape, a.dtype) for a in arrays]
        self.scratch = [pltpu.SemaphoreType.DMA((n, 7)), pltpu.SemaphoreType.DMA((n, 7)),
                        pltpu.SemaphoreType.DMA((n,))]

    def _copy(self, refs, i, k, block, to, own=False):
        ins, outs, (send, recv, _) = refs
        slot = outs[i].at[_slot(block)]
        return pltpu.make_async_remote_copy(
            src_ref=ins[i] if own else slot, dst_ref=slot, send_sem=send.at[i, k], recv_sem=recv.at[i, k],
            device_id=to, device_id_type=MESH)

    def _local(self, refs, i):
        ins, outs, (_, _, loc) = refs
        return pltpu.make_async_copy(ins[i], outs[i].at[_slot(_coords())], loc.at[i])

    def _first(self, refs, i, k):
        me = _coords()
        return self._copy(refs, i, k, me, _flip(me, ((1,) + CHIP_FLIPS)[k]), own=True)

    def _passed(self, refs, i, j):
        me = _coords()
        return self._copy(refs, i, 4 + j, _flip(me, CHIP_FLIPS[j]), _flip(me, 1))

    def start(self, refs):
        for i in range(self.n):
            self._local(refs, i).start()
        for k in range(4):
            for i in range(self.n):
                self._first(refs, i, k).start()

    def forward(self, refs):
        me = _coords()
        for j, d in enumerate(CHIP_FLIPS):
            for i in range(self.n):
                self._copy(refs, i, 1 + j, _flip(me, d), me).wait_recv()
                self._passed(refs, i, j).start()

    def finish(self, refs):
        me = _coords()
        sib = _flip(me, 1)
        for i in range(self.n):
            self._copy(refs, i, 0, sib, me).wait_recv()
        for j, d in enumerate(CHIP_FLIPS):
            for i in range(self.n):
                self._copy(refs, i, 4 + j, _flip(sib, d), me).wait_recv()
        for i in range(self.n):
            for k in range(4):
                self._first(refs, i, k).wait_send()
            for j in range(3):
                self._passed(refs, i, j).wait_send()
            self._local(refs, i).wait()


class _Scatter:
    def __init__(self, arrays):
        self.arrays = list(arrays)
        self.n = n = len(arrays)
        self.mid_frac = None
        self.out_shape = [jax.ShapeDtypeStruct(a.shape, a.dtype) for a in arrays]
        self.scratch = [pltpu.SemaphoreType.DMA((n, 7)), pltpu.SemaphoreType.DMA((n, 7)),
                        pltpu.SemaphoreType.DMA((n,))]

    def _copy(self, refs, i, d, landing):
        ins, outs, (send, recv, _) = refs
        me = _coords()
        peer = _flip(me, d)
        return pltpu.make_async_remote_copy(
            src_ref=ins[i].at[_slot(peer)], dst_ref=outs[i].at[_slot(peer) if landing else _slot(me)],
            send_sem=send.at[i, d - 1], recv_sem=recv.at[i, d - 1], device_id=peer, device_id_type=MESH)

    def _local(self, refs, i):
        ins, outs, (_, _, loc) = refs
        me = _slot(_coords())
        return pltpu.make_async_copy(ins[i].at[me], outs[i].at[me], loc.at[i])

    def start(self, refs):
        for i in range(self.n):
            self._local(refs, i).start()
        for d in range(1, NDEV):
            for i in range(self.n):
                self._copy(refs, i, d, False).start()

    def forward(self, refs):
        pass

    def finish(self, refs):
        for d in range(1, NDEV):
            for i in range(self.n):
                self._copy(refs, i, d, True).wait_recv()
        for d in range(1, NDEV):
            for i in range(self.n):
                self._copy(refs, i, d, False).wait_send()
        for i in range(self.n):
            self._local(refs, i).wait()


class _Both:
    def __init__(self, a, b):
        self.parts = (a, b)
        self.arrays = a.arrays + b.arrays
        self.n = a.n + b.n
        self.mid_frac = a.mid_frac if a.mid_frac is not None else b.mid_frac
        self.out_shape = a.out_shape + b.out_shape
        self.scratch = a.scratch + b.scratch

    def _each(self, refs):
        ins, outs, sems = refs
        na, ns = self.parts[0].n, len(self.parts[0].scratch)
        return ((self.parts[0], (ins[:na], outs[:na], sems[:ns])), (self.parts[1], (ins[na:], outs[na:], sems[ns:])))

    def start(self, refs):
        for part, r in self._each(refs):
            part.start(r)

    def forward(self, refs):
        for part, r in self._each(refs):
            part.forward(r)

    def finish(self, refs):
        for part, r in self._each(refs):
            part.finish(r)


def _call(body, name, grid, in_specs, out_specs, out_shape, scratch, args, comm=None):
    params = _params(*(["arbitrary"] * len(grid)))
    if comm is None:
        outs = pl.pallas_call(
            body, name=name, grid=grid, in_specs=in_specs, out_specs=out_specs, out_shape=out_shape,
            scratch_shapes=scratch, compiler_params=params)(*args)
        return list(outs), []
    n_in, n_out, n_scr, k = len(in_specs), len(out_specs), len(scratch), comm.n
    nsteps = math.prod(grid) if grid else 1
    mid = min(nsteps - 1, int(nsteps * comm.mid_frac)) if comm.mid_frac is not None else None

    def hosted(*refs):
        ins, refs = refs[:n_in], refs[n_in:]
        cins, refs = refs[:k], refs[k:]
        outs, refs = refs[:n_out], refs[n_out:]
        couts, refs = refs[:k], refs[k:]
        scr, sems = refs[:n_scr], refs[n_scr:]
        crefs = (cins, couts, sems)
        if nsteps == 1:
            comm.start(crefs)
            body(*ins, *outs, *scr)
            comm.forward(crefs)
            comm.finish(crefs)
            return
        step = pl.program_id(0)
        for a in range(1, len(grid)):
            step = step * grid[a] + pl.program_id(a)
        pl.when(step == 0)(lambda: comm.start(crefs))
        if mid is not None:
            pl.when(step == mid)(lambda: comm.forward(crefs))
        body(*ins, *outs, *scr)
        pl.when(step == nsteps - 1)(lambda: comm.finish(crefs))

    any_spec = pl.BlockSpec(memory_space=pl.ANY)
    outs = pl.pallas_call(
        hosted, name=name, grid=grid,
        in_specs=list(in_specs) + [any_spec] * k, out_specs=list(out_specs) + [any_spec] * k,
        out_shape=list(out_shape) + comm.out_shape, scratch_shapes=list(scratch) + comm.scratch,
        compiler_params=params)(*args, *comm.arrays)
    return list(outs[:n_out]), list(outs[n_out:])


def _exchange(comm, name):
    return _call(lambda: None, name, (), [], [], [], [], [], comm)[1]


def _rms_matmul(x, g, w, name, comm=None):
    T = x.shape[0]
    nb = w.shape[2]
    tT = _tile(T, 1024)

    def body(x_ref, g_ref, w_ref, z_ref, h_ref):
        @pl.when(pl.program_id(1) == 0)
        def _():
            xf = x_ref[...]
            r = lax.rsqrt(jnp.mean(xf * xf, axis=-1, keepdims=True) + RMS_EPS)
            h_ref[...] = (xf * r * g_ref[...]).astype(h_ref.dtype)

        z_ref[...] = jnp.dot(h_ref[...], w_ref[...], preferred_element_type=F32).astype(z_ref.dtype)

    return _call(
        body, name, (T // tT, NDEV),
        [pl.BlockSpec((tT, D), lambda t, n: (t, 0)),
         pl.BlockSpec((1, D), lambda t, n: (0, 0)),
         pl.BlockSpec((None, D, nb), lambda t, n: (n, 0, 0))],
        [pl.BlockSpec((tT, nb), lambda t, n: (t, n)),
         pl.BlockSpec((tT, D), lambda t, n: (t, 0))],
        [jax.ShapeDtypeStruct((T, NDEV * nb), ACT_DTYPE), jax.ShapeDtypeStruct((T, D), MXU_DTYPE)],
        [], (x, g.reshape(1, D), w), comm)


def _rms_matmul_bwd(dz, w, x, g, dxo, name, comm=None):
    T = x.shape[0]
    nb = w.shape[2]
    tT = _tile(T, 1024)

    def body(dz_ref, w_ref, x_ref, g_ref, dxo_ref, dx_ref, dg_ref, acc):
        t, n = pl.program_id(0), pl.program_id(1)

        @pl.when(n == 0)
        def _():
            acc[...] = jnp.zeros_like(acc)

        @pl.when((n == 0) & (t == 0))
        def _():
            dg_ref[...] = jnp.zeros_like(dg_ref)

        acc[...] += lax.dot_general(_mx(dz_ref[...]), w_ref[...], NT_DIMS, preferred_element_type=F32)

        @pl.when(n == NDEV - 1)
        def _():
            xf = x_ref[...]
            r = lax.rsqrt(jnp.mean(xf * xf, axis=-1, keepdims=True) + RMS_EPS)
            xh = xf * r
            dh = acc[...]
            dxh = dh * g_ref[...]
            dx_ref[...] = dxo_ref[...] + r * (dxh - xh * jnp.mean(dxh * xh, axis=-1, keepdims=True))
            dg_ref[...] += _rowsum(dh * xh)

    return _call(
        body, name, (T // tT, NDEV),
        [pl.BlockSpec((tT, nb), lambda t, n: (t, n)),
         pl.BlockSpec((None, D, nb), lambda t, n: (n, 0, 0)),
         pl.BlockSpec((tT, D), lambda t, n: (t, 0)),
         pl.BlockSpec((1, D), lambda t, n: (0, 0)),
         pl.BlockSpec((tT, D), lambda t, n: (t, 0))],
        [pl.BlockSpec((tT, D), lambda t, n: (t, 0)),
         pl.BlockSpec((1, D), lambda t, n: (0, 0))],
        [jax.ShapeDtypeStruct((T, D), F32), jax.ShapeDtypeStruct((1, D), F32)],
        [pltpu.VMEM((tT, D), F32)], (dz, w, x, g.reshape(1, D), dxo), comm)


def _atb(a, ai, b, bi, nblk, name, comm=None):
    T, M = a.shape[1:]
    N = b.shape[2]
    nb = N // nblk
    tk = _tile(T, 512)
    nk = T // tk

    def body(a_ref, b_ref, o_ref, acc):
        k = pl.program_id(1)

        @pl.when(k == 0)
        def _():
            acc[...] = jnp.zeros_like(acc)

        acc[...] += lax.dot_general(_mx(a_ref[...]), _mx(b_ref[...]), TN_DIMS, preferred_element_type=F32)

        @pl.when(k == nk - 1)
        def _():
            o_ref[...] = acc[...].astype(o_ref.dtype)

    outs, couts = _call(
        body, name, (nblk, nk),
        [pl.BlockSpec((None, tk, M), lambda n, k: (ai, k, 0)),
         pl.BlockSpec((None, tk, nb), lambda n, k: (bi, k, n))],
        [pl.BlockSpec((None, M, nb), lambda n, k: (n, 0, 0))],
        [jax.ShapeDtypeStruct((nblk, M, nb), GRAD_DTYPE)],
        [pltpu.VMEM((M, nb), F32)], (a, b), comm)
    return outs[0], couts


def _loss_head(x, tgt, g, name):
    T = x.shape[0]
    tT = _tile(T, 512)

    def body(x_ref, t_ref, g_ref, loss_ref, dx_ref, dg_ref):
        @pl.when(pl.program_id(0) == 0)
        def _():
            loss_ref[...] = jnp.zeros_like(loss_ref)
            dg_ref[...] = jnp.zeros_like(dg_ref)

        xf = x_ref[...]
        r = lax.rsqrt(jnp.mean(xf * xf, axis=-1, keepdims=True) + RMS_EPS)
        xh = xf * r
        err = xh * g_ref[...] - t_ref[...]
        loss_ref[...] += 0.5 * jnp.sum(jnp.mean(err * err, axis=-1, keepdims=True), axis=0, keepdims=True)
        dy = err * (1.0 / D)
        dxh = dy * g_ref[...]
        dx_ref[...] = r * (dxh - xh * jnp.mean(dxh * xh, axis=-1, keepdims=True))
        dg_ref[...] += _rowsum(dy * xh)

    return pl.pallas_call(
        body, name=name, grid=(T // tT,),
        in_specs=[pl.BlockSpec((tT, D), lambda t: (t, 0)),
                  pl.BlockSpec((tT, D), lambda t: (t, 0)),
                  pl.BlockSpec((1, D), lambda t: (0, 0))],
        out_specs=[pl.BlockSpec((1, 1), lambda t: (0, 0)),
                   pl.BlockSpec((tT, D), lambda t: (t, 0)),
                   pl.BlockSpec((1, D), lambda t: (0, 0))],
        out_shape=[jax.ShapeDtypeStruct((1, 1), F32), jax.ShapeDtypeStruct((T, D), F32),
                   jax.ShapeDtypeStruct((1, D), F32)],
        compiler_params=_params("arbitrary"),
    )(x, tgt, g.reshape(1, D))


def _spatial_gate(wm_ref, bst_ref, vb_ref, sv_ref, n_chunks):
    for c in range(n_chunks):
        rows = slice(c * CHUNK, (c + 1) * CHUNK)
        for g in range(N_GROUPS):
            cols = slice(g * CHUNK, (g + 1) * CHUNK)
            sv_ref[rows, cols] = (jnp.dot(wm_ref[g], vb_ref[rows, cols], preferred_element_type=F32)
                                  + bst_ref[:, g:g + 1])


def _lane_loop(fn):
    def step(i, carry):
        fn(pl.ds(pl.multiple_of(i * LANES, LANES), LANES))
        return carry

    lax.fori_loop(0, D // LANES, step, 0)


def _shifted_copies(buf, sh, n):
    for s in range(1, 8):
        sh[s - 1, 0:n, :] = buf[s:s + n, :]


def _window(buf, sh, base, off, lanes):
    a, s = divmod(off, 8)
    src = buf if s == 0 else sh.at[s - 1]
    return src[base + 8 * a:base + 8 * a + 8, lanes]


def _softmax_rows(s):
    e = jnp.exp(s - jnp.max(s, axis=-1, keepdims=True))
    return e / jnp.sum(e, axis=-1, keepdims=True)


def _branch_fwd(z, kv, wm, bst, ln_a, cw, cvec, name):
    T = z.shape[0]
    tT = _tile(T, 256)
    n_chunks = tT // CHUNK

    def body(z_ref, kv_ref, wm_ref, bst_ref, lna_ref, cw_ref, cvec_ref, br_ref, c_ref, gbuf, gsh, vb, ua):
        @pl.when(pl.program_id(0) == 0)
        def _():
            gbuf[0:HALO, :] = jnp.zeros((HALO, D), F32)

        def seg(s):
            return z_ref[:, s * D:(s + 1) * D].astype(F32)

        u, _ = _gelu(seg(SEG_AU))
        zg = seg(SEG_AG)
        ua[...] = u * (zg * jax.nn.sigmoid(zg))
        gv, _ = _gelu(seg(SEG_AV))
        vhat, _ = _ln_stats(gv)
        vb[...] = _mx(vhat * lna_ref[0:1, :] + lna_ref[1:2, :])
        for c in range(n_chunks):
            rows = slice(c * CHUNK, (c + 1) * CHUNK)
            for g in range(N_GROUPS):
                cols = slice(g * CHUNK, (g + 1) * CHUNK)
                sv = jnp.dot(wm_ref[g], vb[rows, cols], preferred_element_type=F32) + bst_ref[:, g:g + 1]
                br_ref[0, rows, cols] = (sv * ua[rows, cols]).astype(br_ref.dtype)

        gbuf[HALO:HALO + tT, :] = seg(SEG_BA) * jax.nn.sigmoid(seg(SEG_BB))
        _shifted_copies(gbuf, gsh, tT + HALO - 8)
        def conv_lanes(lanes):
            taps = [jnp.broadcast_to(cw_ref[k:k + 1, lanes], (8, LANES)) for k in range(CONV_K)]
            bias = jnp.broadcast_to(cvec_ref[0:1, lanes], (8, LANES))
            for base in range(0, tT, 8):
                acc = [bias, None, None, None]
                for k in range(CONV_K):
                    term = taps[k] * _window(gbuf, gsh, base, k + HALO - CONV_K + 1, lanes)
                    acc[k % 4] = term if acc[k % 4] is None else acc[k % 4] + term
                c_ref[base:base + 8, lanes] = (acc[0] + acc[1]) + (acc[2] + acc[3])

        _lane_loop(conv_lanes)
        gbuf[0:HALO, :] = gbuf[tT:tT + HALO, :]
        chat, _ = _ln_stats(c_ref[...])
        cl = chat * cvec_ref[1:2, :] + cvec_ref[2:3, :]
        zg = seg(SEG_BG)
        br_ref[1] = (cl * jax.nn.sigmoid(cl) * (zg * jax.nn.sigmoid(zg))).astype(br_ref.dtype)

        for h in range(HEADS):
            cols = slice(h * HEAD_DIM, (h + 1) * HEAD_DIM)
            q = _mx(z_ref[:, SEG_CQ * D + h * HEAD_DIM:SEG_CQ * D + (h + 1) * HEAD_DIM])
            s = lax.dot_general(q, kv_ref[:, cols], NT_DIMS, preferred_element_type=F32)
            p = _softmax_rows(s * (1.0 / math.sqrt(HEAD_DIM)))
            att = jnp.dot(_mx(p), kv_ref[:, D + h * HEAD_DIM:D + (h + 1) * HEAD_DIM], preferred_element_type=F32)
            zg = z_ref[:, SEG_CG * D + h * HEAD_DIM:SEG_CG * D + (h + 1) * HEAD_DIM].astype(F32)
            br_ref[2, :, cols] = (att * (zg * jax.nn.sigmoid(zg))).astype(br_ref.dtype)

    full = lambda shape: pl.BlockSpec(shape, lambda t: (0,) * len(shape))
    return pl.pallas_call(
        body, name=name, grid=(T // tT,),
        in_specs=[pl.BlockSpec((tT, SEG_M * D), lambda t: (t, 0)),
                  full(kv.shape), full(wm.shape), full(bst.shape), full(ln_a.shape), full(cw.shape),
                  full(cvec.shape)],
        out_specs=[pl.BlockSpec((3, tT, D), lambda t: (0, t, 0)),
                   pl.BlockSpec((tT, D), lambda t: (t, 0))],
        out_shape=[jax.ShapeDtypeStruct((3, T, D), MXU_DTYPE), jax.ShapeDtypeStruct((T, D), F32)],
        scratch_shapes=[pltpu.VMEM((tT + HALO, D), F32), pltpu.VMEM((7, tT + HALO - 8, D), F32),
                        pltpu.VMEM((tT, D), MXU_DTYPE), pltpu.VMEM((tT, D), F32)],
        compiler_params=_params("arbitrary"),
    )(z, kv, wm, bst, ln_a, cw, cvec)


def _branch_bwd(z, dbr, c, dzm, kv, wm, wmt, bst, ln_a, cw, cvec, name, comm=None):
    T = z.shape[0]
    M = kv.shape[0]
    tT = _tile(T, 128)
    nT = T // tT
    n_chunks = tT // CHUNK
    halo_blocks = tT // HALO

    def body(z_ref, zha_ref, zhb_ref, dbr_ref, c_ref, dzm_ref, kv_ref, wm_ref, wmt_ref, bst_ref, lna_ref,
             cw_ref, cvec_ref, dz_ref, vecg_ref, dbst_ref, dws_ref, dcw_ref, dkv_ref,
             gbuf, dcbuf, vb, dsvb, sv, dvbuf, gsh, dcsh, dglu, vh, gq):
        i = pl.program_id(0)

        @pl.when(i == 0)
        def _():
            vecg_ref[...] = jnp.zeros_like(vecg_ref)
            dbst_ref[...] = jnp.zeros_like(dbst_ref)
            dws_ref[...] = jnp.zeros_like(dws_ref)
            dcw_ref[...] = jnp.zeros_like(dcw_ref)
            dkv_ref[...] = jnp.zeros_like(dkv_ref)
            dcbuf[tT:tT + HALO, :] = jnp.zeros((HALO, D), F32)

        strips = [slice(r0, r0 + STRIP) for r0 in range(0, tT, STRIP)]

        def seg(r, s):
            return z_ref[r, s * D:(s + 1) * D].astype(F32)

        def put(r, s, val):
            dz_ref[r, s * D:(s + 1) * D] = val.astype(dz_ref.dtype)

        for r in strips:
            zv = seg(r, SEG_AV)
            gv, tv = _gelu(zv)
            vhat, rstd = _ln_stats(gv)
            vb[r, :] = _mx(vhat * lna_ref[0:1, :] + lna_ref[1:2, :])
            vh[r, :] = vhat
            gq[r, :] = rstd * _gelu_grad(zv, tv)
        _spatial_gate(wm_ref, bst_ref, vb, sv, n_chunks)
        for r in strips:
            zu, zg = seg(r, SEG_AU), seg(r, SEG_AG)
            u, tu = _gelu(zu)
            sg = jax.nn.sigmoid(zg)
            d_a = dbr_ref[0, r, :].astype(F32)
            put(r, SEG_AU, d_a * sv[r, :] * (zg * sg) * _gelu_grad(zu, tu))
            put(r, SEG_AG, d_a * u * sv[r, :] * _silu_grad(zg, sg))
            dsv = d_a * u * (zg * sg)
            dsvb[r, :] = _mx(dsv)
            in_chunk = slice(r.start % CHUNK, r.start % CHUNK + STRIP)
            for g in range(N_GROUPS):
                dbst_ref[in_chunk, g:g + 1] += jnp.sum(dsv[:, g * CHUNK:(g + 1) * CHUNK], axis=-1, keepdims=True)
        tril = (lax.broadcasted_iota(jnp.int32, (CHUNK, CHUNK), 0)
                >= lax.broadcasted_iota(jnp.int32, (CHUNK, CHUNK), 1))
        for g in range(N_GROUPS):
            cols = slice(g * CHUNK, (g + 1) * CHUNK)
            for cc in range(n_chunks):
                rows = slice(cc * CHUNK, (cc + 1) * CHUNK)
                dws = lax.dot_general(dsvb[rows, cols], vb[rows, cols], NT_DIMS, preferred_element_type=F32)
                dws_ref[g] += jnp.where(tril, dws, 0.0)
                dvbuf[rows, cols] = jnp.dot(wmt_ref[g], dsvb[rows, cols], preferred_element_type=F32)
        for r in strips:
            dv, vhat = dvbuf[r, :], vh[r, :]
            vecg_ref[0:1, :] += _rowsum(dv * vhat)
            vecg_ref[1:2, :] += _rowsum(dv)
            dvh = dv * lna_ref[0:1, :]
            put(r, SEG_AV, (dvh - jnp.mean(dvh, axis=-1, keepdims=True)
                            - vhat * jnp.mean(dvh * vhat, axis=-1, keepdims=True)) * gq[r, :])

        sgb_buf = sv
        halo = zha_ref[...].astype(F32) * jax.nn.sigmoid(zhb_ref[...].astype(F32))
        gbuf[0:HALO, :] = jnp.where(i < nT - 1, halo, 0.0)
        for r in strips:
            za, zg = seg(r, SEG_BA), seg(r, SEG_BG)
            sgb = jax.nn.sigmoid(seg(r, SEG_BB))
            sgb_buf[r, :] = sgb
            gbuf[HALO + r.start:HALO + r.stop, :] = za * sgb
            chat, crstd = _ln_stats(c_ref[r, :])
            cl = chat * cvec_ref[1:2, :] + cvec_ref[2:3, :]
            scl = jax.nn.sigmoid(cl)
            sg = jax.nn.sigmoid(zg)
            d_b = dbr_ref[1, r, :].astype(F32)
            put(r, SEG_BG, d_b * (cl * scl) * _silu_grad(zg, sg))
            dcl = d_b * (zg * sg) * _silu_grad(cl, scl)
            vecg_ref[3:4, :] += _rowsum(dcl * chat)
            vecg_ref[4:5, :] += _rowsum(dcl)
            dc = _ln_grad(dcl, cvec_ref[1:2, :], chat, crstd)
            vecg_ref[2:3, :] += _rowsum(dc)
            dcbuf[r, :] = dc
        _shifted_copies(dcbuf, dcsh, tT + HALO - 8)
        _shifted_copies(gbuf, gsh, tT + HALO - 8)

        def conv_input_grad(lanes):
            taps = [jnp.broadcast_to(cw_ref[k:k + 1, lanes], (8, LANES)) for k in range(CONV_K)]
            for base in range(0, tT, 8):
                acc = [None] * 4
                for k in range(CONV_K):
                    term = taps[k] * _window(dcbuf, dcsh, base, CONV_K - 1 - k, lanes)
                    acc[k % 4] = term if acc[k % 4] is None else acc[k % 4] + term
                dglu[base:base + 8, lanes] = (acc[0] + acc[1]) + (acc[2] + acc[3])

        def conv_weight_grad(lanes):
            wsum = [None] * CONV_K
            for base in range(0, tT, 8):
                dcv = dcbuf[base:base + 8, lanes]
                for k in range(CONV_K):
                    term = dcv * _window(gbuf, gsh, base, k + HALO - CONV_K + 1, lanes)
                    wsum[k] = term if wsum[k] is None else wsum[k] + term
            for k in range(CONV_K):
                dcw_ref[k:k + 1, lanes] += _rowsum(wsum[k])

        _lane_loop(conv_input_grad)
        _lane_loop(conv_weight_grad)
        dcbuf[tT:tT + HALO, :] = dcbuf[0:HALO, :]
        for r in strips:
            dg, sgb = dglu[r, :], sgb_buf[r, :]
            put(r, SEG_BA, dg * sgb)
            put(r, SEG_BB, dg * seg(r, SEG_BA) * sgb * (1.0 - sgb))

        scale = 1.0 / math.sqrt(HEAD_DIM)
        for h in range(HEADS):
            cols = slice(h * HEAD_DIM, (h + 1) * HEAD_DIM)
            qcols = slice(SEG_CQ * D + h * HEAD_DIM, SEG_CQ * D + (h + 1) * HEAD_DIM)
            gcols = slice(SEG_CG * D + h * HEAD_DIM, SEG_CG * D + (h + 1) * HEAD_DIM)
            vcols = slice(D + h * HEAD_DIM, D + (h + 1) * HEAD_DIM)
            q = _mx(z_ref[:, qcols])
            kh, vh = kv_ref[:, cols], kv_ref[:, vcols]
            p = _softmax_rows(lax.dot_general(q, kh, NT_DIMS, preferred_element_type=F32) * scale)
            pb = _mx(p)
            att = jnp.dot(pb, vh, preferred_element_type=F32)
            zg = z_ref[:, gcols].astype(F32)
            sg = jax.nn.sigmoid(zg)
            d_c = dbr_ref[2, :, cols].astype(F32)
            dz_ref[:, gcols] = (d_c * att * _silu_grad(zg, sg)).astype(dz_ref.dtype)
            datt = _mx(d_c * (zg * sg))
            dp = lax.dot_general(datt, vh, NT_DIMS, preferred_element_type=F32)
            dkv_ref[:, vcols] += lax.dot_general(pb, datt, TN_DIMS, preferred_element_type=F32)
            ds = _mx(p * (dp - jnp.sum(dp * p, axis=-1, keepdims=True)) * scale)
            dz_ref[:, qcols] = jnp.dot(ds, kh, preferred_element_type=F32).astype(dz_ref.dtype)
            dkv_ref[:, cols] += lax.dot_general(ds, q, TN_DIMS, preferred_element_type=F32)

        dz_ref[:, SEG_M * D:] = dzm_ref[...].astype(dz_ref.dtype)

    rev = lambda i: nT - 1 - i
    halo_row = lambda i: jnp.maximum(rev(i) * halo_blocks - 1, 0)
    full = lambda shape: pl.BlockSpec(shape, lambda i: (0,) * len(shape))
    return _call(
        body, name, (nT,),
        [pl.BlockSpec((tT, SEG_M * D), lambda i: (rev(i), 0)),
         pl.BlockSpec((HALO, D), lambda i: (halo_row(i), SEG_BA)),
         pl.BlockSpec((HALO, D), lambda i: (halo_row(i), SEG_BB)),
         pl.BlockSpec((3, tT, D), lambda i: (0, rev(i), 0)),
         pl.BlockSpec((tT, D), lambda i: (rev(i), 0)),
         pl.BlockSpec((tT, 3 * D), lambda i: (rev(i), 0)),
         full(kv.shape), full(wm.shape), full(wmt.shape), full(bst.shape), full(ln_a.shape),
         full(cw.shape), full(cvec.shape)],
        [pl.BlockSpec((tT, N_IN), lambda i: (rev(i), 0)),
         full((8, D)), full((CHUNK, N_GROUPS)), full((N_GROUPS, CHUNK, CHUNK)), full((HALO, D)),
         full((M, 2 * D))],
        [jax.ShapeDtypeStruct((T, N_IN), MXU_DTYPE), jax.ShapeDtypeStruct((8, D), F32),
         jax.ShapeDtypeStruct((CHUNK, N_GROUPS), F32),
         jax.ShapeDtypeStruct((N_GROUPS, CHUNK, CHUNK), F32),
         jax.ShapeDtypeStruct((HALO, D), F32), jax.ShapeDtypeStruct((M, 2 * D), F32)],
        [pltpu.VMEM((tT + HALO, D), F32), pltpu.VMEM((tT + HALO, D), F32),
         pltpu.VMEM((tT, D), MXU_DTYPE), pltpu.VMEM((tT, D), MXU_DTYPE),
         pltpu.VMEM((tT, D), F32), pltpu.VMEM((tT, D), F32),
         pltpu.VMEM((7, tT + HALO - 8, D), F32), pltpu.VMEM((7, tT + HALO - 8, D), F32),
         pltpu.VMEM((tT, D), F32), pltpu.VMEM((tT, D), F32), pltpu.VMEM((tT, D), F32)],
        (z, z, z, dbr, c, dzm, kv, wm, wmt, bst, ln_a, cw, cvec), comm)


def _merge_fwd(br, z, x, wb, wo, name):
    T = x.shape[0]
    tT = _tile(T, 512)

    def body(br_ref, z0, z1, z2, x_ref, wb_ref, wo_ref, xn_ref, mg_ref, pj_ref):
        merged = jnp.zeros((tT, D), F32)
        for n, zm in enumerate((z0, z1, z2)):
            proj = jnp.dot(br_ref[n], wb_ref[:, n].reshape(D, D), preferred_element_type=F32)
            pj_ref[n] = proj.astype(pj_ref.dtype)
            merged = merged + jax.nn.sigmoid(zm[...].astype(F32)) * proj
        mg_ref[...] = merged.astype(mg_ref.dtype)
        xn_ref[...] = x_ref[...] + jnp.dot(_mx(merged), wo_ref[...].reshape(D, D), preferred_element_type=F32)

    zspec = lambda n: pl.BlockSpec((tT, D), lambda t: (t, SEG_M + n))
    return pl.pallas_call(
        body, name=name, grid=(T // tT,),
        in_specs=[pl.BlockSpec((3, tT, D), lambda t: (0, t, 0)), zspec(0), zspec(1), zspec(2),
                  pl.BlockSpec((tT, D), lambda t: (t, 0)),
                  pl.BlockSpec(wb.shape, lambda t: (0, 0, 0, 0)),
                  pl.BlockSpec(wo.shape, lambda t: (0, 0, 0))],
        out_specs=[pl.BlockSpec((tT, D), lambda t: (t, 0)),
                   pl.BlockSpec((tT, D), lambda t: (t, 0)),
                   pl.BlockSpec((3, tT, D), lambda t: (0, t, 0))],
        out_shape=[jax.ShapeDtypeStruct((T, D), F32), jax.ShapeDtypeStruct((T, D), MXU_DTYPE),
                   jax.ShapeDtypeStruct((3, T, D), ACT_DTYPE)],
        compiler_params=_params("parallel"),
    )(br, z, z, z, x, wb, wo)


def _merge_bwd(dxo, proj, z, wb, wo, name):
    T = dxo.shape[0]
    tT = _tile(T, 512)

    def body(dxo_ref, pj_ref, z0, z1, z2, wb_ref, wo_ref, dpj_ref, dbr_ref, dzm_ref):
        dmerged = lax.dot_general(_mx(dxo_ref[...]), wo_ref[...].reshape(D, D), NT_DIMS,
                                  preferred_element_type=F32)
        for n, zm in enumerate((z0, z1, z2)):
            gate = jax.nn.sigmoid(zm[...].astype(F32))
            dproj = _mx(gate * dmerged)
            dpj_ref[n] = dproj
            dzm_ref[:, n * D:(n + 1) * D] = (pj_ref[n].astype(F32) * dmerged * gate * (1.0 - gate)
                                             ).astype(dzm_ref.dtype)
            dbr_ref[n] = lax.dot_general(dproj, wb_ref[:, n].reshape(D, D), NT_DIMS,
                                         preferred_element_type=F32).astype(dbr_ref.dtype)

    zspec = lambda n: pl.BlockSpec((tT, D), lambda t: (t, SEG_M + n))
    return pl.pallas_call(
        body, name=name, grid=(T // tT,),
        in_specs=[pl.BlockSpec((tT, D), lambda t: (t, 0)),
                  pl.BlockSpec((3, tT, D), lambda t: (0, t, 0)), zspec(0), zspec(1), zspec(2),
                  pl.BlockSpec(wb.shape, lambda t: (0, 0, 0, 0)),
                  pl.BlockSpec(wo.shape, lambda t: (0, 0, 0))],
        out_specs=[pl.BlockSpec((3, tT, D), lambda t: (0, t, 0)),
                   pl.BlockSpec((3, tT, D), lambda t: (0, t, 0)),
                   pl.BlockSpec((tT, 3 * D), lambda t: (t, 0))],
        out_shape=[jax.ShapeDtypeStruct((3, T, D), MXU_DTYPE), jax.ShapeDtypeStruct((3, T, D), F32),
                   jax.ShapeDtypeStruct((T, 3 * D), MXU_DTYPE)],
        compiler_params=_params("parallel"),
    )(dxo, proj, z, z, z, wb, wo)


def _adamw(parts, w, m, v, idx, name, comm=None):
    R, C = parts.shape[1:]
    tr = 128 if R % 128 == 0 else R
    c1 = 1.0 / (1.0 - ADAM_B1 ** ADAM_STEP)
    c2 = 1.0 / (1.0 - ADAM_B2 ** ADAM_STEP)

    def body(p_ref, w_ref, m_ref, v_ref, g_out, d_out, m_out, v_out):
        g = p_ref[0].astype(F32)
        for p in range(1, NDEV):
            g = g + p_ref[p].astype(F32)
        mn = ADAM_B1 * m_ref[...] + (1.0 - ADAM_B1) * g
        vn = ADAM_B2 * v_ref[...] + (1.0 - ADAM_B2) * (g * g)
        g_out[...] = g
        m_out[...] = mn
        v_out[...] = vn
        d_out[...] = -ADAM_LR * ((mn * c1) / (jnp.sqrt(vn * c2) + ADAM_EPS) + ADAM_WD * w_ref[...])

    wspec = pl.BlockSpec((None, tr, C), lambda r: (idx, r, 0))
    ospec = pl.BlockSpec((tr, C), lambda r: (r, 0))
    return _call(
        body, name, (R // tr,),
        [pl.BlockSpec((NDEV, tr, C), lambda r: (0, r, 0)), wspec, wspec, wspec],
        [ospec] * 4, [jax.ShapeDtypeStruct((R, C), F32)] * 4, [], (parts, w, m, v), comm)


def kernel(x, mem, norm_g, mem_norm_g, w_in, gmlp_ln_g, gmlp_ln_b, w_s, b_s, conv_w, conv_b, conv_ln_g, conv_ln_b, w_kv, w_branch, w_out, final_norm_g, loss_target, m_norm_g, m_mem_norm_g, m_w_in, m_gmlp_ln_g, m_gmlp_ln_b, m_w_s, m_b_s, m_conv_w, m_conv_b, m_conv_ln_g, m_conv_ln_b, m_w_kv, m_w_branch, m_w_out, m_final_norm_g, v_norm_g, v_mem_norm_g, v_w_in, v_gmlp_ln_g, v_gmlp_ln_b, v_w_s, v_b_s, v_conv_w, v_conv_b, v_conv_ln_g, v_conv_ln_b, v_w_kv, v_w_branch, v_w_out, v_final_norm_g):
    L = w_in.shape[0]
    x0, mem0, tgt = x[0], mem[0], loss_target[0]
    T, M = x0.shape[0], mem0.shape[0]
    nbi, nbk, nbc = w_in.shape[2], w_kv.shape[2], conv_w.shape[2]

    first = _exchange(_Gather([_mx(w_in[0])]), "gather_first")
    later = []
    for l in range(L):
        later += [_mx(w_in[l])] * (l > 0) + [_mx(w_kv[l]), _mx(w_branch[l]), _mx(w_out[l]), conv_w[l]]
    gather_later = _Gather(later)

    tril = jnp.tril(jnp.ones((CHUNK, CHUNK), bool))
    wm = [_mx(jnp.where(tril[None], w_s[l], 0.0)) for l in range(L)]
    wmt = [w.transpose(0, 2, 1) for w in wm]
    bst = [b_s[l].T for l in range(L)]
    ln_a = [jnp.stack([gmlp_ln_g[l], gmlp_ln_b[l]]) for l in range(L)]
    cvec = [jnp.stack([conv_b[l], conv_ln_g[l], conv_ln_b[l]]) for l in range(L)]

    win = [first[0]]
    memn, kvs, xs, saved = [], [], [x0], []
    for l in range(L):
        (z, h), full = _rms_matmul(xs[l], norm_g[l], win[l], f"inproj_fwd{l}", gather_later if l == 0 else None)
        if l == 0:
            win += [full[5 * k - 1] for k in range(1, L)]
            wkv = [full[5 * k] for k in range(L)]
            wbr = [full[5 * k + 1] for k in range(L)]
            wou = [full[5 * k + 2] for k in range(L)]
            cwf = [jnp.pad(full[5 * k + 3].transpose(1, 0, 2).reshape(CONV_K, D),
                           ((0, HALO - CONV_K), (0, 0))) for k in range(L)]
        (kv, mn), _ = _rms_matmul(mem0, mem_norm_g[l], wkv[l], f"kv_fwd{l}")
        kvs.append(_mx(kv))
        memn.append(mn)
        br, cpre = _branch_fwd(z, kvs[l], wm[l], bst[l], ln_a[l], cwf[l], cvec[l], f"branch_fwd{l}")
        xn, merged, proj = _merge_fwd(br, z, xs[l], wbr[l], wou[l], f"merge_fwd{l}")
        xs.append(xn)
        saved.append((z, h, br, cpre, merged, proj))
    loss_part, dx, dfg = _loss_head(xs[L], tgt, final_norm_g, "loss_head")

    pending, recv = [("final_norm_g", dfg, True)], {}

    def flush():
        scat = [(k, a) for k, a, g in pending if not g]
        gath = [(k, a) for k, a, g in pending if g]
        pending.clear()
        comms = ([_Scatter([a for _, a in scat])] if scat else []) + ([_Gather([a for _, a in gath])] if gath else [])
        return [k for k, _ in scat + gath], comms[0] if len(comms) == 1 else _Both(*comms)

    def landed(keys, arrays):
        recv.update(zip(keys, arrays))

    for l in reversed(range(L)):
        z, h, br, cpre, merged, proj = saved[l]
        dproj, dbr, dzm = _merge_bwd(dx, proj, z, wbr[l], wou[l], f"merge_bwd{l}")
        for n in range(3):
            dwb, _ = _atb(br, n, dproj, n, 1, f"dwbranch{l}_{n}")
            pending.append((f"w_branch{l}_{n}", dwb.reshape(NDEV, D // NDEV, D), False))
        dwo, _ = _atb(merged[None], 0, dx[None], 0, 1, f"dwout{l}")
        pending.append((f"w_out{l}", dwo.reshape(NDEV, D // NDEV, D), False))
        keys, comm = flush() if l == 0 else (None, None)
        (dz, vecg, dbst, dws, dcw, dkv), got = _branch_bwd(
            z, dbr, cpre, dzm, kvs[l], wm[l], wmt[l], bst[l], ln_a[l], cwf[l], cvec[l], f"branch_bwd{l}", comm)
        if l == 0:
            landed(keys, got)
        dwk, _ = _atb(memn[l][None], 0, dkv[None], 0, NDEV, f"dwkv{l}")
        (_, dmg), _ = _rms_matmul_bwd(dkv, wkv[l], mem0, mem_norm_g[l], jnp.zeros((M, D), F32), f"kv_bwd{l}")
        rest = jnp.concatenate([dmg, vecg[0:2], vecg[2:5], dbst.T.reshape(1, D)], axis=0)
        pending += [(f"w_kv{l}", dwk, False),
                    (f"conv_w{l}", dcw[:CONV_K].reshape(CONV_K, NDEV, nbc).transpose(1, 0, 2), False),
                    (f"small{l}", rest, True), (f"w_s{l}", dws.reshape(N_GROUPS * CHUNK, CHUNK), True)]
        keys, comm = flush() if l == 0 else (None, None)
        dwi, got = _atb(h[None], 0, dz[None], 0, NDEV, f"dwin{l}", comm)
        if l == 0:
            landed(keys, got)
        pending.append((f"w_in{l}", dwi, False))
        keys, comm = flush() if l == 0 else (None, None)
        (dx, dng), got = _rms_matmul_bwd(dz, win[l], xs[l], norm_g[l], dx, f"inproj_bwd{l}", comm)
        if l == 0:
            landed(keys, got)
        pending.append((f"norm_g{l}", dng, True))
    grad_x = dx[None]

    def pack(p):
        rows = []
        for l in range(L):
            rows += [p["norm_g"][l], p["mem_norm_g"][l], p["gmlp_ln_g"][l], p["gmlp_ln_b"][l], p["conv_b"][l],
                     p["conv_ln_g"][l], p["conv_ln_b"][l], p["b_s"][l].reshape(D)]
        return jnp.stack(rows + [p["final_norm_g"]])[None]

    names = ["norm_g", "mem_norm_g", "gmlp_ln_g", "gmlp_ln_b", "conv_b", "conv_ln_g", "conv_ln_b", "b_s",
             "final_norm_g"]
    w_small = pack(dict(zip(names, [norm_g, mem_norm_g, gmlp_ln_g, gmlp_ln_b, conv_b, conv_ln_g, conv_ln_b,
                                    b_s, final_norm_g])))
    m_small = pack(dict(zip(names, [m_norm_g, m_mem_norm_g, m_gmlp_ln_g, m_gmlp_ln_b, m_conv_b, m_conv_ln_g,
                                    m_conv_ln_b, m_b_s, m_final_norm_g])))
    v_small = pack(dict(zip(names, [v_norm_g, v_mem_norm_g, v_gmlp_ln_g, v_gmlp_ln_b, v_conv_b, v_conv_ln_g,
                                    v_conv_ln_b, v_b_s, v_final_norm_g])))
    outs = {}

    def run(key, parts, w, m, v, idx, comm=None):
        outs[key], got = _adamw(parts, w, m, v, idx, "adamw_" + key, comm)
        return got

    keys, comm = flush()
    landed(keys, run("w_in0", recv["w_in0"], w_in, m_w_in, v_w_in, 0, comm))
    parts_small = jnp.concatenate([recv[f"{k}{l}"] for l in range(L) for k in ("norm_g", "small")]
                                  + [recv["final_norm_g"]], axis=1)
    parts_ws = jnp.concatenate([recv[f"w_s{l}"] for l in range(L)], axis=1)
    for l in range(L):
        if l > 0:
            run(f"w_in{l}", recv[f"w_in{l}"], w_in, m_w_in, v_w_in, l)
        run(f"w_kv{l}", recv[f"w_kv{l}"], w_kv, m_w_kv, v_w_kv, l)
        for n in range(3):
            sh = (L * 3, D // NDEV, D)
            run(f"w_branch{l}_{n}", recv[f"w_branch{l}_{n}"], w_branch.reshape(sh), m_w_branch.reshape(sh),
                v_w_branch.reshape(sh), l * 3 + n)
        run(f"w_out{l}", recv[f"w_out{l}"], w_out, m_w_out, v_w_out, l)
        run(f"conv_w{l}", recv[f"conv_w{l}"], conv_w, m_conv_w, v_conv_w, l)
    run("small", parts_small, w_small, m_small, v_small, 0)
    ws_shape = (1, L * N_GROUPS * CHUNK, CHUNK)
    run("w_s", parts_ws, w_s.reshape(ws_shape), m_w_s.reshape(ws_shape), v_w_s.reshape(ws_shape), 0)

    def leaf(name, k):
        if name in ("w_in", "w_kv", "w_out", "conv_w"):
            return jnp.stack([outs[f"{name}{l}"][k] for l in range(L)])
        if name == "w_branch":
            return jnp.stack([jnp.stack([outs[f"w_branch{l}_{n}"][k] for n in range(3)]) for l in range(L)])
        if name == "w_s":
            return outs["w_s"][k].reshape(L, N_GROUPS, CHUNK, CHUNK)
        sm = outs["small"][k]
        if name == "final_norm_g":
            return sm[8 * L]
        j = names.index(name)
        rows = jnp.stack([sm[8 * l + j] for l in range(L)])
        return rows.reshape(L, N_GROUPS, CHUNK) if name == "b_s" else rows

    order = ["norm_g", "mem_norm_g", "w_in", "gmlp_ln_g", "gmlp_ln_b", "w_s", "b_s", "conv_w", "conv_b",
             "conv_ln_g", "conv_ln_b", "w_kv", "w_branch", "w_out", "final_norm_g"]
    loss = lax.psum(loss_part[0, 0], ("x", "y", "c"))
    return (loss, grad_x, *[leaf(nm, k) for k in range(4) for nm in order])
```

```python
import functools
import math

import jax
import jax.numpy as jnp
from jax import lax
from jax.experimental import pallas as pl
from jax.experimental.pallas import tpu as pltpu

F32 = jnp.float32
MXU_DTYPE = jnp.bfloat16
ACT_DTYPE = jnp.bfloat16
GRAD_DTYPE = jnp.bfloat16

D = 1024
N_SEG = 11
N_IN = N_SEG * D
NDEV = 8
CHUNK = 128
N_GROUPS = 8
CONV_K = 31
HALO = 32
LANES = 128
STRIP = 32
HEADS = 4
HEAD_DIM = D // HEADS
RMS_EPS = 1e-6
LN_EPS = 1e-5
ADAM_LR, ADAM_B1, ADAM_B2, ADAM_EPS, ADAM_WD, ADAM_STEP = 0.001, 0.9, 0.999, 1e-08, 0.01, 10
SEG_AU, SEG_AV, SEG_AG, SEG_BA, SEG_BB, SEG_BG, SEG_CQ, SEG_CG, SEG_M = 0, 1, 2, 3, 4, 5, 6, 7, 8

VMEM_LIMIT = 60 * 1024 * 1024
MESH = pl.DeviceIdType.MESH
NT_DIMS = (((1,), (1,)), ((), ()))
TN_DIMS = (((0,), (0,)), ((), ()))


def _params(*sem):
    return pltpu.CompilerParams(dimension_semantics=sem, vmem_limit_bytes=VMEM_LIMIT)


def _tile(n, want):
    t = min(n, want)
    assert n % t == 0, (n, want)
    return t


def _mx(v):
    return v.astype(MXU_DTYPE)


def _gelu(x):
    t = jnp.tanh(0.7978845608028654 * (x + 0.044715 * x * x * x))
    return 0.5 * x * (1.0 + t), t


def _gelu_grad(x, t):
    return 0.5 * (1.0 + t) + 0.5 * x * (1.0 - t * t) * 0.7978845608028654 * (1.0 + 3.0 * 0.044715 * x * x)


def _silu_grad(x, s):
    return s * (1.0 + x * (1.0 - s))


def _ln_stats(v):
    mu = jnp.mean(v, axis=-1, keepdims=True)
    vc = v - mu
    rstd = lax.rsqrt(jnp.mean(vc * vc, axis=-1, keepdims=True) + LN_EPS)
    return vc * rstd, rstd


def _ln_grad(dy, g, vhat, rstd):
    dvh = dy * g
    return rstd * (dvh - jnp.mean(dvh, axis=-1, keepdims=True)
                   - vhat * jnp.mean(dvh * vhat, axis=-1, keepdims=True))


def _rowsum(v):
    return jnp.sum(v, axis=0, keepdims=True)


def _coords():
    return lax.axis_index("x"), lax.axis_index("y"), lax.axis_index("c")


def _flip(pos, d):
    x, y, c = pos
    return (1 - x if d & 4 else x, 1 - y if d & 2 else y, 1 - c if d & 1 else c)


def _slot(pos):
    return 4 * pos[0] + 2 * pos[1] + pos[2]


CHIP_FLIPS = (4, 2, 6)


class _Gather:
    def __init__(self, arrays, mid_frac=0.8):
        self.arrays = list(arrays)
        self.n = n = len(arrays)
        self.mid_frac = mid_frac
        self.out_shape = [jax.ShapeDtypeStruct((NDEV,) + a.shape, a.dtype) for a in arrays]
        self.scratch = [pltpu.SemaphoreType.DMA((n, 7)), pltpu.SemaphoreType.DMA((n, 7)),
                        pltpu.SemaphoreType.DMA((n,))]

    def _copy(self, refs, i, k, block, to, own=False):
        ins, outs, (send, recv, _) = refs
        slot = outs[i].at[_slot(block)]
        return pltpu.make_async_remote_copy(
            src_ref=ins[i] if own else slot, dst_ref=slot, send_sem=send.at[i, k], recv_sem=recv.at[i, k],
            device_id=to, device_id_type=MESH)

    def _local(self, refs, i):
        ins, outs, (_, _, loc) = refs
        return pltpu.make_async_copy(ins[i], outs[i].at[_slot(_coords())], loc.at[i])

    def _first(self, refs, i, k):
        me = _coords()
        return self._copy(refs, i, k, me, _flip(me, ((1,) + CHIP_FLIPS)[k]), own=True)

    def _passed(self, refs, i, j):
        me = _coords()
        return self._copy(refs, i, 4 + j, _flip(me, CHIP_FLIPS[j]), _flip(me, 1))

    def start(self, refs):
        for i in range(self.n):
            self._local(refs, i).start()
        for k in range(4):
            for i in range(self.n):
                self._first(refs, i, k).start()

    def forward(self, refs):
        me = _coords()
        for j, d in enumerate(CHIP_FLIPS):
            for i in range(self.n):
                self._copy(refs, i, 1 + j, _flip(me, d), me).wait_recv()
                self._passed(refs, i, j).start()

    def finish(self, refs):
        me = _coords()
        sib = _flip(me, 1)
        for i in range(self.n):
            self._copy(refs, i, 0, sib, me).wait_recv()
        for j, d in enumerate(CHIP_FLIPS):
            for i in range(self.n):
                self._copy(refs, i, 4 + j, _flip(sib, d), me).wait_recv()
        for i in range(self.n):
            for k in range(4):
                self._first(refs, i, k).wait_send()
            for j in range(3):
                self._passed(refs, i, j).wait_send()
            self._local(refs, i).wait()


class _Scatter:
    def __init__(self, arrays):
        self.arrays = list(arrays)
        self.n = n = len(arrays)
        self.mid_frac = None
        self.out_shape = [jax.ShapeDtypeStruct(a.shape, a.dtype) for a in arrays]
        self.scratch = [pltpu.SemaphoreType.DMA((n, 7)), pltpu.SemaphoreType.DMA((n, 7)),
                        pltpu.SemaphoreType.DMA((n,))]

    def _copy(self, refs, i, d, landing):
        ins, outs, (send, recv, _) = refs
        me = _coords()
        peer = _flip(me, d)
        return pltpu.make_async_remote_copy(
            src_ref=ins[i].at[_slot(peer)], dst_ref=outs[i].at[_slot(peer) if landing else _slot(me)],
            send_sem=send.at[i, d - 1], recv_sem=recv.at[i, d - 1], device_id=peer, device_id_type=MESH)

    def _local(self, refs, i):
        ins, outs, (_, _, loc) = refs
        me = _slot(_coords())
        return pltpu.make_async_copy(ins[i].at[me], outs[i].at[me], loc.at[i])

    def start(self, refs):
        for i in range(self.n):
            self._local(refs, i).start()
        for d in range(1, NDEV):
            for i in range(self.n):
                self._copy(refs, i, d, False).start()

    def forward(self, refs):
        pass

    def finish(self, refs):
        for d in range(1, NDEV):
            for i in range(self.n):
                self._copy(refs, i, d, True).wait_recv()
        for d in range(1, NDEV):
            for i in range(self.n):
                self._copy(refs, i, d, False).wait_send()
        for i in range(self.n):
            self._local(refs, i).wait()


class _Both:
    def __init__(self, a, b):
        self.parts = (a, b)
        self.arrays = a.arrays + b.arrays
        self.n = a.n + b.n
        self.mid_frac = a.mid_frac if a.mid_frac is not None else b.mid_frac
        self.out_shape = a.out_shape + b.out_shape
        self.scratch = a.scratch + b.scratch

    def _each(self, refs):
        ins, outs, sems = refs
        na, ns = self.parts[0].n, len(self.parts[0].scratch)
        return ((self.parts[0], (ins[:na], outs[:na], sems[:ns])), (self.parts[1], (ins[na:], outs[na:], sems[ns:])))

    def start(self, refs):
        for part, r in self._each(refs):
            part.start(r)

    def forward(self, refs):
        for part, r in self._each(refs):
            part.forward(r)

    def finish(self, refs):
        for part, r in self._each(refs):
            part.finish(r)


def _call(body, name, grid, in_specs, out_specs, out_shape, scratch, args, comm=None):
    params = _params(*(["arbitrary"] * len(grid)))
    if comm is None:
        outs = pl.pallas_call(
            body, name=name, grid=grid, in_specs=in_specs, out_specs=out_specs, out_shape=out_shape,
            scratch_shapes=scratch, compiler_params=params)(*args)
        return list(outs), []
    n_in, n_out, n_scr, k = len(in_specs), len(out_specs), len(scratch), comm.n
    nsteps = math.prod(grid) if grid else 1
    mid = min(nsteps - 1, int(nsteps * comm.mid_frac)) if comm.mid_frac is not None else None

    def hosted(*refs):
        ins, refs = refs[:n_in], refs[n_in:]
        cins, refs = refs[:k], refs[k:]
        outs, refs = refs[:n_out], refs[n_out:]
        couts, refs = refs[:k], refs[k:]
        scr, sems = refs[:n_scr], refs[n_scr:]
        crefs = (cins, couts, sems)
        if nsteps == 1:
            comm.start(crefs)
            body(*ins, *outs, *scr)
            comm.forward(crefs)
            comm.finish(crefs)
            return
        step = pl.program_id(0)
        for a in range(1, len(grid)):
            step = step * grid[a] + pl.program_id(a)
        pl.when(step == 0)(lambda: comm.start(crefs))
        if mid is not None:
            pl.when(step == mid)(lambda: comm.forward(crefs))
        body(*ins, *outs, *scr)
        pl.when(step == nsteps - 1)(lambda: comm.finish(crefs))

    any_spec = pl.BlockSpec(memory_space=pl.ANY)
    outs = pl.pallas_call(
        hosted, name=name, grid=grid,
        in_specs=list(in_specs) + [any_spec] * k, out_specs=list(out_specs) + [any_spec] * k,
        out_shape=list(out_shape) + comm.out_shape, scratch_shapes=list(scratch) + comm.scratch,
        compiler_params=params)(*args, *comm.arrays)
    return list(outs[:n_out]), list(outs[n_out:])


def _exchange(comm, name):
    return _call(lambda: None, name, (), [], [], [], [], [], comm)[1]


def _rms_matmul(x, g, w, name, comm=None):
    T = x.shape[0]
    nb = w.shape[2]
    tT = _tile(T, 1024)

    def body(x_ref, g_ref, w_ref, z_ref, h_ref):
        @pl.when(pl.program_id(1) == 0)
        def _():
            xf = x_ref[...]
            r = lax.rsqrt(jnp.mean(xf * xf, axis=-1, keepdims=True) + RMS_EPS)
            h_ref[...] = (xf * r * g_ref[...]).astype(h_ref.dtype)

        z_ref[...] = jnp.dot(h_ref[...], w_ref[...], preferred_element_type=F32).astype(z_ref.dtype)

    return _call(
        body, name, (T // tT, NDEV),
        [pl.BlockSpec((tT, D), lambda t, n: (t, 0)),
         pl.BlockSpec((1, D), lambda t, n: (0, 0)),
         pl.BlockSpec((None, D, nb), lambda t, n: (n, 0, 0))],
        [pl.BlockSpec((tT, nb), lambda t, n: (t, n)),
         pl.BlockSpec((tT, D), lambda t, n: (t, 0))],
        [jax.ShapeDtypeStruct((T, NDEV * nb), ACT_DTYPE), jax.ShapeDtypeStruct((T, D), MXU_DTYPE)],
        [], (x, g.reshape(1, D), w), comm)


def _rms_matmul_bwd(dz, w, x, g, dxo, name, comm=None):
    T = x.shape[0]
    nb = w.shape[2]
    tT = _tile(T, 1024)

    def body(dz_ref, w_ref, x_ref, g_ref, dxo_ref, dx_ref, dg_ref, acc):
        t, n = pl.program_id(0), pl.program_id(1)

        @pl.when(n == 0)
        def _():
            acc[...] = jnp.zeros_like(acc)

        @pl.when((n == 0) & (t == 0))
        def _():
            dg_ref[...] = jnp.zeros_like(dg_ref)

        acc[...] += lax.dot_general(_mx(dz_ref[...]), w_ref[...], NT_DIMS, preferred_element_type=F32)

        @pl.when(n == NDEV - 1)
        def _():
            xf = x_ref[...]
            r = lax.rsqrt(jnp.mean(xf * xf, axis=-1, keepdims=True) + RMS_EPS)
            xh = xf * r
            dh = acc[...]
            dxh = dh * g_ref[...]
            dx_ref[...] = dxo_ref[...] + r * (dxh - xh * jnp.mean(dxh * xh, axis=-1, keepdims=True))
            dg_ref[...] += _rowsum(dh * xh)

    return _call(
        body, name, (T // tT, NDEV),
        [pl.BlockSpec((tT, nb), lambda t, n: (t, n)),
         pl.BlockSpec((None, D, nb), lambda t, n: (n, 0, 0)),
         pl.BlockSpec((tT, D), lambda t, n: (t, 0)),
         pl.BlockSpec((1, D), lambda t, n: (0, 0)),
         pl.BlockSpec((tT, D), lambda t, n: (t, 0))],
        [pl.BlockSpec((tT, D), lambda t, n: (t, 0)),
         pl.BlockSpec((1, D), lambda t, n: (0, 0))],
        [jax.ShapeDtypeStruct((T, D), F32), jax.ShapeDtypeStruct((1, D), F32)],
        [pltpu.VMEM((tT, D), F32)], (dz, w, x, g.reshape(1, D), dxo), comm)


def _atb(a, ai, b, bi, nblk, name, comm=None):
    T, M = a.shape[1:]
    N = b.shape[2]
    nb = N // nblk
    tk = _tile(T, 2048)
    nk = T // tk

    def body(a_ref, b_ref, o_ref, acc):
        k = pl.program_id(1)

        @pl.when(k == 0)
        def _():
            acc[...] = jnp.zeros_like(acc)

        acc[...] += lax.dot_general(_mx(a_ref[...]), _mx(b_ref[...]), TN_DIMS, preferred_element_type=F32)

        @pl.when(k == nk - 1)
        def _():
            o_ref[...] = acc[...].astype(o_ref.dtype)

    outs, couts = _call(
        body, name, (nblk, nk),
        [pl.BlockSpec((None, tk, M), lambda n, k: (ai, k, 0)),
         pl.BlockSpec((None, tk, nb), lambda n, k: (bi, k, n))],
        [pl.BlockSpec((None, M, nb), lambda n, k: (n, 0, 0))],
        [jax.ShapeDtypeStruct((nblk, M, nb), GRAD_DTYPE)],
        [pltpu.VMEM((M, nb), F32)], (a, b), comm)
    return outs[0], couts


def _loss_head(x, tgt, g, name):
    T = x.shape[0]
    tT = _tile(T, 512)

    def body(x_ref, t_ref, g_ref, loss_ref, dx_ref, dg_ref):
        @pl.when(pl.program_id(0) == 0)
        def _():
            loss_ref[...] = jnp.zeros_like(loss_ref)
            dg_ref[...] = jnp.zeros_like(dg_ref)

        xf = x_ref[...]
        r = lax.rsqrt(jnp.mean(xf * xf, axis=-1, keepdims=True) + RMS_EPS)
        xh = xf * r
        err = xh * g_ref[...] - t_ref[...]
        loss_ref[...] += 0.5 * jnp.sum(jnp.mean(err * err, axis=-1, keepdims=True), axis=0, keepdims=True)
        dy = err * (1.0 / D)
        dxh = dy * g_ref[...]
        dx_ref[...] = r * (dxh - xh * jnp.mean(dxh * xh, axis=-1, keepdims=True))
        dg_ref[...] += _rowsum(dy * xh)

    return pl.pallas_call(
        body, name=name, grid=(T // tT,),
        in_specs=[pl.BlockSpec((tT, D), lambda t: (t, 0)),
                  pl.BlockSpec((tT, D), lambda t: (t, 0)),
                  pl.BlockSpec((1, D), lambda t: (0, 0))],
        out_specs=[pl.BlockSpec((1, 1), lambda t: (0, 0)),
                   pl.BlockSpec((tT, D), lambda t: (t, 0)),
                   pl.BlockSpec((1, D), lambda t: (0, 0))],
        out_shape=[jax.ShapeDtypeStruct((1, 1), F32), jax.ShapeDtypeStruct((T, D), F32),
                   jax.ShapeDtypeStruct((1, D), F32)],
        compiler_params=_params("arbitrary"),
    )(x, tgt, g.reshape(1, D))


def _spatial_gate(wm_ref, bst_ref, vb_ref, sv_ref, n_chunks):
    for c in range(n_chunks):
        rows = slice(c * CHUNK, (c + 1) * CHUNK)
        for g in range(N_GROUPS):
            cols = slice(g * CHUNK, (g + 1) * CHUNK)
            sv_ref[rows, cols] = (jnp.dot(wm_ref[g], vb_ref[rows, cols], preferred_element_type=F32)
                                  + bst_ref[:, g:g + 1])


def _lane_loop(fn):
    def step(i, carry):
        fn(pl.ds(pl.multiple_of(i * LANES, LANES), LANES))
        return carry

    lax.fori_loop(0, D // LANES, step, 0)


def _shifted_copies(buf, sh, n):
    for s in range(1, 8):
        sh[s - 1, 0:n, :] = buf[s:s + n, :]


def _window(buf, sh, base, off, lanes):
    a, s = divmod(off, 8)
    src = buf if s == 0 else sh.at[s - 1]
    return src[base + 8 * a:base + 8 * a + 8, lanes]


def _softmax_rows(s):
    e = jnp.exp(s - jnp.max(s, axis=-1, keepdims=True))
    return e / jnp.sum(e, axis=-1, keepdims=True)


def _branch_fwd(z, kv, wm, bst, ln_a, cw, cvec, name, comm=None):
    T = z.shape[0]
    tT = _tile(T, 256)
    n_chunks = tT // CHUNK

    def body(z_ref, kv_ref, wm_ref, bst_ref, lna_ref, cw_ref, cvec_ref, br_ref, c_ref, gbuf, gsh, vb, ua):
        @pl.when(pl.program_id(0) == 0)
        def _():
            gbuf[0:HALO, :] = jnp.zeros((HALO, D), F32)

        def seg(s):
            return z_ref[:, s * D:(s + 1) * D].astype(F32)

        u, _ = _gelu(seg(SEG_AU))
        zg = seg(SEG_AG)
        ua[...] = u * (zg * jax.nn.sigmoid(zg))
        gv, _ = _gelu(seg(SEG_AV))
        vhat, _ = _ln_stats(gv)
        vb[...] = _mx(vhat * lna_ref[0:1, :] + lna_ref[1:2, :])
        for c in range(n_chunks):
            rows = slice(c * CHUNK, (c + 1) * CHUNK)
            for g in range(N_GROUPS):
                cols = slice(g * CHUNK, (g + 1) * CHUNK)
                sv = jnp.dot(wm_ref[g], vb[rows, cols], preferred_element_type=F32) + bst_ref[:, g:g + 1]
                br_ref[0, rows, cols] = (sv * ua[rows, cols]).astype(br_ref.dtype)

        gbuf[HALO:HALO + tT, :] = seg(SEG_BA) * jax.nn.sigmoid(seg(SEG_BB))
        _shifted_copies(gbuf, gsh, tT + HALO - 8)
        def conv_lanes(lanes):
            taps = [jnp.broadcast_to(cw_ref[k:k + 1, lanes], (8, LANES)) for k in range(CONV_K)]
            bias = jnp.broadcast_to(cvec_ref[0:1, lanes], (8, LANES))
            for base in range(0, tT, 8):
                acc = [bias, None, None, None]
                for k in range(CONV_K):
                    term = taps[k] * _window(gbuf, gsh, base, k + HALO - CONV_K + 1, lanes)
                    acc[k % 4] = term if acc[k % 4] is None else acc[k % 4] + term
                c_ref[base:base + 8, lanes] = (acc[0] + acc[1]) + (acc[2] + acc[3])

        _lane_loop(conv_lanes)
        gbuf[0:HALO, :] = gbuf[tT:tT + HALO, :]
        chat, _ = _ln_stats(c_ref[...])
        cl = chat * cvec_ref[1:2, :] + cvec_ref[2:3, :]
        zg = seg(SEG_BG)
        br_ref[1] = (cl * jax.nn.sigmoid(cl) * (zg * jax.nn.sigmoid(zg))).astype(br_ref.dtype)

        for h in range(HEADS):
            cols = slice(h * HEAD_DIM, (h + 1) * HEAD_DIM)
            q = _mx(z_ref[:, SEG_CQ * D + h * HEAD_DIM:SEG_CQ * D + (h + 1) * HEAD_DIM])
            s = lax.dot_general(q, kv_ref[:, cols], NT_DIMS, preferred_element_type=F32)
            p = _softmax_rows(s * (1.0 / math.sqrt(HEAD_DIM)))
            att = jnp.dot(_mx(p), kv_ref[:, D + h * HEAD_DIM:D + (h + 1) * HEAD_DIM], preferred_element_type=F32)
            zg = z_ref[:, SEG_CG * D + h * HEAD_DIM:SEG_CG * D + (h + 1) * HEAD_DIM].astype(F32)
            br_ref[2, :, cols] = (att * (zg * jax.nn.sigmoid(zg))).astype(br_ref.dtype)

    full = lambda shape: pl.BlockSpec(shape, lambda t: (0,) * len(shape))
    return _call(
        body, name, (T // tT,),
        [pl.BlockSpec((tT, SEG_M * D), lambda t: (t, 0)),
         full(kv.shape), full(wm.shape), full(bst.shape), full(ln_a.shape), full(cw.shape), full(cvec.shape)],
        [pl.BlockSpec((3, tT, D), lambda t: (0, t, 0)), pl.BlockSpec((tT, D), lambda t: (t, 0))],
        [jax.ShapeDtypeStruct((3, T, D), MXU_DTYPE), jax.ShapeDtypeStruct((T, D), F32)],
        [pltpu.VMEM((tT + HALO, D), F32), pltpu.VMEM((7, tT + HALO - 8, D), F32),
         pltpu.VMEM((tT, D), MXU_DTYPE), pltpu.VMEM((tT, D), F32)],
        (z, kv, wm, bst, ln_a, cw, cvec), comm)


def _branch_bwd(z, dbr, c, dzm, kv, wm, wmt, bst, ln_a, cw, cvec, name, comm=None):
    T = z.shape[0]
    M = kv.shape[0]
    tT = _tile(T, 128)
    nT = T // tT
    n_chunks = tT // CHUNK
    halo_blocks = tT // HALO

    def body(z_ref, zha_ref, zhb_ref, dbr_ref, c_ref, dzm_ref, kv_ref, wm_ref, wmt_ref, bst_ref, lna_ref,
             cw_ref, cvec_ref, dz_ref, vecg_ref, dbst_ref, dws_ref, dcw_ref, dkv_ref,
             gbuf, dcbuf, vb, dsvb, sv, dvbuf, gsh, dcsh, dglu, vh, gq, dcw8):
        i = pl.program_id(0)

        @pl.when(i == 0)
        def _():
            vecg_ref[...] = jnp.zeros_like(vecg_ref)
            dbst_ref[...] = jnp.zeros_like(dbst_ref)
            dws_ref[...] = jnp.zeros_like(dws_ref)
            dcw8[...] = jnp.zeros_like(dcw8)
            dkv_ref[...] = jnp.zeros_like(dkv_ref)
            dcbuf[tT:tT + HALO, :] = jnp.zeros((HALO, D), F32)

        strips = [slice(r0, r0 + STRIP) for r0 in range(0, tT, STRIP)]

        def seg(r, s):
            return z_ref[r, s * D:(s + 1) * D].astype(F32)

        def put(r, s, val):
            dz_ref[r, s * D:(s + 1) * D] = val.astype(dz_ref.dtype)

        for r in strips:
            zv = seg(r, SEG_AV)
            gv, tv = _gelu(zv)
            vhat, rstd = _ln_stats(gv)
            vb[r, :] = _mx(vhat * lna_ref[0:1, :] + lna_ref[1:2, :])
            vh[r, :] = vhat
            gq[r, :] = rstd * _gelu_grad(zv, tv)
        _spatial_gate(wm_ref, bst_ref, vb, sv, n_chunks)
        for r in strips:
            zu, zg = seg(r, SEG_AU), seg(r, SEG_AG)
            u, tu = _gelu(zu)
            sg = jax.nn.sigmoid(zg)
            d_a = dbr_ref[0, r, :].astype(F32)
            put(r, SEG_AU, d_a * sv[r, :] * (zg * sg) * _gelu_grad(zu, tu))
            put(r, SEG_AG, d_a * u * sv[r, :] * _silu_grad(zg, sg))
            dsv = d_a * u * (zg * sg)
            dsvb[r, :] = _mx(dsv)
            in_chunk = slice(r.start % CHUNK, r.start % CHUNK + STRIP)
            for g in range(N_GROUPS):
                dbst_ref[in_chunk, g:g + 1] += jnp.sum(dsv[:, g * CHUNK:(g + 1) * CHUNK], axis=-1, keepdims=True)
        tril = (lax.broadcasted_iota(jnp.int32, (CHUNK, CHUNK), 0)
                >= lax.broadcasted_iota(jnp.int32, (CHUNK, CHUNK), 1))
        for g in range(N_GROUPS):
            cols = slice(g * CHUNK, (g + 1) * CHUNK)
            for cc in range(n_chunks):
                rows = slice(cc * CHUNK, (cc + 1) * CHUNK)
                dws = lax.dot_general(dsvb[rows, cols], vb[rows, cols], NT_DIMS, preferred_element_type=F32)
                dws_ref[g] += jnp.where(tril, dws, 0.0)
                dvbuf[rows, cols] = jnp.dot(wmt_ref[g], dsvb[rows, cols], preferred_element_type=F32)
        for r in strips:
            dv, vhat = dvbuf[r, :], vh[r, :]
            vecg_ref[0:1, :] += _rowsum(dv * vhat)
            vecg_ref[1:2, :] += _rowsum(dv)
            dvh = dv * lna_ref[0:1, :]
            put(r, SEG_AV, (dvh - jnp.mean(dvh, axis=-1, keepdims=True)
                            - vhat * jnp.mean(dvh * vhat, axis=-1, keepdims=True)) * gq[r, :])

        sgb_buf = sv
        halo = zha_ref[...].astype(F32) * jax.nn.sigmoid(zhb_ref[...].astype(F32))
        gbuf[0:HALO, :] = jnp.where(i < nT - 1, halo, 0.0)
        for r in strips:
            za, zg = seg(r, SEG_BA), seg(r, SEG_BG)
            sgb = jax.nn.sigmoid(seg(r, SEG_BB))
            sgb_buf[r, :] = sgb
            gbuf[HALO + r.start:HALO + r.stop, :] = za * sgb
            chat, crstd = _ln_stats(c_ref[r, :])
            cl = chat * cvec_ref[1:2, :] + cvec_ref[2:3, :]
            scl = jax.nn.sigmoid(cl)
            sg = jax.nn.sigmoid(zg)
            d_b = dbr_ref[1, r, :].astype(F32)
            put(r, SEG_BG, d_b * (cl * scl) * _silu_grad(zg, sg))
            dcl = d_b * (zg * sg) * _silu_grad(cl, scl)
            vecg_ref[3:4, :] += _rowsum(dcl * chat)
            vecg_ref[4:5, :] += _rowsum(dcl)
            dc = _ln_grad(dcl, cvec_ref[1:2, :], chat, crstd)
            vecg_ref[2:3, :] += _rowsum(dc)
            dcbuf[r, :] = dc
        _shifted_copies(dcbuf, dcsh, tT + HALO - 8)
        _shifted_copies(gbuf, gsh, tT + HALO - 8)

        def conv_input_grad(lanes):
            taps = [jnp.broadcast_to(cw_ref[k:k + 1, lanes], (8, LANES)) for k in range(CONV_K)]
            for base in range(0, tT, 8):
                acc = [None] * 4
                for k in range(CONV_K):
                    term = taps[k] * _window(dcbuf, dcsh, base, CONV_K - 1 - k, lanes)
                    acc[k % 4] = term if acc[k % 4] is None else acc[k % 4] + term
                dglu[base:base + 8, lanes] = (acc[0] + acc[1]) + (acc[2] + acc[3])

        def conv_weight_grad(lanes):
            for k0 in range(0, CONV_K, 8):
                ks = range(k0, min(k0 + 8, CONV_K))
                wsum = {}
                for base in range(0, tT, 8):
                    dcv = dcbuf[base:base + 8, lanes]
                    for k in ks:
                        term = dcv * _window(gbuf, gsh, base, k + HALO - CONV_K + 1, lanes)
                        wsum[k] = term if base == 0 else wsum[k] + term
                for k in ks:
                    dcw8[k, :, lanes] += wsum[k]

        _lane_loop(conv_input_grad)
        _lane_loop(conv_weight_grad)

        @pl.when(i == nT - 1)
        def _():
            for k in range(CONV_K):
                dcw_ref[k:k + 1, :] = _rowsum(dcw8[k])
            dcw_ref[CONV_K:HALO, :] = jnp.zeros((HALO - CONV_K, D), F32)

        dcbuf[tT:tT + HALO, :] = dcbuf[0:HALO, :]
        for r in strips:
            dg, sgb = dglu[r, :], sgb_buf[r, :]
            put(r, SEG_BA, dg * sgb)
            put(r, SEG_BB, dg * seg(r, SEG_BA) * sgb * (1.0 - sgb))

        scale = 1.0 / math.sqrt(HEAD_DIM)
        for h in range(HEADS):
            cols = slice(h * HEAD_DIM, (h + 1) * HEAD_DIM)
            qcols = slice(SEG_CQ * D + h * HEAD_DIM, SEG_CQ * D + (h + 1) * HEAD_DIM)
            gcols = slice(SEG_CG * D + h * HEAD_DIM, SEG_CG * D + (h + 1) * HEAD_DIM)
            vcols = slice(D + h * HEAD_DIM, D + (h + 1) * HEAD_DIM)
            q = _mx(z_ref[:, qcols])
            kh, vh = kv_ref[:, cols], kv_ref[:, vcols]
            p = _softmax_rows(lax.dot_general(q, kh, NT_DIMS, preferred_element_type=F32) * scale)
            pb = _mx(p)
            att = jnp.dot(pb, vh, preferred_element_type=F32)
            zg = z_ref[:, gcols].astype(F32)
            sg = jax.nn.sigmoid(zg)
            d_c = dbr_ref[2, :, cols].astype(F32)
            dz_ref[:, gcols] = (d_c * att * _silu_grad(zg, sg)).astype(dz_ref.dtype)
            datt = _mx(d_c * (zg * sg))
            dp = lax.dot_general(datt, vh, NT_DIMS, preferred_element_type=F32)
            dkv_ref[:, vcols] += lax.dot_general(pb, datt, TN_DIMS, preferred_element_type=F32)
            ds = _mx(p * (dp - jnp.sum(dp * p, axis=-1, keepdims=True)) * scale)
            dz_ref[:, qcols] = jnp.dot(ds, kh, preferred_element_type=F32).astype(dz_ref.dtype)
            dkv_ref[:, cols] += lax.dot_general(ds, q, TN_DIMS, preferred_element_type=F32)

        dz_ref[:, SEG_M * D:] = dzm_ref[...].astype(dz_ref.dtype)

    rev = lambda i: nT - 1 - i
    halo_row = lambda i: jnp.maximum(rev(i) * halo_blocks - 1, 0)
    full = lambda shape: pl.BlockSpec(shape, lambda i: (0,) * len(shape))
    return _call(
        body, name, (nT,),
        [pl.BlockSpec((tT, SEG_M * D), lambda i: (rev(i), 0)),
         pl.BlockSpec((HALO, D), lambda i: (halo_row(i), SEG_BA)),
         pl.BlockSpec((HALO, D), lambda i: (halo_row(i), SEG_BB)),
         pl.BlockSpec((3, tT, D), lambda i: (0, rev(i), 0)),
         pl.BlockSpec((tT, D), lambda i: (rev(i), 0)),
         pl.BlockSpec((tT, 3 * D), lambda i: (rev(i), 0)),
         full(kv.shape), full(wm.shape), full(wmt.shape), full(bst.shape), full(ln_a.shape),
         full(cw.shape), full(cvec.shape)],
        [pl.BlockSpec((tT, N_IN), lambda i: (rev(i), 0)),
         full((8, D)), full((CHUNK, N_GROUPS)), full((N_GROUPS, CHUNK, CHUNK)), full((HALO, D)),
         full((M, 2 * D))],
        [jax.ShapeDtypeStruct((T, N_IN), MXU_DTYPE), jax.ShapeDtypeStruct((8, D), F32),
         jax.ShapeDtypeStruct((CHUNK, N_GROUPS), F32),
         jax.ShapeDtypeStruct((N_GROUPS, CHUNK, CHUNK), F32),
         jax.ShapeDtypeStruct((HALO, D), F32), jax.ShapeDtypeStruct((M, 2 * D), F32)],
        [pltpu.VMEM((tT + HALO, D), F32), pltpu.VMEM((tT + HALO, D), F32),
         pltpu.VMEM((tT, D), MXU_DTYPE), pltpu.VMEM((tT, D), MXU_DTYPE),
         pltpu.VMEM((tT, D), F32), pltpu.VMEM((tT, D), F32),
         pltpu.VMEM((7, tT + HALO - 8, D), F32), pltpu.VMEM((7, tT + HALO - 8, D), F32),
         pltpu.VMEM((tT, D), F32), pltpu.VMEM((tT, D), F32), pltpu.VMEM((tT, D), F32),
         pltpu.VMEM((CONV_K, 8, D), F32)],
        (z, z, z, dbr, c, dzm, kv, wm, wmt, bst, ln_a, cw, cvec), comm)


def _merge_fwd(br, z, x, wb, wo, name):
    T = x.shape[0]
    tT = _tile(T, 512)

    def body(br_ref, z0, z1, z2, x_ref, wb_ref, wo_ref, xn_ref, mg_ref, pj_ref):
        merged = jnp.zeros((tT, D), F32)
        for n, zm in enumerate((z0, z1, z2)):
            proj = jnp.dot(br_ref[n], wb_ref[:, n].reshape(D, D), preferred_element_type=F32)
            pj_ref[n] = proj.astype(pj_ref.dtype)
            merged = merged + jax.nn.sigmoid(zm[...].astype(F32)) * proj
        mg_ref[...] = merged.astype(mg_ref.dtype)
        xn_ref[...] = x_ref[...] + jnp.dot(_mx(merged), wo_ref[...].reshape(D, D), preferred_element_type=F32)

    zspec = lambda n: pl.BlockSpec((tT, D), lambda t: (t, SEG_M + n))
    return pl.pallas_call(
        body, name=name, grid=(T // tT,),
        in_specs=[pl.BlockSpec((3, tT, D), lambda t: (0, t, 0)), zspec(0), zspec(1), zspec(2),
                  pl.BlockSpec((tT, D), lambda t: (t, 0)),
                  pl.BlockSpec(wb.shape, lambda t: (0, 0, 0, 0)),
                  pl.BlockSpec(wo.shape, lambda t: (0, 0, 0))],
        out_specs=[pl.BlockSpec((tT, D), lambda t: (t, 0)),
                   pl.BlockSpec((tT, D), lambda t: (t, 0)),
                   pl.BlockSpec((3, tT, D), lambda t: (0, t, 0))],
        out_shape=[jax.ShapeDtypeStruct((T, D), F32), jax.ShapeDtypeStruct((T, D), MXU_DTYPE),
                   jax.ShapeDtypeStruct((3, T, D), ACT_DTYPE)],
        compiler_params=_params("parallel"),
    )(br, z, z, z, x, wb, wo)


def _merge_bwd(dxo, proj, z, wb, wo, name):
    T = dxo.shape[0]
    tT = _tile(T, 512)

    def body(dxo_ref, pj_ref, z0, z1, z2, wb_ref, wo_ref, dpj_ref, dbr_ref, dzm_ref):
        dmerged = lax.dot_general(_mx(dxo_ref[...]), wo_ref[...].reshape(D, D), NT_DIMS,
                                  preferred_element_type=F32)
        for n, zm in enumerate((z0, z1, z2)):
            gate = jax.nn.sigmoid(zm[...].astype(F32))
            dproj = _mx(gate * dmerged)
            dpj_ref[n] = dproj
            dzm_ref[:, n * D:(n + 1) * D] = (pj_ref[n].astype(F32) * dmerged * gate * (1.0 - gate)
                                             ).astype(dzm_ref.dtype)
            dbr_ref[n] = lax.dot_general(dproj, wb_ref[:, n].reshape(D, D), NT_DIMS,
                                         preferred_element_type=F32).astype(dbr_ref.dtype)

    zspec = lambda n: pl.BlockSpec((tT, D), lambda t: (t, SEG_M + n))
    return pl.pallas_call(
        body, name=name, grid=(T // tT,),
        in_specs=[pl.BlockSpec((tT, D), lambda t: (t, 0)),
                  pl.BlockSpec((3, tT, D), lambda t: (0, t, 0)), zspec(0), zspec(1), zspec(2),
                  pl.BlockSpec(wb.shape, lambda t: (0, 0, 0, 0)),
                  pl.BlockSpec(wo.shape, lambda t: (0, 0, 0))],
        out_specs=[pl.BlockSpec((3, tT, D), lambda t: (0, t, 0)),
                   pl.BlockSpec((3, tT, D), lambda t: (0, t, 0)),
                   pl.BlockSpec((tT, 3 * D), lambda t: (t, 0))],
        out_shape=[jax.ShapeDtypeStruct((3, T, D), MXU_DTYPE), jax.ShapeDtypeStruct((3, T, D), F32),
                   jax.ShapeDtypeStruct((T, 3 * D), MXU_DTYPE)],
        compiler_params=_params("parallel"),
    )(dxo, proj, z, z, z, wb, wo)


def _adamw(parts, w, m, v, idx, name, comm=None):
    R, C = parts.shape[1:]
    tr = 128 if R % 128 == 0 else R
    c1 = 1.0 / (1.0 - ADAM_B1 ** ADAM_STEP)
    c2 = 1.0 / (1.0 - ADAM_B2 ** ADAM_STEP)

    def body(p_ref, w_ref, m_ref, v_ref, g_out, d_out, m_out, v_out):
        g = p_ref[0].astype(F32)
        for p in range(1, NDEV):
            g = g + p_ref[p].astype(F32)
        mn = ADAM_B1 * m_ref[...] + (1.0 - ADAM_B1) * g
        vn = ADAM_B2 * v_ref[...] + (1.0 - ADAM_B2) * (g * g)
        g_out[...] = g
        m_out[...] = mn
        v_out[...] = vn
        d_out[...] = -ADAM_LR * ((mn * c1) / (jnp.sqrt(vn * c2) + ADAM_EPS) + ADAM_WD * w_ref[...])

    wspec = pl.BlockSpec((None, tr, C), lambda r: (idx, r, 0))
    ospec = pl.BlockSpec((tr, C), lambda r: (r, 0))
    return _call(
        body, name, (R // tr,),
        [pl.BlockSpec((NDEV, tr, C), lambda r: (0, r, 0)), wspec, wspec, wspec],
        [ospec] * 4, [jax.ShapeDtypeStruct((R, C), F32)] * 4, [], (parts, w, m, v), comm)


def kernel(x, mem, norm_g, mem_norm_g, w_in, gmlp_ln_g, gmlp_ln_b, w_s, b_s, conv_w, conv_b, conv_ln_g, conv_ln_b, w_kv, w_branch, w_out, final_norm_g, loss_target, m_norm_g, m_mem_norm_g, m_w_in, m_gmlp_ln_g, m_gmlp_ln_b, m_w_s, m_b_s, m_conv_w, m_conv_b, m_conv_ln_g, m_conv_ln_b, m_w_kv, m_w_branch, m_w_out, m_final_norm_g, v_norm_g, v_mem_norm_g, v_w_in, v_gmlp_ln_g, v_gmlp_ln_b, v_w_s, v_b_s, v_conv_w, v_conv_b, v_conv_ln_g, v_conv_ln_b, v_w_kv, v_w_branch, v_w_out, v_final_norm_g):
    L = w_in.shape[0]
    x0, mem0, tgt = x[0], mem[0], loss_target[0]
    T, M = x0.shape[0], mem0.shape[0]
    nbi, nbk, nbc = w_in.shape[2], w_kv.shape[2], conv_w.shape[2]

    def shards(l):
        return [_mx(w_in[l]), _mx(w_kv[l]), _mx(w_branch[l]), _mx(w_out[l]), conv_w[l]]

    first = _exchange(_Gather(shards(0)[:1]), "gather_first")
    gather_rest = _Gather(shards(0)[1:])
    gather_upper = _Gather([a for l in range(1, L) for a in shards(l)], mid_frac=0.85) if L > 1 else None

    tril = jnp.tril(jnp.ones((CHUNK, CHUNK), bool))
    wm = [_mx(jnp.where(tril[None], w_s[l], 0.0)) for l in range(L)]
    wmt = [w.transpose(0, 2, 1) for w in wm]
    bst = [b_s[l].T for l in range(L)]
    ln_a = [jnp.stack([gmlp_ln_g[l], gmlp_ln_b[l]]) for l in range(L)]
    cvec = [jnp.stack([conv_b[l], conv_ln_g[l], conv_ln_b[l]]) for l in range(L)]

    def conv_taps(gathered):
        return jnp.pad(gathered.transpose(1, 0, 2).reshape(CONV_K, D), ((0, HALO - CONV_K), (0, 0)))

    win, wkv, wbr, wou, cwf = [first[0]], [], [], [], []
    memn, kvs, xs, saved = [], [], [x0], []
    for l in range(L):
        (z, h), full = _rms_matmul(xs[l], norm_g[l], win[l], f"inproj_fwd{l}", gather_rest if l == 0 else None)
        if l == 0:
            wkv, wbr, wou, cwf = [full[0]], [full[1]], [full[2]], [conv_taps(full[3])]
        (kv, mn), _ = _rms_matmul(mem0, mem_norm_g[l], wkv[l], f"kv_fwd{l}")
        kvs.append(_mx(kv))
        memn.append(mn)
        (br, cpre), full = _branch_fwd(z, kvs[l], wm[l], bst[l], ln_a[l], cwf[l], cvec[l], f"branch_fwd{l}",
                                       gather_upper if l == 0 else None)
        for k in range(1, L if l == 0 else 0):
            f = full[5 * (k - 1):5 * k]
            win.append(f[0])
            wkv.append(f[1])
            wbr.append(f[2])
            wou.append(f[3])
            cwf.append(conv_taps(f[4]))
        xn, merged, proj = _merge_fwd(br, z, xs[l], wbr[l], wou[l], f"merge_fwd{l}")
        xs.append(xn)
        saved.append((z, h, br, cpre, merged, proj))
    loss_part, dx, dfg = _loss_head(xs[L], tgt, final_norm_g, "loss_head")

    pending, recv = [("final_norm_g", dfg, True)], {}

    def flush():
        scat = [(k, a) for k, a, g in pending if not g]
        gath = [(k, a) for k, a, g in pending if g]
        pending.clear()
        comms = ([_Scatter([a for _, a in scat])] if scat else []) + ([_Gather([a for _, a in gath])] if gath else [])
        return [k for k, _ in scat + gath], comms[0] if len(comms) == 1 else _Both(*comms)

    def landed(keys, arrays):
        recv.update(zip(keys, arrays))

    for l in reversed(range(L)):
        z, h, br, cpre, merged, proj = saved[l]
        dproj, dbr, dzm = _merge_bwd(dx, proj, z, wbr[l], wou[l], f"merge_bwd{l}")
        for n in range(3):
            dwb, _ = _atb(br, n, dproj, n, 1, f"dwbranch{l}_{n}")
            pending.append((f"w_branch{l}_{n}", dwb.reshape(NDEV, D // NDEV, D), False))
        dwo, _ = _atb(merged[None], 0, dx[None], 0, 1, f"dwout{l}")
        pending.append((f"w_out{l}", dwo.reshape(NDEV, D // NDEV, D), False))
        keys, comm = flush() if l == 0 else (None, None)
        (dz, vecg, dbst, dws, dcw, dkv), got = _branch_bwd(
            z, dbr, cpre, dzm, kvs[l], wm[l], wmt[l], bst[l], ln_a[l], cwf[l], cvec[l], f"branch_bwd{l}", comm)
        if l == 0:
            landed(keys, got)
        dwk, _ = _atb(memn[l][None], 0, dkv[None], 0, NDEV, f"dwkv{l}")
        (_, dmg), _ = _rms_matmul_bwd(dkv, wkv[l], mem0, mem_norm_g[l], jnp.zeros((M, D), F32), f"kv_bwd{l}")
        rest = jnp.concatenate([dmg, vecg[0:2], vecg[2:5], dbst.T.reshape(1, D)], axis=0)
        pending += [(f"w_kv{l}", dwk, False),
                    (f"conv_w{l}", dcw[:CONV_K].reshape(CONV_K, NDEV, nbc).transpose(1, 0, 2), False),
                    (f"small{l}", rest, True), (f"w_s{l}", dws.reshape(N_GROUPS * CHUNK, CHUNK), True)]
        keys, comm = flush() if l == 0 else (None, None)
        dwi, got = _atb(h[None], 0, dz[None], 0, NDEV, f"dwin{l}", comm)
        if l == 0:
            landed(keys, got)
        pending.append((f"w_in{l}", dwi, False))
        keys, comm = flush() if l == 0 else (None, None)
        (dx, dng), got = _rms_matmul_bwd(dz, win[l], xs[l], norm_g[l], dx, f"inproj_bwd{l}", comm)
        if l == 0:
            landed(keys, got)
        pending.append((f"norm_g{l}", dng, True))
    grad_x = dx[None]

    def pack(p):
        rows = []
        for l in range(L):
            rows += [p["norm_g"][l], p["mem_norm_g"][l], p["gmlp_ln_g"][l], p["gmlp_ln_b"][l], p["conv_b"][l],
                     p["conv_ln_g"][l], p["conv_ln_b"][l], p["b_s"][l].reshape(D)]
        return jnp.stack(rows + [p["final_norm_g"]])[None]

    names = ["norm_g", "mem_norm_g", "gmlp_ln_g", "gmlp_ln_b", "conv_b", "conv_ln_g", "conv_ln_b", "b_s",
             "final_norm_g"]
    w_small = pack(dict(zip(names, [norm_g, mem_norm_g, gmlp_ln_g, gmlp_ln_b, conv_b, conv_ln_g, conv_ln_b,
                                    b_s, final_norm_g])))
    m_small = pack(dict(zip(names, [m_norm_g, m_mem_norm_g, m_gmlp_ln_g, m_gmlp_ln_b, m_conv_b, m_conv_ln_g,
                                    m_conv_ln_b, m_b_s, m_final_norm_g])))
    v_small = pack(dict(zip(names, [v_norm_g, v_mem_norm_g, v_gmlp_ln_g, v_gmlp_ln_b, v_conv_b, v_conv_ln_g,
                                    v_conv_ln_b, v_b_s, v_final_norm_g])))
    outs = {}

    def run(key, parts, w, m, v, idx, comm=None):
        outs[key], got = _adamw(parts, w, m, v, idx, "adamw_" + key, comm)
        return got

    keys, comm = flush()
    landed(keys, run("w_in0", recv["w_in0"], w_in, m_w_in, v_w_in, 0, comm))
    parts_small = jnp.concatenate([recv[f"{k}{l}"] for l in range(L) for k in ("norm_g", "small")]
                                  + [recv["final_norm_g"]], axis=1)
    parts_ws = jnp.concatenate([recv[f"w_s{l}"] for l in range(L)], axis=1)
    for l in range(L):
        if l > 0:
            run(f"w_in{l}", recv[f"w_in{l}"], w_in, m_w_in, v_w_in, l)
        run(f"w_kv{l}", recv[f"w_kv{l}"], w_kv, m_w_kv, v_w_kv, l)
        for n in range(3):
            sh = (L * 3, D // NDEV, D)
            run(f"w_branch{l}_{n}", recv[f"w_branch{l}_{n}"], w_branch.reshape(sh), m_w_branch.reshape(sh),
                v_w_branch.reshape(sh), l * 3 + n)
        run(f"w_out{l}", recv[f"w_out{l}"], w_out, m_w_out, v_w_out, l)
        run(f"conv_w{l}", recv[f"conv_w{l}"], conv_w, m_conv_w, v_conv_w, l)
    run("small", parts_small, w_small, m_small, v_small, 0)
    ws_shape = (1, L * N_GROUPS * CHUNK, CHUNK)
    run("w_s", parts_ws, w_s.reshape(ws_shape), m_w_s.reshape(ws_shape), v_w_s.reshape(ws_shape), 0)

    def leaf(name, k):
        if name in ("w_in", "w_kv", "w_out", "conv_w"):
            return jnp.stack([outs[f"{name}{l}"][k] for l in range(L)])
        if name == "w_branch":
            return jnp.stack([jnp.stack([outs[f"w_branch{l}_{n}"][k] for n in range(3)]) for l in range(L)])
        if name == "w_s":
            return outs["w_s"][k].reshape(L, N_GROUPS, CHUNK, CHUNK)
        sm = outs["small"][k]
        if name == "final_norm_g":
            return sm[8 * L]
        j = names.index(name)
        rows = jnp.stack([sm[8 * l + j] for l in range(L)])
        return rows.reshape(L, N_GROUPS, CHUNK) if name == "b_s" else rows

    order = ["norm_g", "mem_norm_g", "w_in", "gmlp_ln_g", "gmlp_ln_b", "w_s", "b_s", "conv_w", "conv_b",
             "conv_ln_g", "conv_ln_b", "w_kv", "w_branch", "w_out", "final_norm_g"]
    loss = lax.psum(loss_part[0, 0], ("x", "y", "c"))
    return (loss, grad_x, *[leaf(nm, k) for k in range(4) for nm in order])
```

```python
import functools
import math

import jax
import jax.numpy as jnp
from jax import lax
from jax.experimental import pallas as pl
from jax.experimental.pallas import tpu as pltpu

F32 = jnp.float32
MXU_DTYPE = jnp.bfloat16
ACT_DTYPE = jnp.bfloat16
GRAD_DTYPE = jnp.bfloat16

D = 1024
N_SEG = 11
N_IN = N_SEG * D
NDEV = 8
CHUNK = 128
N_GROUPS = 8
CONV_K = 31
HALO = 32
LANES = 128
STRIP = 32
HEADS = 4
HEAD_DIM = D // HEADS
RMS_EPS = 1e-6
LN_EPS = 1e-5
ADAM_LR, ADAM_B1, ADAM_B2, ADAM_EPS, ADAM_WD, ADAM_STEP = 0.001, 0.9, 0.999, 1e-08, 0.01, 10
SEG_AU, SEG_AV, SEG_AG, SEG_BA, SEG_BB, SEG_BG, SEG_CQ, SEG_CG, SEG_M = 0, 1, 2, 3, 4, 5, 6, 7, 8

VMEM_LIMIT = 60 * 1024 * 1024
MESH = pl.DeviceIdType.MESH
NT_DIMS = (((1,), (1,)), ((), ()))
TN_DIMS = (((0,), (0,)), ((), ()))


def _params(*sem):
    return pltpu.CompilerParams(dimension_semantics=sem, vmem_limit_bytes=VMEM_LIMIT)


def _tile(n, want):
    t = min(n, want)
    assert n % t == 0, (n, want)
    return t


def _mx(v):
    return v.astype(MXU_DTYPE)


def _gelu(x):
    t = jnp.tanh(0.7978845608028654 * (x + 0.044715 * x * x * x))
    return 0.5 * x * (1.0 + t), t


def _gelu_grad(x, t):
    return 0.5 * (1.0 + t) + 0.5 * x * (1.0 - t * t) * 0.7978845608028654 * (1.0 + 3.0 * 0.044715 * x * x)


def _silu_grad(x, s):
    return s * (1.0 + x * (1.0 - s))


def _ln_stats(v):
    mu = jnp.mean(v, axis=-1, keepdims=True)
    vc = v - mu
    rstd = lax.rsqrt(jnp.mean(vc * vc, axis=-1, keepdims=True) + LN_EPS)
    return vc * rstd, rstd


def _ln_grad(dy, g, vhat, rstd):
    dvh = dy * g
    return rstd * (dvh - jnp.mean(dvh, axis=-1, keepdims=True)
                   - vhat * jnp.mean(dvh * vhat, axis=-1, keepdims=True))


def _rowsum(v):
    return jnp.sum(v, axis=0, keepdims=True)


def _coords():
    return lax.axis_index("x"), lax.axis_index("y"), lax.axis_index("c")


def _flip(pos, d):
    x, y, c = pos
    return (1 - x if d & 4 else x, 1 - y if d & 2 else y, 1 - c if d & 1 else c)


def _slot(pos):
    return 4 * pos[0] + 2 * pos[1] + pos[2]


CHIP_FLIPS = (4, 2, 6)


class _Gather:
    def __init__(self, arrays, mid_frac=0.8):
        self.arrays = list(arrays)
        self.n = n = len(arrays)
        self.mid_frac = mid_frac
        self.out_shape = [jax.ShapeDtypeStruct((NDEV,) + a.shape, a.dtype) for a in arrays]
        self.scratch = [pltpu.SemaphoreType.DMA((n, 7)), pltpu.SemaphoreType.DMA((n, 7)),
                        pltpu.SemaphoreType.DMA((n,))]

    def _copy(self, refs, i, k, block, to, own=False):
        ins, outs, (send, recv, _) = refs
        slot = outs[i].at[_slot(block)]
        return pltpu.make_async_remote_copy(
            src_ref=ins[i] if own else slot, dst_ref=slot, send_sem=send.at[i, k], recv_sem=recv.at[i, k],
            device_id=to, device_id_type=MESH)

    def _local(self, refs, i):
        ins, outs, (_, _, loc) = refs
        return pltpu.make_async_copy(ins[i], outs[i].at[_slot(_coords())], loc.at[i])

    def _first(self, refs, i, k):
        me = _coords()
        return self._copy(refs, i, k, me, _flip(me, ((1,) + CHIP_FLIPS)[k]), own=True)

    def _passed(self, refs, i, j):
        me = _coords()
        return self._copy(refs, i, 4 + j, _flip(me, CHIP_FLIPS[j]), _flip(me, 1))

    def start(self, refs):
        for i in range(self.n):
            self._local(refs, i).start()
        for k in range(4):
            for i in range(self.n):
                self._first(refs, i, k).start()

    def forward(self, refs):
        me = _coords()
        for j, d in enumerate(CHIP_FLIPS):
            for i in range(self.n):
                self._copy(refs, i, 1 + j, _flip(me, d), me).wait_recv()
                self._passed(refs, i, j).start()

    def finish(self, refs):
        me = _coords()
        sib = _flip(me, 1)
        for i in range(self.n):
            self._copy(refs, i, 0, sib, me).wait_recv()
        for j, d in enumerate(CHIP_FLIPS):
            for i in range(self.n):
                self._copy(refs, i, 4 + j, _flip(sib, d), me).wait_recv()
        for i in range(self.n):
            for k in range(4):
                self._first(refs, i, k).wait_send()
            for j in range(3):
                self._passed(refs, i, j).wait_send()
            self._local(refs, i).wait()


class _Scatter:
    def __init__(self, arrays):
        self.arrays = list(arrays)
        self.n = n = len(arrays)
        self.mid_frac = None
        self.out_shape = [jax.ShapeDtypeStruct(a.shape, a.dtype) for a in arrays]
        self.scratch = [pltpu.SemaphoreType.DMA((n, 7)), pltpu.SemaphoreType.DMA((n, 7)),
                        pltpu.SemaphoreType.DMA((n,))]

    def _copy(self, refs, i, d, landing):
        ins, outs, (send, recv, _) = refs
        me = _coords()
        peer = _flip(me, d)
        return pltpu.make_async_remote_copy(
            src_ref=ins[i].at[_slot(peer)], dst_ref=outs[i].at[_slot(peer) if landing else _slot(me)],
            send_sem=send.at[i, d - 1], recv_sem=recv.at[i, d - 1], device_id=peer, device_id_type=MESH)

    def _local(self, refs, i):
        ins, outs, (_, _, loc) = refs
        me = _slot(_coords())
        return pltpu.make_async_copy(ins[i].at[me], outs[i].at[me], loc.at[i])

    def start(self, refs):
        for i in range(self.n):
            self._local(refs, i).start()
        for d in range(1, NDEV):
            for i in range(self.n):
                self._copy(refs, i, d, False).start()

    def forward(self, refs):
        pass

    def finish(self, refs):
        for d in range(1, NDEV):
            for i in range(self.n):
                self._copy(refs, i, d, True).wait_recv()
        for d in range(1, NDEV):
            for i in range(self.n):
                self._copy(refs, i, d, False).wait_send()
        for i in range(self.n):
            self._local(refs, i).wait()


class _Both:
    def __init__(self, a, b):
        self.parts = (a, b)
        self.arrays = a.arrays + b.arrays
        self.n = a.n + b.n
        self.mid_frac = a.mid_frac if a.mid_frac is not None else b.mid_frac
        self.out_shape = a.out_shape + b.out_shape
        self.scratch = a.scratch + b.scratch

    def _each(self, refs):
        ins, outs, sems = refs
        na, ns = self.parts[0].n, len(self.parts[0].scratch)
        return ((self.parts[0], (ins[:na], outs[:na], sems[:ns])), (self.parts[1], (ins[na:], outs[na:], sems[ns:])))

    def start(self, refs):
        for part, r in self._each(refs):
            part.start(r)

    def forward(self, refs):
        for part, r in self._each(refs):
            part.forward(r)

    def finish(self, refs):
        for part, r in self._each(refs):
            part.finish(r)


def _call(body, name, grid, in_specs, out_specs, out_shape, scratch, args, comm=None):
    params = _params(*(["arbitrary"] * len(grid)))
    if comm is None:
        outs = pl.pallas_call(
            body, name=name, grid=grid, in_specs=in_specs, out_specs=out_specs, out_shape=out_shape,
            scratch_shapes=scratch, compiler_params=params)(*args)
        return list(outs), []
    n_in, n_out, n_scr, k = len(in_specs), len(out_specs), len(scratch), comm.n
    nsteps = math.prod(grid) if grid else 1
    mid = min(nsteps - 1, int(nsteps * comm.mid_frac)) if comm.mid_frac is not None else None

    def hosted(*refs):
        ins, refs = refs[:n_in], refs[n_in:]
        cins, refs = refs[:k], refs[k:]
        outs, refs = refs[:n_out], refs[n_out:]
        couts, refs = refs[:k], refs[k:]
        scr, sems = refs[:n_scr], refs[n_scr:]
        crefs = (cins, couts, sems)
        if nsteps == 1:
            comm.start(crefs)
            body(*ins, *outs, *scr)
            comm.forward(crefs)
            comm.finish(crefs)
            return
        step = pl.program_id(0)
        for a in range(1, len(grid)):
            step = step * grid[a] + pl.program_id(a)
        pl.when(step == 0)(lambda: comm.start(crefs))
        if mid is not None:
            pl.when(step == mid)(lambda: comm.forward(crefs))
        body(*ins, *outs, *scr)
        pl.when(step == nsteps - 1)(lambda: comm.finish(crefs))

    any_spec = pl.BlockSpec(memory_space=pl.ANY)
    outs = pl.pallas_call(
        hosted, name=name, grid=grid,
        in_specs=list(in_specs) + [any_spec] * k, out_specs=list(out_specs) + [any_spec] * k,
        out_shape=list(out_shape) + comm.out_shape, scratch_shapes=list(scratch) + comm.scratch,
        compiler_params=params)(*args, *comm.arrays)
    return list(outs[:n_out]), list(outs[n_out:])


def _exchange(comm, name):
    return _call(lambda: None, name, (), [], [], [], [], [], comm)[1]


def _rms_matmul(x, g, w, name, comm=None):
    T = x.shape[0]
    nb = w.shape[2]
    tT = _tile(T, 1024)
    per = 2

    def body(x_ref, g_ref, w_ref, z_ref, h_ref):
        @pl.when(pl.program_id(1) == 0)
        def _():
            xf = x_ref[...]
            r = lax.rsqrt(jnp.mean(xf * xf, axis=-1, keepdims=True) + RMS_EPS)
            h_ref[...] = (xf * r * g_ref[...]).astype(h_ref.dtype)

        for j in range(per):
            z_ref[:, j * nb:(j + 1) * nb] = jnp.dot(h_ref[...], w_ref[j], preferred_element_type=F32
                                                    ).astype(z_ref.dtype)

    return _call(
        body, name, (T // tT, NDEV // per),
        [pl.BlockSpec((tT, D), lambda t, n: (t, 0)),
         pl.BlockSpec((1, D), lambda t, n: (0, 0)),
         pl.BlockSpec((per, D, nb), lambda t, n: (n, 0, 0))],
        [pl.BlockSpec((tT, per * nb), lambda t, n: (t, n)),
         pl.BlockSpec((tT, D), lambda t, n: (t, 0))],
        [jax.ShapeDtypeStruct((T, NDEV * nb), ACT_DTYPE), jax.ShapeDtypeStruct((T, D), MXU_DTYPE)],
        [], (x, g.reshape(1, D), w), comm)


def _rms_matmul_bwd(dz, w, x, g, dxo, name, comm=None):
    T = x.shape[0]
    nb = w.shape[2]
    tT = _tile(T, 512)
    per = 4
    steps = NDEV // per

    def body(dz_ref, w_ref, x_ref, g_ref, dxo_ref, dx_ref, dg_ref, acc):
        t, n = pl.program_id(0), pl.program_id(1)

        @pl.when((n == 0) & (t == 0))
        def _():
            dg_ref[...] = jnp.zeros_like(dg_ref)

        part = None
        for j in range(per):
            d = lax.dot_general(_mx(dz_ref[:, j * nb:(j + 1) * nb]), w_ref[j], NT_DIMS, preferred_element_type=F32)
            part = d if part is None else part + d

        @pl.when(n == 0)
        def _():
            acc[...] = part

        @pl.when(n > 0)
        def _():
            acc[...] += part

        @pl.when(n == steps - 1)
        def _():
            xf = x_ref[...]
            r = lax.rsqrt(jnp.mean(xf * xf, axis=-1, keepdims=True) + RMS_EPS)
            xh = xf * r
            dh = acc[...]
            dxh = dh * g_ref[...]
            dx_ref[...] = dxo_ref[...] + r * (dxh - xh * jnp.mean(dxh * xh, axis=-1, keepdims=True))
            dg_ref[...] += _rowsum(dh * xh)

    return _call(
        body, name, (T // tT, steps),
        [pl.BlockSpec((tT, per * nb), lambda t, n: (t, n)),
         pl.BlockSpec((per, D, nb), lambda t, n: (n, 0, 0)),
         pl.BlockSpec((tT, D), lambda t, n: (t, 0)),
         pl.BlockSpec((1, D), lambda t, n: (0, 0)),
         pl.BlockSpec((tT, D), lambda t, n: (t, 0))],
        [pl.BlockSpec((tT, D), lambda t, n: (t, 0)),
         pl.BlockSpec((1, D), lambda t, n: (0, 0))],
        [jax.ShapeDtypeStruct((T, D), F32), jax.ShapeDtypeStruct((1, D), F32)],
        [pltpu.VMEM((tT, D), F32)], (dz, w, x, g.reshape(1, D), dxo), comm)


def _atb(a, ai, b, bi, nblk, name, comm=None):
    T, M = a.shape[1:]
    N = b.shape[2]
    nb = N // nblk
    tk = _tile(T, 2048)
    nk = T // tk

    def body(a_ref, b_ref, o_ref, acc):
        k = pl.program_id(1)

        @pl.when(k == 0)
        def _():
            acc[...] = jnp.zeros_like(acc)

        acc[...] += lax.dot_general(_mx(a_ref[...]), _mx(b_ref[...]), TN_DIMS, preferred_element_type=F32)

        @pl.when(k == nk - 1)
        def _():
            o_ref[...] = acc[...].astype(o_ref.dtype)

    outs, couts = _call(
        body, name, (nblk, nk),
        [pl.BlockSpec((None, tk, M), lambda n, k: (ai, k, 0)),
         pl.BlockSpec((None, tk, nb), lambda n, k: (bi, k, n))],
        [pl.BlockSpec((None, M, nb), lambda n, k: (n, 0, 0))],
        [jax.ShapeDtypeStruct((nblk, M, nb), GRAD_DTYPE)],
        [pltpu.VMEM((M, nb), F32)], (a, b), comm)
    return outs[0], couts


def _loss_head(x, tgt, g, name):
    T = x.shape[0]
    tT = _tile(T, 512)

    def body(x_ref, t_ref, g_ref, loss_ref, dx_ref, dg_ref):
        @pl.when(pl.program_id(0) == 0)
        def _():
            loss_ref[...] = jnp.zeros_like(loss_ref)
            dg_ref[...] = jnp.zeros_like(dg_ref)

        xf = x_ref[...]
        r = lax.rsqrt(jnp.mean(xf * xf, axis=-1, keepdims=True) + RMS_EPS)
        xh = xf * r
        err = xh * g_ref[...] - t_ref[...]
        loss_ref[...] += 0.5 * jnp.sum(jnp.mean(err * err, axis=-1, keepdims=True), axis=0, keepdims=True)
        dy = err * (1.0 / D)
        dxh = dy * g_ref[...]
        dx_ref[...] = r * (dxh - xh * jnp.mean(dxh * xh, axis=-1, keepdims=True))
        dg_ref[...] += _rowsum(dy * xh)

    return pl.pallas_call(
        body, name=name, grid=(T // tT,),
        in_specs=[pl.BlockSpec((tT, D), lambda t: (t, 0)),
                  pl.BlockSpec((tT, D), lambda t: (t, 0)),
                  pl.BlockSpec((1, D), lambda t: (0, 0))],
        out_specs=[pl.BlockSpec((1, 1), lambda t: (0, 0)),
                   pl.BlockSpec((tT, D), lambda t: (t, 0)),
                   pl.BlockSpec((1, D), lambda t: (0, 0))],
        out_shape=[jax.ShapeDtypeStruct((1, 1), F32), jax.ShapeDtypeStruct((T, D), F32),
                   jax.ShapeDtypeStruct((1, D), F32)],
        compiler_params=_params("arbitrary"),
    )(x, tgt, g.reshape(1, D))


def _spatial_gate(wm_ref, bst_ref, vb_ref, sv_ref, n_chunks):
    for c in range(n_chunks):
        rows = slice(c * CHUNK, (c + 1) * CHUNK)
        for g in range(N_GROUPS):
            cols = slice(g * CHUNK, (g + 1) * CHUNK)
            sv_ref[rows, cols] = (jnp.dot(wm_ref[g], vb_ref[rows, cols], preferred_element_type=F32)
                                  + bst_ref[:, g:g + 1])


def _lane_loop(fn):
    def step(i, carry):
        fn(pl.ds(pl.multiple_of(i * LANES, LANES), LANES))
        return carry

    lax.fori_loop(0, D // LANES, step, 0)


def _shifted_copies(buf, sh, n):
    for s in range(1, 8):
        sh[s - 1, 0:n, :] = buf[s:s + n, :]


def _window(buf, sh, base, off, lanes):
    a, s = divmod(off, 8)
    src = buf if s == 0 else sh.at[s - 1]
    return src[base + 8 * a:base + 8 * a + 8, lanes]


def _softmax_rows(s):
    e = jnp.exp(s - jnp.max(s, axis=-1, keepdims=True))
    return e / jnp.sum(e, axis=-1, keepdims=True)


def _branch_fwd(z, kv, wm, bst, ln_a, cw, cvec, name, comm=None):
    T = z.shape[0]
    tT = _tile(T, 256)
    n_chunks = tT // CHUNK

    def body(z_ref, kv_ref, wm_ref, bst_ref, lna_ref, cw_ref, cvec_ref, br_ref, c_ref, gbuf, gsh, vb, ua):
        @pl.when(pl.program_id(0) == 0)
        def _():
            gbuf[0:HALO, :] = jnp.zeros((HALO, D), F32)

        def seg(s):
            return z_ref[:, s * D:(s + 1) * D].astype(F32)

        u, _ = _gelu(seg(SEG_AU))
        zg = seg(SEG_AG)
        ua[...] = u * (zg * jax.nn.sigmoid(zg))
        gv, _ = _gelu(seg(SEG_AV))
        vhat, _ = _ln_stats(gv)
        vb[...] = _mx(vhat * lna_ref[0:1, :] + lna_ref[1:2, :])
        for c in range(n_chunks):
            rows = slice(c * CHUNK, (c + 1) * CHUNK)
            for g in range(N_GROUPS):
                cols = slice(g * CHUNK, (g + 1) * CHUNK)
                sv = jnp.dot(wm_ref[g], vb[rows, cols], preferred_element_type=F32) + bst_ref[:, g:g + 1]
                br_ref[0, rows, cols] = (sv * ua[rows, cols]).astype(br_ref.dtype)

        gbuf[HALO:HALO + tT, :] = seg(SEG_BA) * jax.nn.sigmoid(seg(SEG_BB))
        _shifted_copies(gbuf, gsh, tT + HALO - 8)
        def conv_lanes(lanes):
            taps = [jnp.broadcast_to(cw_ref[k:k + 1, lanes], (8, LANES)) for k in range(CONV_K)]
            bias = jnp.broadcast_to(cvec_ref[0:1, lanes], (8, LANES))
            for base in range(0, tT, 8):
                acc = [bias, None, None, None]
                for k in range(CONV_K):
                    term = taps[k] * _window(gbuf, gsh, base, k + HALO - CONV_K + 1, lanes)
                    acc[k % 4] = term if acc[k % 4] is None else acc[k % 4] + term
                c_ref[base:base + 8, lanes] = (acc[0] + acc[1]) + (acc[2] + acc[3])

        _lane_loop(conv_lanes)
        gbuf[0:HALO, :] = gbuf[tT:tT + HALO, :]
        chat, _ = _ln_stats(c_ref[...])
        cl = chat * cvec_ref[1:2, :] + cvec_ref[2:3, :]
        zg = seg(SEG_BG)
        br_ref[1] = (cl * jax.nn.sigmoid(cl) * (zg * jax.nn.sigmoid(zg))).astype(br_ref.dtype)

        for h in range(HEADS):
            cols = slice(h * HEAD_DIM, (h + 1) * HEAD_DIM)
            q = _mx(z_ref[:, SEG_CQ * D + h * HEAD_DIM:SEG_CQ * D + (h + 1) * HEAD_DIM])
            s = lax.dot_general(q, kv_ref[:, cols], NT_DIMS, preferred_element_type=F32)
            p = _softmax_rows(s * (1.0 / math.sqrt(HEAD_DIM)))
            att = jnp.dot(_mx(p), kv_ref[:, D + h * HEAD_DIM:D + (h + 1) * HEAD_DIM], preferred_element_type=F32)
            zg = z_ref[:, SEG_CG * D + h * HEAD_DIM:SEG_CG * D + (h + 1) * HEAD_DIM].astype(F32)
            br_ref[2, :, cols] = (att * (zg * jax.nn.sigmoid(zg))).astype(br_ref.dtype)

    full = lambda shape: pl.BlockSpec(shape, lambda t: (0,) * len(shape))
    return _call(
        body, name, (T // tT,),
        [pl.BlockSpec((tT, SEG_M * D), lambda t: (t, 0)),
         full(kv.shape), full(wm.shape), full(bst.shape), full(ln_a.shape), full(cw.shape), full(cvec.shape)],
        [pl.BlockSpec((3, tT, D), lambda t: (0, t, 0)), pl.BlockSpec((tT, D), lambda t: (t, 0))],
        [jax.ShapeDtypeStruct((3, T, D), MXU_DTYPE), jax.ShapeDtypeStruct((T, D), F32)],
        [pltpu.VMEM((tT + HALO, D), F32), pltpu.VMEM((7, tT + HALO - 8, D), F32),
         pltpu.VMEM((tT, D), MXU_DTYPE), pltpu.VMEM((tT, D), F32)],
        (z, kv, wm, bst, ln_a, cw, cvec), comm)


def _branch_bwd(z, dbr, c, dzm, kv, wm, wmt, bst, ln_a, cw, cvec, name, comm=None):
    T = z.shape[0]
    M = kv.shape[0]
    tT = _tile(T, 128)
    nT = T // tT
    n_chunks = tT // CHUNK
    halo_blocks = tT // HALO

    def body(z_ref, zha_ref, zhb_ref, dbr_ref, c_ref, dzm_ref, kv_ref, wm_ref, wmt_ref, bst_ref, lna_ref,
             cw_ref, cvec_ref, dz_ref, vecg_ref, dbst_ref, dws_ref, dcw_ref, dkv_ref,
             gbuf, dcbuf, vb, dsvb, sv, dvbuf, gsh, dcsh, dglu, vh, gq, dcw8, dcw_step):
        i = pl.program_id(0)

        @pl.when(i == 0)
        def _():
            vecg_ref[...] = jnp.zeros_like(vecg_ref)
            dbst_ref[...] = jnp.zeros_like(dbst_ref)
            dws_ref[...] = jnp.zeros_like(dws_ref)
            dcw8[...] = jnp.zeros_like(dcw8)
            dkv_ref[...] = jnp.zeros_like(dkv_ref)
            dcbuf[tT:tT + HALO, :] = jnp.zeros((HALO, D), F32)

        strips = [slice(r0, r0 + STRIP) for r0 in range(0, tT, STRIP)]

        def seg(r, s):
            return z_ref[r, s * D:(s + 1) * D].astype(F32)

        def put(r, s, val):
            dz_ref[r, s * D:(s + 1) * D] = val.astype(dz_ref.dtype)

        for r in strips:
            zv = seg(r, SEG_AV)
            gv, tv = _gelu(zv)
            vhat, rstd = _ln_stats(gv)
            vb[r, :] = _mx(vhat * lna_ref[0:1, :] + lna_ref[1:2, :])
            vh[r, :] = vhat
            gq[r, :] = rstd * _gelu_grad(zv, tv)
        _spatial_gate(wm_ref, bst_ref, vb, sv, n_chunks)
        for r in strips:
            zu, zg = seg(r, SEG_AU), seg(r, SEG_AG)
            u, tu = _gelu(zu)
            sg = jax.nn.sigmoid(zg)
            d_a = dbr_ref[0, r, :].astype(F32)
            put(r, SEG_AU, d_a * sv[r, :] * (zg * sg) * _gelu_grad(zu, tu))
            put(r, SEG_AG, d_a * u * sv[r, :] * _silu_grad(zg, sg))
            dsv = d_a * u * (zg * sg)
            dsvb[r, :] = _mx(dsv)
            in_chunk = slice(r.start % CHUNK, r.start % CHUNK + STRIP)
            for g in range(N_GROUPS):
                dbst_ref[in_chunk, g:g + 1] += jnp.sum(dsv[:, g * CHUNK:(g + 1) * CHUNK], axis=-1, keepdims=True)
        tril = (lax.broadcasted_iota(jnp.int32, (CHUNK, CHUNK), 0)
                >= lax.broadcasted_iota(jnp.int32, (CHUNK, CHUNK), 1))
        for g in range(N_GROUPS):
            cols = slice(g * CHUNK, (g + 1) * CHUNK)
            for cc in range(n_chunks):
                rows = slice(cc * CHUNK, (cc + 1) * CHUNK)
                dws = lax.dot_general(dsvb[rows, cols], vb[rows, cols], NT_DIMS, preferred_element_type=F32)
                dws_ref[g] += jnp.where(tril, dws, 0.0)
                dvbuf[rows, cols] = jnp.dot(wmt_ref[g], dsvb[rows, cols], preferred_element_type=F32)
        for r in strips:
            dv, vhat = dvbuf[r, :], vh[r, :]
            vecg_ref[0:1, :] += _rowsum(dv * vhat)
            vecg_ref[1:2, :] += _rowsum(dv)
            dvh = dv * lna_ref[0:1, :]
            put(r, SEG_AV, (dvh - jnp.mean(dvh, axis=-1, keepdims=True)
                            - vhat * jnp.mean(dvh * vhat, axis=-1, keepdims=True)) * gq[r, :])

        sgb_buf = sv
        halo = zha_ref[...].astype(F32) * jax.nn.sigmoid(zhb_ref[...].astype(F32))
        gbuf[0:HALO, :] = jnp.where(i < nT - 1, halo, 0.0)
        for r in strips:
            za, zg = seg(r, SEG_BA), seg(r, SEG_BG)
            sgb = jax.nn.sigmoid(seg(r, SEG_BB))
            sgb_buf[r, :] = sgb
            gbuf[HALO + r.start:HALO + r.stop, :] = za * sgb
            chat, crstd = _ln_stats(c_ref[r, :])
            cl = chat * cvec_ref[1:2, :] + cvec_ref[2:3, :]
            scl = jax.nn.sigmoid(cl)
            sg = jax.nn.sigmoid(zg)
            d_b = dbr_ref[1, r, :].astype(F32)
            put(r, SEG_BG, d_b * (cl * scl) * _silu_grad(zg, sg))
            dcl = d_b * (zg * sg) * _silu_grad(cl, scl)
            vecg_ref[3:4, :] += _rowsum(dcl * chat)
            vecg_ref[4:5, :] += _rowsum(dcl)
            dc = _ln_grad(dcl, cvec_ref[1:2, :], chat, crstd)
            vecg_ref[2:3, :] += _rowsum(dc)
            dcbuf[r, :] = dc
        _shifted_copies(dcbuf, dcsh, tT + HALO - 8)
        _shifted_copies(gbuf, gsh, tT + HALO - 8)

        def conv_input_grad(lanes):
            taps = [jnp.broadcast_to(cw_ref[k:k + 1, lanes], (8, LANES)) for k in range(CONV_K)]
            for base in range(0, tT, 8):
                acc = [None] * 4
                for k in range(CONV_K):
                    term = taps[k] * _window(dcbuf, dcsh, base, CONV_K - 1 - k, lanes)
                    acc[k % 4] = term if acc[k % 4] is None else acc[k % 4] + term
                dglu[base:base + 8, lanes] = (acc[0] + acc[1]) + (acc[2] + acc[3])

        def conv_weight_grad(lanes):
            dcv = [dcbuf[base:base + 8, lanes] for base in range(0, tT, 8)]
            for k in range(CONV_K):
                acc = [None] * 4
                for b in range(tT // 8):
                    term = dcv[b] * _window(gbuf, gsh, 8 * b, k + HALO - CONV_K + 1, lanes)
                    acc[b % 4] = term if acc[b % 4] is None else acc[b % 4] + term
                dcw_step[k, :, lanes] = (acc[0] + acc[1]) + (acc[2] + acc[3])

        _lane_loop(conv_input_grad)
        _lane_loop(conv_weight_grad)
        dcw8[...] += dcw_step[...]

        @pl.when(i == nT - 1)
        def _():
            for k in range(CONV_K):
                dcw_ref[k:k + 1, :] = _rowsum(dcw8[k])
            dcw_ref[CONV_K:HALO, :] = jnp.zeros((HALO - CONV_K, D), F32)

        dcbuf[tT:tT + HALO, :] = dcbuf[0:HALO, :]
        for r in strips:
            dg, sgb = dglu[r, :], sgb_buf[r, :]
            put(r, SEG_BA, dg * sgb)
            put(r, SEG_BB, dg * seg(r, SEG_BA) * sgb * (1.0 - sgb))

        scale = 1.0 / math.sqrt(HEAD_DIM)
        for h in range(HEADS):
            cols = slice(h * HEAD_DIM, (h + 1) * HEAD_DIM)
            qcols = slice(SEG_CQ * D + h * HEAD_DIM, SEG_CQ * D + (h + 1) * HEAD_DIM)
            gcols = slice(SEG_CG * D + h * HEAD_DIM, SEG_CG * D + (h + 1) * HEAD_DIM)
            vcols = slice(D + h * HEAD_DIM, D + (h + 1) * HEAD_DIM)
            q = _mx(z_ref[:, qcols])
            kh, vh = kv_ref[:, cols], kv_ref[:, vcols]
            p = _softmax_rows(lax.dot_general(q, kh, NT_DIMS, preferred_element_type=F32) * scale)
            pb = _mx(p)
            att = jnp.dot(pb, vh, preferred_element_type=F32)
            zg = z_ref[:, gcols].astype(F32)
            sg = jax.nn.sigmoid(zg)
            d_c = dbr_ref[2, :, cols].astype(F32)
            dz_ref[:, gcols] = (d_c * att * _silu_grad(zg, sg)).astype(dz_ref.dtype)
            datt = _mx(d_c * (zg * sg))
            dp = lax.dot_general(datt, vh, NT_DIMS, preferred_element_type=F32)
            dkv_ref[:, vcols] += lax.dot_general(pb, datt, TN_DIMS, preferred_element_type=F32)
            ds = _mx(p * (dp - jnp.sum(dp * p, axis=-1, keepdims=True)) * scale)
            dz_ref[:, qcols] = jnp.dot(ds, kh, preferred_element_type=F32).astype(dz_ref.dtype)
            dkv_ref[:, cols] += lax.dot_general(ds, q, TN_DIMS, preferred_element_type=F32)

        dz_ref[:, SEG_M * D:] = dzm_ref[...].astype(dz_ref.dtype)

    rev = lambda i: nT - 1 - i
    halo_row = lambda i: jnp.maximum(rev(i) * halo_blocks - 1, 0)
    full = lambda shape: pl.BlockSpec(shape, lambda i: (0,) * len(shape))
    return _call(
        body, name, (nT,),
        [pl.BlockSpec((tT, SEG_M * D), lambda i: (rev(i), 0)),
         pl.BlockSpec((HALO, D), lambda i: (halo_row(i), SEG_BA)),
         pl.BlockSpec((HALO, D), lambda i: (halo_row(i), SEG_BB)),
         pl.BlockSpec((3, tT, D), lambda i: (0, rev(i), 0)),
         pl.BlockSpec((tT, D), lambda i: (rev(i), 0)),
         pl.BlockSpec((tT, 3 * D), lambda i: (rev(i), 0)),
         full(kv.shape), full(wm.shape), full(wmt.shape), full(bst.shape), full(ln_a.shape),
         full(cw.shape), full(cvec.shape)],
        [pl.BlockSpec((tT, N_IN), lambda i: (rev(i), 0)),
         full((8, D)), full((CHUNK, N_GROUPS)), full((N_GROUPS, CHUNK, CHUNK)), full((HALO, D)),
         full((M, 2 * D))],
        [jax.ShapeDtypeStruct((T, N_IN), MXU_DTYPE), jax.ShapeDtypeStruct((8, D), F32),
         jax.ShapeDtypeStruct((CHUNK, N_GROUPS), F32),
         jax.ShapeDtypeStruct((N_GROUPS, CHUNK, CHUNK), F32),
         jax.ShapeDtypeStruct((HALO, D), F32), jax.ShapeDtypeStruct((M, 2 * D), F32)],
        [pltpu.VMEM((tT + HALO, D), F32), pltpu.VMEM((tT + HALO, D), F32),
         pltpu.VMEM((tT, D), MXU_DTYPE), pltpu.VMEM((tT, D), MXU_DTYPE),
         pltpu.VMEM((tT, D), F32), pltpu.VMEM((tT, D), F32),
         pltpu.VMEM((7, tT + HALO - 8, D), F32), pltpu.VMEM((7, tT + HALO - 8, D), F32),
         pltpu.VMEM((tT, D), F32), pltpu.VMEM((tT, D), F32), pltpu.VMEM((tT, D), F32),
         pltpu.VMEM((CONV_K, 8, D), F32), pltpu.VMEM((CONV_K, 8, D), F32)],
        (z, z, z, dbr, c, dzm, kv, wm, wmt, bst, ln_a, cw, cvec), comm)


def _merge_fwd(br, z, x, wb, wo, name):
    T = x.shape[0]
    tT = _tile(T, 512)

    def body(br_ref, z0, z1, z2, x_ref, wb_ref, wo_ref, xn_ref, mg_ref, pj_ref):
        merged = jnp.zeros((tT, D), F32)
        for n, zm in enumerate((z0, z1, z2)):
            proj = jnp.dot(br_ref[n], wb_ref[:, n].reshape(D, D), preferred_element_type=F32)
            pj_ref[n] = proj.astype(pj_ref.dtype)
            merged = merged + jax.nn.sigmoid(zm[...].astype(F32)) * proj
        mg_ref[...] = merged.astype(mg_ref.dtype)
        xn_ref[...] = x_ref[...] + jnp.dot(_mx(merged), wo_ref[...].reshape(D, D), preferred_element_type=F32)

    zspec = lambda n: pl.BlockSpec((tT, D), lambda t: (t, SEG_M + n))
    return pl.pallas_call(
        body, name=name, grid=(T // tT,),
        in_specs=[pl.BlockSpec((3, tT, D), lambda t: (0, t, 0)), zspec(0), zspec(1), zspec(2),
                  pl.BlockSpec((tT, D), lambda t: (t, 0)),
                  pl.BlockSpec(wb.shape, lambda t: (0, 0, 0, 0)),
                  pl.BlockSpec(wo.shape, lambda t: (0, 0, 0))],
        out_specs=[pl.BlockSpec((tT, D), lambda t: (t, 0)),
                   pl.BlockSpec((tT, D), lambda t: (t, 0)),
                   pl.BlockSpec((3, tT, D), lambda t: (0, t, 0))],
        out_shape=[jax.ShapeDtypeStruct((T, D), F32), jax.ShapeDtypeStruct((T, D), MXU_DTYPE),
                   jax.ShapeDtypeStruct((3, T, D), ACT_DTYPE)],
        compiler_params=_params("parallel"),
    )(br, z, z, z, x, wb, wo)


def _merge_bwd(dxo, proj, z, wb, wo, name):
    T = dxo.shape[0]
    tT = _tile(T, 512)

    def body(dxo_ref, pj_ref, z0, z1, z2, wb_ref, wo_ref, dpj_ref, dbr_ref, dzm_ref):
        dmerged = lax.dot_general(_mx(dxo_ref[...]), wo_ref[...].reshape(D, D), NT_DIMS,
                                  preferred_element_type=F32)
        for n, zm in enumerate((z0, z1, z2)):
            gate = jax.nn.sigmoid(zm[...].astype(F32))
            dproj = _mx(gate * dmerged)
            dpj_ref[n] = dproj
            dzm_ref[:, n * D:(n + 1) * D] = (pj_ref[n].astype(F32) * dmerged * gate * (1.0 - gate)
                                             ).astype(dzm_ref.dtype)
            dbr_ref[n] = lax.dot_general(dproj, wb_ref[:, n].reshape(D, D), NT_DIMS,
                                         preferred_element_type=F32).astype(dbr_ref.dtype)

    zspec = lambda n: pl.BlockSpec((tT, D), lambda t: (t, SEG_M + n))
    return pl.pallas_call(
        body, name=name, grid=(T // tT,),
        in_specs=[pl.BlockSpec((tT, D), lambda t: (t, 0)),
                  pl.BlockSpec((3, tT, D), lambda t: (0, t, 0)), zspec(0), zspec(1), zspec(2),
                  pl.BlockSpec(wb.shape, lambda t: (0, 0, 0, 0)),
                  pl.BlockSpec(wo.shape, lambda t: (0, 0, 0))],
        out_specs=[pl.BlockSpec((3, tT, D), lambda t: (0, t, 0)),
                   pl.BlockSpec((3, tT, D), lambda t: (0, t, 0)),
                   pl.BlockSpec((tT, 3 * D), lambda t: (t, 0))],
        out_shape=[jax.ShapeDtypeStruct((3, T, D), MXU_DTYPE), jax.ShapeDtypeStruct((3, T, D), F32),
                   jax.ShapeDtypeStruct((T, 3 * D), MXU_DTYPE)],
        compiler_params=_params("parallel"),
    )(dxo, proj, z, z, z, wb, wo)


def _adamw(parts, w, m, v, idx, name, comm=None):
    R, C = parts.shape[1:]
    tr = 128 if R % 128 == 0 else R
    c1 = 1.0 / (1.0 - ADAM_B1 ** ADAM_STEP)
    c2 = 1.0 / (1.0 - ADAM_B2 ** ADAM_STEP)

    def body(p_ref, w_ref, m_ref, v_ref, g_out, d_out, m_out, v_out):
        g = p_ref[0].astype(F32)
        for p in range(1, NDEV):
            g = g + p_ref[p].astype(F32)
        mn = ADAM_B1 * m_ref[...] + (1.0 - ADAM_B1) * g
        vn = ADAM_B2 * v_ref[...] + (1.0 - ADAM_B2) * (g * g)
        g_out[...] = g
        m_out[...] = mn
        v_out[...] = vn
        d_out[...] = -ADAM_LR * ((mn * c1) / (jnp.sqrt(vn * c2) + ADAM_EPS) + ADAM_WD * w_ref[...])

    wspec = pl.BlockSpec((None, tr, C), lambda r: (idx, r, 0))
    ospec = pl.BlockSpec((tr, C), lambda r: (r, 0))
    return _call(
        body, name, (R // tr,),
        [pl.BlockSpec((NDEV, tr, C), lambda r: (0, r, 0)), wspec, wspec, wspec],
        [ospec] * 4, [jax.ShapeDtypeStruct((R, C), F32)] * 4, [], (parts, w, m, v), comm)


def kernel(x, mem, norm_g, mem_norm_g, w_in, gmlp_ln_g, gmlp_ln_b, w_s, b_s, conv_w, conv_b, conv_ln_g, conv_ln_b, w_kv, w_branch, w_out, final_norm_g, loss_target, m_norm_g, m_mem_norm_g, m_w_in, m_gmlp_ln_g, m_gmlp_ln_b, m_w_s, m_b_s, m_conv_w, m_conv_b, m_conv_ln_g, m_conv_ln_b, m_w_kv, m_w_branch, m_w_out, m_final_norm_g, v_norm_g, v_mem_norm_g, v_w_in, v_gmlp_ln_g, v_gmlp_ln_b, v_w_s, v_b_s, v_conv_w, v_conv_b, v_conv_ln_g, v_conv_ln_b, v_w_kv, v_w_branch, v_w_out, v_final_norm_g):
    L = w_in.shape[0]
    x0, mem0, tgt = x[0], mem[0], loss_target[0]
    T, M = x0.shape[0], mem0.shape[0]
    nbi, nbk, nbc = w_in.shape[2], w_kv.shape[2], conv_w.shape[2]

    def shards(l):
        return [_mx(w_in[l]), _mx(w_kv[l]), _mx(w_branch[l]), _mx(w_out[l]), conv_w[l]]

    first = _exchange(_Gather(shards(0)[:1]), "gather_first")
    gather_rest = _Gather(shards(0)[1:])
    gather_upper = _Gather([a for l in range(1, L) for a in shards(l)], mid_frac=0.85) if L > 1 else None

    tril = jnp.tril(jnp.ones((CHUNK, CHUNK), bool))
    wm = [_mx(jnp.where(tril[None], w_s[l], 0.0)) for l in range(L)]
    wmt = [w.transpose(0, 2, 1) for w in wm]
    bst = [b_s[l].T for l in range(L)]
    ln_a = [jnp.stack([gmlp_ln_g[l], gmlp_ln_b[l]]) for l in range(L)]
    cvec = [jnp.stack([conv_b[l], conv_ln_g[l], conv_ln_b[l]]) for l in range(L)]

    def conv_taps(gathered):
        return jnp.pad(gathered.transpose(1, 0, 2).reshape(CONV_K, D), ((0, HALO - CONV_K), (0, 0)))

    win, wkv, wbr, wou, cwf = [first[0]], [], [], [], []
    memn, kvs, xs, saved = [], [], [x0], []
    for l in range(L):
        (z, h), full = _rms_matmul(xs[l], norm_g[l], win[l], f"inproj_fwd{l}", gather_rest if l == 0 else None)
        if l == 0:
            wkv, wbr, wou, cwf = [full[0]], [full[1]], [full[2]], [conv_taps(full[3])]
        (kv, mn), _ = _rms_matmul(mem0, mem_norm_g[l], wkv[l], f"kv_fwd{l}")
        kvs.append(_mx(kv))
        memn.append(mn)
        (br, cpre), full = _branch_fwd(z, kvs[l], wm[l], bst[l], ln_a[l], cwf[l], cvec[l], f"branch_fwd{l}",
                                       gather_upper if l == 0 else None)
        for k in range(1, L if l == 0 else 0):
            f = full[5 * (k - 1):5 * k]
            win.append(f[0])
            wkv.append(f[1])
            wbr.append(f[2])
            wou.append(f[3])
            cwf.append(conv_taps(f[4]))
        xn, merged, proj = _merge_fwd(br, z, xs[l], wbr[l], wou[l], f"merge_fwd{l}")
        xs.append(xn)
        saved.append((z, h, br, cpre, merged, proj))
    loss_part, dx, dfg = _loss_head(xs[L], tgt, final_norm_g, "loss_head")

    pending, recv = [("final_norm_g", dfg, True)], {}

    def flush():
        scat = [(k, a) for k, a, g in pending if not g]
        gath = [(k, a) for k, a, g in pending if g]
        pending.clear()
        comms = ([_Scatter([a for _, a in scat])] if scat else []) + ([_Gather([a for _, a in gath])] if gath else [])
        return [k for k, _ in scat + gath], comms[0] if len(comms) == 1 else _Both(*comms)

    def landed(keys, arrays):
        recv.update(zip(keys, arrays))

    for l in reversed(range(L)):
        z, h, br, cpre, merged, proj = saved[l]
        dproj, dbr, dzm = _merge_bwd(dx, proj, z, wbr[l], wou[l], f"merge_bwd{l}")
        for n in range(3):
            dwb, _ = _atb(br, n, dproj, n, 1, f"dwbranch{l}_{n}")
            pending.append((f"w_branch{l}_{n}", dwb.reshape(NDEV, D // NDEV, D), False))
        dwo, _ = _atb(merged[None], 0, dx[None], 0, 1, f"dwout{l}")
        pending.append((f"w_out{l}", dwo.reshape(NDEV, D // NDEV, D), False))
        keys, comm = flush() if l == 0 else (None, None)
        (dz, vecg, dbst, dws, dcw, dkv), got = _branch_bwd(
            z, dbr, cpre, dzm, kvs[l], wm[l], wmt[l], bst[l], ln_a[l], cwf[l], cvec[l], f"branch_bwd{l}", comm)
        if l == 0:
            landed(keys, got)
        dwk, _ = _atb(memn[l][None], 0, dkv[None], 0, NDEV, f"dwkv{l}")
        (_, dmg), _ = _rms_matmul_bwd(dkv, wkv[l], mem0, mem_norm_g[l], jnp.zeros((M, D), F32), f"kv_bwd{l}")
        rest = jnp.concatenate([dmg, vecg[0:2], vecg[2:5], dbst.T.reshape(1, D)], axis=0)
        pending += [(f"w_kv{l}", dwk, False),
                    (f"conv_w{l}", dcw[:CONV_K].reshape(CONV_K, NDEV, nbc).transpose(1, 0, 2), False),
                    (f"small{l}", rest, True), (f"w_s{l}", dws.reshape(N_GROUPS * CHUNK, CHUNK), True)]
        keys, comm = flush() if l == 0 else (None, None)
        dwi, got = _atb(h[None], 0, dz[None], 0, NDEV, f"dwin{l}", comm)
        if l == 0:
            landed(keys, got)
        pending.append((f"w_in{l}", dwi, False))
        keys, comm = flush() if l == 0 else (None, None)
        (dx, dng), got = _rms_matmul_bwd(dz, win[l], xs[l], norm_g[l], dx, f"inproj_bwd{l}", comm)
        if l == 0:
            landed(keys, got)
        pending.append((f"norm_g{l}", dng, True))
    grad_x = dx[None]

    def pack(p):
        rows = []
        for l in range(L):
            rows += [p["norm_g"][l], p["mem_norm_g"][l], p["gmlp_ln_g"][l], p["gmlp_ln_b"][l], p["conv_b"][l],
                     p["conv_ln_g"][l], p["conv_ln_b"][l], p["b_s"][l].reshape(D)]
        return jnp.stack(rows + [p["final_norm_g"]])[None]

    names = ["norm_g", "mem_norm_g", "gmlp_ln_g", "gmlp_ln_b", "conv_b", "conv_ln_g", "conv_ln_b", "b_s",
             "final_norm_g"]
    w_small = pack(dict(zip(names, [norm_g, mem_norm_g, gmlp_ln_g, gmlp_ln_b, conv_b, conv_ln_g, conv_ln_b,
                                    b_s, final_norm_g])))
    m_small = pack(dict(zip(names, [m_norm_g, m_mem_norm_g, m_gmlp_ln_g, m_gmlp_ln_b, m_conv_b, m_conv_ln_g,
                                    m_conv_ln_b, m_b_s, m_final_norm_g])))
    v_small = pack(dict(zip(names, [v_norm_g, v_mem_norm_g, v_gmlp_ln_g, v_gmlp_ln_b, v_conv_b, v_conv_ln_g,
                                    v_conv_ln_b, v_b_s, v_final_norm_g])))
    outs = {}

    def run(key, parts, w, m, v, idx, comm=None):
        outs[key], got = _adamw(parts, w, m, v, idx, "adamw_" + key, comm)
        return got

    keys, comm = flush()
    landed(keys, run("w_in0", recv["w_in0"], w_in, m_w_in, v_w_in, 0, comm))
    parts_small = jnp.concatenate([recv[f"{k}{l}"] for l in range(L) for k in ("norm_g", "small")]
                                  + [recv["final_norm_g"]], axis=1)
    parts_ws = jnp.concatenate([recv[f"w_s{l}"] for l in range(L)], axis=1)
    for l in range(L):
        if l > 0:
            run(f"w_in{l}", recv[f"w_in{l}"], w_in, m_w_in, v_w_in, l)
        run(f"w_kv{l}", recv[f"w_kv{l}"], w_kv, m_w_kv, v_w_kv, l)
        for n in range(3):
            sh = (L * 3, D // NDEV, D)
            run(f"w_branch{l}_{n}", recv[f"w_branch{l}_{n}"], w_branch.reshape(sh), m_w_branch.reshape(sh),
                v_w_branch.reshape(sh), l * 3 + n)
        run(f"w_out{l}", recv[f"w_out{l}"], w_out, m_w_out, v_w_out, l)
        run(f"conv_w{l}", recv[f"conv_w{l}"], conv_w, m_conv_w, v_conv_w, l)
    run("small", parts_small, w_small, m_small, v_small, 0)
    ws_shape = (1, L * N_GROUPS * CHUNK, CHUNK)
    run("w_s", parts_ws, w_s.reshape(ws_shape), m_w_s.reshape(ws_shape), v_w_s.reshape(ws_shape), 0)

    def leaf(name, k):
        if name in ("w_in", "w_kv", "w_out", "conv_w"):
            return jnp.stack([outs[f"{name}{l}"][k] for l in range(L)])
        if name == "w_branch":
            return jnp.stack([jnp.stack([outs[f"w_branch{l}_{n}"][k] for n in range(3)]) for l in range(L)])
        if name == "w_s":
            return outs["w_s"][k].reshape(L, N_GROUPS, CHUNK, CHUNK)
        sm = outs["small"][k]
        if name == "final_norm_g":
            return sm[8 * L]
        j = names.index(name)
        rows = jnp.stack([sm[8 * l + j] for l in range(L)])
        return rows.reshape(L, N_GROUPS, CHUNK) if name == "b_s" else rows

    order = ["norm_g", "mem_norm_g", "w_in", "gmlp_ln_g", "gmlp_ln_b", "w_s", "b_s", "conv_w", "conv_b",
             "conv_ln_g", "conv_ln_b", "w_kv", "w_branch", "w_out", "final_norm_g"]
    loss = lax.psum(loss_part[0, 0], ("x", "y", "c"))
    return (loss, grad_x, *[leaf(nm, k) for k in range(4) for nm in order])
```

```python
import functools
import math

import jax
import jax.numpy as jnp
from jax import lax
from jax.experimental import pallas as pl
from jax.experimental.pallas import tpu as pltpu

F32 = jnp.float32
MXU_DTYPE = jnp.bfloat16
ACT_DTYPE = jnp.bfloat16
GRAD_DTYPE = jnp.bfloat16

D = 1024
N_SEG = 11
N_IN = N_SEG * D
NDEV = 8
CHUNK = 128
N_GROUPS = 8
CONV_K = 31
HALO = 32
LANES = 128
STRIP = 32
HEADS = 4
HEAD_DIM = D // HEADS
RMS_EPS = 1e-6
LN_EPS = 1e-5
ADAM_LR, ADAM_B1, ADAM_B2, ADAM_EPS, ADAM_WD, ADAM_STEP = 0.001, 0.9, 0.999, 1e-08, 0.01, 10
SEG_AU, SEG_AV, SEG_AG, SEG_BA, SEG_BB, SEG_BG, SEG_CQ, SEG_CG, SEG_M = 0, 1, 2, 3, 4, 5, 6, 7, 8

VMEM_LIMIT = 60 * 1024 * 1024
MESH = pl.DeviceIdType.MESH
NT_DIMS = (((1,), (1,)), ((), ()))
TN_DIMS = (((0,), (0,)), ((), ()))


def _params(*sem):
    return pltpu.CompilerParams(dimension_semantics=sem, vmem_limit_bytes=VMEM_LIMIT)


def _tile(n, want):
    t = min(n, want)
    assert n % t == 0, (n, want)
    return t


def _mx(v):
    return v.astype(MXU_DTYPE)


def _gelu(x):
    t = jnp.tanh(0.7978845608028654 * (x + 0.044715 * x * x * x))
    return 0.5 * x * (1.0 + t), t


def _gelu_grad(x, t):
    return 0.5 * (1.0 + t) + 0.5 * x * (1.0 - t * t) * 0.7978845608028654 * (1.0 + 3.0 * 0.044715 * x * x)


def _silu_grad(x, s):
    return s * (1.0 + x * (1.0 - s))


def _ln_stats(v):
    mu = jnp.mean(v, axis=-1, keepdims=True)
    vc = v - mu
    rstd = lax.rsqrt(jnp.mean(vc * vc, axis=-1, keepdims=True) + LN_EPS)
    return vc * rstd, rstd


def _ln_grad(dy, g, vhat, rstd):
    dvh = dy * g
    return rstd * (dvh - jnp.mean(dvh, axis=-1, keepdims=True)
                   - vhat * jnp.mean(dvh * vhat, axis=-1, keepdims=True))


def _rowsum(v):
    return jnp.sum(v, axis=0, keepdims=True)


def _coords():
    return lax.axis_index("x"), lax.axis_index("y"), lax.axis_index("c")


def _flip(pos, d):
    x, y, c = pos
    return (1 - x if d & 4 else x, 1 - y if d & 2 else y, 1 - c if d & 1 else c)


def _slot(pos):
    return 4 * pos[0] + 2 * pos[1] + pos[2]


CHIP_FLIPS = (4, 2, 6)


class _Gather:
    def __init__(self, arrays, mid_frac=0.8):
        self.arrays = list(arrays)
        self.n = n = len(arrays)
        self.mid_frac = mid_frac
        self.out_shape = [jax.ShapeDtypeStruct((NDEV,) + a.shape, a.dtype) for a in arrays]
        self.scratch = [pltpu.SemaphoreType.DMA((n, 7)), pltpu.SemaphoreType.DMA((n, 7)),
                        pltpu.SemaphoreType.DMA((n,))]

    def _copy(self, refs, i, k, block, to, own=False):
        ins, outs, (send, recv, _) = refs
        slot = outs[i].at[_slot(block)]
        return pltpu.make_async_remote_copy(
            src_ref=ins[i] if own else slot, dst_ref=slot, send_sem=send.at[i, k], recv_sem=recv.at[i, k],
            device_id=to, device_id_type=MESH)

    def _local(self, refs, i):
        ins, outs, (_, _, loc) = refs
        return pltpu.make_async_copy(ins[i], outs[i].at[_slot(_coords())], loc.at[i])

    def _first(self, refs, i, k):
        me = _coords()
        return self._copy(refs, i, k, me, _flip(me, ((1,) + CHIP_FLIPS)[k]), own=True)

    def _passed(self, refs, i, j):
        me = _coords()
        return self._copy(refs, i, 4 + j, _flip(me, CHIP_FLIPS[j]), _flip(me, 1))

    def start(self, refs):
        for i in range(self.n):
            self._local(refs, i).start()
        for k in range(4):
            for i in range(self.n):
                self._first(refs, i, k).start()

    def forward(self, refs):
        me = _coords()
        for j, d in enumerate(CHIP_FLIPS):
            for i in range(self.n):
                self._copy(refs, i, 1 + j, _flip(me, d), me).wait_recv()
                self._passed(refs, i, j).start()

    def finish(self, refs):
        me = _coords()
        sib = _flip(me, 1)
        for i in range(self.n):
            self._copy(refs, i, 0, sib, me).wait_recv()
        for j, d in enumerate(CHIP_FLIPS):
            for i in range(self.n):
                self._copy(refs, i, 4 + j, _flip(sib, d), me).wait_recv()
        for i in range(self.n):
            for k in range(4):
                self._first(refs, i, k).wait_send()
            for j in range(3):
                self._passed(refs, i, j).wait_send()
            self._local(refs, i).wait()


class _Scatter:
    def __init__(self, arrays):
        self.arrays = list(arrays)
        self.n = n = len(arrays)
        self.mid_frac = None
        self.out_shape = [jax.ShapeDtypeStruct(a.shape, a.dtype) for a in arrays]
        self.scratch = [pltpu.SemaphoreType.DMA((n, 7)), pltpu.SemaphoreType.DMA((n, 7)),
                        pltpu.SemaphoreType.DMA((n,))]

    def _copy(self, refs, i, d, landing):
        ins, outs, (send, recv, _) = refs
        me = _coords()
        peer = _flip(me, d)
        return pltpu.make_async_remote_copy(
            src_ref=ins[i].at[_slot(peer)], dst_ref=outs[i].at[_slot(peer) if landing else _slot(me)],
            send_sem=send.at[i, d - 1], recv_sem=recv.at[i, d - 1], device_id=peer, device_id_type=MESH)

    def _local(self, refs, i):
        ins, outs, (_, _, loc) = refs
        me = _slot(_coords())
        return pltpu.make_async_copy(ins[i].at[me], outs[i].at[me], loc.at[i])

    def start(self, refs):
        for i in range(self.n):
            self._local(refs, i).start()
        for d in range(1, NDEV):
            for i in range(self.n):
                self._copy(refs, i, d, False).start()

    def forward(self, refs):
        pass

    def finish(self, refs):
        for d in range(1, NDEV):
            for i in range(self.n):
                self._copy(refs, i, d, True).wait_recv()
        for d in range(1, NDEV):
            for i in range(self.n):
                self._copy(refs, i, d, False).wait_send()
        for i in range(self.n):
            self._local(refs, i).wait()


class _Both:
    def __init__(self, a, b):
        self.parts = (a, b)
        self.arrays = a.arrays + b.arrays
        self.n = a.n + b.n
        self.mid_frac = a.mid_frac if a.mid_frac is not None else b.mid_frac
        self.out_shape = a.out_shape + b.out_shape
        self.scratch = a.scratch + b.scratch

    def _each(self, refs):
        ins, outs, sems = refs
        na, ns = self.parts[0].n, len(self.parts[0].scratch)
        return ((self.parts[0], (ins[:na], outs[:na], sems[:ns])), (self.parts[1], (ins[na:], outs[na:], sems[ns:])))

    def start(self, refs):
        for part, r in self._each(refs):
            part.start(r)

    def forward(self, refs):
        for part, r in self._each(refs):
            part.forward(r)

    def finish(self, refs):
        for part, r in self._each(refs):
            part.finish(r)


def _call(body, name, grid, in_specs, out_specs, out_shape, scratch, args, comm=None):
    params = _params(*(["arbitrary"] * len(grid)))
    if comm is None:
        outs = pl.pallas_call(
            body, name=name, grid=grid, in_specs=in_specs, out_specs=out_specs, out_shape=out_shape,
            scratch_shapes=scratch, compiler_params=params)(*args)
        return list(outs), []
    n_in, n_out, n_scr, k = len(in_specs), len(out_specs), len(scratch), comm.n
    nsteps = math.prod(grid) if grid else 1
    mid = min(nsteps - 1, int(nsteps * comm.mid_frac)) if comm.mid_frac is not None else None

    def hosted(*refs):
        ins, refs = refs[:n_in], refs[n_in:]
        cins, refs = refs[:k], refs[k:]
        outs, refs = refs[:n_out], refs[n_out:]
        couts, refs = refs[:k], refs[k:]
        scr, sems = refs[:n_scr], refs[n_scr:]
        crefs = (cins, couts, sems)
        if nsteps == 1:
            comm.start(crefs)
            body(*ins, *outs, *scr)
            comm.forward(crefs)
            comm.finish(crefs)
            return
        step = pl.program_id(0)
        for a in range(1, len(grid)):
            step = step * grid[a] + pl.program_id(a)
        pl.when(step == 0)(lambda: comm.start(crefs))
        if mid is not None:
            pl.when(step == mid)(lambda: comm.forward(crefs))
        body(*ins, *outs, *scr)
        pl.when(step == nsteps - 1)(lambda: comm.finish(crefs))

    any_spec = pl.BlockSpec(memory_space=pl.ANY)
    outs = pl.pallas_call(
        hosted, name=name, grid=grid,
        in_specs=list(in_specs) + [any_spec] * k, out_specs=list(out_specs) + [any_spec] * k,
        out_shape=list(out_shape) + comm.out_shape, scratch_shapes=list(scratch) + comm.scratch,
        compiler_params=params)(*args, *comm.arrays)
    return list(outs[:n_out]), list(outs[n_out:])


def _exchange(comm, name):
    return _call(lambda: None, name, (), [], [], [], [], [], comm)[1]


def _rms_matmul(x, g, w, name, comm=None):
    T = x.shape[0]
    nb = w.shape[2]
    tT = _tile(T, 1024)
    per = 2

    def body(x_ref, g_ref, w_ref, z_ref, h_ref):
        @pl.when(pl.program_id(1) == 0)
        def _():
            xf = x_ref[...]
            r = lax.rsqrt(jnp.mean(xf * xf, axis=-1, keepdims=True) + RMS_EPS)
            h_ref[...] = (xf * r * g_ref[...]).astype(h_ref.dtype)

        for j in range(per):
            z_ref[:, j * nb:(j + 1) * nb] = jnp.dot(h_ref[...], w_ref[j], preferred_element_type=F32
                                                    ).astype(z_ref.dtype)

    return _call(
        body, name, (T // tT, NDEV // per),
        [pl.BlockSpec((tT, D), lambda t, n: (t, 0)),
         pl.BlockSpec((1, D), lambda t, n: (0, 0)),
         pl.BlockSpec((per, D, nb), lambda t, n: (n, 0, 0))],
        [pl.BlockSpec((tT, per * nb), lambda t, n: (t, n)),
         pl.BlockSpec((tT, D), lambda t, n: (t, 0))],
        [jax.ShapeDtypeStruct((T, NDEV * nb), ACT_DTYPE), jax.ShapeDtypeStruct((T, D), MXU_DTYPE)],
        [], (x, g.reshape(1, D), w), comm)


def _rms_matmul_bwd(dz, w, x, g, dxo, name, comm=None):
    T = x.shape[0]
    nb = w.shape[2]
    tT = _tile(T, 512)
    per = 4
    steps = NDEV // per

    def body(dz_ref, w_ref, x_ref, g_ref, dxo_ref, dx_ref, dg_ref, acc):
        t, n = pl.program_id(0), pl.program_id(1)

        @pl.when((n == 0) & (t == 0))
        def _():
            dg_ref[...] = jnp.zeros_like(dg_ref)

        part = None
        for j in range(per):
            d = lax.dot_general(_mx(dz_ref[:, j * nb:(j + 1) * nb]), w_ref[j], NT_DIMS, preferred_element_type=F32)
            part = d if part is None else part + d

        @pl.when(n == 0)
        def _():
            acc[...] = part

        @pl.when(n > 0)
        def _():
            acc[...] += part

        @pl.when(n == steps - 1)
        def _():
            xf = x_ref[...]
            r = lax.rsqrt(jnp.mean(xf * xf, axis=-1, keepdims=True) + RMS_EPS)
            xh = xf * r
            dh = acc[...]
            dxh = dh * g_ref[...]
            dx_ref[...] = dxo_ref[...] + r * (dxh - xh * jnp.mean(dxh * xh, axis=-1, keepdims=True))
            dg_ref[...] += _rowsum(dh * xh)

    return _call(
        body, name, (T // tT, steps),
        [pl.BlockSpec((tT, per * nb), lambda t, n: (t, n)),
         pl.BlockSpec((per, D, nb), lambda t, n: (n, 0, 0)),
         pl.BlockSpec((tT, D), lambda t, n: (t, 0)),
         pl.BlockSpec((1, D), lambda t, n: (0, 0)),
         pl.BlockSpec((tT, D), lambda t, n: (t, 0))],
        [pl.BlockSpec((tT, D), lambda t, n: (t, 0)),
         pl.BlockSpec((1, D), lambda t, n: (0, 0))],
        [jax.ShapeDtypeStruct((T, D), F32), jax.ShapeDtypeStruct((1, D), F32)],
        [pltpu.VMEM((tT, D), F32)], (dz, w, x, g.reshape(1, D), dxo), comm)


def _atb(a, ai, b, bi, nblk, name, comm=None):
    T, M = a.shape[1:]
    N = b.shape[2]
    nb = N // nblk
    tk = _tile(T, 2048)
    nk = T // tk

    def body(a_ref, b_ref, o_ref, acc):
        k = pl.program_id(1)

        @pl.when(k == 0)
        def _():
            acc[...] = jnp.zeros_like(acc)

        acc[...] += lax.dot_general(_mx(a_ref[...]), _mx(b_ref[...]), TN_DIMS, preferred_element_type=F32)

        @pl.when(k == nk - 1)
        def _():
            o_ref[...] = acc[...].astype(o_ref.dtype)

    outs, couts = _call(
        body, name, (nblk, nk),
        [pl.BlockSpec((None, tk, M), lambda n, k: (ai, k, 0)),
         pl.BlockSpec((None, tk, nb), lambda n, k: (bi, k, n))],
        [pl.BlockSpec((None, M, nb), lambda n, k: (n, 0, 0))],
        [jax.ShapeDtypeStruct((nblk, M, nb), GRAD_DTYPE)],
        [pltpu.VMEM((M, nb), F32)], (a, b), comm)
    return outs[0], couts


def _loss_head(x, tgt, g, name):
    T = x.shape[0]
    tT = _tile(T, 512)

    def body(x_ref, t_ref, g_ref, loss_ref, dx_ref, dg_ref):
        @pl.when(pl.program_id(0) == 0)
        def _():
            loss_ref[...] = jnp.zeros_like(loss_ref)
            dg_ref[...] = jnp.zeros_like(dg_ref)

        xf = x_ref[...]
        r = lax.rsqrt(jnp.mean(xf * xf, axis=-1, keepdims=True) + RMS_EPS)
        xh = xf * r
        err = xh * g_ref[...] - t_ref[...]
        loss_ref[...] += 0.5 * jnp.sum(jnp.mean(err * err, axis=-1, keepdims=True), axis=0, keepdims=True)
        dy = err * (1.0 / D)
        dxh = dy * g_ref[...]
        dx_ref[...] = r * (dxh - xh * jnp.mean(dxh * xh, axis=-1, keepdims=True))
        dg_ref[...] += _rowsum(dy * xh)

    return pl.pallas_call(
        body, name=name, grid=(T // tT,),
        in_specs=[pl.BlockSpec((tT, D), lambda t: (t, 0)),
                  pl.BlockSpec((tT, D), lambda t: (t, 0)),
                  pl.BlockSpec((1, D), lambda t: (0, 0))],
        out_specs=[pl.BlockSpec((1, 1), lambda t: (0, 0)),
                   pl.BlockSpec((tT, D), lambda t: (t, 0)),
                   pl.BlockSpec((1, D), lambda t: (0, 0))],
        out_shape=[jax.ShapeDtypeStruct((1, 1), F32), jax.ShapeDtypeStruct((T, D), F32),
                   jax.ShapeDtypeStruct((1, D), F32)],
        compiler_params=_params("arbitrary"),
    )(x, tgt, g.reshape(1, D))


def _spatial_gate(wm_ref, bst_ref, vb_ref, sv_ref, n_chunks):
    for c in range(n_chunks):
        rows = slice(c * CHUNK, (c + 1) * CHUNK)
        for g in range(N_GROUPS):
            cols = slice(g * CHUNK, (g + 1) * CHUNK)
            sv_ref[rows, cols] = (jnp.dot(wm_ref[g], vb_ref[rows, cols], preferred_element_type=F32)
                                  + bst_ref[:, g:g + 1])


def _lane_loop(fn):
    def step(i, carry):
        fn(pl.ds(pl.multiple_of(i * LANES, LANES), LANES))
        return carry

    lax.fori_loop(0, D // LANES, step, 0)


def _shifted_copies(buf, sh, n):
    for s in range(1, 8):
        sh[s - 1, 0:n, :] = buf[s:s + n, :]


def _window(buf, sh, base, off, lanes):
    a, s = divmod(off, 8)
    src = buf if s == 0 else sh.at[s - 1]
    return src[base + 8 * a:base + 8 * a + 8, lanes]


def _softmax_rows(s):
    e = jnp.exp(s - jnp.max(s, axis=-1, keepdims=True))
    return e / jnp.sum(e, axis=-1, keepdims=True)


def _branch_fwd(z, kv, wm, bst, ln_a, cw, cvec, name, comm=None):
    T = z.shape[0]
    tT = _tile(T, 256)
    n_chunks = tT // CHUNK

    def body(z_ref, kv_ref, wm_ref, bst_ref, lna_ref, cw_ref, cvec_ref, br_ref, c_ref, gbuf, gsh, vb, ua):
        @pl.when(pl.program_id(0) == 0)
        def _():
            gbuf[0:HALO, :] = jnp.zeros((HALO, D), F32)

        def seg(s):
            return z_ref[:, s * D:(s + 1) * D].astype(F32)

        u, _ = _gelu(seg(SEG_AU))
        zg = seg(SEG_AG)
        ua[...] = u * (zg * jax.nn.sigmoid(zg))
        gv, _ = _gelu(seg(SEG_AV))
        vhat, _ = _ln_stats(gv)
        vb[...] = _mx(vhat * lna_ref[0:1, :] + lna_ref[1:2, :])
        for c in range(n_chunks):
            rows = slice(c * CHUNK, (c + 1) * CHUNK)
            for g in range(N_GROUPS):
                cols = slice(g * CHUNK, (g + 1) * CHUNK)
                sv = jnp.dot(wm_ref[g], vb[rows, cols], preferred_element_type=F32) + bst_ref[:, g:g + 1]
                br_ref[0, rows, cols] = (sv * ua[rows, cols]).astype(br_ref.dtype)

        gbuf[HALO:HALO + tT, :] = seg(SEG_BA) * jax.nn.sigmoid(seg(SEG_BB))
        _shifted_copies(gbuf, gsh, tT + HALO - 8)
        def conv_lanes(lanes):
            taps = [jnp.broadcast_to(cw_ref[k:k + 1, lanes], (8, LANES)) for k in range(CONV_K)]
            bias = jnp.broadcast_to(cvec_ref[0:1, lanes], (8, LANES))
            for base in range(0, tT, 8):
                acc = [bias, None, None, None]
                for k in range(CONV_K):
                    term = taps[k] * _window(gbuf, gsh, base, k + HALO - CONV_K + 1, lanes)
                    acc[k % 4] = term if acc[k % 4] is None else acc[k % 4] + term
                c_ref[base:base + 8, lanes] = (acc[0] + acc[1]) + (acc[2] + acc[3])

        _lane_loop(conv_lanes)
        gbuf[0:HALO, :] = gbuf[tT:tT + HALO, :]
        chat, _ = _ln_stats(c_ref[...])
        cl = chat * cvec_ref[1:2, :] + cvec_ref[2:3, :]
        zg = seg(SEG_BG)
        br_ref[1] = (cl * jax.nn.sigmoid(cl) * (zg * jax.nn.sigmoid(zg))).astype(br_ref.dtype)

        for h in range(HEADS):
            cols = slice(h * HEAD_DIM, (h + 1) * HEAD_DIM)
            q = _mx(z_ref[:, SEG_CQ * D + h * HEAD_DIM:SEG_CQ * D + (h + 1) * HEAD_DIM])
            s = lax.dot_general(q, kv_ref[:, cols], NT_DIMS, preferred_element_type=F32)
            p = _softmax_rows(s * (1.0 / math.sqrt(HEAD_DIM)))
            att = jnp.dot(_mx(p), kv_ref[:, D + h * HEAD_DIM:D + (h + 1) * HEAD_DIM], preferred_element_type=F32)
            zg = z_ref[:, SEG_CG * D + h * HEAD_DIM:SEG_CG * D + (h + 1) * HEAD_DIM].astype(F32)
            br_ref[2, :, cols] = (att * (zg * jax.nn.sigmoid(zg))).astype(br_ref.dtype)

    full = lambda shape: pl.BlockSpec(shape, lambda t: (0,) * len(shape))
    return _call(
        body, name, (T // tT,),
        [pl.BlockSpec((tT, SEG_M * D), lambda t: (t, 0)),
         full(kv.shape), full(wm.shape), full(bst.shape), full(ln_a.shape), full(cw.shape), full(cvec.shape)],
        [pl.BlockSpec((3, tT, D), lambda t: (0, t, 0)), pl.BlockSpec((tT, D), lambda t: (t, 0))],
        [jax.ShapeDtypeStruct((3, T, D), MXU_DTYPE), jax.ShapeDtypeStruct((T, D), F32)],
        [pltpu.VMEM((tT + HALO, D), F32), pltpu.VMEM((7, tT + HALO - 8, D), F32),
         pltpu.VMEM((tT, D), MXU_DTYPE), pltpu.VMEM((tT, D), F32)],
        (z, kv, wm, bst, ln_a, cw, cvec), comm)


def _branch_bwd(z, dbr, c, dzm, kv, wm, wmt, bst, ln_a, cw, cvec, name, comm=None):
    T = z.shape[0]
    M = kv.shape[0]
    tT = _tile(T, 128)
    nT = T // tT
    n_chunks = tT // CHUNK

    def body(z_ref, dbr_ref, c_ref, dzm_ref, kv_ref, wm_ref, wmt_ref, bst_ref, lna_ref,
             cw_ref, cvec_ref, dz_ref, vecg_ref, dbst_ref, dws_ref, dcw_ref, dkv_ref,
             gbuf, dcbuf, vb, dsvb, sv, dvbuf, dcsh, dglu, vh, gq, dcw8, dcw_step):
        i = pl.program_id(0)

        @pl.when(i == 0)
        def _():
            vecg_ref[...] = jnp.zeros_like(vecg_ref)
            dbst_ref[...] = jnp.zeros_like(dbst_ref)
            dws_ref[...] = jnp.zeros_like(dws_ref)
            dcw8[...] = jnp.zeros_like(dcw8)
            dkv_ref[...] = jnp.zeros_like(dkv_ref)
            dcbuf[tT:tT + HALO, :] = jnp.zeros((HALO, D), F32)

        strips = [slice(r0, r0 + STRIP) for r0 in range(0, tT, STRIP)]

        def seg(r, s):
            return z_ref[r, s * D:(s + 1) * D].astype(F32)

        def put(r, s, val):
            dz_ref[r, s * D:(s + 1) * D] = val.astype(dz_ref.dtype)

        for r in strips:
            zv = seg(r, SEG_AV)
            gv, tv = _gelu(zv)
            vhat, rstd = _ln_stats(gv)
            vb[r, :] = _mx(vhat * lna_ref[0:1, :] + lna_ref[1:2, :])
            vh[r, :] = vhat
            gq[r, :] = rstd * _gelu_grad(zv, tv)
        _spatial_gate(wm_ref, bst_ref, vb, sv, n_chunks)
        for r in strips:
            zu, zg = seg(r, SEG_AU), seg(r, SEG_AG)
            u, tu = _gelu(zu)
            sg = jax.nn.sigmoid(zg)
            d_a = dbr_ref[0, r, :].astype(F32)
            put(r, SEG_AU, d_a * sv[r, :] * (zg * sg) * _gelu_grad(zu, tu))
            put(r, SEG_AG, d_a * u * sv[r, :] * _silu_grad(zg, sg))
            dsv = d_a * u * (zg * sg)
            dsvb[r, :] = _mx(dsv)
            in_chunk = slice(r.start % CHUNK, r.start % CHUNK + STRIP)
            for g in range(N_GROUPS):
                dbst_ref[in_chunk, g:g + 1] += jnp.sum(dsv[:, g * CHUNK:(g + 1) * CHUNK], axis=-1, keepdims=True)
        tril = (lax.broadcasted_iota(jnp.int32, (CHUNK, CHUNK), 0)
                >= lax.broadcasted_iota(jnp.int32, (CHUNK, CHUNK), 1))
        for g in range(N_GROUPS):
            cols = slice(g * CHUNK, (g + 1) * CHUNK)
            for cc in range(n_chunks):
                rows = slice(cc * CHUNK, (cc + 1) * CHUNK)
                dws = lax.dot_general(dsvb[rows, cols], vb[rows, cols], NT_DIMS, preferred_element_type=F32)
                dws_ref[g] += jnp.where(tril, dws, 0.0)
                dvbuf[rows, cols] = jnp.dot(wmt_ref[g], dsvb[rows, cols], preferred_element_type=F32)
        for r in strips:
            dv, vhat = dvbuf[r, :], vh[r, :]
            vecg_ref[0:1, :] += _rowsum(dv * vhat)
            vecg_ref[1:2, :] += _rowsum(dv)
            dvh = dv * lna_ref[0:1, :]
            put(r, SEG_AV, (dvh - jnp.mean(dvh, axis=-1, keepdims=True)
                            - vhat * jnp.mean(dvh * vhat, axis=-1, keepdims=True)) * gq[r, :])

        sgb_buf = sv
        for r in strips:
            za, zg = seg(r, SEG_BA), seg(r, SEG_BG)
            sgb = jax.nn.sigmoid(seg(r, SEG_BB))
            sgb_buf[r, :] = sgb
            gbuf[r, :] = za * sgb
            chat, crstd = _ln_stats(c_ref[r, :])
            cl = chat * cvec_ref[1:2, :] + cvec_ref[2:3, :]
            scl = jax.nn.sigmoid(cl)
            sg = jax.nn.sigmoid(zg)
            d_b = dbr_ref[1, r, :].astype(F32)
            put(r, SEG_BG, d_b * (cl * scl) * _silu_grad(zg, sg))
            dcl = d_b * (zg * sg) * _silu_grad(cl, scl)
            vecg_ref[3:4, :] += _rowsum(dcl * chat)
            vecg_ref[4:5, :] += _rowsum(dcl)
            dc = _ln_grad(dcl, cvec_ref[1:2, :], chat, crstd)
            vecg_ref[2:3, :] += _rowsum(dc)
            dcbuf[r, :] = dc
        _shifted_copies(dcbuf, dcsh, tT + HALO - 8)

        def conv_grads(lanes):
            taps = [jnp.broadcast_to(cw_ref[k:k + 1, lanes], (8, LANES)) for k in range(CONV_K)]
            wsum = [None] * CONV_K
            for base in range(0, tT, 8):
                glu = gbuf[base:base + 8, lanes]
                acc = [None] * 4
                for k in range(CONV_K):
                    win = _window(dcbuf, dcsh, base, CONV_K - 1 - k, lanes)
                    term = taps[k] * win
                    acc[k % 4] = term if acc[k % 4] is None else acc[k % 4] + term
                    term = glu * win
                    wsum[k] = term if wsum[k] is None else wsum[k] + term
                dglu[base:base + 8, lanes] = (acc[0] + acc[1]) + (acc[2] + acc[3])
            for k in range(CONV_K):
                dcw_step[8 * k:8 * k + 8, lanes] = wsum[k]

        _lane_loop(conv_grads)
        dcw8[...] += dcw_step[...]

        @pl.when(i == nT - 1)
        def _():
            for k in range(CONV_K):
                dcw_ref[k:k + 1, :] = _rowsum(dcw8[8 * k:8 * k + 8, :])
            dcw_ref[CONV_K:HALO, :] = jnp.zeros((HALO - CONV_K, D), F32)

        dcbuf[tT:tT + HALO, :] = dcbuf[0:HALO, :]
        for r in strips:
            dg, sgb = dglu[r, :], sgb_buf[r, :]
            put(r, SEG_BA, dg * sgb)
            put(r, SEG_BB, dg * seg(r, SEG_BA) * sgb * (1.0 - sgb))

        scale = 1.0 / math.sqrt(HEAD_DIM)
        for h in range(HEADS):
            cols = slice(h * HEAD_DIM, (h + 1) * HEAD_DIM)
            qcols = slice(SEG_CQ * D + h * HEAD_DIM, SEG_CQ * D + (h + 1) * HEAD_DIM)
            gcols = slice(SEG_CG * D + h * HEAD_DIM, SEG_CG * D + (h + 1) * HEAD_DIM)
            vcols = slice(D + h * HEAD_DIM, D + (h + 1) * HEAD_DIM)
            q = _mx(z_ref[:, qcols])
            kh, vh = kv_ref[:, cols], kv_ref[:, vcols]
            p = _softmax_rows(lax.dot_general(q, kh, NT_DIMS, preferred_element_type=F32) * scale)
            pb = _mx(p)
            att = jnp.dot(pb, vh, preferred_element_type=F32)
            zg = z_ref[:, gcols].astype(F32)
            sg = jax.nn.sigmoid(zg)
            d_c = dbr_ref[2, :, cols].astype(F32)
            dz_ref[:, gcols] = (d_c * att * _silu_grad(zg, sg)).astype(dz_ref.dtype)
            datt = _mx(d_c * (zg * sg))
            dp = lax.dot_general(datt, vh, NT_DIMS, preferred_element_type=F32)
            dkv_ref[:, vcols] += lax.dot_general(pb, datt, TN_DIMS, preferred_element_type=F32)
            ds = _mx(p * (dp - jnp.sum(dp * p, axis=-1, keepdims=True)) * scale)
            dz_ref[:, qcols] = jnp.dot(ds, kh, preferred_element_type=F32).astype(dz_ref.dtype)
            dkv_ref[:, cols] += lax.dot_general(ds, q, TN_DIMS, preferred_element_type=F32)

        dz_ref[:, SEG_M * D:] = dzm_ref[...].astype(dz_ref.dtype)

    rev = lambda i: nT - 1 - i
    full = lambda shape: pl.BlockSpec(shape, lambda i: (0,) * len(shape))
    return _call(
        body, name, (nT,),
        [pl.BlockSpec((tT, SEG_M * D), lambda i: (rev(i), 0)),
         pl.BlockSpec((3, tT, D), lambda i: (0, rev(i), 0)),
         pl.BlockSpec((tT, D), lambda i: (rev(i), 0)),
         pl.BlockSpec((tT, 3 * D), lambda i: (rev(i), 0)),
         full(kv.shape), full(wm.shape), full(wmt.shape), full(bst.shape), full(ln_a.shape),
         full(cw.shape), full(cvec.shape)],
        [pl.BlockSpec((tT, N_IN), lambda i: (rev(i), 0)),
         full((8, D)), full((CHUNK, N_GROUPS)), full((N_GROUPS, CHUNK, CHUNK)), full((HALO, D)),
         full((M, 2 * D))],
        [jax.ShapeDtypeStruct((T, N_IN), MXU_DTYPE), jax.ShapeDtypeStruct((8, D), F32),
         jax.ShapeDtypeStruct((CHUNK, N_GROUPS), F32),
         jax.ShapeDtypeStruct((N_GROUPS, CHUNK, CHUNK), F32),
         jax.ShapeDtypeStruct((HALO, D), F32), jax.ShapeDtypeStruct((M, 2 * D), F32)],
        [pltpu.VMEM((tT, D), F32), pltpu.VMEM((tT + HALO, D), F32),
         pltpu.VMEM((tT, D), MXU_DTYPE), pltpu.VMEM((tT, D), MXU_DTYPE),
         pltpu.VMEM((tT, D), F32), pltpu.VMEM((tT, D), F32),
         pltpu.VMEM((7, tT + HALO - 8, D), F32),
         pltpu.VMEM((tT, D), F32), pltpu.VMEM((tT, D), F32), pltpu.VMEM((tT, D), F32),
         pltpu.VMEM((CONV_K * 8, D), F32), pltpu.VMEM((CONV_K * 8, D), F32)],
        (z, dbr, c, dzm, kv, wm, wmt, bst, ln_a, cw, cvec), comm)


def _merge_fwd(br, z, x, wb, wo, name):
    T = x.shape[0]
    tT = _tile(T, 512)

    def body(br_ref, z0, z1, z2, x_ref, wb_ref, wo_ref, xn_ref, mg_ref, pj_ref):
        merged = jnp.zeros((tT, D), F32)
        for n, zm in enumerate((z0, z1, z2)):
            proj = jnp.dot(br_ref[n], wb_ref[:, n].reshape(D, D), preferred_element_type=F32)
            pj_ref[n] = proj.astype(pj_ref.dtype)
            merged = merged + jax.nn.sigmoid(zm[...].astype(F32)) * proj
        mg_ref[...] = merged.astype(mg_ref.dtype)
        xn_ref[...] = x_ref[...] + jnp.dot(_mx(merged), wo_ref[...].reshape(D, D), preferred_element_type=F32)

    zspec = lambda n: pl.BlockSpec((tT, D), lambda t: (t, SEG_M + n))
    return pl.pallas_call(
        body, name=name, grid=(T // tT,),
        in_specs=[pl.BlockSpec((3, tT, D), lambda t: (0, t, 0)), zspec(0), zspec(1), zspec(2),
                  pl.BlockSpec((tT, D), lambda t: (t, 0)),
                  pl.BlockSpec(wb.shape, lambda t: (0, 0, 0, 0)),
                  pl.BlockSpec(wo.shape, lambda t: (0, 0, 0))],
        out_specs=[pl.BlockSpec((tT, D), lambda t: (t, 0)),
                   pl.BlockSpec((tT, D), lambda t: (t, 0)),
                   pl.BlockSpec((3, tT, D), lambda t: (0, t, 0))],
        out_shape=[jax.ShapeDtypeStruct((T, D), F32), jax.ShapeDtypeStruct((T, D), MXU_DTYPE),
                   jax.ShapeDtypeStruct((3, T, D), ACT_DTYPE)],
        compiler_params=_params("parallel"),
    )(br, z, z, z, x, wb, wo)


def _merge_bwd(dxo, proj, z, wb, wo, name):
    T = dxo.shape[0]
    tT = _tile(T, 512)

    def body(dxo_ref, pj_ref, z0, z1, z2, wb_ref, wo_ref, dpj_ref, dbr_ref, dzm_ref):
        dmerged = lax.dot_general(_mx(dxo_ref[...]), wo_ref[...].reshape(D, D), NT_DIMS,
                                  preferred_element_type=F32)
        for n, zm in enumerate((z0, z1, z2)):
            gate = jax.nn.sigmoid(zm[...].astype(F32))
            dproj = _mx(gate * dmerged)
            dpj_ref[n] = dproj
            dzm_ref[:, n * D:(n + 1) * D] = (pj_ref[n].astype(F32) * dmerged * gate * (1.0 - gate)
                                             ).astype(dzm_ref.dtype)
            dbr_ref[n] = lax.dot_general(dproj, wb_ref[:, n].reshape(D, D), NT_DIMS,
                                         preferred_element_type=F32).astype(dbr_ref.dtype)

    zspec = lambda n: pl.BlockSpec((tT, D), lambda t: (t, SEG_M + n))
    return pl.pallas_call(
        body, name=name, grid=(T // tT,),
        in_specs=[pl.BlockSpec((tT, D), lambda t: (t, 0)),
                  pl.BlockSpec((3, tT, D), lambda t: (0, t, 0)), zspec(0), zspec(1), zspec(2),
                  pl.BlockSpec(wb.shape, lambda t: (0, 0, 0, 0)),
                  pl.BlockSpec(wo.shape, lambda t: (0, 0, 0))],
        out_specs=[pl.BlockSpec((3, tT, D), lambda t: (0, t, 0)),
                   pl.BlockSpec((3, tT, D), lambda t: (0, t, 0)),
                   pl.BlockSpec((tT, 3 * D), lambda t: (t, 0))],
        out_shape=[jax.ShapeDtypeStruct((3, T, D), MXU_DTYPE), jax.ShapeDtypeStruct((3, T, D), ACT_DTYPE),
                   jax.ShapeDtypeStruct((T, 3 * D), MXU_DTYPE)],
        compiler_params=_params("parallel"),
    )(dxo, proj, z, z, z, wb, wo)


def _adamw(parts, w, m, v, idx, name, comm=None):
    R, C = parts.shape[1:]
    tr = 128 if R % 128 == 0 else R
    c1 = 1.0 / (1.0 - ADAM_B1 ** ADAM_STEP)
    c2 = 1.0 / (1.0 - ADAM_B2 ** ADAM_STEP)

    def body(p_ref, w_ref, m_ref, v_ref, g_out, d_out, m_out, v_out):
        g = p_ref[0].astype(F32)
        for p in range(1, NDEV):
            g = g + p_ref[p].astype(F32)
        mn = ADAM_B1 * m_ref[...] + (1.0 - ADAM_B1) * g
        vn = ADAM_B2 * v_ref[...] + (1.0 - ADAM_B2) * (g * g)
        g_out[...] = g
        m_out[...] = mn
        v_out[...] = vn
        d_out[...] = -ADAM_LR * ((mn * c1) / (jnp.sqrt(vn * c2) + ADAM_EPS) + ADAM_WD * w_ref[...])

    wspec = pl.BlockSpec((None, tr, C), lambda r: (idx, r, 0))
    ospec = pl.BlockSpec((tr, C), lambda r: (r, 0))
    return _call(
        body, name, (R // tr,),
        [pl.BlockSpec((NDEV, tr, C), lambda r: (0, r, 0)), wspec, wspec, wspec],
        [ospec] * 4, [jax.ShapeDtypeStruct((R, C), F32)] * 4, [], (parts, w, m, v), comm)


def kernel(x, mem, norm_g, mem_norm_g, w_in, gmlp_ln_g, gmlp_ln_b, w_s, b_s, conv_w, conv_b, conv_ln_g, conv_ln_b, w_kv, w_branch, w_out, final_norm_g, loss_target, m_norm_g, m_mem_norm_g, m_w_in, m_gmlp_ln_g, m_gmlp_ln_b, m_w_s, m_b_s, m_conv_w, m_conv_b, m_conv_ln_g, m_conv_ln_b, m_w_kv, m_w_branch, m_w_out, m_final_norm_g, v_norm_g, v_mem_norm_g, v_w_in, v_gmlp_ln_g, v_gmlp_ln_b, v_w_s, v_b_s, v_conv_w, v_conv_b, v_conv_ln_g, v_conv_ln_b, v_w_kv, v_w_branch, v_w_out, v_final_norm_g):
    L = w_in.shape[0]
    x0, mem0, tgt = x[0], mem[0], loss_target[0]
    T, M = x0.shape[0], mem0.shape[0]
    nbi, nbk, nbc = w_in.shape[2], w_kv.shape[2], conv_w.shape[2]

    def shards(l):
        return [_mx(w_in[l]), _mx(w_kv[l]), _mx(w_branch[l]), _mx(w_out[l]), conv_w[l]]

    first = _exchange(_Gather(shards(0)[:1]), "gather_first")
    gather_rest = _Gather(shards(0)[1:])
    gather_upper = _Gather([a for l in range(1, L) for a in shards(l)], mid_frac=0.85) if L > 1 else None

    tril = jnp.tril(jnp.ones((CHUNK, CHUNK), bool))
    wm = [_mx(jnp.where(tril[None], w_s[l], 0.0)) for l in range(L)]
    wmt = [w.transpose(0, 2, 1) for w in wm]
    bst = [b_s[l].T for l in range(L)]
    ln_a = [jnp.stack([gmlp_ln_g[l], gmlp_ln_b[l]]) for l in range(L)]
    cvec = [jnp.stack([conv_b[l], conv_ln_g[l], conv_ln_b[l]]) for l in range(L)]

    def conv_taps(gathered):
        return jnp.pad(gathered.transpose(1, 0, 2).reshape(CONV_K, D), ((0, HALO - CONV_K), (0, 0)))

    win, wkv, wbr, wou, cwf = [first[0]], [], [], [], []
    memn, kvs, xs, saved = [], [], [x0], []
    for l in range(L):
        (z, h), full = _rms_matmul(xs[l], norm_g[l], win[l], f"inproj_fwd{l}", gather_rest if l == 0 else None)
        if l == 0:
            wkv, wbr, wou, cwf = [full[0]], [full[1]], [full[2]], [conv_taps(full[3])]
        (kv, mn), _ = _rms_matmul(mem0, mem_norm_g[l], wkv[l], f"kv_fwd{l}")
        kvs.append(_mx(kv))
        memn.append(mn)
        (br, cpre), full = _branch_fwd(z, kvs[l], wm[l], bst[l], ln_a[l], cwf[l], cvec[l], f"branch_fwd{l}",
                                       gather_upper if l == 0 else None)
        for k in range(1, L if l == 0 else 0):
            f = full[5 * (k - 1):5 * k]
            win.append(f[0])
            wkv.append(f[1])
            wbr.append(f[2])
            wou.append(f[3])
            cwf.append(conv_taps(f[4]))
        xn, merged, proj = _merge_fwd(br, z, xs[l], wbr[l], wou[l], f"merge_fwd{l}")
        xs.append(xn)
        saved.append((z, h, br, cpre, merged, proj))
    loss_part, dx, dfg = _loss_head(xs[L], tgt, final_norm_g, "loss_head")

    pending, recv = [("final_norm_g", dfg, True)], {}

    def flush():
        scat = [(k, a) for k, a, g in pending if not g]
        gath = [(k, a) for k, a, g in pending if g]
        pending.clear()
        comms = ([_Scatter([a for _, a in scat])] if scat else []) + ([_Gather([a for _, a in gath])] if gath else [])
        return [k for k, _ in scat + gath], comms[0] if len(comms) == 1 else _Both(*comms)

    def landed(keys, arrays):
        recv.update(zip(keys, arrays))

    for l in reversed(range(L)):
        z, h, br, cpre, merged, proj = saved[l]
        dproj, dbr, dzm = _merge_bwd(dx, proj, z, wbr[l], wou[l], f"merge_bwd{l}")
        for n in range(3):
            dwb, _ = _atb(br, n, dproj, n, 1, f"dwbranch{l}_{n}")
            pending.append((f"w_branch{l}_{n}", dwb.reshape(NDEV, D // NDEV, D), False))
        dwo, _ = _atb(merged[None], 0, dx[None], 0, 1, f"dwout{l}")
        pending.append((f"w_out{l}", dwo.reshape(NDEV, D // NDEV, D), False))
        keys, comm = flush() if l == 0 else (None, None)
        (dz, vecg, dbst, dws, dcw, dkv), got = _branch_bwd(
            z, dbr, cpre, dzm, kvs[l], wm[l], wmt[l], bst[l], ln_a[l], cwf[l], cvec[l], f"branch_bwd{l}", comm)
        if l == 0:
            landed(keys, got)
        dwk, _ = _atb(memn[l][None], 0, dkv[None], 0, NDEV, f"dwkv{l}")
        (_, dmg), _ = _rms_matmul_bwd(dkv, wkv[l], mem0, mem_norm_g[l], jnp.zeros((M, D), F32), f"kv_bwd{l}")
        rest = jnp.concatenate([dmg, vecg[0:2], vecg[2:5], dbst.T.reshape(1, D)], axis=0)
        pending += [(f"w_kv{l}", dwk, False),
                    (f"conv_w{l}", dcw[:CONV_K].reshape(CONV_K, NDEV, nbc).transpose(1, 0, 2), False),
                    (f"small{l}", rest, True), (f"w_s{l}", dws.reshape(N_GROUPS * CHUNK, CHUNK), True)]
        keys, comm = flush() if l == 0 else (None, None)
        dwi, got = _atb(h[None], 0, dz[None], 0, NDEV, f"dwin{l}", comm)
        if l == 0:
            landed(keys, got)
        pending.append((f"w_in{l}", dwi, False))
        keys, comm = flush() if l == 0 else (None, None)
        (dx, dng), got = _rms_matmul_bwd(dz, win[l], xs[l], norm_g[l], dx, f"inproj_bwd{l}", comm)
        if l == 0:
            landed(keys, got)
        pending.append((f"norm_g{l}", dng, True))
    grad_x = dx[None]

    def pack(p):
        rows = []
        for l in range(L):
            rows += [p["norm_g"][l], p["mem_norm_g"][l], p["gmlp_ln_g"][l], p["gmlp_ln_b"][l], p["conv_b"][l],
                     p["conv_ln_g"][l], p["conv_ln_b"][l], p["b_s"][l].reshape(D)]
        return jnp.stack(rows + [p["final_norm_g"]])[None]

    names = ["norm_g", "mem_norm_g", "gmlp_ln_g", "gmlp_ln_b", "conv_b", "conv_ln_g", "conv_ln_b", "b_s",
             "final_norm_g"]
    w_small = pack(dict(zip(names, [norm_g, mem_norm_g, gmlp_ln_g, gmlp_ln_b, conv_b, conv_ln_g, conv_ln_b,
                                    b_s, final_norm_g])))
    m_small = pack(dict(zip(names, [m_norm_g, m_mem_norm_g, m_gmlp_ln_g, m_gmlp_ln_b, m_conv_b, m_conv_ln_g,
                                    m_conv_ln_b, m_b_s, m_final_norm_g])))
    v_small = pack(dict(zip(names, [v_norm_g, v_mem_norm_g, v_gmlp_ln_g, v_gmlp_ln_b, v_conv_b, v_conv_ln_g,
                                    v_conv_ln_b, v_b_s, v_final_norm_g])))
    outs = {}

    def run(key, parts, w, m, v, idx, comm=None):
        outs[key], got = _adamw(parts, w, m, v, idx, "adamw_" + key, comm)
        return got

    keys, comm = flush()
    landed(keys, run("w_in0", recv["w_in0"], w_in, m_w_in, v_w_in, 0, comm))
    parts_small = jnp.concatenate([recv[f"{k}{l}"] for l in range(L) for k in ("norm_g", "small")]
                                  + [recv["final_norm_g"]], axis=1)
    parts_ws = jnp.concatenate([recv[f"w_s{l}"] for l in range(L)], axis=1)
    for l in range(L):
        if l > 0:
            run(f"w_in{l}", recv[f"w_in{l}"], w_in, m_w_in, v_w_in, l)
        run(f"w_kv{l}", recv[f"w_kv{l}"], w_kv, m_w_kv, v_w_kv, l)
        for n in range(3):
            sh = (L * 3, D // NDEV, D)
            run(f"w_branch{l}_{n}", recv[f"w_branch{l}_{n}"], w_branch.reshape(sh), m_w_branch.reshape(sh),
                v_w_branch.reshape(sh), l * 3 + n)
        run(f"w_out{l}", recv[f"w_out{l}"], w_out, m_w_out, v_w_out, l)
        run(f"conv_w{l}", recv[f"conv_w{l}"], conv_w, m_conv_w, v_conv_w, l)
    run("small", parts_small, w_small, m_small, v_small, 0)
    ws_shape = (1, L * N_GROUPS * CHUNK, CHUNK)
    run("w_s", parts_ws, w_s.reshape(ws_shape), m_w_s.reshape(ws_shape), v_w_s.reshape(ws_shape), 0)

    def leaf(name, k):
        if name in ("w_in", "w_kv", "w_out", "conv_w"):
            return jnp.stack([outs[f"{name}{l}"][k] for l in range(L)])
        if name == "w_branch":
            return jnp.stack([jnp.stack([outs[f"w_branch{l}_{n}"][k] for n in range(3)]) for l in range(L)])
        if name == "w_s":
            return outs["w_s"][k].reshape(L, N_GROUPS, CHUNK, CHUNK)
        sm = outs["small"][k]
        if name == "final_norm_g":
            return sm[8 * L]
        j = names.index(name)
        rows = jnp.stack([sm[8 * l + j] for l in range(L)])
        return rows.reshape(L, N_GROUPS, CHUNK) if name == "b_s" else rows

    order = ["norm_g", "mem_norm_g", "w_in", "gmlp_ln_g", "gmlp_ln_b", "w_s", "b_s", "conv_w", "conv_b",
             "conv_ln_g", "conv_ln_b", "w_kv", "w_branch", "w_out", "final_norm_g"]
    loss = lax.psum(loss_part[0, 0], ("x", "y", "c"))
    return (loss, grad_x, *[leaf(nm, k) for k in range(4) for nm in order])
```

```python
import functools
import math

import jax
import jax.numpy as jnp
from jax import lax
from jax.experimental import pallas as pl
from jax.experimental.pallas import tpu as pltpu

F32 = jnp.float32
MXU_DTYPE = jnp.bfloat16
ACT_DTYPE = jnp.bfloat16
GRAD_DTYPE = jnp.bfloat16

D = 1024
N_SEG = 11
N_IN = N_SEG * D
NDEV = 8
CHUNK = 128
N_GROUPS = 8
CONV_K = 31
HALO = 32
LANES = 128
STRIP = 32
HEADS = 4
HEAD_DIM = D // HEADS
RMS_EPS = 1e-6
LN_EPS = 1e-5
ADAM_LR, ADAM_B1, ADAM_B2, ADAM_EPS, ADAM_WD, ADAM_STEP = 0.001, 0.9, 0.999, 1e-08, 0.01, 10
SEG_AU, SEG_AV, SEG_AG, SEG_BA, SEG_BB, SEG_BG, SEG_CQ, SEG_CG, SEG_M = 0, 1, 2, 3, 4, 5, 6, 7, 8

VMEM_LIMIT = 60 * 1024 * 1024
MESH = pl.DeviceIdType.MESH
NT_DIMS = (((1,), (1,)), ((), ()))
TN_DIMS = (((0,), (0,)), ((), ()))


def _params(*sem):
    return pltpu.CompilerParams(dimension_semantics=sem, vmem_limit_bytes=VMEM_LIMIT)


def _tile(n, want):
    t = min(n, want)
    assert n % t == 0, (n, want)
    return t


def _mx(v):
    return v.astype(MXU_DTYPE)


def _gelu(x):
    t = jnp.tanh(0.7978845608028654 * (x + 0.044715 * x * x * x))
    return 0.5 * x * (1.0 + t), t


def _gelu_grad(x, t):
    return 0.5 * (1.0 + t) + 0.5 * x * (1.0 - t * t) * 0.7978845608028654 * (1.0 + 3.0 * 0.044715 * x * x)


def _silu_grad(x, s):
    return s * (1.0 + x * (1.0 - s))


def _ln_stats(v):
    mu = jnp.mean(v, axis=-1, keepdims=True)
    vc = v - mu
    rstd = lax.rsqrt(jnp.mean(vc * vc, axis=-1, keepdims=True) + LN_EPS)
    return vc * rstd, rstd


def _ln_grad(dy, g, vhat, rstd):
    dvh = dy * g
    return rstd * (dvh - jnp.mean(dvh, axis=-1, keepdims=True)
                   - vhat * jnp.mean(dvh * vhat, axis=-1, keepdims=True))


def _rowsum(v):
    return jnp.sum(v, axis=0, keepdims=True)


def _coords():
    return lax.axis_index("x"), lax.axis_index("y"), lax.axis_index("c")


def _flip(pos, d):
    x, y, c = pos
    return (1 - x if d & 4 else x, 1 - y if d & 2 else y, 1 - c if d & 1 else c)


def _slot(pos):
    return 4 * pos[0] + 2 * pos[1] + pos[2]


CHIP_FLIPS = (4, 2, 6)


class _Gather:
    def __init__(self, arrays, mid_frac=0.8):
        self.arrays = list(arrays)
        self.n = n = len(arrays)
        self.mid_frac = mid_frac
        self.out_shape = [jax.ShapeDtypeStruct((NDEV,) + a.shape, a.dtype) for a in arrays]
        self.scratch = [pltpu.SemaphoreType.DMA((n, 7)), pltpu.SemaphoreType.DMA((n, 7)),
                        pltpu.SemaphoreType.DMA((n,))]

    def _copy(self, refs, i, k, block, to, own=False):
        ins, outs, (send, recv, _) = refs
        slot = outs[i].at[_slot(block)]
        return pltpu.make_async_remote_copy(
            src_ref=ins[i] if own else slot, dst_ref=slot, send_sem=send.at[i, k], recv_sem=recv.at[i, k],
            device_id=to, device_id_type=MESH)

    def _local(self, refs, i):
        ins, outs, (_, _, loc) = refs
        return pltpu.make_async_copy(ins[i], outs[i].at[_slot(_coords())], loc.at[i])

    def _first(self, refs, i, k):
        me = _coords()
        return self._copy(refs, i, k, me, _flip(me, ((1,) + CHIP_FLIPS)[k]), own=True)

    def _passed(self, refs, i, j):
        me = _coords()
        return self._copy(refs, i, 4 + j, _flip(me, CHIP_FLIPS[j]), _flip(me, 1))

    def start(self, refs):
        for i in range(self.n):
            self._local(refs, i).start()
        for k in range(4):
            for i in range(self.n):
                self._first(refs, i, k).start()

    def forward(self, refs):
        me = _coords()
        for j, d in enumerate(CHIP_FLIPS):
            for i in range(self.n):
                self._copy(refs, i, 1 + j, _flip(me, d), me).wait_recv()
                self._passed(refs, i, j).start()

    def finish(self, refs):
        me = _coords()
        sib = _flip(me, 1)
        for i in range(self.n):
            self._copy(refs, i, 0, sib, me).wait_recv()
        for j, d in enumerate(CHIP_FLIPS):
            for i in range(self.n):
                self._copy(refs, i, 4 + j, _flip(sib, d), me).wait_recv()
        for i in range(self.n):
            for k in range(4):
                self._first(refs, i, k).wait_send()
            for j in range(3):
                self._passed(refs, i, j).wait_send()
            self._local(refs, i).wait()


class _Scatter:
    def __init__(self, arrays):
        self.arrays = list(arrays)
        self.n = n = len(arrays)
        self.mid_frac = None
        self.out_shape = [jax.ShapeDtypeStruct(a.shape, a.dtype) for a in arrays]
        self.scratch = [pltpu.SemaphoreType.DMA((n, 7)), pltpu.SemaphoreType.DMA((n, 7)),
                        pltpu.SemaphoreType.DMA((n,))]

    def _copy(self, refs, i, d, landing):
        ins, outs, (send, recv, _) = refs
        me = _coords()
        peer = _flip(me, d)
        return pltpu.make_async_remote_copy(
            src_ref=ins[i].at[_slot(peer)], dst_ref=outs[i].at[_slot(peer) if landing else _slot(me)],
            send_sem=send.at[i, d - 1], recv_sem=recv.at[i, d - 1], device_id=peer, device_id_type=MESH)

    def _local(self, refs, i):
        ins, outs, (_, _, loc) = refs
        me = _slot(_coords())
        return pltpu.make_async_copy(ins[i].at[me], outs[i].at[me], loc.at[i])

    def start(self, refs):
        for i in range(self.n):
            self._local(refs, i).start()
        for d in range(1, NDEV):
            for i in range(self.n):
                self._copy(refs, i, d, False).start()

    def forward(self, refs):
        pass

    def finish(self, refs):
        for d in range(1, NDEV):
            for i in range(self.n):
                self._copy(refs, i, d, True).wait_recv()
        for d in range(1, NDEV):
            for i in range(self.n):
                self._copy(refs, i, d, False).wait_send()
        for i in range(self.n):
            self._local(refs, i).wait()


class _Both:
    def __init__(self, a, b):
        self.parts = (a, b)
        self.arrays = a.arrays + b.arrays
        self.n = a.n + b.n
        self.mid_frac = a.mid_frac if a.mid_frac is not None else b.mid_frac
        self.out_shape = a.out_shape + b.out_shape
        self.scratch = a.scratch + b.scratch

    def _each(self, refs):
        ins, outs, sems = refs
        na, ns = self.parts[0].n, len(self.parts[0].scratch)
        return ((self.parts[0], (ins[:na], outs[:na], sems[:ns])), (self.parts[1], (ins[na:], outs[na:], sems[ns:])))

    def start(self, refs):
        for part, r in self._each(refs):
            part.start(r)

    def forward(self, refs):
        for part, r in self._each(refs):
            part.forward(r)

    def finish(self, refs):
        for part, r in self._each(refs):
            part.finish(r)


def _call(body, name, grid, in_specs, out_specs, out_shape, scratch, args, comm=None):
    params = _params(*(["arbitrary"] * len(grid)))
    if comm is None:
        outs = pl.pallas_call(
            body, name=name, grid=grid, in_specs=in_specs, out_specs=out_specs, out_shape=out_shape,
            scratch_shapes=scratch, compiler_params=params)(*args)
        return list(outs), []
    n_in, n_out, n_scr, k = len(in_specs), len(out_specs), len(scratch), comm.n
    nsteps = math.prod(grid) if grid else 1
    mid = min(nsteps - 1, int(nsteps * comm.mid_frac)) if comm.mid_frac is not None else None

    def hosted(*refs):
        ins, refs = refs[:n_in], refs[n_in:]
        cins, refs = refs[:k], refs[k:]
        outs, refs = refs[:n_out], refs[n_out:]
        couts, refs = refs[:k], refs[k:]
        scr, sems = refs[:n_scr], refs[n_scr:]
        crefs = (cins, couts, sems)
        if nsteps == 1:
            comm.start(crefs)
            body(*ins, *outs, *scr)
            comm.forward(crefs)
            comm.finish(crefs)
            return
        step = pl.program_id(0)
        for a in range(1, len(grid)):
            step = step * grid[a] + pl.program_id(a)
        pl.when(step == 0)(lambda: comm.start(crefs))
        if mid is not None:
            pl.when(step == mid)(lambda: comm.forward(crefs))
        body(*ins, *outs, *scr)
        pl.when(step == nsteps - 1)(lambda: comm.finish(crefs))

    any_spec = pl.BlockSpec(memory_space=pl.ANY)
    outs = pl.pallas_call(
        hosted, name=name, grid=grid,
        in_specs=list(in_specs) + [any_spec] * k, out_specs=list(out_specs) + [any_spec] * k,
        out_shape=list(out_shape) + comm.out_shape, scratch_shapes=list(scratch) + comm.scratch,
        compiler_params=params)(*args, *comm.arrays)
    return list(outs[:n_out]), list(outs[n_out:])


def _exchange(comm, name):
    return _call(lambda: None, name, (), [], [], [], [], [], comm)[1]


def _rms_matmul(x, g, w, name, comm=None):
    T = x.shape[0]
    nb = w.shape[2]
    tT = _tile(T, 1024)
    per = 2

    def body(x_ref, g_ref, w_ref, z_ref, h_ref):
        @pl.when(pl.program_id(1) == 0)
        def _():
            xf = x_ref[...]
            r = lax.rsqrt(jnp.mean(xf * xf, axis=-1, keepdims=True) + RMS_EPS)
            h_ref[...] = (xf * r * g_ref[...]).astype(h_ref.dtype)

        for j in range(per):
            z_ref[:, j * nb:(j + 1) * nb] = jnp.dot(h_ref[...], w_ref[j], preferred_element_type=F32
                                                    ).astype(z_ref.dtype)

    return _call(
        body, name, (T // tT, NDEV // per),
        [pl.BlockSpec((tT, D), lambda t, n: (t, 0)),
         pl.BlockSpec((1, D), lambda t, n: (0, 0)),
         pl.BlockSpec((per, D, nb), lambda t, n: (n, 0, 0))],
        [pl.BlockSpec((tT, per * nb), lambda t, n: (t, n)),
         pl.BlockSpec((tT, D), lambda t, n: (t, 0))],
        [jax.ShapeDtypeStruct((T, NDEV * nb), ACT_DTYPE), jax.ShapeDtypeStruct((T, D), MXU_DTYPE)],
        [], (x, g.reshape(1, D), w), comm)


def _rms_matmul_bwd(dz, w, x, g, dxo, name, comm=None):
    T = x.shape[0]
    nb = w.shape[2]
    tT = _tile(T, 512)
    per = 4
    steps = NDEV // per

    def body(dz_ref, w_ref, x_ref, g_ref, dxo_ref, dx_ref, dg_ref, acc):
        t, n = pl.program_id(0), pl.program_id(1)

        @pl.when((n == 0) & (t == 0))
        def _():
            dg_ref[...] = jnp.zeros_like(dg_ref)

        part = None
        for j in range(per):
            d = lax.dot_general(_mx(dz_ref[:, j * nb:(j + 1) * nb]), w_ref[j], NT_DIMS, preferred_element_type=F32)
            part = d if part is None else part + d

        @pl.when(n == 0)
        def _():
            acc[...] = part

        @pl.when(n > 0)
        def _():
            acc[...] += part

        @pl.when(n == steps - 1)
        def _():
            xf = x_ref[...]
            r = lax.rsqrt(jnp.mean(xf * xf, axis=-1, keepdims=True) + RMS_EPS)
            xh = xf * r
            dh = acc[...]
            dxh = dh * g_ref[...]
            dx_ref[...] = dxo_ref[...] + r * (dxh - xh * jnp.mean(dxh * xh, axis=-1, keepdims=True))
            dg_ref[...] += _rowsum(dh * xh)

    return _call(
        body, name, (T // tT, steps),
        [pl.BlockSpec((tT, per * nb), lambda t, n: (t, n)),
         pl.BlockSpec((per, D, nb), lambda t, n: (n, 0, 0)),
         pl.BlockSpec((tT, D), lambda t, n: (t, 0)),
         pl.BlockSpec((1, D), lambda t, n: (0, 0)),
         pl.BlockSpec((tT, D), lambda t, n: (t, 0))],
        [pl.BlockSpec((tT, D), lambda t, n: (t, 0)),
         pl.BlockSpec((1, D), lambda t, n: (0, 0))],
        [jax.ShapeDtypeStruct((T, D), F32), jax.ShapeDtypeStruct((1, D), F32)],
        [pltpu.VMEM((tT, D), F32)], (dz, w, x, g.reshape(1, D), dxo), comm)


def _atb(a, ai, b, bi, nblk, name, comm=None):
    T, M = a.shape[1:]
    N = b.shape[2]
    nb = N // nblk
    tk = _tile(T, 2048)
    nk = T // tk

    def body(a_ref, b_ref, o_ref, acc):
        k = pl.program_id(1)

        @pl.when(k == 0)
        def _():
            acc[...] = jnp.zeros_like(acc)

        acc[...] += lax.dot_general(_mx(a_ref[...]), _mx(b_ref[...]), TN_DIMS, preferred_element_type=F32)

        @pl.when(k == nk - 1)
        def _():
            o_ref[...] = acc[...].astype(o_ref.dtype)

    outs, couts = _call(
        body, name, (nblk, nk),
        [pl.BlockSpec((None, tk, M), lambda n, k: (ai, k, 0)),
         pl.BlockSpec((None, tk, nb), lambda n, k: (bi, k, n))],
        [pl.BlockSpec((None, M, nb), lambda n, k: (n, 0, 0))],
        [jax.ShapeDtypeStruct((nblk, M, nb), GRAD_DTYPE)],
        [pltpu.VMEM((M, nb), F32)], (a, b), comm)
    return outs[0], couts


def _loss_head(x, tgt, g, name):
    T = x.shape[0]
    tT = _tile(T, 512)

    def body(x_ref, t_ref, g_ref, loss_ref, dx_ref, dg_ref):
        @pl.when(pl.program_id(0) == 0)
        def _():
            loss_ref[...] = jnp.zeros_like(loss_ref)
            dg_ref[...] = jnp.zeros_like(dg_ref)

        xf = x_ref[...]
        r = lax.rsqrt(jnp.mean(xf * xf, axis=-1, keepdims=True) + RMS_EPS)
        xh = xf * r
        err = xh * g_ref[...] - t_ref[...]
        loss_ref[...] += 0.5 * jnp.sum(jnp.mean(err * err, axis=-1, keepdims=True), axis=0, keepdims=True)
        dy = err * (1.0 / D)
        dxh = dy * g_ref[...]
        dx_ref[...] = r * (dxh - xh * jnp.mean(dxh * xh, axis=-1, keepdims=True))
        dg_ref[...] += _rowsum(dy * xh)

    return pl.pallas_call(
        body, name=name, grid=(T // tT,),
        in_specs=[pl.BlockSpec((tT, D), lambda t: (t, 0)),
                  pl.BlockSpec((tT, D), lambda t: (t, 0)),
                  pl.BlockSpec((1, D), lambda t: (0, 0))],
        out_specs=[pl.BlockSpec((1, 1), lambda t: (0, 0)),
                   pl.BlockSpec((tT, D), lambda t: (t, 0)),
                   pl.BlockSpec((1, D), lambda t: (0, 0))],
        out_shape=[jax.ShapeDtypeStruct((1, 1), F32), jax.ShapeDtypeStruct((T, D), F32),
                   jax.ShapeDtypeStruct((1, D), F32)],
        compiler_params=_params("arbitrary"),
    )(x, tgt, g.reshape(1, D))


def _spatial_gate(wm_ref, bst_ref, vb_ref, sv_ref, n_chunks):
    for c in range(n_chunks):
        rows = slice(c * CHUNK, (c + 1) * CHUNK)
        for g in range(N_GROUPS):
            cols = slice(g * CHUNK, (g + 1) * CHUNK)
            sv_ref[rows, cols] = (jnp.dot(wm_ref[g], vb_ref[rows, cols], preferred_element_type=F32)
                                  + bst_ref[:, g:g + 1])


def _lane_loop(fn):
    def step(i, carry):
        fn(pl.ds(pl.multiple_of(i * LANES, LANES), LANES))
        return carry

    lax.fori_loop(0, D // LANES, step, 0)


def _shifted_copies(buf, sh, n):
    for s in range(1, 8):
        sh[s - 1, 0:n, :] = buf[s:s + n, :]


def _window(buf, sh, base, off, lanes):
    a, s = divmod(off, 8)
    src = buf if s == 0 else sh.at[s - 1]
    return src[base + 8 * a:base + 8 * a + 8, lanes]


def _softmax_rows(s):
    e = jnp.exp(s - jnp.max(s, axis=-1, keepdims=True))
    return e / jnp.sum(e, axis=-1, keepdims=True)


def _branch_fwd(z, kv, wm, bst, ln_a, cw, cvec, name, comm=None):
    T = z.shape[0]
    tT = _tile(T, 256)
    n_chunks = tT // CHUNK

    def body(z_ref, kv_ref, wm_ref, bst_ref, lna_ref, cw_ref, cvec_ref, br_ref, c_ref, gbuf, gsh, vb, ua):
        @pl.when(pl.program_id(0) == 0)
        def _():
            gbuf[0:HALO, :] = jnp.zeros((HALO, D), F32)

        def seg(s):
            return z_ref[:, s * D:(s + 1) * D].astype(F32)

        u, _ = _gelu(seg(SEG_AU))
        zg = seg(SEG_AG)
        ua[...] = u * (zg * jax.nn.sigmoid(zg))
        gv, _ = _gelu(seg(SEG_AV))
        vhat, _ = _ln_stats(gv)
        vb[...] = _mx(vhat * lna_ref[0:1, :] + lna_ref[1:2, :])
        for c in range(n_chunks):
            rows = slice(c * CHUNK, (c + 1) * CHUNK)
            for g in range(N_GROUPS):
                cols = slice(g * CHUNK, (g + 1) * CHUNK)
                sv = jnp.dot(wm_ref[g], vb[rows, cols], preferred_element_type=F32) + bst_ref[:, g:g + 1]
                br_ref[0, rows, cols] = (sv * ua[rows, cols]).astype(br_ref.dtype)

        gbuf[HALO:HALO + tT, :] = seg(SEG_BA) * jax.nn.sigmoid(seg(SEG_BB))
        _shifted_copies(gbuf, gsh, tT + HALO - 8)
        def conv_lanes(lanes):
            taps = [jnp.broadcast_to(cw_ref[k:k + 1, lanes], (8, LANES)) for k in range(CONV_K)]
            bias = jnp.broadcast_to(cvec_ref[0:1, lanes], (8, LANES))
            for base in range(0, tT, 8):
                acc = [bias, None, None, None]
                for k in range(CONV_K):
                    term = taps[k] * _window(gbuf, gsh, base, k + HALO - CONV_K + 1, lanes)
                    acc[k % 4] = term if acc[k % 4] is None else acc[k % 4] + term
                c_ref[base:base + 8, lanes] = (acc[0] + acc[1]) + (acc[2] + acc[3])

        _lane_loop(conv_lanes)
        gbuf[0:HALO, :] = gbuf[tT:tT + HALO, :]
        chat, _ = _ln_stats(c_ref[...])
        cl = chat * cvec_ref[1:2, :] + cvec_ref[2:3, :]
        zg = seg(SEG_BG)
        br_ref[1] = (cl * jax.nn.sigmoid(cl) * (zg * jax.nn.sigmoid(zg))).astype(br_ref.dtype)

        for h in range(HEADS):
            cols = slice(h * HEAD_DIM, (h + 1) * HEAD_DIM)
            q = _mx(z_ref[:, SEG_CQ * D + h * HEAD_DIM:SEG_CQ * D + (h + 1) * HEAD_DIM])
            s = lax.dot_general(q, kv_ref[:, cols], NT_DIMS, preferred_element_type=F32)
            p = _softmax_rows(s * (1.0 / math.sqrt(HEAD_DIM)))
            att = jnp.dot(_mx(p), kv_ref[:, D + h * HEAD_DIM:D + (h + 1) * HEAD_DIM], preferred_element_type=F32)
            zg = z_ref[:, SEG_CG * D + h * HEAD_DIM:SEG_CG * D + (h + 1) * HEAD_DIM].astype(F32)
            br_ref[2, :, cols] = (att * (zg * jax.nn.sigmoid(zg))).astype(br_ref.dtype)

    full = lambda shape: pl.BlockSpec(shape, lambda t: (0,) * len(shape))
    return _call(
        body, name, (T // tT,),
        [pl.BlockSpec((tT, SEG_M * D), lambda t: (t, 0)),
         full(kv.shape), full(wm.shape), full(bst.shape), full(ln_a.shape), full(cw.shape), full(cvec.shape)],
        [pl.BlockSpec((3, tT, D), lambda t: (0, t, 0)), pl.BlockSpec((tT, D), lambda t: (t, 0))],
        [jax.ShapeDtypeStruct((3, T, D), MXU_DTYPE), jax.ShapeDtypeStruct((T, D), F32)],
        [pltpu.VMEM((tT + HALO, D), F32), pltpu.VMEM((7, tT + HALO - 8, D), F32),
         pltpu.VMEM((tT, D), MXU_DTYPE), pltpu.VMEM((tT, D), F32)],
        (z, kv, wm, bst, ln_a, cw, cvec), comm)


def _branch_bwd(z, dbr, c, dzm, kv, wm, wmt, bst, ln_a, cw, cvec, name, comm=None):
    T = z.shape[0]
    M = kv.shape[0]
    tT = _tile(T, 128)
    nT = T // tT
    n_chunks = tT // CHUNK

    def body(z_ref, dbr_ref, c_ref, dzm_ref, kv_ref, wm_ref, wmt_ref, bst_ref, lna_ref,
             cw_ref, cvec_ref, dz_ref, vecg_ref, dbst_ref, dws_ref, dcw_ref, dkv_ref,
             gbuf, dcbuf, vb, dsvb, sv, dvbuf, dcsh, dglu, vh, gq, dcw8, dcw_step):
        i = pl.program_id(0)

        @pl.when(i == 0)
        def _():
            vecg_ref[...] = jnp.zeros_like(vecg_ref)
            dbst_ref[...] = jnp.zeros_like(dbst_ref)
            dws_ref[...] = jnp.zeros_like(dws_ref)
            dcw8[...] = jnp.zeros_like(dcw8)
            dkv_ref[...] = jnp.zeros_like(dkv_ref)
            dcbuf[tT:tT + HALO, :] = jnp.zeros((HALO, D), F32)

        strips = [slice(r0, r0 + STRIP) for r0 in range(0, tT, STRIP)]

        def seg(r, s):
            return z_ref[r, s * D:(s + 1) * D].astype(F32)

        def put(r, s, val):
            dz_ref[r, s * D:(s + 1) * D] = val.astype(dz_ref.dtype)

        for r in strips:
            zv = seg(r, SEG_AV)
            gv, tv = _gelu(zv)
            vhat, rstd = _ln_stats(gv)
            vb[r, :] = _mx(vhat * lna_ref[0:1, :] + lna_ref[1:2, :])
            vh[r, :] = vhat
            gq[r, :] = rstd * _gelu_grad(zv, tv)
        _spatial_gate(wm_ref, bst_ref, vb, sv, n_chunks)
        for r in strips:
            zu, zg = seg(r, SEG_AU), seg(r, SEG_AG)
            u, tu = _gelu(zu)
            sg = jax.nn.sigmoid(zg)
            d_a = dbr_ref[0, r, :].astype(F32)
            put(r, SEG_AU, d_a * sv[r, :] * (zg * sg) * _gelu_grad(zu, tu))
            put(r, SEG_AG, d_a * u * sv[r, :] * _silu_grad(zg, sg))
            dsv = d_a * u * (zg * sg)
            dsvb[r, :] = _mx(dsv)
            in_chunk = slice(r.start % CHUNK, r.start % CHUNK + STRIP)
            for g in range(N_GROUPS):
                dbst_ref[in_chunk, g:g + 1] += jnp.sum(dsv[:, g * CHUNK:(g + 1) * CHUNK], axis=-1, keepdims=True)
        tril = (lax.broadcasted_iota(jnp.int32, (CHUNK, CHUNK), 0)
                >= lax.broadcasted_iota(jnp.int32, (CHUNK, CHUNK), 1))
        for g in range(N_GROUPS):
            cols = slice(g * CHUNK, (g + 1) * CHUNK)
            for cc in range(n_chunks):
                rows = slice(cc * CHUNK, (cc + 1) * CHUNK)
                dws = lax.dot_general(dsvb[rows, cols], vb[rows, cols], NT_DIMS, preferred_element_type=F32)
                dws_ref[g] += jnp.where(tril, dws, 0.0)
                dvbuf[rows, cols] = jnp.dot(wmt_ref[g], dsvb[rows, cols], preferred_element_type=F32)
        for r in strips:
            dv, vhat = dvbuf[r, :], vh[r, :]
            vecg_ref[0:1, :] += _rowsum(dv * vhat)
            vecg_ref[1:2, :] += _rowsum(dv)
            dvh = dv * lna_ref[0:1, :]
            put(r, SEG_AV, (dvh - jnp.mean(dvh, axis=-1, keepdims=True)
                            - vhat * jnp.mean(dvh * vhat, axis=-1, keepdims=True)) * gq[r, :])

        sgb_buf = sv
        for r in strips:
            za, zg = seg(r, SEG_BA), seg(r, SEG_BG)
            sgb = jax.nn.sigmoid(seg(r, SEG_BB))
            sgb_buf[r, :] = sgb
            gbuf[r, :] = za * sgb
            chat, crstd = _ln_stats(c_ref[r, :])
            cl = chat * cvec_ref[1:2, :] + cvec_ref[2:3, :]
            scl = jax.nn.sigmoid(cl)
            sg = jax.nn.sigmoid(zg)
            d_b = dbr_ref[1, r, :].astype(F32)
            put(r, SEG_BG, d_b * (cl * scl) * _silu_grad(zg, sg))
            dcl = d_b * (zg * sg) * _silu_grad(cl, scl)
            vecg_ref[3:4, :] += _rowsum(dcl * chat)
            vecg_ref[4:5, :] += _rowsum(dcl)
            dc = _ln_grad(dcl, cvec_ref[1:2, :], chat, crstd)
            vecg_ref[2:3, :] += _rowsum(dc)
            dcbuf[r, :] = dc
        _shifted_copies(dcbuf, dcsh, tT + HALO - 8)

        def conv_grads(lanes):
            taps = [jnp.broadcast_to(cw_ref[k:k + 1, lanes], (8, LANES)) for k in range(CONV_K)]
            wsum = [None] * CONV_K
            for base in range(0, tT, 8):
                glu = gbuf[base:base + 8, lanes]
                acc = [None] * 4
                for k in range(CONV_K):
                    win = _window(dcbuf, dcsh, base, CONV_K - 1 - k, lanes)
                    term = taps[k] * win
                    acc[k % 4] = term if acc[k % 4] is None else acc[k % 4] + term
                    term = glu * win
                    wsum[k] = term if wsum[k] is None else wsum[k] + term
                dglu[base:base + 8, lanes] = (acc[0] + acc[1]) + (acc[2] + acc[3])
            for k in range(CONV_K):
                dcw_step[8 * k:8 * k + 8, lanes] = wsum[k]

        _lane_loop(conv_grads)
        dcw8[...] += dcw_step[...]

        @pl.when(i == nT - 1)
        def _():
            for k in range(CONV_K):
                dcw_ref[k:k + 1, :] = _rowsum(dcw8[8 * k:8 * k + 8, :])
            dcw_ref[CONV_K:HALO, :] = jnp.zeros((HALO - CONV_K, D), F32)

        dcbuf[tT:tT + HALO, :] = dcbuf[0:HALO, :]
        for r in strips:
            dg, sgb = dglu[r, :], sgb_buf[r, :]
            put(r, SEG_BA, dg * sgb)
            put(r, SEG_BB, dg * seg(r, SEG_BA) * sgb * (1.0 - sgb))

        scale = 1.0 / math.sqrt(HEAD_DIM)
        for h in range(HEADS):
            cols = slice(h * HEAD_DIM, (h + 1) * HEAD_DIM)
            qcols = slice(SEG_CQ * D + h * HEAD_DIM, SEG_CQ * D + (h + 1) * HEAD_DIM)
            gcols = slice(SEG_CG * D + h * HEAD_DIM, SEG_CG * D + (h + 1) * HEAD_DIM)
            vcols = slice(D + h * HEAD_DIM, D + (h + 1) * HEAD_DIM)
            q = _mx(z_ref[:, qcols])
            kh, vh = kv_ref[:, cols], kv_ref[:, vcols]
            p = _softmax_rows(lax.dot_general(q, kh, NT_DIMS, preferred_element_type=F32) * scale)
            pb = _mx(p)
            att = jnp.dot(pb, vh, preferred_element_type=F32)
            zg = z_ref[:, gcols].astype(F32)
            sg = jax.nn.sigmoid(zg)
            d_c = dbr_ref[2, :, cols].astype(F32)
            dz_ref[:, gcols] = (d_c * att * _silu_grad(zg, sg)).astype(dz_ref.dtype)
            datt = _mx(d_c * (zg * sg))
            dp = lax.dot_general(datt, vh, NT_DIMS, preferred_element_type=F32)
            dkv_ref[:, vcols] += lax.dot_general(pb, datt, TN_DIMS, preferred_element_type=F32)
            ds = _mx(p * (dp - jnp.sum(dp * p, axis=-1, keepdims=True)) * scale)
            dz_ref[:, qcols] = jnp.dot(ds, kh, preferred_element_type=F32).astype(dz_ref.dtype)
            dkv_ref[:, cols] += lax.dot_general(ds, q, TN_DIMS, preferred_element_type=F32)

        dz_ref[:, SEG_M * D:] = dzm_ref[...].astype(dz_ref.dtype)

    rev = lambda i: nT - 1 - i
    full = lambda shape: pl.BlockSpec(shape, lambda i: (0,) * len(shape))
    return _call(
        body, name, (nT,),
        [pl.BlockSpec((tT, SEG_M * D), lambda i: (rev(i), 0)),
         pl.BlockSpec((3, tT, D), lambda i: (0, rev(i), 0)),
         pl.BlockSpec((tT, D), lambda i: (rev(i), 0)),
         pl.BlockSpec((tT, 3 * D), lambda i: (rev(i), 0)),
         full(kv.shape), full(wm.shape), full(wmt.shape), full(bst.shape), full(ln_a.shape),
         full(cw.shape), full(cvec.shape)],
        [pl.BlockSpec((tT, N_IN), lambda i: (rev(i), 0)),
         full((8, D)), full((CHUNK, N_GROUPS)), full((N_GROUPS, CHUNK, CHUNK)), full((HALO, D)),
         full((M, 2 * D))],
        [jax.ShapeDtypeStruct((T, N_IN), MXU_DTYPE), jax.ShapeDtypeStruct((8, D), F32),
         jax.ShapeDtypeStruct((CHUNK, N_GROUPS), F32),
         jax.ShapeDtypeStruct((N_GROUPS, CHUNK, CHUNK), F32),
         jax.ShapeDtypeStruct((HALO, D), F32), jax.ShapeDtypeStruct((M, 2 * D), F32)],
        [pltpu.VMEM((tT, D), F32), pltpu.VMEM((tT + HALO, D), F32),
         pltpu.VMEM((tT, D), MXU_DTYPE), pltpu.VMEM((tT, D), MXU_DTYPE),
         pltpu.VMEM((tT, D), F32), pltpu.VMEM((tT, D), F32),
         pltpu.VMEM((7, tT + HALO - 8, D), F32),
         pltpu.VMEM((tT, D), F32), pltpu.VMEM((tT, D), F32), pltpu.VMEM((tT, D), F32),
         pltpu.VMEM((CONV_K * 8, D), F32), pltpu.VMEM((CONV_K * 8, D), F32)],
        (z, dbr, c, dzm, kv, wm, wmt, bst, ln_a, cw, cvec), comm)


def _merge_fwd(br, z, x, wb, wo, name):
    T = x.shape[0]
    tT = _tile(T, 512)

    def body(br_ref, z0, z1, z2, x_ref, wb_ref, wo_ref, xn_ref, mg_ref, pj_ref):
        merged = jnp.zeros((tT, D), F32)
        for n, zm in enumerate((z0, z1, z2)):
            proj = jnp.dot(br_ref[n], wb_ref[:, n].reshape(D, D), preferred_element_type=F32)
            pj_ref[n] = proj.astype(pj_ref.dtype)
            merged = merged + jax.nn.sigmoid(zm[...].astype(F32)) * proj
        mg_ref[...] = merged.astype(mg_ref.dtype)
        xn_ref[...] = x_ref[...] + jnp.dot(_mx(merged), wo_ref[...].reshape(D, D), preferred_element_type=F32)

    zspec = lambda n: pl.BlockSpec((tT, D), lambda t: (t, SEG_M + n))
    return pl.pallas_call(
        body, name=name, grid=(T // tT,),
        in_specs=[pl.BlockSpec((3, tT, D), lambda t: (0, t, 0)), zspec(0), zspec(1), zspec(2),
                  pl.BlockSpec((tT, D), lambda t: (t, 0)),
                  pl.BlockSpec(wb.shape, lambda t: (0, 0, 0, 0)),
                  pl.BlockSpec(wo.shape, lambda t: (0, 0, 0))],
        out_specs=[pl.BlockSpec((tT, D), lambda t: (t, 0)),
                   pl.BlockSpec((tT, D), lambda t: (t, 0)),
                   pl.BlockSpec((3, tT, D), lambda t: (0, t, 0))],
        out_shape=[jax.ShapeDtypeStruct((T, D), F32), jax.ShapeDtypeStruct((T, D), MXU_DTYPE),
                   jax.ShapeDtypeStruct((3, T, D), ACT_DTYPE)],
        compiler_params=_params("parallel"),
    )(br, z, z, z, x, wb, wo)


def _merge_bwd(dxo, proj, z, wb, wo, name):
    T = dxo.shape[0]
    tT = _tile(T, 512)

    def body(dxo_ref, pj_ref, z0, z1, z2, wb_ref, wo_ref, dpj_ref, dbr_ref, dzm_ref):
        dmerged = lax.dot_general(_mx(dxo_ref[...]), wo_ref[...].reshape(D, D), NT_DIMS,
                                  preferred_element_type=F32)
        for n, zm in enumerate((z0, z1, z2)):
            gate = jax.nn.sigmoid(zm[...].astype(F32))
            dproj = _mx(gate * dmerged)
            dpj_ref[n] = dproj
            dzm_ref[:, n * D:(n + 1) * D] = (pj_ref[n].astype(F32) * dmerged * gate * (1.0 - gate)
                                             ).astype(dzm_ref.dtype)
            dbr_ref[n] = lax.dot_general(dproj, wb_ref[:, n].reshape(D, D), NT_DIMS,
                                         preferred_element_type=F32).astype(dbr_ref.dtype)

    zspec = lambda n: pl.BlockSpec((tT, D), lambda t: (t, SEG_M + n))
    return pl.pallas_call(
        body, name=name, grid=(T // tT,),
        in_specs=[pl.BlockSpec((tT, D), lambda t: (t, 0)),
                  pl.BlockSpec((3, tT, D), lambda t: (0, t, 0)), zspec(0), zspec(1), zspec(2),
                  pl.BlockSpec(wb.shape, lambda t: (0, 0, 0, 0)),
                  pl.BlockSpec(wo.shape, lambda t: (0, 0, 0))],
        out_specs=[pl.BlockSpec((3, tT, D), lambda t: (0, t, 0)),
                   pl.BlockSpec((3, tT, D), lambda t: (0, t, 0)),
                   pl.BlockSpec((tT, 3 * D), lambda t: (t, 0))],
        out_shape=[jax.ShapeDtypeStruct((3, T, D), MXU_DTYPE), jax.ShapeDtypeStruct((3, T, D), ACT_DTYPE),
                   jax.ShapeDtypeStruct((T, 3 * D), MXU_DTYPE)],
        compiler_params=_params("parallel"),
    )(dxo, proj, z, z, z, wb, wo)


def _adamw(parts, w, m, v, idx, name, comm=None):
    R, C = parts.shape[1:]
    tr = 128 if R % 128 == 0 else R
    c1 = 1.0 / (1.0 - ADAM_B1 ** ADAM_STEP)
    c2 = 1.0 / (1.0 - ADAM_B2 ** ADAM_STEP)

    def body(p_ref, w_ref, m_ref, v_ref, g_out, d_out, m_out, v_out):
        g = p_ref[0].astype(F32)
        for p in range(1, NDEV):
            g = g + p_ref[p].astype(F32)
        mn = ADAM_B1 * m_ref[...] + (1.0 - ADAM_B1) * g
        vn = ADAM_B2 * v_ref[...] + (1.0 - ADAM_B2) * (g * g)
        g_out[...] = g
        m_out[...] = mn
        v_out[...] = vn
        d_out[...] = -ADAM_LR * ((mn * c1) / (jnp.sqrt(vn * c2) + ADAM_EPS) + ADAM_WD * w_ref[...])

    wspec = pl.BlockSpec((None, tr, C), lambda r: (idx, r, 0))
    ospec = pl.BlockSpec((tr, C), lambda r: (r, 0))
    return _call(
        body, name, (R // tr,),
        [pl.BlockSpec((NDEV, tr, C), lambda r: (0, r, 0)), wspec, wspec, wspec],
        [ospec] * 4, [jax.ShapeDtypeStruct((R, C), F32)] * 4, [], (parts, w, m, v), comm)


def kernel(x, mem, norm_g, mem_norm_g, w_in, gmlp_ln_g, gmlp_ln_b, w_s, b_s, conv_w, conv_b, conv_ln_g, conv_ln_b, w_kv, w_branch, w_out, final_norm_g, loss_target, m_norm_g, m_mem_norm_g, m_w_in, m_gmlp_ln_g, m_gmlp_ln_b, m_w_s, m_b_s, m_conv_w, m_conv_b, m_conv_ln_g, m_conv_ln_b, m_w_kv, m_w_branch, m_w_out, m_final_norm_g, v_norm_g, v_mem_norm_g, v_w_in, v_gmlp_ln_g, v_gmlp_ln_b, v_w_s, v_b_s, v_conv_w, v_conv_b, v_conv_ln_g, v_conv_ln_b, v_w_kv, v_w_branch, v_w_out, v_final_norm_g):
    L = w_in.shape[0]
    x0, mem0, tgt = x[0], mem[0], loss_target[0]
    T, M = x0.shape[0], mem0.shape[0]
    nbi, nbk, nbc = w_in.shape[2], w_kv.shape[2], conv_w.shape[2]

    def shards(l):
        return [_mx(w_in[l]), _mx(w_kv[l]), _mx(w_branch[l]), _mx(w_out[l]), conv_w[l]]

    first = _exchange(_Gather(shards(0)[:1]), "gather_first")
    gather_rest = _Gather(shards(0)[1:])
    gather_upper = _Gather([a for l in range(1, L) for a in shards(l)], mid_frac=0.85) if L > 1 else None

    tril = jnp.tril(jnp.ones((CHUNK, CHUNK), bool))
    wm = [_mx(jnp.where(tril[None], w_s[l], 0.0)) for l in range(L)]
    wmt = [w.transpose(0, 2, 1) for w in wm]
    bst = [b_s[l].T for l in range(L)]
    ln_a = [jnp.stack([gmlp_ln_g[l], gmlp_ln_b[l]]) for l in range(L)]
    cvec = [jnp.stack([conv_b[l], conv_ln_g[l], conv_ln_b[l]]) for l in range(L)]

    def conv_taps(gathered):
        return jnp.pad(gathered.transpose(1, 0, 2).reshape(CONV_K, D), ((0, HALO - CONV_K), (0, 0)))

    win, wkv, wbr, wou, cwf = [first[0]], [], [], [], []
    memn, kvs, xs, saved = [], [], [x0], []
    for l in range(L):
        (z, h), full = _rms_matmul(xs[l], norm_g[l], win[l], f"inproj_fwd{l}", gather_rest if l == 0 else None)
        if l == 0:
            wkv, wbr, wou, cwf = [full[0]], [full[1]], [full[2]], [conv_taps(full[3])]
        (kv, mn), _ = _rms_matmul(mem0, mem_norm_g[l], wkv[l], f"kv_fwd{l}")
        kvs.append(_mx(kv))
        memn.append(mn)
        (br, cpre), full = _branch_fwd(z, kvs[l], wm[l], bst[l], ln_a[l], cwf[l], cvec[l], f"branch_fwd{l}",
                                       gather_upper if l == 0 else None)
        for k in range(1, L if l == 0 else 0):
            f = full[5 * (k - 1):5 * k]
            win.append(f[0])
            wkv.append(f[1])
            wbr.append(f[2])
            wou.append(f[3])
            cwf.append(conv_taps(f[4]))
        xn, merged, proj = _merge_fwd(br, z, xs[l], wbr[l], wou[l], f"merge_fwd{l}")
        xs.append(xn)
        saved.append((z, h, br, cpre, merged, proj))
    loss_part, dx, dfg = _loss_head(xs[L], tgt, final_norm_g, "loss_head")

    pending, recv = [("final_norm_g", dfg, True)], {}

    def flush():
        scat = [(k, a) for k, a, g in pending if not g]
        gath = [(k, a) for k, a, g in pending if g]
        pending.clear()
        comms = ([_Scatter([a for _, a in scat])] if scat else []) + ([_Gather([a for _, a in gath])] if gath else [])
        return [k for k, _ in scat + gath], comms[0] if len(comms) == 1 else _Both(*comms)

    def landed(keys, arrays):
        recv.update(zip(keys, arrays))

    for l in reversed(range(L)):
        z, h, br, cpre, merged, proj = saved[l]
        dproj, dbr, dzm = _merge_bwd(dx, proj, z, wbr[l], wou[l], f"merge_bwd{l}")
        for n in range(3):
            dwb, _ = _atb(br, n, dproj, n, 1, f"dwbranch{l}_{n}")
            pending.append((f"w_branch{l}_{n}", dwb.reshape(NDEV, D // NDEV, D), False))
        dwo, _ = _atb(merged[None], 0, dx[None], 0, 1, f"dwout{l}")
        pending.append((f"w_out{l}", dwo.reshape(NDEV, D // NDEV, D), False))
        keys, comm = flush()
        (dz, vecg, dbst, dws, dcw, dkv), got = _branch_bwd(
            z, dbr, cpre, dzm, kvs[l], wm[l], wmt[l], bst[l], ln_a[l], cwf[l], cvec[l], f"branch_bwd{l}", comm)
        landed(keys, got)
        dwk, _ = _atb(memn[l][None], 0, dkv[None], 0, NDEV, f"dwkv{l}")
        (_, dmg), _ = _rms_matmul_bwd(dkv, wkv[l], mem0, mem_norm_g[l], jnp.zeros((M, D), F32), f"kv_bwd{l}")
        rest = jnp.concatenate([dmg, vecg[0:2], vecg[2:5], dbst.T.reshape(1, D)], axis=0)
        pending += [(f"w_kv{l}", dwk, False),
                    (f"conv_w{l}", dcw[:CONV_K].reshape(CONV_K, NDEV, nbc).transpose(1, 0, 2), False),
                    (f"small{l}", rest, True), (f"w_s{l}", dws.reshape(N_GROUPS * CHUNK, CHUNK), True)]
        keys, comm = flush()
        dwi, got = _atb(h[None], 0, dz[None], 0, NDEV, f"dwin{l}", comm)
        landed(keys, got)
        pending.append((f"w_in{l}", dwi, False))
        keys, comm = flush()
        (dx, dng), got = _rms_matmul_bwd(dz, win[l], xs[l], norm_g[l], dx, f"inproj_bwd{l}", comm)
        landed(keys, got)
        pending.append((f"norm_g{l}", dng, True))
    grad_x = dx[None]

    def pack(p):
        rows = []
        for l in range(L):
            rows += [p["norm_g"][l], p["mem_norm_g"][l], p["gmlp_ln_g"][l], p["gmlp_ln_b"][l], p["conv_b"][l],
                     p["conv_ln_g"][l], p["conv_ln_b"][l], p["b_s"][l].reshape(D)]
        return jnp.stack(rows + [p["final_norm_g"]])[None]

    names = ["norm_g", "mem_norm_g", "gmlp_ln_g", "gmlp_ln_b", "conv_b", "conv_ln_g", "conv_ln_b", "b_s",
             "final_norm_g"]
    w_small = pack(dict(zip(names, [norm_g, mem_norm_g, gmlp_ln_g, gmlp_ln_b, conv_b, conv_ln_g, conv_ln_b,
                                    b_s, final_norm_g])))
    m_small = pack(dict(zip(names, [m_norm_g, m_mem_norm_g, m_gmlp_ln_g, m_gmlp_ln_b, m_conv_b, m_conv_ln_g,
                                    m_conv_ln_b, m_b_s, m_final_norm_g])))
    v_small = pack(dict(zip(names, [v_norm_g, v_mem_norm_g, v_gmlp_ln_g, v_gmlp_ln_b, v_conv_b, v_conv_ln_g,
                                    v_conv_ln_b, v_b_s, v_final_norm_g])))
    outs = {}

    def run(key, parts, w, m, v, idx, comm=None):
        outs[key], got = _adamw(parts, w, m, v, idx, "adamw_" + key, comm)
        return got

    keys, comm = flush()
    landed(keys, run("w_in0", recv["w_in0"], w_in, m_w_in, v_w_in, 0, comm))
    parts_small = jnp.concatenate([recv[f"{k}{l}"] for l in range(L) for k in ("norm_g", "small")]
                                  + [recv["final_norm_g"]], axis=1)
    parts_ws = jnp.concatenate([recv[f"w_s{l}"] for l in range(L)], axis=1)
    for l in range(L):
        if l > 0:
            run(f"w_in{l}", recv[f"w_in{l}"], w_in, m_w_in, v_w_in, l)
        run(f"w_kv{l}", recv[f"w_kv{l}"], w_kv, m_w_kv, v_w_kv, l)
        for n in range(3):
            sh = (L * 3, D // NDEV, D)
            run(f"w_branch{l}_{n}", recv[f"w_branch{l}_{n}"], w_branch.reshape(sh), m_w_branch.reshape(sh),
                v_w_branch.reshape(sh), l * 3 + n)
        run(f"w_out{l}", recv[f"w_out{l}"], w_out, m_w_out, v_w_out, l)
        run(f"conv_w{l}", recv[f"conv_w{l}"], conv_w, m_conv_w, v_conv_w, l)
    run("small", parts_small, w_small, m_small, v_small, 0)
    ws_shape = (1, L * N_GROUPS * CHUNK, CHUNK)
    run("w_s", parts_ws, w_s.reshape(ws_shape), m_w_s.reshape(ws_shape), v_w_s.reshape(ws_shape), 0)

    def leaf(name, k):
        if name in ("w_in", "w_kv", "w_out", "conv_w"):
            return jnp.stack([outs[f"{name}{l}"][k] for l in range(L)])
        if name == "w_branch":
            return jnp.stack([jnp.stack([outs[f"w_branch{l}_{n}"][k] for n in range(3)]) for l in range(L)])
        if name == "w_s":
            return outs["w_s"][k].reshape(L, N_GROUPS, CHUNK, CHUNK)
        sm = outs["small"][k]
        if name == "final_norm_g":
            return sm[8 * L]
        j = names.index(name)
        rows = jnp.stack([sm[8 * l + j] for l in range(L)])
        return rows.reshape(L, N_GROUPS, CHUNK) if name == "b_s" else rows

    order = ["norm_g", "mem_norm_g", "w_in", "gmlp_ln_g", "gmlp_ln_b", "w_s", "b_s", "conv_w", "conv_b",
             "conv_ln_g", "conv_ln_b", "w_kv", "w_branch", "w_out", "final_norm_g"]
    loss = lax.psum(loss_part[0, 0], ("x", "y", "c"))
    return (loss, grad_x, *[leaf(nm, k) for k in range(4) for nm in order])
```

```python
import functools
import math

import jax
import jax.numpy as jnp
from jax import lax
from jax.experimental import pallas as pl
from jax.experimental.pallas import tpu as pltpu

F32 = jnp.float32
MXU_DTYPE = jnp.bfloat16
ACT_DTYPE = jnp.bfloat16
GRAD_DTYPE = jnp.bfloat16

D = 1024
N_SEG = 11
N_IN = N_SEG * D
NDEV = 8
CHUNK = 128
N_GROUPS = 8
CONV_K = 31
HALO = 32
LANES = 128
STRIP = 32
HEADS = 4
HEAD_DIM = D // HEADS
RMS_EPS = 1e-6
LN_EPS = 1e-5
ADAM_LR, ADAM_B1, ADAM_B2, ADAM_EPS, ADAM_WD, ADAM_STEP = 0.001, 0.9, 0.999, 1e-08, 0.01, 10
SEG_AU, SEG_AV, SEG_AG, SEG_BA, SEG_BB, SEG_BG, SEG_CQ, SEG_CG, SEG_M = 0, 1, 2, 3, 4, 5, 6, 7, 8

VMEM_LIMIT = 60 * 1024 * 1024
MESH = pl.DeviceIdType.MESH
NT_DIMS = (((1,), (1,)), ((), ()))
TN_DIMS = (((0,), (0,)), ((), ()))


def _params(*sem):
    return pltpu.CompilerParams(dimension_semantics=sem, vmem_limit_bytes=VMEM_LIMIT)


def _tile(n, want):
    t = min(n, want)
    assert n % t == 0, (n, want)
    return t


def _mx(v):
    return v.astype(MXU_DTYPE)


def _gelu(x):
    t = jnp.tanh(0.7978845608028654 * (x + 0.044715 * x * x * x))
    return 0.5 * x * (1.0 + t), t


def _gelu_grad(x, t):
    return 0.5 * (1.0 + t) + 0.5 * x * (1.0 - t * t) * 0.7978845608028654 * (1.0 + 3.0 * 0.044715 * x * x)


def _silu_grad(x, s):
    return s * (1.0 + x * (1.0 - s))


def _ln_stats(v):
    mu = jnp.mean(v, axis=-1, keepdims=True)
    vc = v - mu
    rstd = lax.rsqrt(jnp.mean(vc * vc, axis=-1, keepdims=True) + LN_EPS)
    return vc * rstd, rstd


def _ln_grad(dy, g, vhat, rstd):
    dvh = dy * g
    return rstd * (dvh - jnp.mean(dvh, axis=-1, keepdims=True)
                   - vhat * jnp.mean(dvh * vhat, axis=-1, keepdims=True))


def _rowsum(v):
    return jnp.sum(v, axis=0, keepdims=True)


def _coords():
    return lax.axis_index("x"), lax.axis_index("y"), lax.axis_index("c")


def _flip(pos, d):
    x, y, c = pos
    return (1 - x if d & 4 else x, 1 - y if d & 2 else y, 1 - c if d & 1 else c)


def _slot(pos):
    return 4 * pos[0] + 2 * pos[1] + pos[2]


CHIP_FLIPS = (4, 2, 6)


class _Gather:
    def __init__(self, arrays, mid_frac=0.8):
        self.arrays = list(arrays)
        self.n = n = len(arrays)
        self.mid_frac = mid_frac
        self.out_shape = [jax.ShapeDtypeStruct((NDEV,) + a.shape, a.dtype) for a in arrays]
        self.scratch = [pltpu.SemaphoreType.DMA((n, 7)), pltpu.SemaphoreType.DMA((n, 7)),
                        pltpu.SemaphoreType.DMA((n,))]

    def _copy(self, refs, i, k, block, to, own=False):
        ins, outs, (send, recv, _) = refs
        slot = outs[i].at[_slot(block)]
        return pltpu.make_async_remote_copy(
            src_ref=ins[i] if own else slot, dst_ref=slot, send_sem=send.at[i, k], recv_sem=recv.at[i, k],
            device_id=to, device_id_type=MESH)

    def _local(self, refs, i):
        ins, outs, (_, _, loc) = refs
        return pltpu.make_async_copy(ins[i], outs[i].at[_slot(_coords())], loc.at[i])

    def _first(self, refs, i, k):
        me = _coords()
        return self._copy(refs, i, k, me, _flip(me, ((1,) + CHIP_FLIPS)[k]), own=True)

    def _passed(self, refs, i, j):
        me = _coords()
        return self._copy(refs, i, 4 + j, _flip(me, CHIP_FLIPS[j]), _flip(me, 1))

    def start(self, refs):
        for i in range(self.n):
            self._local(refs, i).start()
        for k in range(4):
            for i in range(self.n):
                self._first(refs, i, k).start()

    def forward(self, refs):
        me = _coords()
        for j, d in enumerate(CHIP_FLIPS):
            for i in range(self.n):
                self._copy(refs, i, 1 + j, _flip(me, d), me).wait_recv()
                self._passed(refs, i, j).start()

    def finish(self, refs):
        me = _coords()
        sib = _flip(me, 1)
        for i in range(self.n):
            self._copy(refs, i, 0, sib, me).wait_recv()
        for j, d in enumerate(CHIP_FLIPS):
            for i in range(self.n):
                self._copy(refs, i, 4 + j, _flip(sib, d), me).wait_recv()
        for i in range(self.n):
            for k in range(4):
                self._first(refs, i, k).wait_send()
            for j in range(3):
                self._passed(refs, i, j).wait_send()
            self._local(refs, i).wait()


class _Scatter:
    def __init__(self, arrays):
        self.arrays = list(arrays)
        self.n = n = len(arrays)
        self.mid_frac = None
        self.out_shape = [jax.ShapeDtypeStruct(a.shape, a.dtype) for a in arrays]
        self.scratch = [pltpu.SemaphoreType.DMA((n, 7)), pltpu.SemaphoreType.DMA((n, 7)),
                        pltpu.SemaphoreType.DMA((n,))]

    def _copy(self, refs, i, d, landing):
        ins, outs, (send, recv, _) = refs
        me = _coords()
        peer = _flip(me, d)
        return pltpu.make_async_remote_copy(
            src_ref=ins[i].at[_slot(peer)], dst_ref=outs[i].at[_slot(peer) if landing else _slot(me)],
            send_sem=send.at[i, d - 1], recv_sem=recv.at[i, d - 1], device_id=peer, device_id_type=MESH)

    def _local(self, refs, i):
        ins, outs, (_, _, loc) = refs
        me = _slot(_coords())
        return pltpu.make_async_copy(ins[i].at[me], outs[i].at[me], loc.at[i])

    def start(self, refs):
        for i in range(self.n):
            self._local(refs, i).start()
        for d in range(1, NDEV):
            for i in range(self.n):
                self._copy(refs, i, d, False).start()

    def forward(self, refs):
        pass

    def finish(self, refs):
        for d in range(1, NDEV):
            for i in range(self.n):
                self._copy(refs, i, d, True).wait_recv()
        for d in range(1, NDEV):
            for i in range(self.n):
                self._copy(refs, i, d, False).wait_send()
        for i in range(self.n):
            self._local(refs, i).wait()


class _Both:
    def __init__(self, a, b):
        self.parts = (a, b)
        self.arrays = a.arrays + b.arrays
        self.n = a.n + b.n
        self.mid_frac = a.mid_frac if a.mid_frac is not None else b.mid_frac
        self.out_shape = a.out_shape + b.out_shape
        self.scratch = a.scratch + b.scratch

    def _each(self, refs):
        ins, outs, sems = refs
        na, ns = self.parts[0].n, len(self.parts[0].scratch)
        return ((self.parts[0], (ins[:na], outs[:na], sems[:ns])), (self.parts[1], (ins[na:], outs[na:], sems[ns:])))

    def start(self, refs):
        for part, r in self._each(refs):
            part.start(r)

    def forward(self, refs):
        for part, r in self._each(refs):
            part.forward(r)

    def finish(self, refs):
        for part, r in self._each(refs):
            part.finish(r)


def _call(body, name, grid, in_specs, out_specs, out_shape, scratch, args, comm=None):
    params = _params(*(["arbitrary"] * len(grid)))
    if comm is None:
        outs = pl.pallas_call(
            body, name=name, grid=grid, in_specs=in_specs, out_specs=out_specs, out_shape=out_shape,
            scratch_shapes=scratch, compiler_params=params)(*args)
        return list(outs), []
    n_in, n_out, n_scr, k = len(in_specs), len(out_specs), len(scratch), comm.n
    nsteps = math.prod(grid) if grid else 1
    mid = min(nsteps - 1, int(nsteps * comm.mid_frac)) if comm.mid_frac is not None else None

    def hosted(*refs):
        ins, refs = refs[:n_in], refs[n_in:]
        cins, refs = refs[:k], refs[k:]
        outs, refs = refs[:n_out], refs[n_out:]
        couts, refs = refs[:k], refs[k:]
        scr, sems = refs[:n_scr], refs[n_scr:]
        crefs = (cins, couts, sems)
        if nsteps == 1:
            comm.start(crefs)
            body(*ins, *outs, *scr)
            comm.forward(crefs)
            comm.finish(crefs)
            return
        step = pl.program_id(0)
        for a in range(1, len(grid)):
            step = step * grid[a] + pl.program_id(a)
        pl.when(step == 0)(lambda: comm.start(crefs))
        if mid is not None:
            pl.when(step == mid)(lambda: comm.forward(crefs))
        body(*ins, *outs, *scr)
        pl.when(step == nsteps - 1)(lambda: comm.finish(crefs))

    any_spec = pl.BlockSpec(memory_space=pl.ANY)
    outs = pl.pallas_call(
        hosted, name=name, grid=grid,
        in_specs=list(in_specs) + [any_spec] * k, out_specs=list(out_specs) + [any_spec] * k,
        out_shape=list(out_shape) + comm.out_shape, scratch_shapes=list(scratch) + comm.scratch,
        compiler_params=params)(*args, *comm.arrays)
    return list(outs[:n_out]), list(outs[n_out:])


def _exchange(comm, name):
    return _call(lambda: None, name, (), [], [], [], [], [], comm)[1]


def _rms_matmul(x, g, w, name, comm=None):
    T = x.shape[0]
    nb = w.shape[2]
    tT = _tile(T, 1024)
    per = 2

    def body(x_ref, g_ref, w_ref, z_ref, h_ref):
        @pl.when(pl.program_id(1) == 0)
        def _():
            xf = x_ref[...]
            r = lax.rsqrt(jnp.mean(xf * xf, axis=-1, keepdims=True) + RMS_EPS)
            h_ref[...] = (xf * r * g_ref[...]).astype(h_ref.dtype)

        for j in range(per):
            z_ref[:, j * nb:(j + 1) * nb] = jnp.dot(h_ref[...], w_ref[j], preferred_element_type=F32
                                                    ).astype(z_ref.dtype)

    return _call(
        body, name, (T // tT, NDEV // per),
        [pl.BlockSpec((tT, D), lambda t, n: (t, 0)),
         pl.BlockSpec((1, D), lambda t, n: (0, 0)),
         pl.BlockSpec((per, D, nb), lambda t, n: (n, 0, 0))],
        [pl.BlockSpec((tT, per * nb), lambda t, n: (t, n)),
         pl.BlockSpec((tT, D), lambda t, n: (t, 0))],
        [jax.ShapeDtypeStruct((T, NDEV * nb), ACT_DTYPE), jax.ShapeDtypeStruct((T, D), MXU_DTYPE)],
        [], (x, g.reshape(1, D), w), comm)


ARRIVAL = (0, 1) + CHIP_FLIPS + tuple(d ^ 1 for d in CHIP_FLIPS)


def _rms_matmul_gathering(x, g, w_shard, name, comm=None):
    T = x.shape[0]
    nb = w_shard.shape[1]
    tT = _tile(T, 1024)
    nT = T // tT

    def body(x_ref, g_ref, wsh_ref, z_ref, h_ref, wfull_ref, h_all, wbuf, zbuf, fetch_sems, z_sems, send_sems,
             recv_sems, own_sem):
        n, t = pl.program_id(0), pl.program_id(1)
        step = n * nT + t
        me = _coords()
        sib = _flip(me, 1)

        def remote(k, block, to, own=False):
            slot = wfull_ref.at[_slot(block)]
            return pltpu.make_async_remote_copy(
                src_ref=wsh_ref if own else slot, dst_ref=slot, send_sem=send_sems.at[k], recv_sem=recv_sems.at[k],
                device_id=to, device_id_type=MESH)

        def first(k):
            return remote(k, me, _flip(me, ARRIVAL[1 + k]), own=True)

        def passed(j):
            return remote(4 + j, _flip(me, CHIP_FLIPS[j]), sib)

        own_copy = pltpu.make_async_copy(wsh_ref, wfull_ref.at[_slot(me)], own_sem)

        def fetch(src, nn):
            return pltpu.make_async_copy(src, wbuf.at[nn % 2], fetch_sems.at[nn % 2])

        @pl.when(step == 0)
        def _():
            own_copy.start()
            for k in range(4):
                first(k).start()
            fetch(wsh_ref, 0).start()

        @pl.when(t == 0)
        def _():
            fetch(wfull_ref.at[0], n).wait()

        @pl.when(n == 0)
        def _():
            xf = x_ref[...]
            r = lax.rsqrt(jnp.mean(xf * xf, axis=-1, keepdims=True) + RMS_EPS)
            h = (xf * r * g_ref[...]).astype(h_ref.dtype)
            h_ref[...] = h
            h_all[t] = h

        def z_copy(s, col):
            return pltpu.make_async_copy(
                zbuf.at[s % 2], z_ref.at[pl.ds(pl.multiple_of(t * tT, tT), tT), pl.ds(col * nb, nb)],
                z_sems.at[s % 2])

        @pl.when(step >= 2)
        def _():
            z_copy(step, 0).wait()

        d = sum(jnp.where(n == nn, ARRIVAL[nn], 0) for nn in range(NDEV))
        col = _slot((me[0] ^ ((d >> 2) & 1), me[1] ^ ((d >> 1) & 1), me[2] ^ (d & 1)))
        zbuf[step % 2] = jnp.dot(h_all[t], wbuf[n % 2], preferred_element_type=F32).astype(zbuf.dtype)
        z_copy(step, col).start()

        for nn in range(1, NDEV):
            @pl.when((n == nn - 1) & (t == nT - 1))
            def _(nn=nn):
                block = _flip(me, ARRIVAL[nn])
                if nn == 1:
                    remote(0, sib, me).wait_recv()
                elif nn < 5:
                    remote(nn - 1, block, me).wait_recv()
                    passed(nn - 2).start()
                else:
                    remote(nn - 1, block, me).wait_recv()
                fetch(wfull_ref.at[_slot(block)], nn).start()

        @pl.when(step == NDEV * nT - 1)
        def _():
            z_copy(step - 1, 0).wait()
            z_copy(step, 0).wait()
            for k in range(4):
                first(k).wait_send()
            for j in range(3):
                passed(j).wait_send()
            own_copy.wait()

    keep = lambda n, t: (jnp.where(n == 0, t, nT - 1), 0)
    any_spec = pl.BlockSpec(memory_space=pl.ANY)
    return _call(
        body, name, (NDEV, nT),
        [pl.BlockSpec((tT, D), keep), pl.BlockSpec((1, D), lambda n, t: (0, 0)), any_spec],
        [any_spec, pl.BlockSpec((tT, D), keep), any_spec],
        [jax.ShapeDtypeStruct((T, NDEV * nb), ACT_DTYPE), jax.ShapeDtypeStruct((T, D), MXU_DTYPE),
         jax.ShapeDtypeStruct((NDEV,) + w_shard.shape, w_shard.dtype)],
        [pltpu.VMEM((nT, tT, D), MXU_DTYPE), pltpu.VMEM((2, D, nb), w_shard.dtype),
         pltpu.VMEM((2, tT, nb), ACT_DTYPE), pltpu.SemaphoreType.DMA((2,)), pltpu.SemaphoreType.DMA((2,)),
         pltpu.SemaphoreType.DMA((7,)), pltpu.SemaphoreType.DMA((7,)), pltpu.SemaphoreType.DMA],
        (x, g.reshape(1, D), w_shard), comm)


def _rms_matmul_bwd(dz, w, x, g, dxo, name, comm=None):
    T = x.shape[0]
    nb = w.shape[2]
    tT = _tile(T, 512)
    per = 4
    steps = NDEV // per

    def body(dz_ref, w_ref, x_ref, g_ref, dxo_ref, dx_ref, dg_ref, acc):
        t, n = pl.program_id(0), pl.program_id(1)

        @pl.when((n == 0) & (t == 0))
        def _():
            dg_ref[...] = jnp.zeros_like(dg_ref)

        part = None
        for j in range(per):
            d = lax.dot_general(_mx(dz_ref[:, j * nb:(j + 1) * nb]), w_ref[j], NT_DIMS, preferred_element_type=F32)
            part = d if part is None else part + d

        @pl.when(n == 0)
        def _():
            acc[...] = part

        @pl.when(n > 0)
        def _():
            acc[...] += part

        @pl.when(n == steps - 1)
        def _():
            xf = x_ref[...]
            r = lax.rsqrt(jnp.mean(xf * xf, axis=-1, keepdims=True) + RMS_EPS)
            xh = xf * r
            dh = acc[...]
            dxh = dh * g_ref[...]
            dx_ref[...] = dxo_ref[...] + r * (dxh - xh * jnp.mean(dxh * xh, axis=-1, keepdims=True))
            dg_ref[...] += _rowsum(dh * xh)

    return _call(
        body, name, (T // tT, steps),
        [pl.BlockSpec((tT, per * nb), lambda t, n: (t, n)),
         pl.BlockSpec((per, D, nb), lambda t, n: (n, 0, 0)),
         pl.BlockSpec((tT, D), lambda t, n: (t, 0)),
         pl.BlockSpec((1, D), lambda t, n: (0, 0)),
         pl.BlockSpec((tT, D), lambda t, n: (t, 0))],
        [pl.BlockSpec((tT, D), lambda t, n: (t, 0)),
         pl.BlockSpec((1, D), lambda t, n: (0, 0))],
        [jax.ShapeDtypeStruct((T, D), F32), jax.ShapeDtypeStruct((1, D), F32)],
        [pltpu.VMEM((tT, D), F32)], (dz, w, x, g.reshape(1, D), dxo), comm)


def _atb(a, ai, b, bi, nblk, name, comm=None):
    T, M = a.shape[1:]
    N = b.shape[2]
    nb = N // nblk
    tk = _tile(T, 2048)
    nk = T // tk

    def body(a_ref, b_ref, o_ref, acc):
        k = pl.program_id(1)

        @pl.when(k == 0)
        def _():
            acc[...] = jnp.zeros_like(acc)

        acc[...] += lax.dot_general(_mx(a_ref[...]), _mx(b_ref[...]), TN_DIMS, preferred_element_type=F32)

        @pl.when(k == nk - 1)
        def _():
            o_ref[...] = acc[...].astype(o_ref.dtype)

    outs, couts = _call(
        body, name, (nblk, nk),
        [pl.BlockSpec((None, tk, M), lambda n, k: (ai, k, 0)),
         pl.BlockSpec((None, tk, nb), lambda n, k: (bi, k, n))],
        [pl.BlockSpec((None, M, nb), lambda n, k: (n, 0, 0))],
        [jax.ShapeDtypeStruct((nblk, M, nb), GRAD_DTYPE)],
        [pltpu.VMEM((M, nb), F32)], (a, b), comm)
    return outs[0], couts


def _loss_head(x, tgt, g, name):
    T = x.shape[0]
    tT = _tile(T, 512)

    def body(x_ref, t_ref, g_ref, loss_ref, dx_ref, dg_ref):
        @pl.when(pl.program_id(0) == 0)
        def _():
            loss_ref[...] = jnp.zeros_like(loss_ref)
            dg_ref[...] = jnp.zeros_like(dg_ref)

        xf = x_ref[...]
        r = lax.rsqrt(jnp.mean(xf * xf, axis=-1, keepdims=True) + RMS_EPS)
        xh = xf * r
        err = xh * g_ref[...] - t_ref[...]
        loss_ref[...] += 0.5 * jnp.sum(jnp.mean(err * err, axis=-1, keepdims=True), axis=0, keepdims=True)
        dy = err * (1.0 / D)
        dxh = dy * g_ref[...]
        dx_ref[...] = r * (dxh - xh * jnp.mean(dxh * xh, axis=-1, keepdims=True))
        dg_ref[...] += _rowsum(dy * xh)

    return pl.pallas_call(
        body, name=name, grid=(T // tT,),
        in_specs=[pl.BlockSpec((tT, D), lambda t: (t, 0)),
                  pl.BlockSpec((tT, D), lambda t: (t, 0)),
                  pl.BlockSpec((1, D), lambda t: (0, 0))],
        out_specs=[pl.BlockSpec((1, 1), lambda t: (0, 0)),
                   pl.BlockSpec((tT, D), lambda t: (t, 0)),
                   pl.BlockSpec((1, D), lambda t: (0, 0))],
        out_shape=[jax.ShapeDtypeStruct((1, 1), F32), jax.ShapeDtypeStruct((T, D), F32),
                   jax.ShapeDtypeStruct((1, D), F32)],
        compiler_params=_params("arbitrary"),
    )(x, tgt, g.reshape(1, D))


def _spatial_gate(wm_ref, bst_ref, vb_ref, sv_ref, n_chunks):
    for c in range(n_chunks):
        rows = slice(c * CHUNK, (c + 1) * CHUNK)
        for g in range(N_GROUPS):
            cols = slice(g * CHUNK, (g + 1) * CHUNK)
            sv_ref[rows, cols] = (jnp.dot(wm_ref[g], vb_ref[rows, cols], preferred_element_type=F32)
                                  + bst_ref[:, g:g + 1])


def _lane_loop(fn):
    def step(i, carry):
        fn(pl.ds(pl.multiple_of(i * LANES, LANES), LANES))
        return carry

    lax.fori_loop(0, D // LANES, step, 0)


def _shifted_copies(buf, sh, n):
    for s in range(1, 8):
        sh[s - 1, 0:n, :] = buf[s:s + n, :]


def _window(buf, sh, base, off, lanes):
    a, s = divmod(off, 8)
    src = buf if s == 0 else sh.at[s - 1]
    return src[base + 8 * a:base + 8 * a + 8, lanes]


def _softmax_rows(s):
    e = jnp.exp(s - jnp.max(s, axis=-1, keepdims=True))
    return e / jnp.sum(e, axis=-1, keepdims=True)


def _branch_fwd(z, kv, wm, bst, ln_a, cw, cvec, name, comm=None):
    T = z.shape[0]
    tT = _tile(T, 256)
    n_chunks = tT // CHUNK

    def body(z_ref, kv_ref, wm_ref, bst_ref, lna_ref, cw_ref, cvec_ref, br_ref, c_ref, gbuf, gsh, vb, ua):
        @pl.when(pl.program_id(0) == 0)
        def _():
            gbuf[0:HALO, :] = jnp.zeros((HALO, D), F32)

        def seg(s):
            return z_ref[:, s * D:(s + 1) * D].astype(F32)

        u, _ = _gelu(seg(SEG_AU))
        zg = seg(SEG_AG)
        ua[...] = u * (zg * jax.nn.sigmoid(zg))
        gv, _ = _gelu(seg(SEG_AV))
        vhat, _ = _ln_stats(gv)
        vb[...] = _mx(vhat * lna_ref[0:1, :] + lna_ref[1:2, :])
        for c in range(n_chunks):
            rows = slice(c * CHUNK, (c + 1) * CHUNK)
            for g in range(N_GROUPS):
                cols = slice(g * CHUNK, (g + 1) * CHUNK)
                sv = jnp.dot(wm_ref[g], vb[rows, cols], preferred_element_type=F32) + bst_ref[:, g:g + 1]
                br_ref[0, rows, cols] = (sv * ua[rows, cols]).astype(br_ref.dtype)

        gbuf[HALO:HALO + tT, :] = seg(SEG_BA) * jax.nn.sigmoid(seg(SEG_BB))
        _shifted_copies(gbuf, gsh, tT + HALO - 8)
        def conv_lanes(lanes):
            taps = [jnp.broadcast_to(cw_ref[k:k + 1, lanes], (8, LANES)) for k in range(CONV_K)]
            bias = jnp.broadcast_to(cvec_ref[0:1, lanes], (8, LANES))
            for base in range(0, tT, 8):
                acc = [bias, None, None, None]
                for k in range(CONV_K):
                    term = taps[k] * _window(gbuf, gsh, base, k + HALO - CONV_K + 1, lanes)
                    acc[k % 4] = term if acc[k % 4] is None else acc[k % 4] + term
                c_ref[base:base + 8, lanes] = (acc[0] + acc[1]) + (acc[2] + acc[3])

        _lane_loop(conv_lanes)
        gbuf[0:HALO, :] = gbuf[tT:tT + HALO, :]
        chat, _ = _ln_stats(c_ref[...])
        cl = chat * cvec_ref[1:2, :] + cvec_ref[2:3, :]
        zg = seg(SEG_BG)
        br_ref[1] = (cl * jax.nn.sigmoid(cl) * (zg * jax.nn.sigmoid(zg))).astype(br_ref.dtype)

        for h in range(HEADS):
            cols = slice(h * HEAD_DIM, (h + 1) * HEAD_DIM)
            q = _mx(z_ref[:, SEG_CQ * D + h * HEAD_DIM:SEG_CQ * D + (h + 1) * HEAD_DIM])
            s = lax.dot_general(q, kv_ref[:, cols], NT_DIMS, preferred_element_type=F32)
            p = _softmax_rows(s * (1.0 / math.sqrt(HEAD_DIM)))
            att = jnp.dot(_mx(p), kv_ref[:, D + h * HEAD_DIM:D + (h + 1) * HEAD_DIM], preferred_element_type=F32)
            zg = z_ref[:, SEG_CG * D + h * HEAD_DIM:SEG_CG * D + (h + 1) * HEAD_DIM].astype(F32)
            br_ref[2, :, cols] = (att * (zg * jax.nn.sigmoid(zg))).astype(br_ref.dtype)

    full = lambda shape: pl.BlockSpec(shape, lambda t: (0,) * len(shape))
    return _call(
        body, name, (T // tT,),
        [pl.BlockSpec((tT, SEG_M * D), lambda t: (t, 0)),
         full(kv.shape), full(wm.shape), full(bst.shape), full(ln_a.shape), full(cw.shape), full(cvec.shape)],
        [pl.BlockSpec((3, tT, D), lambda t: (0, t, 0)), pl.BlockSpec((tT, D), lambda t: (t, 0))],
        [jax.ShapeDtypeStruct((3, T, D), MXU_DTYPE), jax.ShapeDtypeStruct((T, D), F32)],
        [pltpu.VMEM((tT + HALO, D), F32), pltpu.VMEM((7, tT + HALO - 8, D), F32),
         pltpu.VMEM((tT, D), MXU_DTYPE), pltpu.VMEM((tT, D), F32)],
        (z, kv, wm, bst, ln_a, cw, cvec), comm)


def _branch_bwd(z, dbr, c, dzm, kv, wm, wmt, bst, ln_a, cw, cvec, name, comm=None):
    T = z.shape[0]
    M = kv.shape[0]
    tT = _tile(T, 128)
    nT = T // tT
    n_chunks = tT // CHUNK

    def body(z_ref, dbr_ref, c_ref, dzm_ref, kv_ref, wm_ref, wmt_ref, bst_ref, lna_ref,
             cw_ref, cvec_ref, dz_ref, vecg_ref, dbst_ref, dws_ref, dcw_ref, dkv_ref,
             gbuf, dcbuf, vb, dsvb, sv, dvbuf, dcsh, dglu, vh, gq, dcw8, dcw_step):
        i = pl.program_id(0)

        @pl.when(i == 0)
        def _():
            vecg_ref[...] = jnp.zeros_like(vecg_ref)
            dbst_ref[...] = jnp.zeros_like(dbst_ref)
            dws_ref[...] = jnp.zeros_like(dws_ref)
            dcw8[...] = jnp.zeros_like(dcw8)
            dkv_ref[...] = jnp.zeros_like(dkv_ref)
            dcbuf[tT:tT + HALO, :] = jnp.zeros((HALO, D), F32)

        strips = [slice(r0, r0 + STRIP) for r0 in range(0, tT, STRIP)]

        def seg(r, s):
            return z_ref[r, s * D:(s + 1) * D].astype(F32)

        def put(r, s, val):
            dz_ref[r, s * D:(s + 1) * D] = val.astype(dz_ref.dtype)

        for r in strips:
            zv = seg(r, SEG_AV)
            gv, tv = _gelu(zv)
            vhat, rstd = _ln_stats(gv)
            vb[r, :] = _mx(vhat * lna_ref[0:1, :] + lna_ref[1:2, :])
            vh[r, :] = vhat
            gq[r, :] = rstd * _gelu_grad(zv, tv)
        _spatial_gate(wm_ref, bst_ref, vb, sv, n_chunks)
        for r in strips:
            zu, zg = seg(r, SEG_AU), seg(r, SEG_AG)
            u, tu = _gelu(zu)
            sg = jax.nn.sigmoid(zg)
            d_a = dbr_ref[0, r, :].astype(F32)
            put(r, SEG_AU, d_a * sv[r, :] * (zg * sg) * _gelu_grad(zu, tu))
            put(r, SEG_AG, d_a * u * sv[r, :] * _silu_grad(zg, sg))
            dsv = d_a * u * (zg * sg)
            dsvb[r, :] = _mx(dsv)
            in_chunk = slice(r.start % CHUNK, r.start % CHUNK + STRIP)
            for g in range(N_GROUPS):
                dbst_ref[in_chunk, g:g + 1] += jnp.sum(dsv[:, g * CHUNK:(g + 1) * CHUNK], axis=-1, keepdims=True)
        tril = (lax.broadcasted_iota(jnp.int32, (CHUNK, CHUNK), 0)
                >= lax.broadcasted_iota(jnp.int32, (CHUNK, CHUNK), 1))
        for g in range(N_GROUPS):
            cols = slice(g * CHUNK, (g + 1) * CHUNK)
            for cc in range(n_chunks):
                rows = slice(cc * CHUNK, (cc + 1) * CHUNK)
                dws = lax.dot_general(dsvb[rows, cols], vb[rows, cols], NT_DIMS, preferred_element_type=F32)
                dws_ref[g] += jnp.where(tril, dws, 0.0)
                dvbuf[rows, cols] = jnp.dot(wmt_ref[g], dsvb[rows, cols], preferred_element_type=F32)
        for r in strips:
            dv, vhat = dvbuf[r, :], vh[r, :]
            vecg_ref[0:1, :] += _rowsum(dv * vhat)
            vecg_ref[1:2, :] += _rowsum(dv)
            dvh = dv * lna_ref[0:1, :]
            put(r, SEG_AV, (dvh - jnp.mean(dvh, axis=-1, keepdims=True)
                            - vhat * jnp.mean(dvh * vhat, axis=-1, keepdims=True)) * gq[r, :])

        sgb_buf = sv
        for r in strips:
            za, zg = seg(r, SEG_BA), seg(r, SEG_BG)
            sgb = jax.nn.sigmoid(seg(r, SEG_BB))
            sgb_buf[r, :] = sgb
            gbuf[r, :] = za * sgb
            chat, crstd = _ln_stats(c_ref[r, :])
            cl = chat * cvec_ref[1:2, :] + cvec_ref[2:3, :]
            scl = jax.nn.sigmoid(cl)
            sg = jax.nn.sigmoid(zg)
            d_b = dbr_ref[1, r, :].astype(F32)
            put(r, SEG_BG, d_b * (cl * scl) * _silu_grad(zg, sg))
            dcl = d_b * (zg * sg) * _silu_grad(cl, scl)
            vecg_ref[3:4, :] += _rowsum(dcl * chat)
            vecg_ref[4:5, :] += _rowsum(dcl)
            dc = _ln_grad(dcl, cvec_ref[1:2, :], chat, crstd)
            vecg_ref[2:3, :] += _rowsum(dc)
            dcbuf[r, :] = dc
        _shifted_copies(dcbuf, dcsh, tT + HALO - 8)

        def conv_grads(lanes):
            taps = [jnp.broadcast_to(cw_ref[k:k + 1, lanes], (8, LANES)) for k in range(CONV_K)]
            wsum = [None] * CONV_K
            for base in range(0, tT, 8):
                glu = gbuf[base:base + 8, lanes]
                acc = [None] * 4
                for k in range(CONV_K):
                    win = _window(dcbuf, dcsh, base, CONV_K - 1 - k, lanes)
                    term = taps[k] * win
                    acc[k % 4] = term if acc[k % 4] is None else acc[k % 4] + term
                    term = glu * win
                    wsum[k] = term if wsum[k] is None else wsum[k] + term
                dglu[base:base + 8, lanes] = (acc[0] + acc[1]) + (acc[2] + acc[3])
            for k in range(CONV_K):
                dcw_step[8 * k:8 * k + 8, lanes] = wsum[k]

        _lane_loop(conv_grads)
        dcw8[...] += dcw_step[...]

        @pl.when(i == nT - 1)
        def _():
            for k in range(CONV_K):
                dcw_ref[k:k + 1, :] = _rowsum(dcw8[8 * k:8 * k + 8, :])
            dcw_ref[CONV_K:HALO, :] = jnp.zeros((HALO - CONV_K, D), F32)

        dcbuf[tT:tT + HALO, :] = dcbuf[0:HALO, :]
        for r in strips:
            dg, sgb = dglu[r, :], sgb_buf[r, :]
            put(r, SEG_BA, dg * sgb)
            put(r, SEG_BB, dg * seg(r, SEG_BA) * sgb * (1.0 - sgb))

        scale = 1.0 / math.sqrt(HEAD_DIM)
        for h in range(HEADS):
            cols = slice(h * HEAD_DIM, (h + 1) * HEAD_DIM)
            qcols = slice(SEG_CQ * D + h * HEAD_DIM, SEG_CQ * D + (h + 1) * HEAD_DIM)
            gcols = slice(SEG_CG * D + h * HEAD_DIM, SEG_CG * D + (h + 1) * HEAD_DIM)
            vcols = slice(D + h * HEAD_DIM, D + (h + 1) * HEAD_DIM)
            q = _mx(z_ref[:, qcols])
            kh, vh = kv_ref[:, cols], kv_ref[:, vcols]
            p = _softmax_rows(lax.dot_general(q, kh, NT_DIMS, preferred_element_type=F32) * scale)
            pb = _mx(p)
            att = jnp.dot(pb, vh, preferred_element_type=F32)
            zg = z_ref[:, gcols].astype(F32)
            sg = jax.nn.sigmoid(zg)
            d_c = dbr_ref[2, :, cols].astype(F32)
            dz_ref[:, gcols] = (d_c * att * _silu_grad(zg, sg)).astype(dz_ref.dtype)
            datt = _mx(d_c * (zg * sg))
            dp = lax.dot_general(datt, vh, NT_DIMS, preferred_element_type=F32)
            dkv_ref[:, vcols] += lax.dot_general(pb, datt, TN_DIMS, preferred_element_type=F32)
            ds = _mx(p * (dp - jnp.sum(dp * p, axis=-1, keepdims=True)) * scale)
            dz_ref[:, qcols] = jnp.dot(ds, kh, preferred_element_type=F32).astype(dz_ref.dtype)
            dkv_ref[:, cols] += lax.dot_general(ds, q, TN_DIMS, preferred_element_type=F32)

        dz_ref[:, SEG_M * D:] = dzm_ref[...].astype(dz_ref.dtype)

    rev = lambda i: nT - 1 - i
    full = lambda shape: pl.BlockSpec(shape, lambda i: (0,) * len(shape))
    return _call(
        body, name, (nT,),
        [pl.BlockSpec((tT, SEG_M * D), lambda i: (rev(i), 0)),
         pl.BlockSpec((3, tT, D), lambda i: (0, rev(i), 0)),
         pl.BlockSpec((tT, D), lambda i: (rev(i), 0)),
         pl.BlockSpec((tT, 3 * D), lambda i: (rev(i), 0)),
         full(kv.shape), full(wm.shape), full(wmt.shape), full(bst.shape), full(ln_a.shape),
         full(cw.shape), full(cvec.shape)],
        [pl.BlockSpec((tT, N_IN), lambda i: (rev(i), 0)),
         full((8, D)), full((CHUNK, N_GROUPS)), full((N_GROUPS, CHUNK, CHUNK)), full((HALO, D)),
         full((M, 2 * D))],
        [jax.ShapeDtypeStruct((T, N_IN), MXU_DTYPE), jax.ShapeDtypeStruct((8, D), F32),
         jax.ShapeDtypeStruct((CHUNK, N_GROUPS), F32),
         jax.ShapeDtypeStruct((N_GROUPS, CHUNK, CHUNK), F32),
         jax.ShapeDtypeStruct((HALO, D), F32), jax.ShapeDtypeStruct((M, 2 * D), F32)],
        [pltpu.VMEM((tT, D), F32), pltpu.VMEM((tT + HALO, D), F32),
         pltpu.VMEM((tT, D), MXU_DTYPE), pltpu.VMEM((tT, D), MXU_DTYPE),
         pltpu.VMEM((tT, D), F32), pltpu.VMEM((tT, D), F32),
         pltpu.VMEM((7, tT + HALO - 8, D), F32),
         pltpu.VMEM((tT, D), F32), pltpu.VMEM((tT, D), F32), pltpu.VMEM((tT, D), F32),
         pltpu.VMEM((CONV_K * 8, D), F32), pltpu.VMEM((CONV_K * 8, D), F32)],
        (z, dbr, c, dzm, kv, wm, wmt, bst, ln_a, cw, cvec), comm)


def _merge_fwd(br, z, x, wb, wo, name):
    T = x.shape[0]
    tT = _tile(T, 512)

    def body(br_ref, z0, z1, z2, x_ref, wb_ref, wo_ref, xn_ref, mg_ref, pj_ref):
        merged = jnp.zeros((tT, D), F32)
        for n, zm in enumerate((z0, z1, z2)):
            proj = jnp.dot(br_ref[n], wb_ref[:, n].reshape(D, D), preferred_element_type=F32)
            pj_ref[n] = proj.astype(pj_ref.dtype)
            merged = merged + jax.nn.sigmoid(zm[...].astype(F32)) * proj
        mg_ref[...] = merged.astype(mg_ref.dtype)
        xn_ref[...] = x_ref[...] + jnp.dot(_mx(merged), wo_ref[...].reshape(D, D), preferred_element_type=F32)

    zspec = lambda n: pl.BlockSpec((tT, D), lambda t: (t, SEG_M + n))
    return pl.pallas_call(
        body, name=name, grid=(T // tT,),
        in_specs=[pl.BlockSpec((3, tT, D), lambda t: (0, t, 0)), zspec(0), zspec(1), zspec(2),
                  pl.BlockSpec((tT, D), lambda t: (t, 0)),
                  pl.BlockSpec(wb.shape, lambda t: (0, 0, 0, 0)),
                  pl.BlockSpec(wo.shape, lambda t: (0, 0, 0))],
        out_specs=[pl.BlockSpec((tT, D), lambda t: (t, 0)),
                   pl.BlockSpec((tT, D), lambda t: (t, 0)),
                   pl.BlockSpec((3, tT, D), lambda t: (0, t, 0))],
        out_shape=[jax.ShapeDtypeStruct((T, D), F32), jax.ShapeDtypeStruct((T, D), MXU_DTYPE),
                   jax.ShapeDtypeStruct((3, T, D), ACT_DTYPE)],
        compiler_params=_params("parallel"),
    )(br, z, z, z, x, wb, wo)


def _merge_bwd(dxo, proj, z, wb, wo, name):
    T = dxo.shape[0]
    tT = _tile(T, 512)

    def body(dxo_ref, pj_ref, z0, z1, z2, wb_ref, wo_ref, dpj_ref, dbr_ref, dzm_ref):
        dmerged = lax.dot_general(_mx(dxo_ref[...]), wo_ref[...].reshape(D, D), NT_DIMS,
                                  preferred_element_type=F32)
        for n, zm in enumerate((z0, z1, z2)):
            gate = jax.nn.sigmoid(zm[...].astype(F32))
            dproj = _mx(gate * dmerged)
            dpj_ref[n] = dproj
            dzm_ref[:, n * D:(n + 1) * D] = (pj_ref[n].astype(F32) * dmerged * gate * (1.0 - gate)
                                             ).astype(dzm_ref.dtype)
            dbr_ref[n] = lax.dot_general(dproj, wb_ref[:, n].reshape(D, D), NT_DIMS,
                                         preferred_element_type=F32).astype(dbr_ref.dtype)

    zspec = lambda n: pl.BlockSpec((tT, D), lambda t: (t, SEG_M + n))
    return pl.pallas_call(
        body, name=name, grid=(T // tT,),
        in_specs=[pl.BlockSpec((tT, D), lambda t: (t, 0)),
                  pl.BlockSpec((3, tT, D), lambda t: (0, t, 0)), zspec(0), zspec(1), zspec(2),
                  pl.BlockSpec(wb.shape, lambda t: (0, 0, 0, 0)),
                  pl.BlockSpec(wo.shape, lambda t: (0, 0, 0))],
        out_specs=[pl.BlockSpec((3, tT, D), lambda t: (0, t, 0)),
                   pl.BlockSpec((3, tT, D), lambda t: (0, t, 0)),
                   pl.BlockSpec((tT, 3 * D), lambda t: (t, 0))],
        out_shape=[jax.ShapeDtypeStruct((3, T, D), MXU_DTYPE), jax.ShapeDtypeStruct((3, T, D), ACT_DTYPE),
                   jax.ShapeDtypeStruct((T, 3 * D), MXU_DTYPE)],
        compiler_params=_params("parallel"),
    )(dxo, proj, z, z, z, wb, wo)


def _adamw(parts, w, m, v, idx, name, comm=None):
    R, C = parts.shape[1:]
    tr = 128 if R % 128 == 0 else R
    c1 = 1.0 / (1.0 - ADAM_B1 ** ADAM_STEP)
    c2 = 1.0 / (1.0 - ADAM_B2 ** ADAM_STEP)

    def body(p_ref, w_ref, m_ref, v_ref, g_out, d_out, m_out, v_out):
        g = p_ref[0].astype(F32)
        for p in range(1, NDEV):
            g = g + p_ref[p].astype(F32)
        mn = ADAM_B1 * m_ref[...] + (1.0 - ADAM_B1) * g
        vn = ADAM_B2 * v_ref[...] + (1.0 - ADAM_B2) * (g * g)
        g_out[...] = g
        m_out[...] = mn
        v_out[...] = vn
        d_out[...] = -ADAM_LR * ((mn * c1) / (jnp.sqrt(vn * c2) + ADAM_EPS) + ADAM_WD * w_ref[...])

    wspec = pl.BlockSpec((None, tr, C), lambda r: (idx, r, 0))
    ospec = pl.BlockSpec((tr, C), lambda r: (r, 0))
    return _call(
        body, name, (R // tr,),
        [pl.BlockSpec((NDEV, tr, C), lambda r: (0, r, 0)), wspec, wspec, wspec],
        [ospec] * 4, [jax.ShapeDtypeStruct((R, C), F32)] * 4, [], (parts, w, m, v), comm)


def kernel(x, mem, norm_g, mem_norm_g, w_in, gmlp_ln_g, gmlp_ln_b, w_s, b_s, conv_w, conv_b, conv_ln_g, conv_ln_b, w_kv, w_branch, w_out, final_norm_g, loss_target, m_norm_g, m_mem_norm_g, m_w_in, m_gmlp_ln_g, m_gmlp_ln_b, m_w_s, m_b_s, m_conv_w, m_conv_b, m_conv_ln_g, m_conv_ln_b, m_w_kv, m_w_branch, m_w_out, m_final_norm_g, v_norm_g, v_mem_norm_g, v_w_in, v_gmlp_ln_g, v_gmlp_ln_b, v_w_s, v_b_s, v_conv_w, v_conv_b, v_conv_ln_g, v_conv_ln_b, v_w_kv, v_w_branch, v_w_out, v_final_norm_g):
    L = w_in.shape[0]
    x0, mem0, tgt = x[0], mem[0], loss_target[0]
    T, M = x0.shape[0], mem0.shape[0]
    nbi, nbk, nbc = w_in.shape[2], w_kv.shape[2], conv_w.shape[2]

    def shards(l):
        return [_mx(w_in[l]), _mx(w_kv[l]), _mx(w_branch[l]), _mx(w_out[l]), conv_w[l]]

    gather_rest = _Gather(shards(0)[1:])
    gather_upper = _Gather([a for l in range(1, L) for a in shards(l)], mid_frac=0.85) if L > 1 else None

    tril = jnp.tril(jnp.ones((CHUNK, CHUNK), bool))
    wm = [_mx(jnp.where(tril[None], w_s[l], 0.0)) for l in range(L)]
    wmt = [w.transpose(0, 2, 1) for w in wm]
    bst = [b_s[l].T for l in range(L)]
    ln_a = [jnp.stack([gmlp_ln_g[l], gmlp_ln_b[l]]) for l in range(L)]
    cvec = [jnp.stack([conv_b[l], conv_ln_g[l], conv_ln_b[l]]) for l in range(L)]

    def conv_taps(gathered):
        return jnp.pad(gathered.transpose(1, 0, 2).reshape(CONV_K, D), ((0, HALO - CONV_K), (0, 0)))

    win, wkv, wbr, wou, cwf = [], [], [], [], []
    memn, kvs, xs, saved = [], [], [x0], []
    for l in range(L):
        if l == 0:
            (z, h, w0), full = _rms_matmul_gathering(x0, norm_g[0], shards(0)[0], "inproj_fwd0", gather_rest)
            win, wkv, wbr, wou, cwf = [w0], [full[0]], [full[1]], [full[2]], [conv_taps(full[3])]
        else:
            (z, h), _ = _rms_matmul(xs[l], norm_g[l], win[l], f"inproj_fwd{l}")
        (kv, mn), _ = _rms_matmul(mem0, mem_norm_g[l], wkv[l], f"kv_fwd{l}")
        kvs.append(_mx(kv))
        memn.append(mn)
        (br, cpre), full = _branch_fwd(z, kvs[l], wm[l], bst[l], ln_a[l], cwf[l], cvec[l], f"branch_fwd{l}",
                                       gather_upper if l == 0 else None)
        for k in range(1, L if l == 0 else 0):
            f = full[5 * (k - 1):5 * k]
            win.append(f[0])
            wkv.append(f[1])
            wbr.append(f[2])
            wou.append(f[3])
            cwf.append(conv_taps(f[4]))
        xn, merged, proj = _merge_fwd(br, z, xs[l], wbr[l], wou[l], f"merge_fwd{l}")
        xs.append(xn)
        saved.append((z, h, br, cpre, merged, proj))
    loss_part, dx, dfg = _loss_head(xs[L], tgt, final_norm_g, "loss_head")

    pending, recv = [("final_norm_g", dfg, True)], {}

    def flush():
        scat = [(k, a) for k, a, g in pending if not g]
        gath = [(k, a) for k, a, g in pending if g]
        pending.clear()
        comms = ([_Scatter([a for _, a in scat])] if scat else []) + ([_Gather([a for _, a in gath])] if gath else [])
        return [k for k, _ in scat + gath], comms[0] if len(comms) == 1 else _Both(*comms)

    def landed(keys, arrays):
        recv.update(zip(keys, arrays))

    for l in reversed(range(L)):
        z, h, br, cpre, merged, proj = saved[l]
        dproj, dbr, dzm = _merge_bwd(dx, proj, z, wbr[l], wou[l], f"merge_bwd{l}")
        for n in range(3):
            dwb, _ = _atb(br, n, dproj, n, 1, f"dwbranch{l}_{n}")
            pending.append((f"w_branch{l}_{n}", dwb.reshape(NDEV, D // NDEV, D), False))
        dwo, _ = _atb(merged[None], 0, dx[None], 0, 1, f"dwout{l}")
        pending.append((f"w_out{l}", dwo.reshape(NDEV, D // NDEV, D), False))
        keys, comm = flush()
        (dz, vecg, dbst, dws, dcw, dkv), got = _branch_bwd(
            z, dbr, cpre, dzm, kvs[l], wm[l], wmt[l], bst[l], ln_a[l], cwf[l], cvec[l], f"branch_bwd{l}", comm)
        landed(keys, got)
        dwk, _ = _atb(memn[l][None], 0, dkv[None], 0, NDEV, f"dwkv{l}")
        (_, dmg), _ = _rms_matmul_bwd(dkv, wkv[l], mem0, mem_norm_g[l], jnp.zeros((M, D), F32), f"kv_bwd{l}")
        rest = jnp.concatenate([dmg, vecg[0:2], vecg[2:5], dbst.T.reshape(1, D)], axis=0)
        pending += [(f"w_kv{l}", dwk, False),
                    (f"conv_w{l}", dcw[:CONV_K].reshape(CONV_K, NDEV, nbc).transpose(1, 0, 2), False),
                    (f"small{l}", rest, True), (f"w_s{l}", dws.reshape(N_GROUPS * CHUNK, CHUNK), True)]
        keys, comm = flush()
        dwi, got = _atb(h[None], 0, dz[None], 0, NDEV, f"dwin{l}", comm)
        landed(keys, got)
        pending.append((f"w_in{l}", dwi, False))
        keys, comm = flush()
        (dx, dng), got = _rms_matmul_bwd(dz, win[l], xs[l], norm_g[l], dx, f"inproj_bwd{l}", comm)
        landed(keys, got)
        pending.append((f"norm_g{l}", dng, True))
    grad_x = dx[None]

    def pack(p):
        rows = []
        for l in range(L):
            rows += [p["norm_g"][l], p["mem_norm_g"][l], p["gmlp_ln_g"][l], p["gmlp_ln_b"][l], p["conv_b"][l],
                     p["conv_ln_g"][l], p["conv_ln_b"][l], p["b_s"][l].reshape(D)]
        return jnp.stack(rows + [p["final_norm_g"]])[None]

    names = ["norm_g", "mem_norm_g", "gmlp_ln_g", "gmlp_ln_b", "conv_b", "conv_ln_g", "conv_ln_b", "b_s",
             "final_norm_g"]
    w_small = pack(dict(zip(names, [norm_g, mem_norm_g, gmlp_ln_g, gmlp_ln_b, conv_b, conv_ln_g, conv_ln_b,
                                    b_s, final_norm_g])))
    m_small = pack(dict(zip(names, [m_norm_g, m_mem_norm_g, m_gmlp_ln_g, m_gmlp_ln_b, m_conv_b, m_conv_ln_g,
                                    m_conv_ln_b, m_b_s, m_final_norm_g])))
    v_small = pack(dict(zip(names, [v_norm_g, v_mem_norm_g, v_gmlp_ln_g, v_gmlp_ln_b, v_conv_b, v_conv_ln_g,
                                    v_conv_ln_b, v_b_s, v_final_norm_g])))
    outs = {}

    def run(key, parts, w, m, v, idx, comm=None):
        outs[key], got = _adamw(parts, w, m, v, idx, "adamw_" + key, comm)
        return got

    keys, comm = flush()
    landed(keys, run("w_in0", recv["w_in0"], w_in, m_w_in, v_w_in, 0, comm))
    parts_small = jnp.concatenate([recv[f"{k}{l}"] for l in range(L) for k in ("norm_g", "small")]
                                  + [recv["final_norm_g"]], axis=1)
    parts_ws = jnp.concatenate([recv[f"w_s{l}"] for l in range(L)], axis=1)
    for l in range(L):
        if l > 0:
            run(f"w_in{l}", recv[f"w_in{l}"], w_in, m_w_in, v_w_in, l)
        run(f"w_kv{l}", recv[f"w_kv{l}"], w_kv, m_w_kv, v_w_kv, l)
        for n in range(3):
            sh = (L * 3, D // NDEV, D)
            run(f"w_branch{l}_{n}", recv[f"w_branch{l}_{n}"], w_branch.reshape(sh), m_w_branch.reshape(sh),
                v_w_branch.reshape(sh), l * 3 + n)
        run(f"w_out{l}", recv[f"w_out{l}"], w_out, m_w_out, v_w_out, l)
        run(f"conv_w{l}", recv[f"conv_w{l}"], conv_w, m_conv_w, v_conv_w, l)
    run("small", parts_small, w_small, m_small, v_small, 0)
    ws_shape = (1, L * N_GROUPS * CHUNK, CHUNK)
    run("w_s", parts_ws, w_s.reshape(ws_shape), m_w_s.reshape(ws_shape), v_w_s.reshape(ws_shape), 0)

    def leaf(name, k):
        if name in ("w_in", "w_kv", "w_out", "conv_w"):
            return jnp.stack([outs[f"{name}{l}"][k] for l in range(L)])
        if name == "w_branch":
            return jnp.stack([jnp.stack([outs[f"w_branch{l}_{n}"][k] for n in range(3)]) for l in range(L)])
        if name == "w_s":
            return outs["w_s"][k].reshape(L, N_GROUPS, CHUNK, CHUNK)
        sm = outs["small"][k]
        if name == "final_norm_g":
            return sm[8 * L]
        j = names.index(name)
        rows = jnp.stack([sm[8 * l + j] for l in range(L)])
        return rows.reshape(L, N_GROUPS, CHUNK) if name == "b_s" else rows

    order = ["norm_g", "mem_norm_g", "w_in", "gmlp_ln_g", "gmlp_ln_b", "w_s", "b_s", "conv_w", "conv_b",
             "conv_ln_g", "conv_ln_b", "w_kv", "w_branch", "w_out", "final_norm_g"]
    loss = lax.psum(loss_part[0, 0], ("x", "y", "c"))
    return (loss, grad_x, *[leaf(nm, k) for k in range(4) for nm in order])
```

```python
import functools
import math

import jax
import jax.numpy as jnp
from jax import lax
from jax.experimental import pallas as pl
from jax.experimental.pallas import tpu as pltpu

F32 = jnp.float32
MXU_DTYPE = jnp.bfloat16
ACT_DTYPE = jnp.bfloat16
GRAD_DTYPE = jnp.bfloat16

D = 1024
N_SEG = 11
N_IN = N_SEG * D
NDEV = 8
CHUNK = 128
N_GROUPS = 8
CONV_K = 31
HALO = 32
LANES = 128
STRIP = 32
HEADS = 4
HEAD_DIM = D // HEADS
RMS_EPS = 1e-6
LN_EPS = 1e-5
ADAM_LR, ADAM_B1, ADAM_B2, ADAM_EPS, ADAM_WD, ADAM_STEP = 0.001, 0.9, 0.999, 1e-08, 0.01, 10
SEG_AU, SEG_AV, SEG_AG, SEG_BA, SEG_BB, SEG_BG, SEG_CQ, SEG_CG, SEG_M = 0, 1, 2, 3, 4, 5, 6, 7, 8

VMEM_LIMIT = 60 * 1024 * 1024
MESH = pl.DeviceIdType.MESH
NT_DIMS = (((1,), (1,)), ((), ()))
TN_DIMS = (((0,), (0,)), ((), ()))


def _params(*sem):
    return pltpu.CompilerParams(dimension_semantics=sem, vmem_limit_bytes=VMEM_LIMIT)


def _tile(n, want):
    t = min(n, want)
    assert n % t == 0, (n, want)
    return t


def _mx(v):
    return v.astype(MXU_DTYPE)


def _gelu(x):
    t = jnp.tanh(0.7978845608028654 * (x + 0.044715 * x * x * x))
    return 0.5 * x * (1.0 + t), t


def _gelu_grad(x, t):
    return 0.5 * (1.0 + t) + 0.5 * x * (1.0 - t * t) * 0.7978845608028654 * (1.0 + 3.0 * 0.044715 * x * x)


def _silu_grad(x, s):
    return s * (1.0 + x * (1.0 - s))


def _ln_stats(v):
    mu = jnp.mean(v, axis=-1, keepdims=True)
    vc = v - mu
    rstd = lax.rsqrt(jnp.mean(vc * vc, axis=-1, keepdims=True) + LN_EPS)
    return vc * rstd, rstd


def _ln_grad(dy, g, vhat, rstd):
    dvh = dy * g
    return rstd * (dvh - jnp.mean(dvh, axis=-1, keepdims=True)
                   - vhat * jnp.mean(dvh * vhat, axis=-1, keepdims=True))


def _rowsum(v):
    return jnp.sum(v, axis=0, keepdims=True)


def _coords():
    return lax.axis_index("x"), lax.axis_index("y"), lax.axis_index("c")


def _flip(pos, d):
    x, y, c = pos
    return (1 - x if d & 4 else x, 1 - y if d & 2 else y, 1 - c if d & 1 else c)


def _slot(pos):
    return 4 * pos[0] + 2 * pos[1] + pos[2]


CHIP_FLIPS = (4, 2, 6)


class _Gather:
    def __init__(self, arrays, mid_frac=0.8):
        self.arrays = list(arrays)
        self.n = n = len(arrays)
        self.mid_frac = mid_frac
        self.out_shape = [jax.ShapeDtypeStruct((NDEV,) + a.shape, a.dtype) for a in arrays]
        self.scratch = [pltpu.SemaphoreType.DMA((n, 7)), pltpu.SemaphoreType.DMA((n, 7)),
                        pltpu.SemaphoreType.DMA((n,))]

    def _copy(self, refs, i, k, block, to, own=False):
        ins, outs, (send, recv, _) = refs
        slot = outs[i].at[_slot(block)]
        return pltpu.make_async_remote_copy(
            src_ref=ins[i] if own else slot, dst_ref=slot, send_sem=send.at[i, k], recv_sem=recv.at[i, k],
            device_id=to, device_id_type=MESH)

    def _local(self, refs, i):
        ins, outs, (_, _, loc) = refs
        return pltpu.make_async_copy(ins[i], outs[i].at[_slot(_coords())], loc.at[i])

    def _first(self, refs, i, k):
        me = _coords()
        return self._copy(refs, i, k, me, _flip(me, ((1,) + CHIP_FLIPS)[k]), own=True)

    def _passed(self, refs, i, j):
        me = _coords()
        return self._copy(refs, i, 4 + j, _flip(me, CHIP_FLIPS[j]), _flip(me, 1))

    def start(self, refs):
        for i in range(self.n):
            self._local(refs, i).start()
        for k in range(4):
            for i in range(self.n):
                self._first(refs, i, k).start()

    def forward(self, refs):
        me = _coords()
        for j, d in enumerate(CHIP_FLIPS):
            for i in range(self.n):
                self._copy(refs, i, 1 + j, _flip(me, d), me).wait_recv()
                self._passed(refs, i, j).start()

    def finish(self, refs):
        me = _coords()
        sib = _flip(me, 1)
        for i in range(self.n):
            self._copy(refs, i, 0, sib, me).wait_recv()
        for j, d in enumerate(CHIP_FLIPS):
            for i in range(self.n):
                self._copy(refs, i, 4 + j, _flip(sib, d), me).wait_recv()
        for i in range(self.n):
            for k in range(4):
                self._first(refs, i, k).wait_send()
            for j in range(3):
                self._passed(refs, i, j).wait_send()
            self._local(refs, i).wait()


class _Scatter:
    def __init__(self, arrays):
        self.arrays = list(arrays)
        self.n = n = len(arrays)
        self.mid_frac = None
        self.out_shape = [jax.ShapeDtypeStruct(a.shape, a.dtype) for a in arrays]
        self.scratch = [pltpu.SemaphoreType.DMA((n, 7)), pltpu.SemaphoreType.DMA((n, 7)),
                        pltpu.SemaphoreType.DMA((n,))]

    def _copy(self, refs, i, d, landing):
        ins, outs, (send, recv, _) = refs
        me = _coords()
        peer = _flip(me, d)
        return pltpu.make_async_remote_copy(
            src_ref=ins[i].at[_slot(peer)], dst_ref=outs[i].at[_slot(peer) if landing else _slot(me)],
            send_sem=send.at[i, d - 1], recv_sem=recv.at[i, d - 1], device_id=peer, device_id_type=MESH)

    def _local(self, refs, i):
        ins, outs, (_, _, loc) = refs
        me = _slot(_coords())
        return pltpu.make_async_copy(ins[i].at[me], outs[i].at[me], loc.at[i])

    def start(self, refs):
        for i in range(self.n):
            self._local(refs, i).start()
        for d in range(1, NDEV):
            for i in range(self.n):
                self._copy(refs, i, d, False).start()

    def forward(self, refs):
        pass

    def finish(self, refs):
        for d in range(1, NDEV):
            for i in range(self.n):
                self._copy(refs, i, d, True).wait_recv()
        for d in range(1, NDEV):
            for i in range(self.n):
                self._copy(refs, i, d, False).wait_send()
        for i in range(self.n):
            self._local(refs, i).wait()


class _Both:
    def __init__(self, a, b):
        self.parts = (a, b)
        self.arrays = a.arrays + b.arrays
        self.n = a.n + b.n
        self.mid_frac = a.mid_frac if a.mid_frac is not None else b.mid_frac
        self.out_shape = a.out_shape + b.out_shape
        self.scratch = a.scratch + b.scratch

    def _each(self, refs):
        ins, outs, sems = refs
        na, ns = self.parts[0].n, len(self.parts[0].scratch)
        return ((self.parts[0], (ins[:na], outs[:na], sems[:ns])), (self.parts[1], (ins[na:], outs[na:], sems[ns:])))

    def start(self, refs):
        for part, r in self._each(refs):
            part.start(r)

    def forward(self, refs):
        for part, r in self._each(refs):
            part.forward(r)

    def finish(self, refs):
        for part, r in self._each(refs):
            part.finish(r)


def _call(body, name, grid, in_specs, out_specs, out_shape, scratch, args, comm=None, start_frac=0.0):
    params = _params(*(["arbitrary"] * len(grid)))
    if comm is None:
        outs = pl.pallas_call(
            body, name=name, grid=grid, in_specs=in_specs, out_specs=out_specs, out_shape=out_shape,
            scratch_shapes=scratch, compiler_params=params)(*args)
        return list(outs), []
    n_in, n_out, n_scr, k = len(in_specs), len(out_specs), len(scratch), comm.n
    nsteps = math.prod(grid) if grid else 1
    first = int(nsteps * start_frac)
    mid = max(first, min(nsteps - 1, int(nsteps * comm.mid_frac))) if comm.mid_frac is not None else None

    def hosted(*refs):
        ins, refs = refs[:n_in], refs[n_in:]
        cins, refs = refs[:k], refs[k:]
        outs, refs = refs[:n_out], refs[n_out:]
        couts, refs = refs[:k], refs[k:]
        scr, sems = refs[:n_scr], refs[n_scr:]
        crefs = (cins, couts, sems)
        if nsteps == 1:
            comm.start(crefs)
            body(*ins, *outs, *scr)
            comm.forward(crefs)
            comm.finish(crefs)
            return
        step = pl.program_id(0)
        for a in range(1, len(grid)):
            step = step * grid[a] + pl.program_id(a)
        pl.when(step == first)(lambda: comm.start(crefs))
        if mid is not None:
            pl.when(step == mid)(lambda: comm.forward(crefs))
        body(*ins, *outs, *scr)
        pl.when(step == nsteps - 1)(lambda: comm.finish(crefs))

    any_spec = pl.BlockSpec(memory_space=pl.ANY)
    outs = pl.pallas_call(
        hosted, name=name, grid=grid,
        in_specs=list(in_specs) + [any_spec] * k, out_specs=list(out_specs) + [any_spec] * k,
        out_shape=list(out_shape) + comm.out_shape, scratch_shapes=list(scratch) + comm.scratch,
        compiler_params=params)(*args, *comm.arrays)
    return list(outs[:n_out]), list(outs[n_out:])


def _exchange(comm, name):
    return _call(lambda: None, name, (), [], [], [], [], [], comm)[1]


def _rms_matmul(x, g, w, name, comm=None):
    T = x.shape[0]
    nb = w.shape[2]
    tT = _tile(T, 1024)
    per = 2

    def body(x_ref, g_ref, w_ref, z_ref, h_ref):
        @pl.when(pl.program_id(1) == 0)
        def _():
            xf = x_ref[...]
            r = lax.rsqrt(jnp.mean(xf * xf, axis=-1, keepdims=True) + RMS_EPS)
            h_ref[...] = (xf * r * g_ref[...]).astype(h_ref.dtype)

        for j in range(per):
            z_ref[:, j * nb:(j + 1) * nb] = jnp.dot(h_ref[...], w_ref[j], preferred_element_type=F32
                                                    ).astype(z_ref.dtype)

    return _call(
        body, name, (T // tT, NDEV // per),
        [pl.BlockSpec((tT, D), lambda t, n: (t, 0)),
         pl.BlockSpec((1, D), lambda t, n: (0, 0)),
         pl.BlockSpec((per, D, nb), lambda t, n: (n, 0, 0))],
        [pl.BlockSpec((tT, per * nb), lambda t, n: (t, n)),
         pl.BlockSpec((tT, D), lambda t, n: (t, 0))],
        [jax.ShapeDtypeStruct((T, NDEV * nb), ACT_DTYPE), jax.ShapeDtypeStruct((T, D), MXU_DTYPE)],
        [], (x, g.reshape(1, D), w), comm)


ARRIVAL = (0, 1) + CHIP_FLIPS + tuple(d ^ 1 for d in CHIP_FLIPS)


def _rms_matmul_gathering(x, g, w_shard, name, comm=None):
    T = x.shape[0]
    nb = w_shard.shape[1]
    tT = _tile(T, 1024)
    nT = T // tT

    def body(x_ref, g_ref, wsh_ref, z_ref, h_ref, wfull_ref, h_all, wbuf, zbuf, fetch_sems, z_sems, send_sems,
             recv_sems, own_sem):
        n, t = pl.program_id(0), pl.program_id(1)
        step = n * nT + t
        me = _coords()
        sib = _flip(me, 1)

        def remote(k, block, to, own=False):
            slot = wfull_ref.at[_slot(block)]
            return pltpu.make_async_remote_copy(
                src_ref=wsh_ref if own else slot, dst_ref=slot, send_sem=send_sems.at[k], recv_sem=recv_sems.at[k],
                device_id=to, device_id_type=MESH)

        def first(k):
            return remote(k, me, _flip(me, ARRIVAL[1 + k]), own=True)

        def passed(j):
            return remote(4 + j, _flip(me, CHIP_FLIPS[j]), sib)

        own_copy = pltpu.make_async_copy(wsh_ref, wfull_ref.at[_slot(me)], own_sem)

        def fetch(src, nn):
            return pltpu.make_async_copy(src, wbuf.at[nn % 2], fetch_sems.at[nn % 2])

        @pl.when(step == 0)
        def _():
            own_copy.start()
            for k in range(3):
                first(k).start()
            fetch(wsh_ref, 0).start()

        @pl.when(t == 0)
        def _():
            fetch(wfull_ref.at[0], n).wait()

        @pl.when(n == 0)
        def _():
            xf = x_ref[...]
            r = lax.rsqrt(jnp.mean(xf * xf, axis=-1, keepdims=True) + RMS_EPS)
            h = (xf * r * g_ref[...]).astype(h_ref.dtype)
            h_ref[...] = h
            h_all[t] = h

        def z_copy(s, col):
            return pltpu.make_async_copy(
                zbuf.at[s % 2], z_ref.at[pl.ds(pl.multiple_of(t * tT, tT), tT), pl.ds(col * nb, nb)],
                z_sems.at[s % 2])

        @pl.when(step >= 2)
        def _():
            z_copy(step, 0).wait()

        d = sum(jnp.where(n == nn, ARRIVAL[nn], 0) for nn in range(NDEV))
        col = _slot((me[0] ^ ((d >> 2) & 1), me[1] ^ ((d >> 1) & 1), me[2] ^ (d & 1)))
        zbuf[step % 2] = jnp.dot(h_all[t], wbuf[n % 2], preferred_element_type=F32).astype(zbuf.dtype)
        z_copy(step, col).start()

        for nn in range(1, NDEV):
            @pl.when((n == nn - 1) & (t == nT - 1))
            def _(nn=nn):
                block = _flip(me, ARRIVAL[nn])
                if nn == 1:
                    remote(0, sib, me).wait_recv()
                elif nn < 5:
                    if nn == 2:
                        first(3).start()
                    remote(nn - 1, block, me).wait_recv()
                    passed(nn - 2).start()
                else:
                    remote(nn - 1, block, me).wait_recv()
                fetch(wfull_ref.at[_slot(block)], nn).start()

        @pl.when(step == NDEV * nT - 1)
        def _():
            z_copy(step - 1, 0).wait()
            z_copy(step, 0).wait()
            for k in range(4):
                first(k).wait_send()
            for j in range(3):
                passed(j).wait_send()
            own_copy.wait()

    keep = lambda n, t: (jnp.where(n == 0, t, nT - 1), 0)
    any_spec = pl.BlockSpec(memory_space=pl.ANY)
    return _call(
        body, name, (NDEV, nT),
        [pl.BlockSpec((tT, D), keep), pl.BlockSpec((1, D), lambda n, t: (0, 0)), any_spec],
        [any_spec, pl.BlockSpec((tT, D), keep), any_spec],
        [jax.ShapeDtypeStruct((T, NDEV * nb), ACT_DTYPE), jax.ShapeDtypeStruct((T, D), MXU_DTYPE),
         jax.ShapeDtypeStruct((NDEV,) + w_shard.shape, w_shard.dtype)],
        [pltpu.VMEM((nT, tT, D), MXU_DTYPE), pltpu.VMEM((2, D, nb), w_shard.dtype),
         pltpu.VMEM((2, tT, nb), ACT_DTYPE), pltpu.SemaphoreType.DMA((2,)), pltpu.SemaphoreType.DMA((2,)),
         pltpu.SemaphoreType.DMA((7,)), pltpu.SemaphoreType.DMA((7,)), pltpu.SemaphoreType.DMA],
        (x, g.reshape(1, D), w_shard), comm, start_frac=0.5)


def _rms_matmul_bwd(dz, w, x, g, dxo, name, comm=None):
    T = x.shape[0]
    nb = w.shape[2]
    tT = _tile(T, 512)
    per = 4
    steps = NDEV // per

    def body(dz_ref, w_ref, x_ref, g_ref, dxo_ref, dx_ref, dg_ref, acc):
        t, n = pl.program_id(0), pl.program_id(1)

        @pl.when((n == 0) & (t == 0))
        def _():
            dg_ref[...] = jnp.zeros_like(dg_ref)

        part = None
        for j in range(per):
            d = lax.dot_general(_mx(dz_ref[:, j * nb:(j + 1) * nb]), w_ref[j], NT_DIMS, preferred_element_type=F32)
            part = d if part is None else part + d

        @pl.when(n == 0)
        def _():
            acc[...] = part

        @pl.when(n > 0)
        def _():
            acc[...] += part

        @pl.when(n == steps - 1)
        def _():
            xf = x_ref[...]
            r = lax.rsqrt(jnp.mean(xf * xf, axis=-1, keepdims=True) + RMS_EPS)
            xh = xf * r
            dh = acc[...]
            dxh = dh * g_ref[...]
            dx_ref[...] = dxo_ref[...] + r * (dxh - xh * jnp.mean(dxh * xh, axis=-1, keepdims=True))
            dg_ref[...] += _rowsum(dh * xh)

    return _call(
        body, name, (T // tT, steps),
        [pl.BlockSpec((tT, per * nb), lambda t, n: (t, n)),
         pl.BlockSpec((per, D, nb), lambda t, n: (n, 0, 0)),
         pl.BlockSpec((tT, D), lambda t, n: (t, 0)),
         pl.BlockSpec((1, D), lambda t, n: (0, 0)),
         pl.BlockSpec((tT, D), lambda t, n: (t, 0))],
        [pl.BlockSpec((tT, D), lambda t, n: (t, 0)),
         pl.BlockSpec((1, D), lambda t, n: (0, 0))],
        [jax.ShapeDtypeStruct((T, D), F32), jax.ShapeDtypeStruct((1, D), F32)],
        [pltpu.VMEM((tT, D), F32)], (dz, w, x, g.reshape(1, D), dxo), comm)


def _atb(a, ai, b, bi, nblk, name, comm=None):
    T, M = a.shape[1:]
    N = b.shape[2]
    nb = N // nblk
    tk = _tile(T, 2048)
    nk = T // tk

    def body(a_ref, b_ref, o_ref, acc):
        k = pl.program_id(1)

        @pl.when(k == 0)
        def _():
            acc[...] = jnp.zeros_like(acc)

        acc[...] += lax.dot_general(_mx(a_ref[...]), _mx(b_ref[...]), TN_DIMS, preferred_element_type=F32)

        @pl.when(k == nk - 1)
        def _():
            o_ref[...] = acc[...].astype(o_ref.dtype)

    outs, couts = _call(
        body, name, (nblk, nk),
        [pl.BlockSpec((None, tk, M), lambda n, k: (ai, k, 0)),
         pl.BlockSpec((None, tk, nb), lambda n, k: (bi, k, n))],
        [pl.BlockSpec((None, M, nb), lambda n, k: (n, 0, 0))],
        [jax.ShapeDtypeStruct((nblk, M, nb), GRAD_DTYPE)],
        [pltpu.VMEM((M, nb), F32)], (a, b), comm)
    return outs[0], couts


def _loss_head(x, tgt, g, name):
    T = x.shape[0]
    tT = _tile(T, 512)

    def body(x_ref, t_ref, g_ref, loss_ref, dx_ref, dg_ref):
        @pl.when(pl.program_id(0) == 0)
        def _():
            loss_ref[...] = jnp.zeros_like(loss_ref)
            dg_ref[...] = jnp.zeros_like(dg_ref)

        xf = x_ref[...]
        r = lax.rsqrt(jnp.mean(xf * xf, axis=-1, keepdims=True) + RMS_EPS)
        xh = xf * r
        err = xh * g_ref[...] - t_ref[...]
        loss_ref[...] += 0.5 * jnp.sum(jnp.mean(err * err, axis=-1, keepdims=True), axis=0, keepdims=True)
        dy = err * (1.0 / D)
        dxh = dy * g_ref[...]
        dx_ref[...] = r * (dxh - xh * jnp.mean(dxh * xh, axis=-1, keepdims=True))
        dg_ref[...] += _rowsum(dy * xh)

    return pl.pallas_call(
        body, name=name, grid=(T // tT,),
        in_specs=[pl.BlockSpec((tT, D), lambda t: (t, 0)),
                  pl.BlockSpec((tT, D), lambda t: (t, 0)),
                  pl.BlockSpec((1, D), lambda t: (0, 0))],
        out_specs=[pl.BlockSpec((1, 1), lambda t: (0, 0)),
                   pl.BlockSpec((tT, D), lambda t: (t, 0)),
                   pl.BlockSpec((1, D), lambda t: (0, 0))],
        out_shape=[jax.ShapeDtypeStruct((1, 1), F32), jax.ShapeDtypeStruct((T, D), F32),
                   jax.ShapeDtypeStruct((1, D), F32)],
        compiler_params=_params("arbitrary"),
    )(x, tgt, g.reshape(1, D))


def _spatial_gate(wm_ref, bst_ref, vb_ref, sv_ref, n_chunks):
    for c in range(n_chunks):
        rows = slice(c * CHUNK, (c + 1) * CHUNK)
        for g in range(N_GROUPS):
            cols = slice(g * CHUNK, (g + 1) * CHUNK)
            sv_ref[rows, cols] = (jnp.dot(wm_ref[g], vb_ref[rows, cols], preferred_element_type=F32)
                                  + bst_ref[:, g:g + 1])


def _lane_loop(fn):
    def step(i, carry):
        fn(pl.ds(pl.multiple_of(i * LANES, LANES), LANES))
        return carry

    lax.fori_loop(0, D // LANES, step, 0)


def _shifted_copies(buf, sh, n):
    for s in range(1, 8):
        sh[s - 1, 0:n, :] = buf[s:s + n, :]


def _window(buf, sh, base, off, lanes):
    a, s = divmod(off, 8)
    src = buf if s == 0 else sh.at[s - 1]
    return src[base + 8 * a:base + 8 * a + 8, lanes]


def _softmax_rows(s):
    e = jnp.exp(s - jnp.max(s, axis=-1, keepdims=True))
    return e / jnp.sum(e, axis=-1, keepdims=True)


def _branch_fwd(z, kv, wm, bst, ln_a, cw, cvec, name, comm=None):
    T = z.shape[0]
    tT = _tile(T, 256)
    n_chunks = tT // CHUNK

    def body(z_ref, kv_ref, wm_ref, bst_ref, lna_ref, cw_ref, cvec_ref, br_ref, c_ref, gbuf, gsh, vb, ua):
        @pl.when(pl.program_id(0) == 0)
        def _():
            gbuf[0:HALO, :] = jnp.zeros((HALO, D), F32)

        def seg(s):
            return z_ref[:, s * D:(s + 1) * D].astype(F32)

        u, _ = _gelu(seg(SEG_AU))
        zg = seg(SEG_AG)
        ua[...] = u * (zg * jax.nn.sigmoid(zg))
        gv, _ = _gelu(seg(SEG_AV))
        vhat, _ = _ln_stats(gv)
        vb[...] = _mx(vhat * lna_ref[0:1, :] + lna_ref[1:2, :])
        for c in range(n_chunks):
            rows = slice(c * CHUNK, (c + 1) * CHUNK)
            for g in range(N_GROUPS):
                cols = slice(g * CHUNK, (g + 1) * CHUNK)
                sv = jnp.dot(wm_ref[g], vb[rows, cols], preferred_element_type=F32) + bst_ref[:, g:g + 1]
                br_ref[0, rows, cols] = (sv * ua[rows, cols]).astype(br_ref.dtype)

        gbuf[HALO:HALO + tT, :] = seg(SEG_BA) * jax.nn.sigmoid(seg(SEG_BB))
        _shifted_copies(gbuf, gsh, tT + HALO - 8)
        def conv_lanes(lanes):
            taps = [jnp.broadcast_to(cw_ref[k:k + 1, lanes], (8, LANES)) for k in range(CONV_K)]
            bias = jnp.broadcast_to(cvec_ref[0:1, lanes], (8, LANES))
            for base in range(0, tT, 8):
                acc = [bias, None, None, None]
                for k in range(CONV_K):
                    term = taps[k] * _window(gbuf, gsh, base, k + HALO - CONV_K + 1, lanes)
                    acc[k % 4] = term if acc[k % 4] is None else acc[k % 4] + term
                c_ref[base:base + 8, lanes] = (acc[0] + acc[1]) + (acc[2] + acc[3])

        _lane_loop(conv_lanes)
        gbuf[0:HALO, :] = gbuf[tT:tT + HALO, :]
        chat, _ = _ln_stats(c_ref[...])
        cl = chat * cvec_ref[1:2, :] + cvec_ref[2:3, :]
        zg = seg(SEG_BG)
        br_ref[1] = (cl * jax.nn.sigmoid(cl) * (zg * jax.nn.sigmoid(zg))).astype(br_ref.dtype)

        for h in range(HEADS):
            cols = slice(h * HEAD_DIM, (h + 1) * HEAD_DIM)
            q = _mx(z_ref[:, SEG_CQ * D + h * HEAD_DIM:SEG_CQ * D + (h + 1) * HEAD_DIM])
            s = lax.dot_general(q, kv_ref[:, cols], NT_DIMS, preferred_element_type=F32)
            p = _softmax_rows(s * (1.0 / math.sqrt(HEAD_DIM)))
            att = jnp.dot(_mx(p), kv_ref[:, D + h * HEAD_DIM:D + (h + 1) * HEAD_DIM], preferred_element_type=F32)
            zg = z_ref[:, SEG_CG * D + h * HEAD_DIM:SEG_CG * D + (h + 1) * HEAD_DIM].astype(F32)
            br_ref[2, :, cols] = (att * (zg * jax.nn.sigmoid(zg))).astype(br_ref.dtype)

    full = lambda shape: pl.BlockSpec(shape, lambda t: (0,) * len(shape))
    return _call(
        body, name, (T // tT,),
        [pl.BlockSpec((tT, SEG_M * D), lambda t: (t, 0)),
         full(kv.shape), full(wm.shape), full(bst.shape), full(ln_a.shape), full(cw.shape), full(cvec.shape)],
        [pl.BlockSpec((3, tT, D), lambda t: (0, t, 0)), pl.BlockSpec((tT, D), lambda t: (t, 0))],
        [jax.ShapeDtypeStruct((3, T, D), MXU_DTYPE), jax.ShapeDtypeStruct((T, D), F32)],
        [pltpu.VMEM((tT + HALO, D), F32), pltpu.VMEM((7, tT + HALO - 8, D), F32),
         pltpu.VMEM((tT, D), MXU_DTYPE), pltpu.VMEM((tT, D), F32)],
        (z, kv, wm, bst, ln_a, cw, cvec), comm)


def _branch_bwd(z, dbr, c, dzm, kv, wm, wmt, bst, ln_a, cw, cvec, name, comm=None):
    T = z.shape[0]
    M = kv.shape[0]
    tT = _tile(T, 128)
    nT = T // tT
    n_chunks = tT // CHUNK

    def body(z_ref, dbr_ref, c_ref, dzm_ref, kv_ref, wm_ref, wmt_ref, bst_ref, lna_ref,
             cw_ref, cvec_ref, dz_ref, vecg_ref, dbst_ref, dws_ref, dcw_ref, dkv_ref,
             gbuf, dcbuf, vb, dsvb, sv, dvbuf, dcsh, dglu, vh, gq, dcw8, dcw_step):
        i = pl.program_id(0)

        @pl.when(i == 0)
        def _():
            vecg_ref[...] = jnp.zeros_like(vecg_ref)
            dbst_ref[...] = jnp.zeros_like(dbst_ref)
            dws_ref[...] = jnp.zeros_like(dws_ref)
            dcw8[...] = jnp.zeros_like(dcw8)
            dkv_ref[...] = jnp.zeros_like(dkv_ref)
            dcbuf[tT:tT + HALO, :] = jnp.zeros((HALO, D), F32)

        strips = [slice(r0, r0 + STRIP) for r0 in range(0, tT, STRIP)]

        def seg(r, s):
            return z_ref[r, s * D:(s + 1) * D].astype(F32)

        def put(r, s, val):
            dz_ref[r, s * D:(s + 1) * D] = val.astype(dz_ref.dtype)

        for r in strips:
            zv = seg(r, SEG_AV)
            gv, tv = _gelu(zv)
            vhat, rstd = _ln_stats(gv)
            vb[r, :] = _mx(vhat * lna_ref[0:1, :] + lna_ref[1:2, :])
            vh[r, :] = vhat
            gq[r, :] = rstd * _gelu_grad(zv, tv)
        _spatial_gate(wm_ref, bst_ref, vb, sv, n_chunks)
        for r in strips:
            zu, zg = seg(r, SEG_AU), seg(r, SEG_AG)
            u, tu = _gelu(zu)
            sg = jax.nn.sigmoid(zg)
            d_a = dbr_ref[0, r, :].astype(F32)
            put(r, SEG_AU, d_a * sv[r, :] * (zg * sg) * _gelu_grad(zu, tu))
            put(r, SEG_AG, d_a * u * sv[r, :] * _silu_grad(zg, sg))
            dsv = d_a * u * (zg * sg)
            dsvb[r, :] = _mx(dsv)
            in_chunk = slice(r.start % CHUNK, r.start % CHUNK + STRIP)
            for g in range(N_GROUPS):
                dbst_ref[in_chunk, g:g + 1] += jnp.sum(dsv[:, g * CHUNK:(g + 1) * CHUNK], axis=-1, keepdims=True)
        tril = (lax.broadcasted_iota(jnp.int32, (CHUNK, CHUNK), 0)
                >= lax.broadcasted_iota(jnp.int32, (CHUNK, CHUNK), 1))
        for g in range(N_GROUPS):
            cols = slice(g * CHUNK, (g + 1) * CHUNK)
            for cc in range(n_chunks):
                rows = slice(cc * CHUNK, (cc + 1) * CHUNK)
                dws = lax.dot_general(dsvb[rows, cols], vb[rows, cols], NT_DIMS, preferred_element_type=F32)
                dws_ref[g] += jnp.where(tril, dws, 0.0)
                dvbuf[rows, cols] = jnp.dot(wmt_ref[g], dsvb[rows, cols], preferred_element_type=F32)
        for r in strips:
            dv, vhat = dvbuf[r, :], vh[r, :]
            vecg_ref[0:1, :] += _rowsum(dv * vhat)
            vecg_ref[1:2, :] += _rowsum(dv)
            dvh = dv * lna_ref[0:1, :]
            put(r, SEG_AV, (dvh - jnp.mean(dvh, axis=-1, keepdims=True)
                            - vhat * jnp.mean(dvh * vhat, axis=-1, keepdims=True)) * gq[r, :])

        sgb_buf = sv
        for r in strips:
            za, zg = seg(r, SEG_BA), seg(r, SEG_BG)
            sgb = jax.nn.sigmoid(seg(r, SEG_BB))
            sgb_buf[r, :] = sgb
            gbuf[r, :] = za * sgb
            chat, crstd = _ln_stats(c_ref[r, :])
            cl = chat * cvec_ref[1:2, :] + cvec_ref[2:3, :]
            scl = jax.nn.sigmoid(cl)
            sg = jax.nn.sigmoid(zg)
            d_b = dbr_ref[1, r, :].astype(F32)
            put(r, SEG_BG, d_b * (cl * scl) * _silu_grad(zg, sg))
            dcl = d_b * (zg * sg) * _silu_grad(cl, scl)
            vecg_ref[3:4, :] += _rowsum(dcl * chat)
            vecg_ref[4:5, :] += _rowsum(dcl)
            dc = _ln_grad(dcl, cvec_ref[1:2, :], chat, crstd)
            vecg_ref[2:3, :] += _rowsum(dc)
            dcbuf[r, :] = dc
        _shifted_copies(dcbuf, dcsh, tT + HALO - 8)

        def conv_grads(lanes):
            taps = [jnp.broadcast_to(cw_ref[k:k + 1, lanes], (8, LANES)) for k in range(CONV_K)]
            wsum = [None] * CONV_K
            for base in range(0, tT, 8):
                glu = gbuf[base:base + 8, lanes]
                acc = [None] * 4
                for k in range(CONV_K):
                    win = _window(dcbuf, dcsh, base, CONV_K - 1 - k, lanes)
                    term = taps[k] * win
                    acc[k % 4] = term if acc[k % 4] is None else acc[k % 4] + term
                    term = glu * win
                    wsum[k] = term if wsum[k] is None else wsum[k] + term
                dglu[base:base + 8, lanes] = (acc[0] + acc[1]) + (acc[2] + acc[3])
            for k in range(CONV_K):
                dcw_step[8 * k:8 * k + 8, lanes] = wsum[k]

        _lane_loop(conv_grads)
        dcw8[...] += dcw_step[...]

        @pl.when(i == nT - 1)
        def _():
            for k in range(CONV_K):
                dcw_ref[k:k + 1, :] = _rowsum(dcw8[8 * k:8 * k + 8, :])
            dcw_ref[CONV_K:HALO, :] = jnp.zeros((HALO - CONV_K, D), F32)

        dcbuf[tT:tT + HALO, :] = dcbuf[0:HALO, :]
        for r in strips:
            dg, sgb = dglu[r, :], sgb_buf[r, :]
            put(r, SEG_BA, dg * sgb)
            put(r, SEG_BB, dg * seg(r, SEG_BA) * sgb * (1.0 - sgb))

        scale = 1.0 / math.sqrt(HEAD_DIM)
        for h in range(HEADS):
            cols = slice(h * HEAD_DIM, (h + 1) * HEAD_DIM)
            qcols = slice(SEG_CQ * D + h * HEAD_DIM, SEG_CQ * D + (h + 1) * HEAD_DIM)
            gcols = slice(SEG_CG * D + h * HEAD_DIM, SEG_CG * D + (h + 1) * HEAD_DIM)
            vcols = slice(D + h * HEAD_DIM, D + (h + 1) * HEAD_DIM)
            q = _mx(z_ref[:, qcols])
            kh, vh = kv_ref[:, cols], kv_ref[:, vcols]
            p = _softmax_rows(lax.dot_general(q, kh, NT_DIMS, preferred_element_type=F32) * scale)
            pb = _mx(p)
            att = jnp.dot(pb, vh, preferred_element_type=F32)
            zg = z_ref[:, gcols].astype(F32)
            sg = jax.nn.sigmoid(zg)
            d_c = dbr_ref[2, :, cols].astype(F32)
            dz_ref[:, gcols] = (d_c * att * _silu_grad(zg, sg)).astype(dz_ref.dtype)
            datt = _mx(d_c * (zg * sg))
            dp = lax.dot_general(datt, vh, NT_DIMS, preferred_element_type=F32)
            dkv_ref[:, vcols] += lax.dot_general(pb, datt, TN_DIMS, preferred_element_type=F32)
            ds = _mx(p * (dp - jnp.sum(dp * p, axis=-1, keepdims=True)) * scale)
            dz_ref[:, qcols] = jnp.dot(ds, kh, preferred_element_type=F32).astype(dz_ref.dtype)
            dkv_ref[:, cols] += lax.dot_general(ds, q, TN_DIMS, preferred_element_type=F32)

        dz_ref[:, SEG_M * D:] = dzm_ref[...].astype(dz_ref.dtype)

    rev = lambda i: nT - 1 - i
    full = lambda shape: pl.BlockSpec(shape, lambda i: (0,) * len(shape))
    return _call(
        body, name, (nT,),
        [pl.BlockSpec((tT, SEG_M * D), lambda i: (rev(i), 0)),
         pl.BlockSpec((3, tT, D), lambda i: (0, rev(i), 0)),
         pl.BlockSpec((tT, D), lambda i: (rev(i), 0)),
         pl.BlockSpec((tT, 3 * D), lambda i: (rev(i), 0)),
         full(kv.shape), full(wm.shape), full(wmt.shape), full(bst.shape), full(ln_a.shape),
         full(cw.shape), full(cvec.shape)],
        [pl.BlockSpec((tT, N_IN), lambda i: (rev(i), 0)),
         full((8, D)), full((CHUNK, N_GROUPS)), full((N_GROUPS, CHUNK, CHUNK)), full((HALO, D)),
         full((M, 2 * D))],
        [jax.ShapeDtypeStruct((T, N_IN), MXU_DTYPE), jax.ShapeDtypeStruct((8, D), F32),
         jax.ShapeDtypeStruct((CHUNK, N_GROUPS), F32),
         jax.ShapeDtypeStruct((N_GROUPS, CHUNK, CHUNK), F32),
         jax.ShapeDtypeStruct((HALO, D), F32), jax.ShapeDtypeStruct((M, 2 * D), F32)],
        [pltpu.VMEM((tT, D), F32), pltpu.VMEM((tT + HALO, D), F32),
         pltpu.VMEM((tT, D), MXU_DTYPE), pltpu.VMEM((tT, D), MXU_DTYPE),
         pltpu.VMEM((tT, D), F32), pltpu.VMEM((tT, D), F32),
         pltpu.VMEM((7, tT + HALO - 8, D), F32),
         pltpu.VMEM((tT, D), F32), pltpu.VMEM((tT, D), F32), pltpu.VMEM((tT, D), F32),
         pltpu.VMEM((CONV_K * 8, D), F32), pltpu.VMEM((CONV_K * 8, D), F32)],
        (z, dbr, c, dzm, kv, wm, wmt, bst, ln_a, cw, cvec), comm)


def _merge_fwd(br, z, x, wb, wo, name):
    T = x.shape[0]
    tT = _tile(T, 512)

    def body(br_ref, z0, z1, z2, x_ref, wb_ref, wo_ref, xn_ref, mg_ref, pj_ref):
        merged = jnp.zeros((tT, D), F32)
        for n, zm in enumerate((z0, z1, z2)):
            proj = jnp.dot(br_ref[n], wb_ref[:, n].reshape(D, D), preferred_element_type=F32)
            pj_ref[n] = proj.astype(pj_ref.dtype)
            merged = merged + jax.nn.sigmoid(zm[...].astype(F32)) * proj
        mg_ref[...] = merged.astype(mg_ref.dtype)
        xn_ref[...] = x_ref[...] + jnp.dot(_mx(merged), wo_ref[...].reshape(D, D), preferred_element_type=F32)

    zspec = lambda n: pl.BlockSpec((tT, D), lambda t: (t, SEG_M + n))
    return pl.pallas_call(
        body, name=name, grid=(T // tT,),
        in_specs=[pl.BlockSpec((3, tT, D), lambda t: (0, t, 0)), zspec(0), zspec(1), zspec(2),
                  pl.BlockSpec((tT, D), lambda t: (t, 0)),
                  pl.BlockSpec(wb.shape, lambda t: (0, 0, 0, 0)),
                  pl.BlockSpec(wo.shape, lambda t: (0, 0, 0))],
        out_specs=[pl.BlockSpec((tT, D), lambda t: (t, 0)),
                   pl.BlockSpec((tT, D), lambda t: (t, 0)),
                   pl.BlockSpec((3, tT, D), lambda t: (0, t, 0))],
        out_shape=[jax.ShapeDtypeStruct((T, D), F32), jax.ShapeDtypeStruct((T, D), MXU_DTYPE),
                   jax.ShapeDtypeStruct((3, T, D), ACT_DTYPE)],
        compiler_params=_params("parallel"),
    )(br, z, z, z, x, wb, wo)


def _merge_bwd(dxo, proj, z, wb, wo, name):
    T = dxo.shape[0]
    tT = _tile(T, 512)

    def body(dxo_ref, pj_ref, z0, z1, z2, wb_ref, wo_ref, dpj_ref, dbr_ref, dzm_ref):
        dmerged = lax.dot_general(_mx(dxo_ref[...]), wo_ref[...].reshape(D, D), NT_DIMS,
                                  preferred_element_type=F32)
        for n, zm in enumerate((z0, z1, z2)):
            gate = jax.nn.sigmoid(zm[...].astype(F32))
            dproj = _mx(gate * dmerged)
            dpj_ref[n] = dproj
            dzm_ref[:, n * D:(n + 1) * D] = (pj_ref[n].astype(F32) * dmerged * gate * (1.0 - gate)
                                             ).astype(dzm_ref.dtype)
            dbr_ref[n] = lax.dot_general(dproj, wb_ref[:, n].reshape(D, D), NT_DIMS,
                                         preferred_element_type=F32).astype(dbr_ref.dtype)

    zspec = lambda n: pl.BlockSpec((tT, D), lambda t: (t, SEG_M + n))
    return pl.pallas_call(
        body, name=name, grid=(T // tT,),
        in_specs=[pl.BlockSpec((tT, D), lambda t: (t, 0)),
                  pl.BlockSpec((3, tT, D), lambda t: (0, t, 0)), zspec(0), zspec(1), zspec(2),
                  pl.BlockSpec(wb.shape, lambda t: (0, 0, 0, 0)),
                  pl.BlockSpec(wo.shape, lambda t: (0, 0, 0))],
        out_specs=[pl.BlockSpec((3, tT, D), lambda t: (0, t, 0)),
                   pl.BlockSpec((3, tT, D), lambda t: (0, t, 0)),
                   pl.BlockSpec((tT, 3 * D), lambda t: (t, 0))],
        out_shape=[jax.ShapeDtypeStruct((3, T, D), MXU_DTYPE), jax.ShapeDtypeStruct((3, T, D), ACT_DTYPE),
                   jax.ShapeDtypeStruct((T, 3 * D), MXU_DTYPE)],
        compiler_params=_params("parallel"),
    )(dxo, proj, z, z, z, wb, wo)


def _adamw(parts, w, m, v, idx, name, comm=None):
    R, C = parts.shape[1:]
    tr = 128 if R % 128 == 0 else R
    c1 = 1.0 / (1.0 - ADAM_B1 ** ADAM_STEP)
    c2 = 1.0 / (1.0 - ADAM_B2 ** ADAM_STEP)

    def body(p_ref, w_ref, m_ref, v_ref, g_out, d_out, m_out, v_out):
        g = p_ref[0].astype(F32)
        for p in range(1, NDEV):
            g = g + p_ref[p].astype(F32)
        mn = ADAM_B1 * m_ref[...] + (1.0 - ADAM_B1) * g
        vn = ADAM_B2 * v_ref[...] + (1.0 - ADAM_B2) * (g * g)
        g_out[...] = g
        m_out[...] = mn
        v_out[...] = vn
        d_out[...] = -ADAM_LR * ((mn * c1) / (jnp.sqrt(vn * c2) + ADAM_EPS) + ADAM_WD * w_ref[...])

    wspec = pl.BlockSpec((None, tr, C), lambda r: (idx, r, 0))
    ospec = pl.BlockSpec((tr, C), lambda r: (r, 0))
    return _call(
        body, name, (R // tr,),
        [pl.BlockSpec((NDEV, tr, C), lambda r: (0, r, 0)), wspec, wspec, wspec],
        [ospec] * 4, [jax.ShapeDtypeStruct((R, C), F32)] * 4, [], (parts, w, m, v), comm)


def kernel(x, mem, norm_g, mem_norm_g, w_in, gmlp_ln_g, gmlp_ln_b, w_s, b_s, conv_w, conv_b, conv_ln_g, conv_ln_b, w_kv, w_branch, w_out, final_norm_g, loss_target, m_norm_g, m_mem_norm_g, m_w_in, m_gmlp_ln_g, m_gmlp_ln_b, m_w_s, m_b_s, m_conv_w, m_conv_b, m_conv_ln_g, m_conv_ln_b, m_w_kv, m_w_branch, m_w_out, m_final_norm_g, v_norm_g, v_mem_norm_g, v_w_in, v_gmlp_ln_g, v_gmlp_ln_b, v_w_s, v_b_s, v_conv_w, v_conv_b, v_conv_ln_g, v_conv_ln_b, v_w_kv, v_w_branch, v_w_out, v_final_norm_g):
    L = w_in.shape[0]
    x0, mem0, tgt = x[0], mem[0], loss_target[0]
    T, M = x0.shape[0], mem0.shape[0]
    nbi, nbk, nbc = w_in.shape[2], w_kv.shape[2], conv_w.shape[2]

    def shards(l):
        return [_mx(w_in[l]), _mx(w_kv[l]), _mx(w_branch[l]), _mx(w_out[l]), conv_w[l]]

    gather_rest = _Gather(shards(0)[1:])
    gather_upper = _Gather([a for l in range(1, L) for a in shards(l)], mid_frac=0.85) if L > 1 else None

    tril = jnp.tril(jnp.ones((CHUNK, CHUNK), bool))
    wm = [_mx(jnp.where(tril[None], w_s[l], 0.0)) for l in range(L)]
    wmt = [w.transpose(0, 2, 1) for w in wm]
    bst = [b_s[l].T for l in range(L)]
    ln_a = [jnp.stack([gmlp_ln_g[l], gmlp_ln_b[l]]) for l in range(L)]
    cvec = [jnp.stack([conv_b[l], conv_ln_g[l], conv_ln_b[l]]) for l in range(L)]

    def conv_taps(gathered):
        return jnp.pad(gathered.transpose(1, 0, 2).reshape(CONV_K, D), ((0, HALO - CONV_K), (0, 0)))

    win, wkv, wbr, wou, cwf = [], [], [], [], []
    memn, kvs, xs, saved = [], [], [x0], []
    for l in range(L):
        if l == 0:
            (z, h, w0), full = _rms_matmul_gathering(x0, norm_g[0], shards(0)[0], "inproj_fwd0", gather_rest)
            win, wkv, wbr, wou, cwf = [w0], [full[0]], [full[1]], [full[2]], [conv_taps(full[3])]
        else:
            (z, h), _ = _rms_matmul(xs[l], norm_g[l], win[l], f"inproj_fwd{l}")
        (kv, mn), _ = _rms_matmul(mem0, mem_norm_g[l], wkv[l], f"kv_fwd{l}")
        kvs.append(_mx(kv))
        memn.append(mn)
        (br, cpre), full = _branch_fwd(z, kvs[l], wm[l], bst[l], ln_a[l], cwf[l], cvec[l], f"branch_fwd{l}",
                                       gather_upper if l == 0 else None)
        for k in range(1, L if l == 0 else 0):
            f = full[5 * (k - 1):5 * k]
            win.append(f[0])
            wkv.append(f[1])
            wbr.append(f[2])
            wou.append(f[3])
            cwf.append(conv_taps(f[4]))
        xn, merged, proj = _merge_fwd(br, z, xs[l], wbr[l], wou[l], f"merge_fwd{l}")
        xs.append(xn)
        saved.append((z, h, br, cpre, merged, proj))
    loss_part, dx, dfg = _loss_head(xs[L], tgt, final_norm_g, "loss_head")

    pending, recv = [("final_norm_g", dfg, True)], {}

    def flush():
        scat = [(k, a) for k, a, g in pending if not g]
        gath = [(k, a) for k, a, g in pending if g]
        pending.clear()
        comms = ([_Scatter([a for _, a in scat])] if scat else []) + ([_Gather([a for _, a in gath])] if gath else [])
        return [k for k, _ in scat + gath], comms[0] if len(comms) == 1 else _Both(*comms)

    def landed(keys, arrays):
        recv.update(zip(keys, arrays))

    for l in reversed(range(L)):
        z, h, br, cpre, merged, proj = saved[l]
        dproj, dbr, dzm = _merge_bwd(dx, proj, z, wbr[l], wou[l], f"merge_bwd{l}")
        for n in range(3):
            dwb, _ = _atb(br, n, dproj, n, 1, f"dwbranch{l}_{n}")
            pending.append((f"w_branch{l}_{n}", dwb.reshape(NDEV, D // NDEV, D), False))
        dwo, _ = _atb(merged[None], 0, dx[None], 0, 1, f"dwout{l}")
        pending.append((f"w_out{l}", dwo.reshape(NDEV, D // NDEV, D), False))
        keys, comm = flush()
        (dz, vecg, dbst, dws, dcw, dkv), got = _branch_bwd(
            z, dbr, cpre, dzm, kvs[l], wm[l], wmt[l], bst[l], ln_a[l], cwf[l], cvec[l], f"branch_bwd{l}", comm)
        landed(keys, got)
        dwk, _ = _atb(memn[l][None], 0, dkv[None], 0, NDEV, f"dwkv{l}")
        (_, dmg), _ = _rms_matmul_bwd(dkv, wkv[l], mem0, mem_norm_g[l], jnp.zeros((M, D), F32), f"kv_bwd{l}")
        rest = jnp.concatenate([dmg, vecg[0:2], vecg[2:5], dbst.T.reshape(1, D)], axis=0)
        pending += [(f"w_kv{l}", dwk, False),
                    (f"conv_w{l}", dcw[:CONV_K].reshape(CONV_K, NDEV, nbc).transpose(1, 0, 2), False),
                    (f"small{l}", rest, True), (f"w_s{l}", dws.reshape(N_GROUPS * CHUNK, CHUNK), True)]
        keys, comm = flush()
        dwi, got = _atb(h[None], 0, dz[None], 0, NDEV, f"dwin{l}", comm)
        landed(keys, got)
        pending.append((f"w_in{l}", dwi, False))
        keys, comm = flush()
        (dx, dng), got = _rms_matmul_bwd(dz, win[l], xs[l], norm_g[l], dx, f"inproj_bwd{l}", comm)
        landed(keys, got)
        pending.append((f"norm_g{l}", dng, True))
    grad_x = dx[None]

    def pack(p):
        rows = []
        for l in range(L):
            rows += [p["norm_g"][l], p["mem_norm_g"][l], p["gmlp_ln_g"][l], p["gmlp_ln_b"][l], p["conv_b"][l],
                     p["conv_ln_g"][l], p["conv_ln_b"][l], p["b_s"][l].reshape(D)]
        return jnp.stack(rows + [p["final_norm_g"]])[None]

    names = ["norm_g", "mem_norm_g", "gmlp_ln_g", "gmlp_ln_b", "conv_b", "conv_ln_g", "conv_ln_b", "b_s",
             "final_norm_g"]
    w_small = pack(dict(zip(names, [norm_g, mem_norm_g, gmlp_ln_g, gmlp_ln_b, conv_b, conv_ln_g, conv_ln_b,
                                    b_s, final_norm_g])))
    m_small = pack(dict(zip(names, [m_norm_g, m_mem_norm_g, m_gmlp_ln_g, m_gmlp_ln_b, m_conv_b, m_conv_ln_g,
                                    m_conv_ln_b, m_b_s, m_final_norm_g])))
    v_small = pack(dict(zip(names, [v_norm_g, v_mem_norm_g, v_gmlp_ln_g, v_gmlp_ln_b, v_conv_b, v_conv_ln_g,
                                    v_conv_ln_b, v_b_s, v_final_norm_g])))
    outs = {}

    def run(key, parts, w, m, v, idx, comm=None):
        outs[key], got = _adamw(parts, w, m, v, idx, "adamw_" + key, comm)
        return got

    keys, comm = flush()
    landed(keys, run("w_in0", recv["w_in0"], w_in, m_w_in, v_w_in, 0, comm))
    parts_small = jnp.concatenate([recv[f"{k}{l}"] for l in range(L) for k in ("norm_g", "small")]
                                  + [recv["final_norm_g"]], axis=1)
    parts_ws = jnp.concatenate([recv[f"w_s{l}"] for l in range(L)], axis=1)
    for l in range(L):
        if l > 0:
            run(f"w_in{l}", recv[f"w_in{l}"], w_in, m_w_in, v_w_in, l)
        run(f"w_kv{l}", recv[f"w_kv{l}"], w_kv, m_w_kv, v_w_kv, l)
        for n in range(3):
            sh = (L * 3, D // NDEV, D)
            run(f"w_branch{l}_{n}", recv[f"w_branch{l}_{n}"], w_branch.reshape(sh), m_w_branch.reshape(sh),
                v_w_branch.reshape(sh), l * 3 + n)
        run(f"w_out{l}", recv[f"w_out{l}"], w_out, m_w_out, v_w_out, l)
        run(f"conv_w{l}", recv[f"conv_w{l}"], conv_w, m_conv_w, v_conv_w, l)
    run("small", parts_small, w_small, m_small, v_small, 0)
    ws_shape = (1, L * N_GROUPS * CHUNK, CHUNK)
    run("w_s", parts_ws, w_s.reshape(ws_shape), m_w_s.reshape(ws_shape), v_w_s.reshape(ws_shape), 0)

    def leaf(name, k):
        if name in ("w_in", "w_kv", "w_out", "conv_w"):
            return jnp.stack([outs[f"{name}{l}"][k] for l in range(L)])
        if name == "w_branch":
            return jnp.stack([jnp.stack([outs[f"w_branch{l}_{n}"][k] for n in range(3)]) for l in range(L)])
        if name == "w_s":
            return outs["w_s"][k].reshape(L, N_GROUPS, CHUNK, CHUNK)
        sm = outs["small"][k]
        if name == "final_norm_g":
            return sm[8 * L]
        j = names.index(name)
        rows = jnp.stack([sm[8 * l + j] for l in range(L)])
        return rows.reshape(L, N_GROUPS, CHUNK) if name == "b_s" else rows

    order = ["norm_g", "mem_norm_g", "w_in", "gmlp_ln_g", "gmlp_ln_b", "w_s", "b_s", "conv_w", "conv_b",
             "conv_ln_g", "conv_ln_b", "w_kv", "w_branch", "w_out", "final_norm_g"]
    loss = lax.psum(loss_part[0, 0], ("x", "y", "c"))
    return (loss, grad_x, *[leaf(nm, k) for k in range(4) for nm in order])
```

```python
import math

import jax
import jax.numpy as jnp
from jax import lax
from jax.experimental import pallas as pl
from jax.experimental.pallas import tpu as pltpu

F32 = jnp.float32
MXU_DTYPE = jnp.bfloat16
ACT_DTYPE = jnp.bfloat16
GRAD_DTYPE = jnp.bfloat16

D = 1024
N_SEG = 11
N_IN = N_SEG * D
NDEV = 8
CHUNK = 128
N_GROUPS = 8
CONV_K = 31
HALO = 32
LANES = 128
STRIP = 32
HEADS = 4
HEAD_DIM = D // HEADS
RMS_EPS = 1e-6
LN_EPS = 1e-5
ADAM_LR, ADAM_B1, ADAM_B2, ADAM_EPS, ADAM_WD, ADAM_STEP = 0.001, 0.9, 0.999, 1e-08, 0.01, 10
SEG_AU, SEG_AV, SEG_AG, SEG_BA, SEG_BB, SEG_BG, SEG_CQ, SEG_CG, SEG_M = 0, 1, 2, 3, 4, 5, 6, 7, 8

VMEM_LIMIT = 60 * 1024 * 1024
MESH = pl.DeviceIdType.MESH
NT_DIMS = (((1,), (1,)), ((), ()))
TN_DIMS = (((0,), (0,)), ((), ()))


def _params(*sem):
    return pltpu.CompilerParams(dimension_semantics=sem, vmem_limit_bytes=VMEM_LIMIT)


def _tile(n, want):
    t = min(n, want)
    assert n % t == 0, (n, want)
    return t


def _mx(v):
    return v.astype(MXU_DTYPE)


def _gelu(x):
    t = jnp.tanh(0.7978845608028654 * (x + 0.044715 * x * x * x))
    return 0.5 * x * (1.0 + t), t


def _gelu_grad(x, t):
    return 0.5 * (1.0 + t) + 0.5 * x * (1.0 - t * t) * 0.7978845608028654 * (1.0 + 3.0 * 0.044715 * x * x)


def _silu_grad(x, s):
    return s * (1.0 + x * (1.0 - s))


def _ln_stats(v):
    mu = jnp.mean(v, axis=-1, keepdims=True)
    vc = v - mu
    rstd = lax.rsqrt(jnp.mean(vc * vc, axis=-1, keepdims=True) + LN_EPS)
    return vc * rstd, rstd


def _ln_grad(dy, g, vhat, rstd):
    dvh = dy * g
    return rstd * (dvh - jnp.mean(dvh, axis=-1, keepdims=True)
                   - vhat * jnp.mean(dvh * vhat, axis=-1, keepdims=True))


def _rowsum(v):
    return jnp.sum(v, axis=0, keepdims=True)


def _coords():
    return lax.axis_index("x"), lax.axis_index("y"), lax.axis_index("c")


def _flip(pos, d):
    x, y, c = pos
    return (1 - x if d & 4 else x, 1 - y if d & 2 else y, 1 - c if d & 1 else c)


def _slot(pos):
    return 4 * pos[0] + 2 * pos[1] + pos[2]


CHIP_FLIPS = (4, 2, 6)


class _Gather:
    def __init__(self, arrays, mid_frac=0.8):
        self.arrays = list(arrays)
        self.n = n = len(arrays)
        self.mid_frac = mid_frac
        self.out_shape = [jax.ShapeDtypeStruct((NDEV,) + a.shape, a.dtype) for a in arrays]
        self.scratch = [pltpu.SemaphoreType.DMA((n, 7)), pltpu.SemaphoreType.DMA((n, 7)),
                        pltpu.SemaphoreType.DMA((n,))]

    def _copy(self, refs, i, k, block, to, own=False):
        ins, outs, (send, recv, _) = refs
        slot = outs[i].at[_slot(block)]
        return pltpu.make_async_remote_copy(
            src_ref=ins[i] if own else slot, dst_ref=slot, send_sem=send.at[i, k], recv_sem=recv.at[i, k],
            device_id=to, device_id_type=MESH)

    def _local(self, refs, i):
        ins, outs, (_, _, loc) = refs
        return pltpu.make_async_copy(ins[i], outs[i].at[_slot(_coords())], loc.at[i])

    def _first(self, refs, i, k):
        me = _coords()
        return self._copy(refs, i, k, me, _flip(me, ((1,) + CHIP_FLIPS)[k]), own=True)

    def _passed(self, refs, i, j):
        me = _coords()
        return self._copy(refs, i, 4 + j, _flip(me, CHIP_FLIPS[j]), _flip(me, 1))

    def start(self, refs):
        for i in range(self.n):
            self._local(refs, i).start()
        for k in range(4):
            for i in range(self.n):
                self._first(refs, i, k).start()

    def forward(self, refs):
        me = _coords()
        for j, d in enumerate(CHIP_FLIPS):
            for i in range(self.n):
                self._copy(refs, i, 1 + j, _flip(me, d), me).wait_recv()
                self._passed(refs, i, j).start()

    def finish(self, refs):
        me = _coords()
        sib = _flip(me, 1)
        for i in range(self.n):
            self._copy(refs, i, 0, sib, me).wait_recv()
        for j, d in enumerate(CHIP_FLIPS):
            for i in range(self.n):
                self._copy(refs, i, 4 + j, _flip(sib, d), me).wait_recv()
        for i in range(self.n):
            for k in range(4):
                self._first(refs, i, k).wait_send()
            for j in range(3):
                self._passed(refs, i, j).wait_send()
            self._local(refs, i).wait()


class _Scatter:
    def __init__(self, arrays):
        self.arrays = list(arrays)
        self.n = n = len(arrays)
        self.mid_frac = None
        self.out_shape = [jax.ShapeDtypeStruct(a.shape, a.dtype) for a in arrays]
        self.scratch = [pltpu.SemaphoreType.DMA((n, 7)), pltpu.SemaphoreType.DMA((n, 7)),
                        pltpu.SemaphoreType.DMA((n,))]

    def _copy(self, refs, i, d, landing):
        ins, outs, (send, recv, _) = refs
        me = _coords()
        peer = _flip(me, d)
        return pltpu.make_async_remote_copy(
            src_ref=ins[i].at[_slot(peer)], dst_ref=outs[i].at[_slot(peer) if landing else _slot(me)],
            send_sem=send.at[i, d - 1], recv_sem=recv.at[i, d - 1], device_id=peer, device_id_type=MESH)

    def _local(self, refs, i):
        ins, outs, (_, _, loc) = refs
        me = _slot(_coords())
        return pltpu.make_async_copy(ins[i].at[me], outs[i].at[me], loc.at[i])

    def start(self, refs):
        for i in range(self.n):
            self._local(refs, i).start()
        for d in range(1, NDEV):
            for i in range(self.n):
                self._copy(refs, i, d, False).start()

    def forward(self, refs):
        pass

    def finish(self, refs):
        for d in range(1, NDEV):
            for i in range(self.n):
                self._copy(refs, i, d, True).wait_recv()
        for d in range(1, NDEV):
            for i in range(self.n):
                self._copy(refs, i, d, False).wait_send()
        for i in range(self.n):
            self._local(refs, i).wait()


class _Both:
    def __init__(self, a, b):
        self.parts = (a, b)
        self.arrays = a.arrays + b.arrays
        self.n = a.n + b.n
        self.mid_frac = a.mid_frac if a.mid_frac is not None else b.mid_frac
        self.out_shape = a.out_shape + b.out_shape
        self.scratch = a.scratch + b.scratch

    def _each(self, refs):
        ins, outs, sems = refs
        na, ns = self.parts[0].n, len(self.parts[0].scratch)
        return ((self.parts[0], (ins[:na], outs[:na], sems[:ns])), (self.parts[1], (ins[na:], outs[na:], sems[ns:])))

    def start(self, refs):
        for part, r in self._each(refs):
            part.start(r)

    def forward(self, refs):
        for part, r in self._each(refs):
            part.forward(r)

    def finish(self, refs):
        for part, r in self._each(refs):
            part.finish(r)


def _call(body, name, grid, in_specs, out_specs, out_shape, scratch, args, comm=None, start_frac=0.0):
    params = _params(*(["arbitrary"] * len(grid)))
    if comm is None:
        outs = pl.pallas_call(
            body, name=name, grid=grid, in_specs=in_specs, out_specs=out_specs, out_shape=out_shape,
            scratch_shapes=scratch, compiler_params=params)(*args)
        return list(outs), []
    n_in, n_out, n_scr, k = len(in_specs), len(out_specs), len(scratch), comm.n
    nsteps = math.prod(grid) if grid else 1
    first = int(nsteps * start_frac)
    mid = max(first, min(nsteps - 1, int(nsteps * comm.mid_frac))) if comm.mid_frac is not None else None

    def hosted(*refs):
        ins, refs = refs[:n_in], refs[n_in:]
        cins, refs = refs[:k], refs[k:]
        outs, refs = refs[:n_out], refs[n_out:]
        couts, refs = refs[:k], refs[k:]
        scr, sems = refs[:n_scr], refs[n_scr:]
        crefs = (cins, couts, sems)
        if nsteps == 1:
            comm.start(crefs)
            body(*ins, *outs, *scr)
            comm.forward(crefs)
            comm.finish(crefs)
            return
        step = pl.program_id(0)
        for a in range(1, len(grid)):
            step = step * grid[a] + pl.program_id(a)
        pl.when(step == first)(lambda: comm.start(crefs))
        if mid is not None:
            pl.when(step == mid)(lambda: comm.forward(crefs))
        body(*ins, *outs, *scr)
        pl.when(step == nsteps - 1)(lambda: comm.finish(crefs))

    any_spec = pl.BlockSpec(memory_space=pl.ANY)
    outs = pl.pallas_call(
        hosted, name=name, grid=grid,
        in_specs=list(in_specs) + [any_spec] * k, out_specs=list(out_specs) + [any_spec] * k,
        out_shape=list(out_shape) + comm.out_shape, scratch_shapes=list(scratch) + comm.scratch,
        compiler_params=params)(*args, *comm.arrays)
    return list(outs[:n_out]), list(outs[n_out:])


def _rms_matmul(x, g, w, name, comm=None):
    T = x.shape[0]
    nb = w.shape[2]
    tT = _tile(T, 1024)
    per = 2

    def body(x_ref, g_ref, w_ref, z_ref, h_ref):
        @pl.when(pl.program_id(1) == 0)
        def _():
            xf = x_ref[...]
            r = lax.rsqrt(jnp.mean(xf * xf, axis=-1, keepdims=True) + RMS_EPS)
            h_ref[...] = (xf * r * g_ref[...]).astype(h_ref.dtype)

        for j in range(per):
            z_ref[:, j * nb:(j + 1) * nb] = jnp.dot(h_ref[...], w_ref[j], preferred_element_type=F32
                                                    ).astype(z_ref.dtype)

    return _call(
        body, name, (T // tT, NDEV // per),
        [pl.BlockSpec((tT, D), lambda t, n: (t, 0)),
         pl.BlockSpec((1, D), lambda t, n: (0, 0)),
         pl.BlockSpec((per, D, nb), lambda t, n: (n, 0, 0))],
        [pl.BlockSpec((tT, per * nb), lambda t, n: (t, n)),
         pl.BlockSpec((tT, D), lambda t, n: (t, 0))],
        [jax.ShapeDtypeStruct((T, NDEV * nb), ACT_DTYPE), jax.ShapeDtypeStruct((T, D), MXU_DTYPE)],
        [], (x, g.reshape(1, D), w), comm)


ARRIVAL = (0, 1) + CHIP_FLIPS + tuple(d ^ 1 for d in CHIP_FLIPS)


def _rms_matmul_gathering(x, g, w_shard, name, comm=None):
    T = x.shape[0]
    nb = w_shard.shape[1]
    tT = _tile(T, 1024)
    nT = T // tT

    def body(x_ref, g_ref, wsh_ref, z_ref, h_ref, wfull_ref, h_all, wbuf, zbuf, fetch_sems, z_sems, send_sems,
             recv_sems, own_sem):
        n, t = pl.program_id(0), pl.program_id(1)
        step = n * nT + t
        me = _coords()
        sib = _flip(me, 1)

        def remote(k, block, to, own=False):
            slot = wfull_ref.at[_slot(block)]
            return pltpu.make_async_remote_copy(
                src_ref=wsh_ref if own else slot, dst_ref=slot, send_sem=send_sems.at[k], recv_sem=recv_sems.at[k],
                device_id=to, device_id_type=MESH)

        def first(k):
            return remote(k, me, _flip(me, ARRIVAL[1 + k]), own=True)

        def passed(j):
            return remote(4 + j, _flip(me, CHIP_FLIPS[j]), sib)

        own_copy = pltpu.make_async_copy(wsh_ref, wfull_ref.at[_slot(me)], own_sem)

        def fetch(src, nn):
            return pltpu.make_async_copy(src, wbuf.at[nn % 2], fetch_sems.at[nn % 2])

        @pl.when(step == 0)
        def _():
            own_copy.start()
            for k in range(3):
                first(k).start()
            fetch(wsh_ref, 0).start()

        @pl.when(t == 0)
        def _():
            fetch(wfull_ref.at[0], n).wait()

        @pl.when(n == 0)
        def _():
            xf = x_ref[...]
            r = lax.rsqrt(jnp.mean(xf * xf, axis=-1, keepdims=True) + RMS_EPS)
            h = (xf * r * g_ref[...]).astype(h_ref.dtype)
            h_ref[...] = h
            h_all[t] = h

        def z_copy(s, col):
            return pltpu.make_async_copy(
                zbuf.at[s % 2], z_ref.at[pl.ds(pl.multiple_of(t * tT, tT), tT), pl.ds(col * nb, nb)],
                z_sems.at[s % 2])

        @pl.when(step >= 2)
        def _():
            z_copy(step, 0).wait()

        d = sum(jnp.where(n == nn, ARRIVAL[nn], 0) for nn in range(NDEV))
        col = _slot((me[0] ^ ((d >> 2) & 1), me[1] ^ ((d >> 1) & 1), me[2] ^ (d & 1)))
        zbuf[step % 2] = jnp.dot(h_all[t], wbuf[n % 2], preferred_element_type=F32).astype(zbuf.dtype)
        z_copy(step, col).start()

        for nn in range(1, NDEV):
            @pl.when((n == nn - 1) & (t == nT - 1))
            def _(nn=nn):
                block = _flip(me, ARRIVAL[nn])
                if nn == 1:
                    remote(0, sib, me).wait_recv()
                elif nn < 5:
                    if nn == 2:
                        first(3).start()
                    remote(nn - 1, block, me).wait_recv()
                    passed(nn - 2).start()
                else:
                    remote(nn - 1, block, me).wait_recv()
                fetch(wfull_ref.at[_slot(block)], nn).start()

        @pl.when(step == NDEV * nT - 1)
        def _():
            z_copy(step - 1, 0).wait()
            z_copy(step, 0).wait()
            for k in range(4):
                first(k).wait_send()
            for j in range(3):
                passed(j).wait_send()
            own_copy.wait()

    keep = lambda n, t: (jnp.where(n == 0, t, nT - 1), 0)
    any_spec = pl.BlockSpec(memory_space=pl.ANY)
    return _call(
        body, name, (NDEV, nT),
        [pl.BlockSpec((tT, D), keep), pl.BlockSpec((1, D), lambda n, t: (0, 0)), any_spec],
        [any_spec, pl.BlockSpec((tT, D), keep), any_spec],
        [jax.ShapeDtypeStruct((T, NDEV * nb), ACT_DTYPE), jax.ShapeDtypeStruct((T, D), MXU_DTYPE),
         jax.ShapeDtypeStruct((NDEV,) + w_shard.shape, w_shard.dtype)],
        [pltpu.VMEM((nT, tT, D), MXU_DTYPE), pltpu.VMEM((2, D, nb), w_shard.dtype),
         pltpu.VMEM((2, tT, nb), ACT_DTYPE), pltpu.SemaphoreType.DMA((2,)), pltpu.SemaphoreType.DMA((2,)),
         pltpu.SemaphoreType.DMA((7,)), pltpu.SemaphoreType.DMA((7,)), pltpu.SemaphoreType.DMA],
        (x, g.reshape(1, D), w_shard), comm, start_frac=0.5)


def _rms_matmul_bwd(dz, w, x, g, dxo, name, comm=None):
    T = x.shape[0]
    nb = w.shape[2]
    tT = _tile(T, 512)
    per = 4
    steps = NDEV // per

    def body(dz_ref, w_ref, x_ref, g_ref, dxo_ref, dx_ref, dg_ref, acc):
        t, n = pl.program_id(0), pl.program_id(1)

        @pl.when((n == 0) & (t == 0))
        def _():
            dg_ref[...] = jnp.zeros_like(dg_ref)

        part = None
        for j in range(per):
            d = lax.dot_general(_mx(dz_ref[:, j * nb:(j + 1) * nb]), w_ref[j], NT_DIMS, preferred_element_type=F32)
            part = d if part is None else part + d

        @pl.when(n == 0)
        def _():
            acc[...] = part

        @pl.when(n > 0)
        def _():
            acc[...] += part

        @pl.when(n == steps - 1)
        def _():
            xf = x_ref[...]
            r = lax.rsqrt(jnp.mean(xf * xf, axis=-1, keepdims=True) + RMS_EPS)
            xh = xf * r
            dh = acc[...]
            dxh = dh * g_ref[...]
            dx_ref[...] = dxo_ref[...] + r * (dxh - xh * jnp.mean(dxh * xh, axis=-1, keepdims=True))
            dg_ref[...] += _rowsum(dh * xh)

    return _call(
        body, name, (T // tT, steps),
        [pl.BlockSpec((tT, per * nb), lambda t, n: (t, n)),
         pl.BlockSpec((per, D, nb), lambda t, n: (n, 0, 0)),
         pl.BlockSpec((tT, D), lambda t, n: (t, 0)),
         pl.BlockSpec((1, D), lambda t, n: (0, 0)),
         pl.BlockSpec((tT, D), lambda t, n: (t, 0))],
        [pl.BlockSpec((tT, D), lambda t, n: (t, 0)),
         pl.BlockSpec((1, D), lambda t, n: (0, 0))],
        [jax.ShapeDtypeStruct((T, D), F32), jax.ShapeDtypeStruct((1, D), F32)],
        [pltpu.VMEM((tT, D), F32)], (dz, w, x, g.reshape(1, D), dxo), comm)


def _atb(a, ai, b, bi, nblk, name, comm=None):
    T, M = a.shape[1:]
    N = b.shape[2]
    nb = N // nblk
    tk = _tile(T, 2048)
    nk = T // tk

    def body(a_ref, b_ref, o_ref, acc):
        k = pl.program_id(1)

        @pl.when(k == 0)
        def _():
            acc[...] = jnp.zeros_like(acc)

        acc[...] += lax.dot_general(_mx(a_ref[...]), _mx(b_ref[...]), TN_DIMS, preferred_element_type=F32)

        @pl.when(k == nk - 1)
        def _():
            o_ref[...] = acc[...].astype(o_ref.dtype)

    outs, couts = _call(
        body, name, (nblk, nk),
        [pl.BlockSpec((None, tk, M), lambda n, k: (ai, k, 0)),
         pl.BlockSpec((None, tk, nb), lambda n, k: (bi, k, n))],
        [pl.BlockSpec((None, M, nb), lambda n, k: (n, 0, 0))],
        [jax.ShapeDtypeStruct((nblk, M, nb), GRAD_DTYPE)],
        [pltpu.VMEM((M, nb), F32)], (a, b), comm)
    return outs[0], couts


def _loss_head(x, tgt, g, name):
    T = x.shape[0]
    tT = _tile(T, 512)

    def body(x_ref, t_ref, g_ref, loss_ref, dx_ref, dg_ref):
        @pl.when(pl.program_id(0) == 0)
        def _():
            loss_ref[...] = jnp.zeros_like(loss_ref)
            dg_ref[...] = jnp.zeros_like(dg_ref)

        xf = x_ref[...]
        r = lax.rsqrt(jnp.mean(xf * xf, axis=-1, keepdims=True) + RMS_EPS)
        xh = xf * r
        err = xh * g_ref[...] - t_ref[...]
        loss_ref[...] += 0.5 * jnp.sum(jnp.mean(err * err, axis=-1, keepdims=True), axis=0, keepdims=True)
        dy = err * (1.0 / D)
        dxh = dy * g_ref[...]
        dx_ref[...] = r * (dxh - xh * jnp.mean(dxh * xh, axis=-1, keepdims=True))
        dg_ref[...] += _rowsum(dy * xh)

    return pl.pallas_call(
        body, name=name, grid=(T // tT,),
        in_specs=[pl.BlockSpec((tT, D), lambda t: (t, 0)),
                  pl.BlockSpec((tT, D), lambda t: (t, 0)),
                  pl.BlockSpec((1, D), lambda t: (0, 0))],
        out_specs=[pl.BlockSpec((1, 1), lambda t: (0, 0)),
                   pl.BlockSpec((tT, D), lambda t: (t, 0)),
                   pl.BlockSpec((1, D), lambda t: (0, 0))],
        out_shape=[jax.ShapeDtypeStruct((1, 1), F32), jax.ShapeDtypeStruct((T, D), F32),
                   jax.ShapeDtypeStruct((1, D), F32)],
        compiler_params=_params("arbitrary"),
    )(x, tgt, g.reshape(1, D))


def _spatial_gate(wm_ref, bst_ref, vb_ref, sv_ref, n_chunks):
    for c in range(n_chunks):
        rows = slice(c * CHUNK, (c + 1) * CHUNK)
        for g in range(N_GROUPS):
            cols = slice(g * CHUNK, (g + 1) * CHUNK)
            sv_ref[rows, cols] = (jnp.dot(wm_ref[g], vb_ref[rows, cols], preferred_element_type=F32)
                                  + bst_ref[:, g:g + 1])


def _lane_loop(fn):
    def step(i, carry):
        fn(pl.ds(pl.multiple_of(i * LANES, LANES), LANES))
        return carry

    lax.fori_loop(0, D // LANES, step, 0)


def _shifted_copies(buf, sh, n):
    for s in range(1, 8):
        sh[s - 1, 0:n, :] = buf[s:s + n, :]


def _window(buf, sh, base, off, lanes):
    a, s = divmod(off, 8)
    src = buf if s == 0 else sh.at[s - 1]
    return src[base + 8 * a:base + 8 * a + 8, lanes]


def _softmax_rows(s):
    e = jnp.exp(s - jnp.max(s, axis=-1, keepdims=True))
    return e / jnp.sum(e, axis=-1, keepdims=True)


def _branch_fwd(z, kv, wm, bst, ln_a, cw, cvec, name, comm=None):
    T = z.shape[0]
    tT = _tile(T, 256)
    n_chunks = tT // CHUNK

    def body(z_ref, kv_ref, wm_ref, bst_ref, lna_ref, cw_ref, cvec_ref, br_ref, c_ref, gbuf, gsh, vb, ua):
        @pl.when(pl.program_id(0) == 0)
        def _():
            gbuf[0:HALO, :] = jnp.zeros((HALO, D), F32)

        def seg(s):
            return z_ref[:, s * D:(s + 1) * D].astype(F32)

        u, _ = _gelu(seg(SEG_AU))
        zg = seg(SEG_AG)
        ua[...] = u * (zg * jax.nn.sigmoid(zg))
        gv, _ = _gelu(seg(SEG_AV))
        vhat, _ = _ln_stats(gv)
        vb[...] = _mx(vhat * lna_ref[0:1, :] + lna_ref[1:2, :])
        for c in range(n_chunks):
            rows = slice(c * CHUNK, (c + 1) * CHUNK)
            for g in range(N_GROUPS):
                cols = slice(g * CHUNK, (g + 1) * CHUNK)
                sv = jnp.dot(wm_ref[g], vb[rows, cols], preferred_element_type=F32) + bst_ref[:, g:g + 1]
                br_ref[0, rows, cols] = (sv * ua[rows, cols]).astype(br_ref.dtype)

        gbuf[HALO:HALO + tT, :] = seg(SEG_BA) * jax.nn.sigmoid(seg(SEG_BB))
        _shifted_copies(gbuf, gsh, tT + HALO - 8)
        def conv_lanes(lanes):
            taps = [jnp.broadcast_to(cw_ref[k:k + 1, lanes], (8, LANES)) for k in range(CONV_K)]
            bias = jnp.broadcast_to(cvec_ref[0:1, lanes], (8, LANES))
            for base in range(0, tT, 8):
                acc = [bias, None, None, None]
                for k in range(CONV_K):
                    term = taps[k] * _window(gbuf, gsh, base, k + HALO - CONV_K + 1, lanes)
                    acc[k % 4] = term if acc[k % 4] is None else acc[k % 4] + term
                c_ref[base:base + 8, lanes] = (acc[0] + acc[1]) + (acc[2] + acc[3])

        _lane_loop(conv_lanes)
        gbuf[0:HALO, :] = gbuf[tT:tT + HALO, :]
        chat, _ = _ln_stats(c_ref[...])
        cl = chat * cvec_ref[1:2, :] + cvec_ref[2:3, :]
        zg = seg(SEG_BG)
        br_ref[1] = (cl * jax.nn.sigmoid(cl) * (zg * jax.nn.sigmoid(zg))).astype(br_ref.dtype)

        for h in range(HEADS):
            cols = slice(h * HEAD_DIM, (h + 1) * HEAD_DIM)
            q = _mx(z_ref[:, SEG_CQ * D + h * HEAD_DIM:SEG_CQ * D + (h + 1) * HEAD_DIM])
            s = lax.dot_general(q, kv_ref[:, cols], NT_DIMS, preferred_element_type=F32)
            p = _softmax_rows(s * (1.0 / math.sqrt(HEAD_DIM)))
            att = jnp.dot(_mx(p), kv_ref[:, D + h * HEAD_DIM:D + (h + 1) * HEAD_DIM], preferred_element_type=F32)
            zg = z_ref[:, SEG_CG * D + h * HEAD_DIM:SEG_CG * D + (h + 1) * HEAD_DIM].astype(F32)
            br_ref[2, :, cols] = (att * (zg * jax.nn.sigmoid(zg))).astype(br_ref.dtype)

    full = lambda shape: pl.BlockSpec(shape, lambda t: (0,) * len(shape))
    return _call(
        body, name, (T // tT,),
        [pl.BlockSpec((tT, SEG_M * D), lambda t: (t, 0)),
         full(kv.shape), full(wm.shape), full(bst.shape), full(ln_a.shape), full(cw.shape), full(cvec.shape)],
        [pl.BlockSpec((3, tT, D), lambda t: (0, t, 0)), pl.BlockSpec((tT, D), lambda t: (t, 0))],
        [jax.ShapeDtypeStruct((3, T, D), MXU_DTYPE), jax.ShapeDtypeStruct((T, D), F32)],
        [pltpu.VMEM((tT + HALO, D), F32), pltpu.VMEM((7, tT + HALO - 8, D), F32),
         pltpu.VMEM((tT, D), MXU_DTYPE), pltpu.VMEM((tT, D), F32)],
        (z, kv, wm, bst, ln_a, cw, cvec), comm)


def _branch_bwd(z, dbr, c, dzm, kv, wm, wmt, bst, ln_a, cw, cvec, name, comm=None):
    T = z.shape[0]
    M = kv.shape[0]
    tT = _tile(T, 128)
    nT = T // tT
    n_chunks = tT // CHUNK

    def body(z_ref, dbr_ref, c_ref, dzm_ref, kv_ref, wm_ref, wmt_ref, bst_ref, lna_ref,
             cw_ref, cvec_ref, dz_ref, vecg_ref, dbst_ref, dws_ref, dcw_ref, dkv_ref,
             gbuf, dcbuf, vb, dsvb, sv, dvbuf, dcsh, dglu, vh, gq, dcw8, dcw_step):
        i = pl.program_id(0)

        @pl.when(i == 0)
        def _():
            vecg_ref[...] = jnp.zeros_like(vecg_ref)
            dbst_ref[...] = jnp.zeros_like(dbst_ref)
            dws_ref[...] = jnp.zeros_like(dws_ref)
            dcw8[...] = jnp.zeros_like(dcw8)
            dkv_ref[...] = jnp.zeros_like(dkv_ref)
            dcbuf[tT:tT + HALO, :] = jnp.zeros((HALO, D), F32)

        strips = [slice(r0, r0 + STRIP) for r0 in range(0, tT, STRIP)]

        def seg(r, s):
            return z_ref[r, s * D:(s + 1) * D].astype(F32)

        def put(r, s, val):
            dz_ref[r, s * D:(s + 1) * D] = val.astype(dz_ref.dtype)

        for r in strips:
            zv = seg(r, SEG_AV)
            gv, tv = _gelu(zv)
            vhat, rstd = _ln_stats(gv)
            vb[r, :] = _mx(vhat * lna_ref[0:1, :] + lna_ref[1:2, :])
            vh[r, :] = vhat
            gq[r, :] = rstd * _gelu_grad(zv, tv)
        _spatial_gate(wm_ref, bst_ref, vb, sv, n_chunks)
        for r in strips:
            zu, zg = seg(r, SEG_AU), seg(r, SEG_AG)
            u, tu = _gelu(zu)
            sg = jax.nn.sigmoid(zg)
            d_a = dbr_ref[0, r, :].astype(F32)
            put(r, SEG_AU, d_a * sv[r, :] * (zg * sg) * _gelu_grad(zu, tu))
            put(r, SEG_AG, d_a * u * sv[r, :] * _silu_grad(zg, sg))
            dsv = d_a * u * (zg * sg)
            dsvb[r, :] = _mx(dsv)
            in_chunk = slice(r.start % CHUNK, r.start % CHUNK + STRIP)
            for g in range(N_GROUPS):
                dbst_ref[in_chunk, g:g + 1] += jnp.sum(dsv[:, g * CHUNK:(g + 1) * CHUNK], axis=-1, keepdims=True)
        tril = (lax.broadcasted_iota(jnp.int32, (CHUNK, CHUNK), 0)
                >= lax.broadcasted_iota(jnp.int32, (CHUNK, CHUNK), 1))
        for g in range(N_GROUPS):
            cols = slice(g * CHUNK, (g + 1) * CHUNK)
            for cc in range(n_chunks):
                rows = slice(cc * CHUNK, (cc + 1) * CHUNK)
                dws = lax.dot_general(dsvb[rows, cols], vb[rows, cols], NT_DIMS, preferred_element_type=F32)
                dws_ref[g] += jnp.where(tril, dws, 0.0)
                dvbuf[rows, cols] = jnp.dot(wmt_ref[g], dsvb[rows, cols], preferred_element_type=F32)
        for r in strips:
            dv, vhat = dvbuf[r, :], vh[r, :]
            vecg_ref[0:1, :] += _rowsum(dv * vhat)
            vecg_ref[1:2, :] += _rowsum(dv)
            dvh = dv * lna_ref[0:1, :]
            put(r, SEG_AV, (dvh - jnp.mean(dvh, axis=-1, keepdims=True)
                            - vhat * jnp.mean(dvh * vhat, axis=-1, keepdims=True)) * gq[r, :])

        sgb_buf = sv
        for r in strips:
            za, zg = seg(r, SEG_BA), seg(r, SEG_BG)
            sgb = jax.nn.sigmoid(seg(r, SEG_BB))
            sgb_buf[r, :] = sgb
            gbuf[r, :] = za * sgb
            chat, crstd = _ln_stats(c_ref[r, :])
            cl = chat * cvec_ref[1:2, :] + cvec_ref[2:3, :]
            scl = jax.nn.sigmoid(cl)
            sg = jax.nn.sigmoid(zg)
            d_b = dbr_ref[1, r, :].astype(F32)
            put(r, SEG_BG, d_b * (cl * scl) * _silu_grad(zg, sg))
            dcl = d_b * (zg * sg) * _silu_grad(cl, scl)
            vecg_ref[3:4, :] += _rowsum(dcl * chat)
            vecg_ref[4:5, :] += _rowsum(dcl)
            dc = _ln_grad(dcl, cvec_ref[1:2, :], chat, crstd)
            vecg_ref[2:3, :] += _rowsum(dc)
            dcbuf[r, :] = dc
        _shifted_copies(dcbuf, dcsh, tT + HALO - 8)

        def conv_grads(lanes):
            taps = [jnp.broadcast_to(cw_ref[k:k + 1, lanes], (8, LANES)) for k in range(CONV_K)]
            wsum = [None] * CONV_K
            for base in range(0, tT, 8):
                glu = gbuf[base:base + 8, lanes]
                acc = [None] * 4
                for k in range(CONV_K):
                    win = _window(dcbuf, dcsh, base, CONV_K - 1 - k, lanes)
                    term = taps[k] * win
                    acc[k % 4] = term if acc[k % 4] is None else acc[k % 4] + term
                    term = glu * win
                    wsum[k] = term if wsum[k] is None else wsum[k] + term
                dglu[base:base + 8, lanes] = (acc[0] + acc[1]) + (acc[2] + acc[3])
            for k in range(CONV_K):
                dcw_step[8 * k:8 * k + 8, lanes] = wsum[k]

        _lane_loop(conv_grads)
        dcw8[...] += dcw_step[...]

        @pl.when(i == nT - 1)
        def _():
            for k in range(CONV_K):
                dcw_ref[k:k + 1, :] = _rowsum(dcw8[8 * k:8 * k + 8, :])
            dcw_ref[CONV_K:HALO, :] = jnp.zeros((HALO - CONV_K, D), F32)

        dcbuf[tT:tT + HALO, :] = dcbuf[0:HALO, :]
        for r in strips:
            dg, sgb = dglu[r, :], sgb_buf[r, :]
            put(r, SEG_BA, dg * sgb)
            put(r, SEG_BB, dg * seg(r, SEG_BA) * sgb * (1.0 - sgb))

        scale = 1.0 / math.sqrt(HEAD_DIM)
        for h in range(HEADS):
            cols = slice(h * HEAD_DIM, (h + 1) * HEAD_DIM)
            qcols = slice(SEG_CQ * D + h * HEAD_DIM, SEG_CQ * D + (h + 1) * HEAD_DIM)
            gcols = slice(SEG_CG * D + h * HEAD_DIM, SEG_CG * D + (h + 1) * HEAD_DIM)
            vcols = slice(D + h * HEAD_DIM, D + (h + 1) * HEAD_DIM)
            q = _mx(z_ref[:, qcols])
            kh, vh = kv_ref[:, cols], kv_ref[:, vcols]
            p = _softmax_rows(lax.dot_general(q, kh, NT_DIMS, preferred_element_type=F32) * scale)
            pb = _mx(p)
            att = jnp.dot(pb, vh, preferred_element_type=F32)
            zg = z_ref[:, gcols].astype(F32)
            sg = jax.nn.sigmoid(zg)
            d_c = dbr_ref[2, :, cols].astype(F32)
            dz_ref[:, gcols] = (d_c * att * _silu_grad(zg, sg)).astype(dz_ref.dtype)
            datt = _mx(d_c * (zg * sg))
            dp = lax.dot_general(datt, vh, NT_DIMS, preferred_element_type=F32)
            dkv_ref[:, vcols] += lax.dot_general(pb, datt, TN_DIMS, preferred_element_type=F32)
            ds = _mx(p * (dp - jnp.sum(dp * p, axis=-1, keepdims=True)) * scale)
            dz_ref[:, qcols] = jnp.dot(ds, kh, preferred_element_type=F32).astype(dz_ref.dtype)
            dkv_ref[:, cols] += lax.dot_general(ds, q, TN_DIMS, preferred_element_type=F32)

        dz_ref[:, SEG_M * D:] = dzm_ref[...].astype(dz_ref.dtype)

    rev = lambda i: nT - 1 - i
    full = lambda shape: pl.BlockSpec(shape, lambda i: (0,) * len(shape))
    return _call(
        body, name, (nT,),
        [pl.BlockSpec((tT, SEG_M * D), lambda i: (rev(i), 0)),
         pl.BlockSpec((3, tT, D), lambda i: (0, rev(i), 0)),
         pl.BlockSpec((tT, D), lambda i: (rev(i), 0)),
         pl.BlockSpec((tT, 3 * D), lambda i: (rev(i), 0)),
         full(kv.shape), full(wm.shape), full(wmt.shape), full(bst.shape), full(ln_a.shape),
         full(cw.shape), full(cvec.shape)],
        [pl.BlockSpec((tT, N_IN), lambda i: (rev(i), 0)),
         full((8, D)), full((CHUNK, N_GROUPS)), full((N_GROUPS, CHUNK, CHUNK)), full((HALO, D)),
         full((M, 2 * D))],
        [jax.ShapeDtypeStruct((T, N_IN), MXU_DTYPE), jax.ShapeDtypeStruct((8, D), F32),
         jax.ShapeDtypeStruct((CHUNK, N_GROUPS), F32),
         jax.ShapeDtypeStruct((N_GROUPS, CHUNK, CHUNK), F32),
         jax.ShapeDtypeStruct((HALO, D), F32), jax.ShapeDtypeStruct((M, 2 * D), F32)],
        [pltpu.VMEM((tT, D), F32), pltpu.VMEM((tT + HALO, D), F32),
         pltpu.VMEM((tT, D), MXU_DTYPE), pltpu.VMEM((tT, D), MXU_DTYPE),
         pltpu.VMEM((tT, D), F32), pltpu.VMEM((tT, D), F32),
         pltpu.VMEM((7, tT + HALO - 8, D), F32),
         pltpu.VMEM((tT, D), F32), pltpu.VMEM((tT, D), F32), pltpu.VMEM((tT, D), F32),
         pltpu.VMEM((CONV_K * 8, D), F32), pltpu.VMEM((CONV_K * 8, D), F32)],
        (z, dbr, c, dzm, kv, wm, wmt, bst, ln_a, cw, cvec), comm)


def _merge_fwd(br, z, x, wb, wo, name):
    T = x.shape[0]
    tT = _tile(T, 512)

    def body(br_ref, z0, z1, z2, x_ref, wb_ref, wo_ref, xn_ref, mg_ref, pj_ref):
        merged = jnp.zeros((tT, D), F32)
        for n, zm in enumerate((z0, z1, z2)):
            proj = jnp.dot(br_ref[n], wb_ref[:, n].reshape(D, D), preferred_element_type=F32)
            pj_ref[n] = proj.astype(pj_ref.dtype)
            merged = merged + jax.nn.sigmoid(zm[...].astype(F32)) * proj
        mg_ref[...] = merged.astype(mg_ref.dtype)
        xn_ref[...] = x_ref[...] + jnp.dot(_mx(merged), wo_ref[...].reshape(D, D), preferred_element_type=F32)

    zspec = lambda n: pl.BlockSpec((tT, D), lambda t: (t, SEG_M + n))
    return pl.pallas_call(
        body, name=name, grid=(T // tT,),
        in_specs=[pl.BlockSpec((3, tT, D), lambda t: (0, t, 0)), zspec(0), zspec(1), zspec(2),
                  pl.BlockSpec((tT, D), lambda t: (t, 0)),
                  pl.BlockSpec(wb.shape, lambda t: (0, 0, 0, 0)),
                  pl.BlockSpec(wo.shape, lambda t: (0, 0, 0))],
        out_specs=[pl.BlockSpec((tT, D), lambda t: (t, 0)),
                   pl.BlockSpec((tT, D), lambda t: (t, 0)),
                   pl.BlockSpec((3, tT, D), lambda t: (0, t, 0))],
        out_shape=[jax.ShapeDtypeStruct((T, D), F32), jax.ShapeDtypeStruct((T, D), MXU_DTYPE),
                   jax.ShapeDtypeStruct((3, T, D), ACT_DTYPE)],
        compiler_params=_params("parallel"),
    )(br, z, z, z, x, wb, wo)


def _merge_bwd(dxo, proj, z, wb, wo, name):
    T = dxo.shape[0]
    tT = _tile(T, 512)

    def body(dxo_ref, pj_ref, z0, z1, z2, wb_ref, wo_ref, dpj_ref, dbr_ref, dzm_ref):
        dmerged = lax.dot_general(_mx(dxo_ref[...]), wo_ref[...].reshape(D, D), NT_DIMS,
                                  preferred_element_type=F32)
        for n, zm in enumerate((z0, z1, z2)):
            gate = jax.nn.sigmoid(zm[...].astype(F32))
            dproj = _mx(gate * dmerged)
            dpj_ref[n] = dproj
            dzm_ref[:, n * D:(n + 1) * D] = (pj_ref[n].astype(F32) * dmerged * gate * (1.0 - gate)
                                             ).astype(dzm_ref.dtype)
            dbr_ref[n] = lax.dot_general(dproj, wb_ref[:, n].reshape(D, D), NT_DIMS,
                                         preferred_element_type=F32).astype(dbr_ref.dtype)

    zspec = lambda n: pl.BlockSpec((tT, D), lambda t: (t, SEG_M + n))
    return pl.pallas_call(
        body, name=name, grid=(T // tT,),
        in_specs=[pl.BlockSpec((tT, D), lambda t: (t, 0)),
                  pl.BlockSpec((3, tT, D), lambda t: (0, t, 0)), zspec(0), zspec(1), zspec(2),
                  pl.BlockSpec(wb.shape, lambda t: (0, 0, 0, 0)),
                  pl.BlockSpec(wo.shape, lambda t: (0, 0, 0))],
        out_specs=[pl.BlockSpec((3, tT, D), lambda t: (0, t, 0)),
                   pl.BlockSpec((3, tT, D), lambda t: (0, t, 0)),
                   pl.BlockSpec((tT, 3 * D), lambda t: (t, 0))],
        out_shape=[jax.ShapeDtypeStruct((3, T, D), MXU_DTYPE), jax.ShapeDtypeStruct((3, T, D), ACT_DTYPE),
                   jax.ShapeDtypeStruct((T, 3 * D), MXU_DTYPE)],
        compiler_params=_params("parallel"),
    )(dxo, proj, z, z, z, wb, wo)


def _adamw(parts, w, m, v, name, comm=None):
    G, R, C = w.shape
    tr = 128 if R % 128 == 0 else R
    nr = R // tr
    c1 = 1.0 / (1.0 - ADAM_B1 ** ADAM_STEP)
    c2 = 1.0 / (1.0 - ADAM_B2 ** ADAM_STEP)

    def body(*refs):
        p_refs, (w_ref, m_ref, v_ref, g_out, d_out, m_out, v_out) = refs[:G], refs[G:]
        for i in range(G):
            @pl.when(pl.program_id(0) == i)
            def _(p_ref=p_refs[i]):
                g = p_ref[0].astype(F32)
                for p in range(1, NDEV):
                    g = g + p_ref[p].astype(F32)
                mn = ADAM_B1 * m_ref[...] + (1.0 - ADAM_B1) * g
                vn = ADAM_B2 * v_ref[...] + (1.0 - ADAM_B2) * (g * g)
                g_out[...] = g
                m_out[...] = mn
                v_out[...] = vn
                d_out[...] = -ADAM_LR * ((mn * c1) / (jnp.sqrt(vn * c2) + ADAM_EPS) + ADAM_WD * w_ref[...])

    def parts_spec(i):
        return pl.BlockSpec((NDEV, tr, C), lambda l, r: (0, jnp.where(l == i, r, jnp.where(l > i, nr - 1, 0)), 0))

    spec = pl.BlockSpec((None, tr, C), lambda l, r: (l, r, 0))
    return _call(
        body, name, (G, nr), [parts_spec(i) for i in range(G)] + [spec] * 3,
        [spec] * 4, [jax.ShapeDtypeStruct((G, R, C), F32)] * 4, [], (*parts, w, m, v), comm)


def kernel(x, mem, norm_g, mem_norm_g, w_in, gmlp_ln_g, gmlp_ln_b, w_s, b_s, conv_w, conv_b, conv_ln_g, conv_ln_b, w_kv, w_branch, w_out, final_norm_g, loss_target, m_norm_g, m_mem_norm_g, m_w_in, m_gmlp_ln_g, m_gmlp_ln_b, m_w_s, m_b_s, m_conv_w, m_conv_b, m_conv_ln_g, m_conv_ln_b, m_w_kv, m_w_branch, m_w_out, m_final_norm_g, v_norm_g, v_mem_norm_g, v_w_in, v_gmlp_ln_g, v_gmlp_ln_b, v_w_s, v_b_s, v_conv_w, v_conv_b, v_conv_ln_g, v_conv_ln_b, v_w_kv, v_w_branch, v_w_out, v_final_norm_g):
    L = w_in.shape[0]
    x0, mem0, tgt = x[0], mem[0], loss_target[0]
    T, M = x0.shape[0], mem0.shape[0]
    nbc = conv_w.shape[2]

    def shards(l):
        return [_mx(w_in[l]), _mx(w_kv[l]), _mx(w_branch[l]), _mx(w_out[l]), conv_w[l]]

    gather_rest = _Gather(shards(0)[1:])
    gather_upper = _Gather([a for l in range(1, L) for a in shards(l)], mid_frac=0.85) if L > 1 else None

    tril = jnp.tril(jnp.ones((CHUNK, CHUNK), bool))
    wm = [_mx(jnp.where(tril[None], w_s[l], 0.0)) for l in range(L)]
    wmt = [w.transpose(0, 2, 1) for w in wm]
    bst = [b_s[l].T for l in range(L)]
    ln_a = [jnp.stack([gmlp_ln_g[l], gmlp_ln_b[l]]) for l in range(L)]
    cvec = [jnp.stack([conv_b[l], conv_ln_g[l], conv_ln_b[l]]) for l in range(L)]

    def conv_taps(gathered):
        return jnp.pad(gathered.transpose(1, 0, 2).reshape(CONV_K, D), ((0, HALO - CONV_K), (0, 0)))

    win, wkv, wbr, wou, cwf = [], [], [], [], []
    memn, kvs, xs, saved = [], [], [x0], []
    for l in range(L):
        if l == 0:
            (z, h, w0), full = _rms_matmul_gathering(x0, norm_g[0], shards(0)[0], "inproj_fwd0", gather_rest)
            win, wkv, wbr, wou, cwf = [w0], [full[0]], [full[1]], [full[2]], [conv_taps(full[3])]
        else:
            (z, h), _ = _rms_matmul(xs[l], norm_g[l], win[l], f"inproj_fwd{l}")
        (kv, mn), _ = _rms_matmul(mem0, mem_norm_g[l], wkv[l], f"kv_fwd{l}")
        kvs.append(_mx(kv))
        memn.append(mn)
        (br, cpre), full = _branch_fwd(z, kvs[l], wm[l], bst[l], ln_a[l], cwf[l], cvec[l], f"branch_fwd{l}",
                                       gather_upper if l == 0 else None)
        for k in range(1, L if l == 0 else 0):
            f = full[5 * (k - 1):5 * k]
            win.append(f[0])
            wkv.append(f[1])
            wbr.append(f[2])
            wou.append(f[3])
            cwf.append(conv_taps(f[4]))
        xn, merged, proj = _merge_fwd(br, z, xs[l], wbr[l], wou[l], f"merge_fwd{l}")
        xs.append(xn)
        saved.append((z, h, br, cpre, merged, proj))
    loss_part, dx, dfg = _loss_head(xs[L], tgt, final_norm_g, "loss_head")

    pending, recv = [("final_norm_g", dfg, True)], {}

    def flush():
        scat = [(k, a) for k, a, g in pending if not g]
        gath = [(k, a) for k, a, g in pending if g]
        pending.clear()
        comms = ([_Scatter([a for _, a in scat])] if scat else []) + ([_Gather([a for _, a in gath])] if gath else [])
        return [k for k, _ in scat + gath], comms[0] if len(comms) == 1 else _Both(*comms)

    def landed(keys, arrays):
        recv.update(zip(keys, arrays))

    for l in reversed(range(L)):
        z, h, br, cpre, merged, proj = saved[l]
        dproj, dbr, dzm = _merge_bwd(dx, proj, z, wbr[l], wou[l], f"merge_bwd{l}")
        for n in range(3):
            dwb, _ = _atb(br, n, dproj, n, 1, f"dwbranch{l}_{n}")
            pending.append((f"w_branch{l}_{n}", dwb.reshape(NDEV, D // NDEV, D), False))
        dwo, _ = _atb(merged[None], 0, dx[None], 0, 1, f"dwout{l}")
        pending.append((f"w_out{l}", dwo.reshape(NDEV, D // NDEV, D), False))
        keys, comm = flush()
        (dz, vecg, dbst, dws, dcw, dkv), got = _branch_bwd(
            z, dbr, cpre, dzm, kvs[l], wm[l], wmt[l], bst[l], ln_a[l], cwf[l], cvec[l], f"branch_bwd{l}", comm)
        landed(keys, got)
        dwk, _ = _atb(memn[l][None], 0, dkv[None], 0, NDEV, f"dwkv{l}")
        (_, dmg), _ = _rms_matmul_bwd(dkv, wkv[l], mem0, mem_norm_g[l], jnp.zeros((M, D), F32), f"kv_bwd{l}")
        rest = jnp.concatenate([dmg, vecg[0:2], vecg[2:5], dbst.T.reshape(1, D)], axis=0)
        pending += [(f"w_kv{l}", dwk, False),
                    (f"conv_w{l}", dcw[:CONV_K].reshape(CONV_K, NDEV, nbc).transpose(1, 0, 2), False),
                    (f"small{l}", rest, True), (f"w_s{l}", dws.reshape(N_GROUPS * CHUNK, CHUNK), True)]
        keys, comm = flush()
        dwi, got = _atb(h[None], 0, dz[None], 0, NDEV, f"dwin{l}", comm)
        landed(keys, got)
        pending.append((f"w_in{l}", dwi, False))
        keys, comm = flush()
        (dx, dng), got = _rms_matmul_bwd(dz, win[l], xs[l], norm_g[l], dx, f"inproj_bwd{l}", comm)
        landed(keys, got)
        pending.append((f"norm_g{l}", dng, True))
    grad_x = dx[None]

    def pack(p):
        rows = []
        for l in range(L):
            rows += [p["norm_g"][l], p["mem_norm_g"][l], p["gmlp_ln_g"][l], p["gmlp_ln_b"][l], p["conv_b"][l],
                     p["conv_ln_g"][l], p["conv_ln_b"][l], p["b_s"][l].reshape(D)]
        return jnp.stack(rows + [p["final_norm_g"]])[None]

    names = ["norm_g", "mem_norm_g", "gmlp_ln_g", "gmlp_ln_b", "conv_b", "conv_ln_g", "conv_ln_b", "b_s",
             "final_norm_g"]
    w_small = pack(dict(zip(names, [norm_g, mem_norm_g, gmlp_ln_g, gmlp_ln_b, conv_b, conv_ln_g, conv_ln_b,
                                    b_s, final_norm_g])))
    m_small = pack(dict(zip(names, [m_norm_g, m_mem_norm_g, m_gmlp_ln_g, m_gmlp_ln_b, m_conv_b, m_conv_ln_g,
                                    m_conv_ln_b, m_b_s, m_final_norm_g])))
    v_small = pack(dict(zip(names, [v_norm_g, v_mem_norm_g, v_gmlp_ln_g, v_gmlp_ln_b, v_conv_b, v_conv_ln_g,
                                    v_conv_ln_b, v_b_s, v_final_norm_g])))
    outs = {}

    def run(key, parts, w, m, v, comm=None):
        outs[key], got = _adamw(parts, w, m, v, "adamw_" + key, comm)
        return got

    keys, comm = flush()
    landed(keys, run("w_in", [recv[f"w_in{l}"] for l in range(L)], w_in, m_w_in, v_w_in, comm))
    parts_small = jnp.concatenate([recv[f"{k}{l}"] for l in range(L) for k in ("norm_g", "small")]
                                  + [recv["final_norm_g"]], axis=1)
    parts_ws = jnp.concatenate([recv[f"w_s{l}"] for l in range(L)], axis=1)
    run("w_kv", [recv[f"w_kv{l}"] for l in range(L)], w_kv, m_w_kv, v_w_kv)
    sh = (L * 3, D // NDEV, D)
    run("w_branch", [recv[f"w_branch{l}_{n}"] for l in range(L) for n in range(3)],
        w_branch.reshape(sh), m_w_branch.reshape(sh), v_w_branch.reshape(sh))
    run("w_out", [recv[f"w_out{l}"] for l in range(L)], w_out, m_w_out, v_w_out)
    run("conv_w", [recv[f"conv_w{l}"] for l in range(L)], conv_w, m_conv_w, v_conv_w)
    run("small", [parts_small], w_small, m_small, v_small)
    ws_shape = (1, L * N_GROUPS * CHUNK, CHUNK)
    run("w_s", [parts_ws], w_s.reshape(ws_shape), m_w_s.reshape(ws_shape), v_w_s.reshape(ws_shape))

    def leaf(name, k):
        if name in ("w_in", "w_kv", "w_out", "conv_w"):
            return outs[name][k]
        if name == "w_branch":
            return outs[name][k].reshape(w_branch.shape)
        if name == "w_s":
            return outs["w_s"][k].reshape(L, N_GROUPS, CHUNK, CHUNK)
        sm = outs["small"][k][0]
        if name == "final_norm_g":
            return sm[8 * L]
        j = names.index(name)
        rows = jnp.stack([sm[8 * l + j] for l in range(L)])
        return rows.reshape(L, N_GROUPS, CHUNK) if name == "b_s" else rows

    order = ["norm_g", "mem_norm_g", "w_in", "gmlp_ln_g", "gmlp_ln_b", "w_s", "b_s", "conv_w", "conv_b",
             "conv_ln_g", "conv_ln_b", "w_kv", "w_branch", "w_out", "final_norm_g"]
    loss = lax.psum(loss_part[0, 0], ("x", "y", "c"))
    return (loss, grad_x, *[leaf(nm, k) for k in range(4) for nm in order])
```

```python
import math

import jax
import jax.numpy as jnp
from jax import lax
from jax.experimental import pallas as pl
from jax.experimental.pallas import tpu as pltpu

F32 = jnp.float32
MXU_DTYPE = jnp.bfloat16
ACT_DTYPE = jnp.bfloat16
GRAD_DTYPE = jnp.bfloat16

D = 1024
N_SEG = 11
N_IN = N_SEG * D
NDEV = 8
CHUNK = 128
N_GROUPS = 8
CONV_K = 31
HALO = 32
LANES = 128
STRIP = 32
HEADS = 4
HEAD_DIM = D // HEADS
RMS_EPS = 1e-6
LN_EPS = 1e-5
ADAM_LR, ADAM_B1, ADAM_B2, ADAM_EPS, ADAM_WD, ADAM_STEP = 0.001, 0.9, 0.999, 1e-08, 0.01, 10
SEG_AU, SEG_AV, SEG_AG, SEG_BA, SEG_BB, SEG_BG, SEG_CQ, SEG_CG, SEG_M = 0, 1, 2, 3, 4, 5, 6, 7, 8

VMEM_LIMIT = 60 * 1024 * 1024
MESH = pl.DeviceIdType.MESH
NT_DIMS = (((1,), (1,)), ((), ()))
TN_DIMS = (((0,), (0,)), ((), ()))


def _params(*sem):
    return pltpu.CompilerParams(dimension_semantics=sem, vmem_limit_bytes=VMEM_LIMIT)


def _tile(n, want):
    t = min(n, want)
    assert n % t == 0, (n, want)
    return t


def _mx(v):
    return v.astype(MXU_DTYPE)


def _gelu(x):
    t = jnp.tanh(0.7978845608028654 * (x + 0.044715 * x * x * x))
    return 0.5 * x * (1.0 + t), t


def _gelu_grad(x, t):
    return 0.5 * (1.0 + t) + 0.5 * x * (1.0 - t * t) * 0.7978845608028654 * (1.0 + 3.0 * 0.044715 * x * x)


def _silu_grad(x, s):
    return s * (1.0 + x * (1.0 - s))


def _ln_stats(v):
    mu = jnp.mean(v, axis=-1, keepdims=True)
    vc = v - mu
    rstd = lax.rsqrt(jnp.mean(vc * vc, axis=-1, keepdims=True) + LN_EPS)
    return vc * rstd, rstd


def _ln_grad(dy, g, vhat, rstd):
    dvh = dy * g
    return rstd * (dvh - jnp.mean(dvh, axis=-1, keepdims=True)
                   - vhat * jnp.mean(dvh * vhat, axis=-1, keepdims=True))


def _rowsum(v):
    return jnp.sum(v, axis=0, keepdims=True)


def _coords():
    return lax.axis_index("x"), lax.axis_index("y"), lax.axis_index("c")


def _flip(pos, d):
    x, y, c = pos
    return (1 - x if d & 4 else x, 1 - y if d & 2 else y, 1 - c if d & 1 else c)


def _slot(pos):
    return 4 * pos[0] + 2 * pos[1] + pos[2]


CHIP_FLIPS = (4, 2, 6)


class _Gather:
    def __init__(self, arrays, mid_frac=0.8):
        self.arrays = list(arrays)
        self.n = n = len(arrays)
        self.mid_frac = mid_frac
        self.out_shape = [jax.ShapeDtypeStruct((NDEV,) + a.shape, a.dtype) for a in arrays]
        self.scratch = [pltpu.SemaphoreType.DMA((n, 7)), pltpu.SemaphoreType.DMA((n, 7)),
                        pltpu.SemaphoreType.DMA((n,))]

    def _copy(self, refs, i, k, block, to, own=False):
        ins, outs, (send, recv, _) = refs
        slot = outs[i].at[_slot(block)]
        return pltpu.make_async_remote_copy(
            src_ref=ins[i] if own else slot, dst_ref=slot, send_sem=send.at[i, k], recv_sem=recv.at[i, k],
            device_id=to, device_id_type=MESH)

    def _local(self, refs, i):
        ins, outs, (_, _, loc) = refs
        return pltpu.make_async_copy(ins[i], outs[i].at[_slot(_coords())], loc.at[i])

    def _first(self, refs, i, k):
        me = _coords()
        return self._copy(refs, i, k, me, _flip(me, ((1,) + CHIP_FLIPS)[k]), own=True)

    def _passed(self, refs, i, j):
        me = _coords()
        return self._copy(refs, i, 4 + j, _flip(me, CHIP_FLIPS[j]), _flip(me, 1))

    def start(self, refs):
        for i in range(self.n):
            self._local(refs, i).start()
        for k in range(4):
            for i in range(self.n):
                self._first(refs, i, k).start()

    def forward(self, refs):
        me = _coords()
        for j, d in enumerate(CHIP_FLIPS):
            for i in range(self.n):
                self._copy(refs, i, 1 + j, _flip(me, d), me).wait_recv()
                self._passed(refs, i, j).start()

    def finish(self, refs):
        me = _coords()
        sib = _flip(me, 1)
        for i in range(self.n):
            self._copy(refs, i, 0, sib, me).wait_recv()
        for j, d in enumerate(CHIP_FLIPS):
            for i in range(self.n):
                self._copy(refs, i, 4 + j, _flip(sib, d), me).wait_recv()
        for i in range(self.n):
            for k in range(4):
                self._first(refs, i, k).wait_send()
            for j in range(3):
                self._passed(refs, i, j).wait_send()
            self._local(refs, i).wait()


class _Scatter:
    def __init__(self, arrays):
        self.arrays = list(arrays)
        self.n = n = len(arrays)
        self.mid_frac = None
        self.out_shape = [jax.ShapeDtypeStruct(a.shape, a.dtype) for a in arrays]
        self.scratch = [pltpu.SemaphoreType.DMA((n, 7)), pltpu.SemaphoreType.DMA((n, 7)),
                        pltpu.SemaphoreType.DMA((n,))]

    def _copy(self, refs, i, d, landing):
        ins, outs, (send, recv, _) = refs
        me = _coords()
        peer = _flip(me, d)
        return pltpu.make_async_remote_copy(
            src_ref=ins[i].at[_slot(peer)], dst_ref=outs[i].at[_slot(peer) if landing else _slot(me)],
            send_sem=send.at[i, d - 1], recv_sem=recv.at[i, d - 1], device_id=peer, device_id_type=MESH)

    def _local(self, refs, i):
        ins, outs, (_, _, loc) = refs
        me = _slot(_coords())
        return pltpu.make_async_copy(ins[i].at[me], outs[i].at[me], loc.at[i])

    def start(self, refs):
        for i in range(self.n):
            self._local(refs, i).start()
        for d in range(1, NDEV):
            for i in range(self.n):
                self._copy(refs, i, d, False).start()

    def forward(self, refs):
        pass

    def finish(self, refs):
        for d in range(1, NDEV):
            for i in range(self.n):
                self._copy(refs, i, d, True).wait_recv()
        for d in range(1, NDEV):
            for i in range(self.n):
                self._copy(refs, i, d, False).wait_send()
        for i in range(self.n):
            self._local(refs, i).wait()


class _Both:
    def __init__(self, a, b):
        self.parts = (a, b)
        self.arrays = a.arrays + b.arrays
        self.n = a.n + b.n
        self.mid_frac = a.mid_frac if a.mid_frac is not None else b.mid_frac
        self.out_shape = a.out_shape + b.out_shape
        self.scratch = a.scratch + b.scratch

    def _each(self, refs):
        ins, outs, sems = refs
        na, ns = self.parts[0].n, len(self.parts[0].scratch)
        return ((self.parts[0], (ins[:na], outs[:na], sems[:ns])), (self.parts[1], (ins[na:], outs[na:], sems[ns:])))

    def start(self, refs):
        for part, r in self._each(refs):
            part.start(r)

    def forward(self, refs):
        for part, r in self._each(refs):
            part.forward(r)

    def finish(self, refs):
        for part, r in self._each(refs):
            part.finish(r)


def _call(body, name, grid, in_specs, out_specs, out_shape, scratch, args, comm=None, start_frac=0.0):
    params = _params(*(["arbitrary"] * len(grid)))
    if comm is None:
        outs = pl.pallas_call(
            body, name=name, grid=grid, in_specs=in_specs, out_specs=out_specs, out_shape=out_shape,
            scratch_shapes=scratch, compiler_params=params)(*args)
        return list(outs), []
    n_in, n_out, n_scr, k = len(in_specs), len(out_specs), len(scratch), comm.n
    nsteps = math.prod(grid) if grid else 1
    first = int(nsteps * start_frac)
    mid = max(first, min(nsteps - 1, int(nsteps * comm.mid_frac))) if comm.mid_frac is not None else None

    def hosted(*refs):
        ins, refs = refs[:n_in], refs[n_in:]
        cins, refs = refs[:k], refs[k:]
        outs, refs = refs[:n_out], refs[n_out:]
        couts, refs = refs[:k], refs[k:]
        scr, sems = refs[:n_scr], refs[n_scr:]
        crefs = (cins, couts, sems)
        if nsteps == 1:
            comm.start(crefs)
            body(*ins, *outs, *scr)
            comm.forward(crefs)
            comm.finish(crefs)
            return
        step = pl.program_id(0)
        for a in range(1, len(grid)):
            step = step * grid[a] + pl.program_id(a)
        pl.when(step == first)(lambda: comm.start(crefs))
        if mid is not None:
            pl.when(step == mid)(lambda: comm.forward(crefs))
        body(*ins, *outs, *scr)
        pl.when(step == nsteps - 1)(lambda: comm.finish(crefs))

    any_spec = pl.BlockSpec(memory_space=pl.ANY)
    outs = pl.pallas_call(
        hosted, name=name, grid=grid,
        in_specs=list(in_specs) + [any_spec] * k, out_specs=list(out_specs) + [any_spec] * k,
        out_shape=list(out_shape) + comm.out_shape, scratch_shapes=list(scratch) + comm.scratch,
        compiler_params=params)(*args, *comm.arrays)
    return list(outs[:n_out]), list(outs[n_out:])


def _rms_matmul(x, g, w, name, comm=None):
    T = x.shape[0]
    nb = w.shape[2]
    tT = _tile(T, 1024)
    per = 2

    def body(x_ref, g_ref, w_ref, z_ref, h_ref):
        @pl.when(pl.program_id(1) == 0)
        def _():
            xf = x_ref[...]
            r = lax.rsqrt(jnp.mean(xf * xf, axis=-1, keepdims=True) + RMS_EPS)
            h_ref[...] = (xf * r * g_ref[...]).astype(h_ref.dtype)

        for j in range(per):
            z_ref[:, j * nb:(j + 1) * nb] = jnp.dot(h_ref[...], w_ref[j], preferred_element_type=F32
                                                    ).astype(z_ref.dtype)

    return _call(
        body, name, (T // tT, NDEV // per),
        [pl.BlockSpec((tT, D), lambda t, n: (t, 0)),
         pl.BlockSpec((1, D), lambda t, n: (0, 0)),
         pl.BlockSpec((per, D, nb), lambda t, n: (n, 0, 0))],
        [pl.BlockSpec((tT, per * nb), lambda t, n: (t, n)),
         pl.BlockSpec((tT, D), lambda t, n: (t, 0))],
        [jax.ShapeDtypeStruct((T, NDEV * nb), ACT_DTYPE), jax.ShapeDtypeStruct((T, D), MXU_DTYPE)],
        [], (x, g.reshape(1, D), w), comm)


ARRIVAL = (0, 1) + CHIP_FLIPS + tuple(d ^ 1 for d in CHIP_FLIPS)


def _rms_matmul_gathering(x, g, w_shard, name, comm=None):
    T = x.shape[0]
    nb = w_shard.shape[1]
    tT = _tile(T, 1024)
    nT = T // tT

    def body(x_ref, g_ref, wsh_ref, z_ref, h_ref, wfull_ref, h_all, wbuf, zbuf, fetch_sems, z_sems, send_sems,
             recv_sems, own_sem):
        n, t = pl.program_id(0), pl.program_id(1)
        step = n * nT + t
        me = _coords()
        sib = _flip(me, 1)

        def remote(k, block, to, own=False):
            slot = wfull_ref.at[_slot(block)]
            return pltpu.make_async_remote_copy(
                src_ref=wsh_ref if own else slot, dst_ref=slot, send_sem=send_sems.at[k], recv_sem=recv_sems.at[k],
                device_id=to, device_id_type=MESH)

        def first(k):
            return remote(k, me, _flip(me, ARRIVAL[1 + k]), own=True)

        def passed(j):
            return remote(4 + j, _flip(me, CHIP_FLIPS[j]), sib)

        own_copy = pltpu.make_async_copy(wsh_ref, wfull_ref.at[_slot(me)], own_sem)

        def fetch(src, nn):
            return pltpu.make_async_copy(src, wbuf.at[nn % 2], fetch_sems.at[nn % 2])

        @pl.when(step == 0)
        def _():
            own_copy.start()
            for k in range(3):
                first(k).start()
            fetch(wsh_ref, 0).start()

        @pl.when(t == 0)
        def _():
            fetch(wfull_ref.at[0], n).wait()

        @pl.when(n == 0)
        def _():
            xf = x_ref[...]
            r = lax.rsqrt(jnp.mean(xf * xf, axis=-1, keepdims=True) + RMS_EPS)
            h = (xf * r * g_ref[...]).astype(h_ref.dtype)
            h_ref[...] = h
            h_all[t] = h

        def z_copy(s, col):
            return pltpu.make_async_copy(
                zbuf.at[s % 2], z_ref.at[pl.ds(pl.multiple_of(t * tT, tT), tT), pl.ds(col * nb, nb)],
                z_sems.at[s % 2])

        @pl.when(step >= 2)
        def _():
            z_copy(step, 0).wait()

        d = sum(jnp.where(n == nn, ARRIVAL[nn], 0) for nn in range(NDEV))
        col = _slot((me[0] ^ ((d >> 2) & 1), me[1] ^ ((d >> 1) & 1), me[2] ^ (d & 1)))
        zbuf[step % 2] = jnp.dot(h_all[t], wbuf[n % 2], preferred_element_type=F32).astype(zbuf.dtype)
        z_copy(step, col).start()

        for nn in range(1, NDEV):
            @pl.when((n == nn - 1) & (t == nT - 1))
            def _(nn=nn):
                block = _flip(me, ARRIVAL[nn])
                if nn == 1:
                    remote(0, sib, me).wait_recv()
                elif nn < 5:
                    if nn == 2:
                        first(3).start()
                    remote(nn - 1, block, me).wait_recv()
                    passed(nn - 2).start()
                else:
                    remote(nn - 1, block, me).wait_recv()
                fetch(wfull_ref.at[_slot(block)], nn).start()

        @pl.when(step == NDEV * nT - 1)
        def _():
            z_copy(step - 1, 0).wait()
            z_copy(step, 0).wait()
            for k in range(4):
                first(k).wait_send()
            for j in range(3):
                passed(j).wait_send()
            own_copy.wait()

    keep = lambda n, t: (jnp.where(n == 0, t, nT - 1), 0)
    any_spec = pl.BlockSpec(memory_space=pl.ANY)
    return _call(
        body, name, (NDEV, nT),
        [pl.BlockSpec((tT, D), keep), pl.BlockSpec((1, D), lambda n, t: (0, 0)), any_spec],
        [any_spec, pl.BlockSpec((tT, D), keep), any_spec],
        [jax.ShapeDtypeStruct((T, NDEV * nb), ACT_DTYPE), jax.ShapeDtypeStruct((T, D), MXU_DTYPE),
         jax.ShapeDtypeStruct((NDEV,) + w_shard.shape, w_shard.dtype)],
        [pltpu.VMEM((nT, tT, D), MXU_DTYPE), pltpu.VMEM((2, D, nb), w_shard.dtype),
         pltpu.VMEM((2, tT, nb), ACT_DTYPE), pltpu.SemaphoreType.DMA((2,)), pltpu.SemaphoreType.DMA((2,)),
         pltpu.SemaphoreType.DMA((7,)), pltpu.SemaphoreType.DMA((7,)), pltpu.SemaphoreType.DMA],
        (x, g.reshape(1, D), w_shard), comm, start_frac=0.5)


def _rms_matmul_bwd(dz, w, x, g, dxo, name, comm=None):
    T = x.shape[0]
    nb = w.shape[2]
    tT = _tile(T, 512)
    per = 4
    steps = NDEV // per

    def body(dz_ref, w_ref, x_ref, g_ref, dxo_ref, dx_ref, dg_ref, acc):
        t, n = pl.program_id(0), pl.program_id(1)

        @pl.when((n == 0) & (t == 0))
        def _():
            dg_ref[...] = jnp.zeros_like(dg_ref)

        part = None
        for j in range(per):
            d = lax.dot_general(_mx(dz_ref[:, j * nb:(j + 1) * nb]), w_ref[j], NT_DIMS, preferred_element_type=F32)
            part = d if part is None else part + d

        @pl.when(n == 0)
        def _():
            acc[...] = part

        @pl.when(n > 0)
        def _():
            acc[...] += part

        @pl.when(n == steps - 1)
        def _():
            xf = x_ref[...]
            r = lax.rsqrt(jnp.mean(xf * xf, axis=-1, keepdims=True) + RMS_EPS)
            xh = xf * r
            dh = acc[...]
            dxh = dh * g_ref[...]
            dx_ref[...] = dxo_ref[...] + r * (dxh - xh * jnp.mean(dxh * xh, axis=-1, keepdims=True))
            dg_ref[...] += _rowsum(dh * xh)

    return _call(
        body, name, (T // tT, steps),
        [pl.BlockSpec((tT, per * nb), lambda t, n: (t, n)),
         pl.BlockSpec((per, D, nb), lambda t, n: (n, 0, 0)),
         pl.BlockSpec((tT, D), lambda t, n: (t, 0)),
         pl.BlockSpec((1, D), lambda t, n: (0, 0)),
         pl.BlockSpec((tT, D), lambda t, n: (t, 0))],
        [pl.BlockSpec((tT, D), lambda t, n: (t, 0)),
         pl.BlockSpec((1, D), lambda t, n: (0, 0))],
        [jax.ShapeDtypeStruct((T, D), F32), jax.ShapeDtypeStruct((1, D), F32)],
        [pltpu.VMEM((tT, D), F32)], (dz, w, x, g.reshape(1, D), dxo), comm)


def _atb(a, b, nblk, name, comm=None):
    G, T, M = a.shape
    N = b.shape[2]
    nb = N // nblk
    tk = _tile(T, 2048)
    nk = T // tk

    def body(a_ref, b_ref, o_ref, acc):
        k = pl.program_id(1)

        @pl.when(k == 0)
        def _():
            acc[...] = jnp.zeros_like(acc)

        acc[...] += lax.dot_general(_mx(a_ref[...]), _mx(b_ref[...]), TN_DIMS, preferred_element_type=F32)

        @pl.when(k == nk - 1)
        def _():
            o_ref[...] = acc[...].astype(o_ref.dtype)

    outs, couts = _call(
        body, name, (G * nblk, nk),
        [pl.BlockSpec((None, tk, M), lambda n, k: (n // nblk, k, 0)),
         pl.BlockSpec((None, tk, nb), lambda n, k: (n // nblk, k, n % nblk))],
        [pl.BlockSpec((None, M, nb), lambda n, k: (n, 0, 0))],
        [jax.ShapeDtypeStruct((G * nblk, M, nb), GRAD_DTYPE)],
        [pltpu.VMEM((M, nb), F32)], (a, b), comm)
    return outs[0], couts


def _loss_head(x, tgt, g, name):
    T = x.shape[0]
    tT = _tile(T, 512)

    def body(x_ref, t_ref, g_ref, loss_ref, dx_ref, dg_ref):
        @pl.when(pl.program_id(0) == 0)
        def _():
            loss_ref[...] = jnp.zeros_like(loss_ref)
            dg_ref[...] = jnp.zeros_like(dg_ref)

        xf = x_ref[...]
        r = lax.rsqrt(jnp.mean(xf * xf, axis=-1, keepdims=True) + RMS_EPS)
        xh = xf * r
        err = xh * g_ref[...] - t_ref[...]
        loss_ref[...] += 0.5 * jnp.sum(jnp.mean(err * err, axis=-1, keepdims=True), axis=0, keepdims=True)
        dy = err * (1.0 / D)
        dxh = dy * g_ref[...]
        dx_ref[...] = r * (dxh - xh * jnp.mean(dxh * xh, axis=-1, keepdims=True))
        dg_ref[...] += _rowsum(dy * xh)

    return pl.pallas_call(
        body, name=name, grid=(T // tT,),
        in_specs=[pl.BlockSpec((tT, D), lambda t: (t, 0)),
                  pl.BlockSpec((tT, D), lambda t: (t, 0)),
                  pl.BlockSpec((1, D), lambda t: (0, 0))],
        out_specs=[pl.BlockSpec((1, 1), lambda t: (0, 0)),
                   pl.BlockSpec((tT, D), lambda t: (t, 0)),
                   pl.BlockSpec((1, D), lambda t: (0, 0))],
        out_shape=[jax.ShapeDtypeStruct((1, 1), F32), jax.ShapeDtypeStruct((T, D), F32),
                   jax.ShapeDtypeStruct((1, D), F32)],
        compiler_params=_params("arbitrary"),
    )(x, tgt, g.reshape(1, D))


def _spatial_gate(wm_ref, bst_ref, vb_ref, sv_ref, n_chunks):
    for c in range(n_chunks):
        rows = slice(c * CHUNK, (c + 1) * CHUNK)
        for g in range(N_GROUPS):
            cols = slice(g * CHUNK, (g + 1) * CHUNK)
            sv_ref[rows, cols] = (jnp.dot(wm_ref[g], vb_ref[rows, cols], preferred_element_type=F32)
                                  + bst_ref[:, g:g + 1])


def _lane_loop(fn):
    def step(i, carry):
        fn(pl.ds(pl.multiple_of(i * LANES, LANES), LANES))
        return carry

    lax.fori_loop(0, D // LANES, step, 0)


def _shifted_copies(buf, sh, n):
    for s in range(1, 8):
        sh[s - 1, 0:n, :] = buf[s:s + n, :]


def _window(buf, sh, base, off, lanes):
    a, s = divmod(off, 8)
    src = buf if s == 0 else sh.at[s - 1]
    return src[base + 8 * a:base + 8 * a + 8, lanes]


def _softmax_rows(s):
    e = jnp.exp(s - jnp.max(s, axis=-1, keepdims=True))
    return e / jnp.sum(e, axis=-1, keepdims=True)


def _branch_fwd(z, kv, wm, bst, ln_a, cw, cvec, name, comm=None):
    T = z.shape[0]
    tT = _tile(T, 256)
    n_chunks = tT // CHUNK

    def body(z_ref, kv_ref, wm_ref, bst_ref, lna_ref, cw_ref, cvec_ref, br_ref, c_ref, gbuf, gsh, vb, ua):
        @pl.when(pl.program_id(0) == 0)
        def _():
            gbuf[0:HALO, :] = jnp.zeros((HALO, D), F32)

        def seg(s):
            return z_ref[:, s * D:(s + 1) * D].astype(F32)

        u, _ = _gelu(seg(SEG_AU))
        zg = seg(SEG_AG)
        ua[...] = u * (zg * jax.nn.sigmoid(zg))
        gv, _ = _gelu(seg(SEG_AV))
        vhat, _ = _ln_stats(gv)
        vb[...] = _mx(vhat * lna_ref[0:1, :] + lna_ref[1:2, :])
        for c in range(n_chunks):
            rows = slice(c * CHUNK, (c + 1) * CHUNK)
            for g in range(N_GROUPS):
                cols = slice(g * CHUNK, (g + 1) * CHUNK)
                sv = jnp.dot(wm_ref[g], vb[rows, cols], preferred_element_type=F32) + bst_ref[:, g:g + 1]
                br_ref[0, rows, cols] = (sv * ua[rows, cols]).astype(br_ref.dtype)

        gbuf[HALO:HALO + tT, :] = seg(SEG_BA) * jax.nn.sigmoid(seg(SEG_BB))
        _shifted_copies(gbuf, gsh, tT + HALO - 8)
        def conv_lanes(lanes):
            taps = [jnp.broadcast_to(cw_ref[k:k + 1, lanes], (8, LANES)) for k in range(CONV_K)]
            bias = jnp.broadcast_to(cvec_ref[0:1, lanes], (8, LANES))
            for base in range(0, tT, 8):
                acc = [bias, None, None, None]
                for k in range(CONV_K):
                    term = taps[k] * _window(gbuf, gsh, base, k + HALO - CONV_K + 1, lanes)
                    acc[k % 4] = term if acc[k % 4] is None else acc[k % 4] + term
                c_ref[base:base + 8, lanes] = (acc[0] + acc[1]) + (acc[2] + acc[3])

        _lane_loop(conv_lanes)
        gbuf[0:HALO, :] = gbuf[tT:tT + HALO, :]
        chat, _ = _ln_stats(c_ref[...])
        cl = chat * cvec_ref[1:2, :] + cvec_ref[2:3, :]
        zg = seg(SEG_BG)
        br_ref[1] = (cl * jax.nn.sigmoid(cl) * (zg * jax.nn.sigmoid(zg))).astype(br_ref.dtype)

        for h in range(HEADS):
            cols = slice(h * HEAD_DIM, (h + 1) * HEAD_DIM)
            q = _mx(z_ref[:, SEG_CQ * D + h * HEAD_DIM:SEG_CQ * D + (h + 1) * HEAD_DIM])
            s = lax.dot_general(q, kv_ref[:, cols], NT_DIMS, preferred_element_type=F32)
            p = _softmax_rows(s * (1.0 / math.sqrt(HEAD_DIM)))
            att = jnp.dot(_mx(p), kv_ref[:, D + h * HEAD_DIM:D + (h + 1) * HEAD_DIM], preferred_element_type=F32)
            zg = z_ref[:, SEG_CG * D + h * HEAD_DIM:SEG_CG * D + (h + 1) * HEAD_DIM].astype(F32)
            br_ref[2, :, cols] = (att * (zg * jax.nn.sigmoid(zg))).astype(br_ref.dtype)

    full = lambda shape: pl.BlockSpec(shape, lambda t: (0,) * len(shape))
    return _call(
        body, name, (T // tT,),
        [pl.BlockSpec((tT, SEG_M * D), lambda t: (t, 0)),
         full(kv.shape), full(wm.shape), full(bst.shape), full(ln_a.shape), full(cw.shape), full(cvec.shape)],
        [pl.BlockSpec((3, tT, D), lambda t: (0, t, 0)), pl.BlockSpec((tT, D), lambda t: (t, 0))],
        [jax.ShapeDtypeStruct((3, T, D), MXU_DTYPE), jax.ShapeDtypeStruct((T, D), F32)],
        [pltpu.VMEM((tT + HALO, D), F32), pltpu.VMEM((7, tT + HALO - 8, D), F32),
         pltpu.VMEM((tT, D), MXU_DTYPE), pltpu.VMEM((tT, D), F32)],
        (z, kv, wm, bst, ln_a, cw, cvec), comm)


def _branch_bwd(z, dbr, c, dzm, kv, wm, wmt, bst, ln_a, cw, cvec, name, comm=None):
    T = z.shape[0]
    M = kv.shape[0]
    tT = _tile(T, 128)
    nT = T // tT
    n_chunks = tT // CHUNK

    def body(z_ref, dbr_ref, c_ref, dzm_ref, kv_ref, wm_ref, wmt_ref, bst_ref, lna_ref,
             cw_ref, cvec_ref, dz_ref, vecg_ref, dbst_ref, dws_ref, dcw_ref, dkv_ref,
             gbuf, dcbuf, vb, dsvb, sv, dvbuf, dcsh, dglu, vh, gq, dcw8, dcw_step):
        i = pl.program_id(0)

        @pl.when(i == 0)
        def _():
            vecg_ref[...] = jnp.zeros_like(vecg_ref)
            dbst_ref[...] = jnp.zeros_like(dbst_ref)
            dws_ref[...] = jnp.zeros_like(dws_ref)
            dcw8[...] = jnp.zeros_like(dcw8)
            dkv_ref[...] = jnp.zeros_like(dkv_ref)
            dcbuf[tT:tT + HALO, :] = jnp.zeros((HALO, D), F32)

        strips = [slice(r0, r0 + STRIP) for r0 in range(0, tT, STRIP)]

        def seg(r, s):
            return z_ref[r, s * D:(s + 1) * D].astype(F32)

        def put(r, s, val):
            dz_ref[r, s * D:(s + 1) * D] = val.astype(dz_ref.dtype)

        for r in strips:
            zv = seg(r, SEG_AV)
            gv, tv = _gelu(zv)
            vhat, rstd = _ln_stats(gv)
            vb[r, :] = _mx(vhat * lna_ref[0:1, :] + lna_ref[1:2, :])
            vh[r, :] = vhat
            gq[r, :] = rstd * _gelu_grad(zv, tv)
        _spatial_gate(wm_ref, bst_ref, vb, sv, n_chunks)
        for r in strips:
            zu, zg = seg(r, SEG_AU), seg(r, SEG_AG)
            u, tu = _gelu(zu)
            sg = jax.nn.sigmoid(zg)
            d_a = dbr_ref[0, r, :].astype(F32)
            put(r, SEG_AU, d_a * sv[r, :] * (zg * sg) * _gelu_grad(zu, tu))
            put(r, SEG_AG, d_a * u * sv[r, :] * _silu_grad(zg, sg))
            dsv = d_a * u * (zg * sg)
            dsvb[r, :] = _mx(dsv)
            in_chunk = slice(r.start % CHUNK, r.start % CHUNK + STRIP)
            for g in range(N_GROUPS):
                dbst_ref[in_chunk, g:g + 1] += jnp.sum(dsv[:, g * CHUNK:(g + 1) * CHUNK], axis=-1, keepdims=True)
        tril = (lax.broadcasted_iota(jnp.int32, (CHUNK, CHUNK), 0)
                >= lax.broadcasted_iota(jnp.int32, (CHUNK, CHUNK), 1))
        for g in range(N_GROUPS):
            cols = slice(g * CHUNK, (g + 1) * CHUNK)
            for cc in range(n_chunks):
                rows = slice(cc * CHUNK, (cc + 1) * CHUNK)
                dws = lax.dot_general(dsvb[rows, cols], vb[rows, cols], NT_DIMS, preferred_element_type=F32)
                dws_ref[g] += jnp.where(tril, dws, 0.0)
                dvbuf[rows, cols] = jnp.dot(wmt_ref[g], dsvb[rows, cols], preferred_element_type=F32)
        for r in strips:
            dv, vhat = dvbuf[r, :], vh[r, :]
            vecg_ref[0:1, :] += _rowsum(dv * vhat)
            vecg_ref[1:2, :] += _rowsum(dv)
            dvh = dv * lna_ref[0:1, :]
            put(r, SEG_AV, (dvh - jnp.mean(dvh, axis=-1, keepdims=True)
                            - vhat * jnp.mean(dvh * vhat, axis=-1, keepdims=True)) * gq[r, :])

        sgb_buf = sv
        for r in strips:
            za, zg = seg(r, SEG_BA), seg(r, SEG_BG)
            sgb = jax.nn.sigmoid(seg(r, SEG_BB))
            sgb_buf[r, :] = sgb
            gbuf[r, :] = za * sgb
            chat, crstd = _ln_stats(c_ref[r, :])
            cl = chat * cvec_ref[1:2, :] + cvec_ref[2:3, :]
            scl = jax.nn.sigmoid(cl)
            sg = jax.nn.sigmoid(zg)
            d_b = dbr_ref[1, r, :].astype(F32)
            put(r, SEG_BG, d_b * (cl * scl) * _silu_grad(zg, sg))
            dcl = d_b * (zg * sg) * _silu_grad(cl, scl)
            vecg_ref[3:4, :] += _rowsum(dcl * chat)
            vecg_ref[4:5, :] += _rowsum(dcl)
            dc = _ln_grad(dcl, cvec_ref[1:2, :], chat, crstd)
            vecg_ref[2:3, :] += _rowsum(dc)
            dcbuf[r, :] = dc
        _shifted_copies(dcbuf, dcsh, tT + HALO - 8)

        def conv_grads(lanes):
            taps = [jnp.broadcast_to(cw_ref[k:k + 1, lanes], (8, LANES)) for k in range(CONV_K)]
            wsum = [None] * CONV_K
            for base in range(0, tT, 8):
                glu = gbuf[base:base + 8, lanes]
                acc = [None] * 4
                for k in range(CONV_K):
                    win = _window(dcbuf, dcsh, base, CONV_K - 1 - k, lanes)
                    term = taps[k] * win
                    acc[k % 4] = term if acc[k % 4] is None else acc[k % 4] + term
                    term = glu * win
                    wsum[k] = term if wsum[k] is None else wsum[k] + term
                dglu[base:base + 8, lanes] = (acc[0] + acc[1]) + (acc[2] + acc[3])
            for k in range(CONV_K):
                dcw_step[8 * k:8 * k + 8, lanes] = wsum[k]

        _lane_loop(conv_grads)
        dcw8[...] += dcw_step[...]

        @pl.when(i == nT - 1)
        def _():
            for k in range(CONV_K):
                dcw_ref[k:k + 1, :] = _rowsum(dcw8[8 * k:8 * k + 8, :])
            dcw_ref[CONV_K:HALO, :] = jnp.zeros((HALO - CONV_K, D), F32)

        dcbuf[tT:tT + HALO, :] = dcbuf[0:HALO, :]
        for r in strips:
            dg, sgb = dglu[r, :], sgb_buf[r, :]
            put(r, SEG_BA, dg * sgb)
            put(r, SEG_BB, dg * seg(r, SEG_BA) * sgb * (1.0 - sgb))

        scale = 1.0 / math.sqrt(HEAD_DIM)
        for h in range(HEADS):
            cols = slice(h * HEAD_DIM, (h + 1) * HEAD_DIM)
            qcols = slice(SEG_CQ * D + h * HEAD_DIM, SEG_CQ * D + (h + 1) * HEAD_DIM)
            gcols = slice(SEG_CG * D + h * HEAD_DIM, SEG_CG * D + (h + 1) * HEAD_DIM)
            vcols = slice(D + h * HEAD_DIM, D + (h + 1) * HEAD_DIM)
            q = _mx(z_ref[:, qcols])
            kh, vh = kv_ref[:, cols], kv_ref[:, vcols]
            p = _softmax_rows(lax.dot_general(q, kh, NT_DIMS, preferred_element_type=F32) * scale)
            pb = _mx(p)
            att = jnp.dot(pb, vh, preferred_element_type=F32)
            zg = z_ref[:, gcols].astype(F32)
            sg = jax.nn.sigmoid(zg)
            d_c = dbr_ref[2, :, cols].astype(F32)
            dz_ref[:, gcols] = (d_c * att * _silu_grad(zg, sg)).astype(dz_ref.dtype)
            datt = _mx(d_c * (zg * sg))
            dp = lax.dot_general(datt, vh, NT_DIMS, preferred_element_type=F32)
            dkv_ref[:, vcols] += lax.dot_general(pb, datt, TN_DIMS, preferred_element_type=F32)
            ds = _mx(p * (dp - jnp.sum(dp * p, axis=-1, keepdims=True)) * scale)
            dz_ref[:, qcols] = jnp.dot(ds, kh, preferred_element_type=F32).astype(dz_ref.dtype)
            dkv_ref[:, cols] += lax.dot_general(ds, q, TN_DIMS, preferred_element_type=F32)

        dz_ref[:, SEG_M * D:] = dzm_ref[...].astype(dz_ref.dtype)

    rev = lambda i: nT - 1 - i
    full = lambda shape: pl.BlockSpec(shape, lambda i: (0,) * len(shape))
    return _call(
        body, name, (nT,),
        [pl.BlockSpec((tT, SEG_M * D), lambda i: (rev(i), 0)),
         pl.BlockSpec((3, tT, D), lambda i: (0, rev(i), 0)),
         pl.BlockSpec((tT, D), lambda i: (rev(i), 0)),
         pl.BlockSpec((tT, 3 * D), lambda i: (rev(i), 0)),
         full(kv.shape), full(wm.shape), full(wmt.shape), full(bst.shape), full(ln_a.shape),
         full(cw.shape), full(cvec.shape)],
        [pl.BlockSpec((tT, N_IN), lambda i: (rev(i), 0)),
         full((8, D)), full((CHUNK, N_GROUPS)), full((N_GROUPS, CHUNK, CHUNK)), full((HALO, D)),
         full((M, 2 * D))],
        [jax.ShapeDtypeStruct((T, N_IN), MXU_DTYPE), jax.ShapeDtypeStruct((8, D), F32),
         jax.ShapeDtypeStruct((CHUNK, N_GROUPS), F32),
         jax.ShapeDtypeStruct((N_GROUPS, CHUNK, CHUNK), F32),
         jax.ShapeDtypeStruct((HALO, D), F32), jax.ShapeDtypeStruct((M, 2 * D), F32)],
        [pltpu.VMEM((tT, D), F32), pltpu.VMEM((tT + HALO, D), F32),
         pltpu.VMEM((tT, D), MXU_DTYPE), pltpu.VMEM((tT, D), MXU_DTYPE),
         pltpu.VMEM((tT, D), F32), pltpu.VMEM((tT, D), F32),
         pltpu.VMEM((7, tT + HALO - 8, D), F32),
         pltpu.VMEM((tT, D), F32), pltpu.VMEM((tT, D), F32), pltpu.VMEM((tT, D), F32),
         pltpu.VMEM((CONV_K * 8, D), F32), pltpu.VMEM((CONV_K * 8, D), F32)],
        (z, dbr, c, dzm, kv, wm, wmt, bst, ln_a, cw, cvec), comm)


def _merge_fwd(br, z, x, wb, wo, name):
    T = x.shape[0]
    tT = _tile(T, 512)

    def body(br_ref, z0, z1, z2, x_ref, wb_ref, wo_ref, xn_ref, mg_ref, pj_ref):
        merged = jnp.zeros((tT, D), F32)
        for n, zm in enumerate((z0, z1, z2)):
            proj = jnp.dot(br_ref[n], wb_ref[:, n].reshape(D, D), preferred_element_type=F32)
            pj_ref[n] = proj.astype(pj_ref.dtype)
            merged = merged + jax.nn.sigmoid(zm[...].astype(F32)) * proj
        mg_ref[...] = merged.astype(mg_ref.dtype)
        xn_ref[...] = x_ref[...] + jnp.dot(_mx(merged), wo_ref[...].reshape(D, D), preferred_element_type=F32)

    zspec = lambda n: pl.BlockSpec((tT, D), lambda t: (t, SEG_M + n))
    return pl.pallas_call(
        body, name=name, grid=(T // tT,),
        in_specs=[pl.BlockSpec((3, tT, D), lambda t: (0, t, 0)), zspec(0), zspec(1), zspec(2),
                  pl.BlockSpec((tT, D), lambda t: (t, 0)),
                  pl.BlockSpec(wb.shape, lambda t: (0, 0, 0, 0)),
                  pl.BlockSpec(wo.shape, lambda t: (0, 0, 0))],
        out_specs=[pl.BlockSpec((tT, D), lambda t: (t, 0)),
                   pl.BlockSpec((tT, D), lambda t: (t, 0)),
                   pl.BlockSpec((3, tT, D), lambda t: (0, t, 0))],
        out_shape=[jax.ShapeDtypeStruct((T, D), F32), jax.ShapeDtypeStruct((T, D), MXU_DTYPE),
                   jax.ShapeDtypeStruct((3, T, D), ACT_DTYPE)],
        compiler_params=_params("parallel"),
    )(br, z, z, z, x, wb, wo)


def _merge_bwd(dxo, proj, z, wb, wo, name):
    T = dxo.shape[0]
    tT = _tile(T, 512)

    def body(dxo_ref, pj_ref, z0, z1, z2, wb_ref, wo_ref, dpj_ref, dbr_ref, dzm_ref):
        dmerged = lax.dot_general(_mx(dxo_ref[...]), wo_ref[...].reshape(D, D), NT_DIMS,
                                  preferred_element_type=F32)
        for n, zm in enumerate((z0, z1, z2)):
            gate = jax.nn.sigmoid(zm[...].astype(F32))
            dproj = _mx(gate * dmerged)
            dpj_ref[n] = dproj
            dzm_ref[:, n * D:(n + 1) * D] = (pj_ref[n].astype(F32) * dmerged * gate * (1.0 - gate)
                                             ).astype(dzm_ref.dtype)
            dbr_ref[n] = lax.dot_general(dproj, wb_ref[:, n].reshape(D, D), NT_DIMS,
                                         preferred_element_type=F32).astype(dbr_ref.dtype)

    zspec = lambda n: pl.BlockSpec((tT, D), lambda t: (t, SEG_M + n))
    return pl.pallas_call(
        body, name=name, grid=(T // tT,),
        in_specs=[pl.BlockSpec((tT, D), lambda t: (t, 0)),
                  pl.BlockSpec((3, tT, D), lambda t: (0, t, 0)), zspec(0), zspec(1), zspec(2),
                  pl.BlockSpec(wb.shape, lambda t: (0, 0, 0, 0)),
                  pl.BlockSpec(wo.shape, lambda t: (0, 0, 0))],
        out_specs=[pl.BlockSpec((3, tT, D), lambda t: (0, t, 0)),
                   pl.BlockSpec((3, tT, D), lambda t: (0, t, 0)),
                   pl.BlockSpec((tT, 3 * D), lambda t: (t, 0))],
        out_shape=[jax.ShapeDtypeStruct((3, T, D), MXU_DTYPE), jax.ShapeDtypeStruct((3, T, D), ACT_DTYPE),
                   jax.ShapeDtypeStruct((T, 3 * D), MXU_DTYPE)],
        compiler_params=_params("parallel"),
    )(dxo, proj, z, z, z, wb, wo)


def _adamw(parts, w, m, v, name, comm=None):
    G, R, C = w.shape
    tr = 128 if R % 128 == 0 else R
    nr = R // tr
    c1 = 1.0 / (1.0 - ADAM_B1 ** ADAM_STEP)
    c2 = 1.0 / (1.0 - ADAM_B2 ** ADAM_STEP)

    def body(*refs):
        p_refs, (w_ref, m_ref, v_ref, g_out, d_out, m_out, v_out) = refs[:G], refs[G:]
        for i in range(G):
            @pl.when(pl.program_id(0) == i)
            def _(p_ref=p_refs[i]):
                g = p_ref[0].astype(F32)
                for p in range(1, NDEV):
                    g = g + p_ref[p].astype(F32)
                mn = ADAM_B1 * m_ref[...] + (1.0 - ADAM_B1) * g
                vn = ADAM_B2 * v_ref[...] + (1.0 - ADAM_B2) * (g * g)
                g_out[...] = g
                m_out[...] = mn
                v_out[...] = vn
                d_out[...] = -ADAM_LR * ((mn * c1) / (jnp.sqrt(vn * c2) + ADAM_EPS) + ADAM_WD * w_ref[...])

    def parts_spec(i):
        return pl.BlockSpec((NDEV, tr, C), lambda l, r: (0, jnp.where(l == i, r, jnp.where(l > i, nr - 1, 0)), 0))

    spec = pl.BlockSpec((None, tr, C), lambda l, r: (l, r, 0))
    return _call(
        body, name, (G, nr), [parts_spec(i) for i in range(G)] + [spec] * 3,
        [spec] * 4, [jax.ShapeDtypeStruct((G, R, C), F32)] * 4, [], (*parts, w, m, v), comm)


def kernel(x, mem, norm_g, mem_norm_g, w_in, gmlp_ln_g, gmlp_ln_b, w_s, b_s, conv_w, conv_b, conv_ln_g, conv_ln_b, w_kv, w_branch, w_out, final_norm_g, loss_target, m_norm_g, m_mem_norm_g, m_w_in, m_gmlp_ln_g, m_gmlp_ln_b, m_w_s, m_b_s, m_conv_w, m_conv_b, m_conv_ln_g, m_conv_ln_b, m_w_kv, m_w_branch, m_w_out, m_final_norm_g, v_norm_g, v_mem_norm_g, v_w_in, v_gmlp_ln_g, v_gmlp_ln_b, v_w_s, v_b_s, v_conv_w, v_conv_b, v_conv_ln_g, v_conv_ln_b, v_w_kv, v_w_branch, v_w_out, v_final_norm_g):
    L = w_in.shape[0]
    x0, mem0, tgt = x[0], mem[0], loss_target[0]
    T, M = x0.shape[0], mem0.shape[0]
    nbc = conv_w.shape[2]

    def shards(l):
        return [_mx(w_in[l]), _mx(w_kv[l]), _mx(w_branch[l]), _mx(w_out[l]), conv_w[l]]

    gather_rest = _Gather(shards(0)[1:])
    gather_upper = _Gather([a for l in range(1, L) for a in shards(l)], mid_frac=0.85) if L > 1 else None

    tril = jnp.tril(jnp.ones((CHUNK, CHUNK), bool))
    wm = [_mx(jnp.where(tril[None], w_s[l], 0.0)) for l in range(L)]
    wmt = [w.transpose(0, 2, 1) for w in wm]
    bst = [b_s[l].T for l in range(L)]
    ln_a = [jnp.stack([gmlp_ln_g[l], gmlp_ln_b[l]]) for l in range(L)]
    cvec = [jnp.stack([conv_b[l], conv_ln_g[l], conv_ln_b[l]]) for l in range(L)]

    def conv_taps(gathered):
        return jnp.pad(gathered.transpose(1, 0, 2).reshape(CONV_K, D), ((0, HALO - CONV_K), (0, 0)))

    win, wkv, wbr, wou, cwf = [], [], [], [], []
    memn, kvs, xs, saved = [], [], [x0], []
    for l in range(L):
        if l == 0:
            (z, h, w0), full = _rms_matmul_gathering(x0, norm_g[0], shards(0)[0], "inproj_fwd0", gather_rest)
            win, wkv, wbr, wou, cwf = [w0], [full[0]], [full[1]], [full[2]], [conv_taps(full[3])]
        else:
            (z, h), _ = _rms_matmul(xs[l], norm_g[l], win[l], f"inproj_fwd{l}")
        (kv, mn), _ = _rms_matmul(mem0, mem_norm_g[l], wkv[l], f"kv_fwd{l}")
        kvs.append(_mx(kv))
        memn.append(mn)
        (br, cpre), full = _branch_fwd(z, kvs[l], wm[l], bst[l], ln_a[l], cwf[l], cvec[l], f"branch_fwd{l}",
                                       gather_upper if l == 0 else None)
        for k in range(1, L if l == 0 else 0):
            f = full[5 * (k - 1):5 * k]
            win.append(f[0])
            wkv.append(f[1])
            wbr.append(f[2])
            wou.append(f[3])
            cwf.append(conv_taps(f[4]))
        xn, merged, proj = _merge_fwd(br, z, xs[l], wbr[l], wou[l], f"merge_fwd{l}")
        xs.append(xn)
        saved.append((z, h, br, cpre, merged, proj))
    loss_part, dx, dfg = _loss_head(xs[L], tgt, final_norm_g, "loss_head")

    pending, recv = [("final_norm_g", dfg, True)], {}

    def flush():
        scat = [(k, a) for k, a, g in pending if not g]
        gath = [(k, a) for k, a, g in pending if g]
        pending.clear()
        comms = ([_Scatter([a for _, a in scat])] if scat else []) + ([_Gather([a for _, a in gath])] if gath else [])
        return [k for k, _ in scat + gath], comms[0] if len(comms) == 1 else _Both(*comms)

    def landed(keys, arrays):
        recv.update(zip(keys, arrays))

    for l in reversed(range(L)):
        z, h, br, cpre, merged, proj = saved[l]
        dproj, dbr, dzm = _merge_bwd(dx, proj, z, wbr[l], wou[l], f"merge_bwd{l}")
        dwb, _ = _atb(br, dproj, 1, f"dwbranch{l}")
        for n in range(3):
            pending.append((f"w_branch{l}_{n}", dwb[n].reshape(NDEV, D // NDEV, D), False))
        dwo, _ = _atb(merged[None], dx[None], 1, f"dwout{l}")
        pending.append((f"w_out{l}", dwo.reshape(NDEV, D // NDEV, D), False))
        keys, comm = flush()
        (dz, vecg, dbst, dws, dcw, dkv), got = _branch_bwd(
            z, dbr, cpre, dzm, kvs[l], wm[l], wmt[l], bst[l], ln_a[l], cwf[l], cvec[l], f"branch_bwd{l}", comm)
        landed(keys, got)
        dwk, _ = _atb(memn[l][None], dkv[None], NDEV, f"dwkv{l}")
        (_, dmg), _ = _rms_matmul_bwd(dkv, wkv[l], mem0, mem_norm_g[l], jnp.zeros((M, D), F32), f"kv_bwd{l}")
        rest = jnp.concatenate([dmg, vecg[0:2], vecg[2:5], dbst.T.reshape(1, D)], axis=0)
        pending += [(f"w_kv{l}", dwk, False),
                    (f"conv_w{l}", dcw[:CONV_K].reshape(CONV_K, NDEV, nbc).transpose(1, 0, 2), False),
                    (f"small{l}", rest, True), (f"w_s{l}", dws.reshape(N_GROUPS * CHUNK, CHUNK), True)]
        keys, comm = flush()
        dwi, got = _atb(h[None], dz[None], NDEV, f"dwin{l}", comm)
        landed(keys, got)
        pending.append((f"w_in{l}", dwi, False))
        keys, comm = flush()
        (dx, dng), got = _rms_matmul_bwd(dz, win[l], xs[l], norm_g[l], dx, f"inproj_bwd{l}", comm)
        landed(keys, got)
        pending.append((f"norm_g{l}", dng, True))
    grad_x = dx[None]

    def pack(p):
        rows = []
        for l in range(L):
            rows += [p["norm_g"][l], p["mem_norm_g"][l], p["gmlp_ln_g"][l], p["gmlp_ln_b"][l], p["conv_b"][l],
                     p["conv_ln_g"][l], p["conv_ln_b"][l], p["b_s"][l].reshape(D)]
        return jnp.stack(rows + [p["final_norm_g"]])[None]

    names = ["norm_g", "mem_norm_g", "gmlp_ln_g", "gmlp_ln_b", "conv_b", "conv_ln_g", "conv_ln_b", "b_s",
             "final_norm_g"]
    w_small = pack(dict(zip(names, [norm_g, mem_norm_g, gmlp_ln_g, gmlp_ln_b, conv_b, conv_ln_g, conv_ln_b,
                                    b_s, final_norm_g])))
    m_small = pack(dict(zip(names, [m_norm_g, m_mem_norm_g, m_gmlp_ln_g, m_gmlp_ln_b, m_conv_b, m_conv_ln_g,
                                    m_conv_ln_b, m_b_s, m_final_norm_g])))
    v_small = pack(dict(zip(names, [v_norm_g, v_mem_norm_g, v_gmlp_ln_g, v_gmlp_ln_b, v_conv_b, v_conv_ln_g,
                                    v_conv_ln_b, v_b_s, v_final_norm_g])))
    outs = {}

    def run(key, parts, w, m, v, comm=None):
        outs[key], got = _adamw(parts, w, m, v, "adamw_" + key, comm)
        return got

    keys, comm = flush()
    landed(keys, run("w_in", [recv[f"w_in{l}"] for l in range(L)], w_in, m_w_in, v_w_in, comm))
    parts_small = jnp.concatenate([recv[f"{k}{l}"] for l in range(L) for k in ("norm_g", "small")]
                                  + [recv["final_norm_g"]], axis=1)
    parts_ws = jnp.concatenate([recv[f"w_s{l}"] for l in range(L)], axis=1)
    run("w_kv", [recv[f"w_kv{l}"] for l in range(L)], w_kv, m_w_kv, v_w_kv)
    sh = (L * 3, D // NDEV, D)
    run("w_branch", [recv[f"w_branch{l}_{n}"] for l in range(L) for n in range(3)],
        w_branch.reshape(sh), m_w_branch.reshape(sh), v_w_branch.reshape(sh))
    run("w_out", [recv[f"w_out{l}"] for l in range(L)], w_out, m_w_out, v_w_out)
    run("conv_w", [recv[f"conv_w{l}"] for l in range(L)], conv_w, m_conv_w, v_conv_w)
    run("small", [parts_small], w_small, m_small, v_small)
    ws_shape = (1, L * N_GROUPS * CHUNK, CHUNK)
    run("w_s", [parts_ws], w_s.reshape(ws_shape), m_w_s.reshape(ws_shape), v_w_s.reshape(ws_shape))

    def leaf(name, k):
        if name in ("w_in", "w_kv", "w_out", "conv_w"):
            return outs[name][k]
        if name == "w_branch":
            return outs[name][k].reshape(w_branch.shape)
        if name == "w_s":
            return outs["w_s"][k].reshape(L, N_GROUPS, CHUNK, CHUNK)
        sm = outs["small"][k][0]
        if name == "final_norm_g":
            return sm[8 * L]
        j = names.index(name)
        rows = jnp.stack([sm[8 * l + j] for l in range(L)])
        return rows.reshape(L, N_GROUPS, CHUNK) if name == "b_s" else rows

    order = ["norm_g", "mem_norm_g", "w_in", "gmlp_ln_g", "gmlp_ln_b", "w_s", "b_s", "conv_w", "conv_b",
             "conv_ln_g", "conv_ln_b", "w_kv", "w_branch", "w_out", "final_norm_g"]
    loss = lax.psum(loss_part[0, 0], ("x", "y", "c"))
    return (loss, grad_x, *[leaf(nm, k) for k in range(4) for nm in order])
```

```python
import math

import jax
import jax.numpy as jnp
from jax import lax
from jax.experimental import pallas as pl
from jax.experimental.pallas import tpu as pltpu

F32 = jnp.float32
MXU_DTYPE = jnp.bfloat16
ACT_DTYPE = jnp.bfloat16
GRAD_DTYPE = jnp.bfloat16

D = 1024
N_SEG = 11
N_IN = N_SEG * D
NDEV = 8
CHUNK = 128
N_GROUPS = 8
CONV_K = 31
HALO = 32
LANES = 128
STRIP = 32
HEADS = 4
HEAD_DIM = D // HEADS
RMS_EPS = 1e-6
LN_EPS = 1e-5
ADAM_LR, ADAM_B1, ADAM_B2, ADAM_EPS, ADAM_WD, ADAM_STEP = 0.001, 0.9, 0.999, 1e-08, 0.01, 10
SEG_AU, SEG_AV, SEG_AG, SEG_BA, SEG_BB, SEG_BG, SEG_CQ, SEG_CG, SEG_M = 0, 1, 2, 3, 4, 5, 6, 7, 8

VMEM_LIMIT = 60 * 1024 * 1024
MESH = pl.DeviceIdType.MESH
NT_DIMS = (((1,), (1,)), ((), ()))
TN_DIMS = (((0,), (0,)), ((), ()))


def _params(*sem):
    return pltpu.CompilerParams(dimension_semantics=sem, vmem_limit_bytes=VMEM_LIMIT)


def _tile(n, want):
    t = min(n, want)
    assert n % t == 0, (n, want)
    return t


def _mx(v):
    return v.astype(MXU_DTYPE)


def _gelu(x):
    t = jnp.tanh(0.7978845608028654 * (x + 0.044715 * x * x * x))
    return 0.5 * x * (1.0 + t), t


def _gelu_grad(x, t):
    return 0.5 * (1.0 + t) + 0.5 * x * (1.0 - t * t) * 0.7978845608028654 * (1.0 + 3.0 * 0.044715 * x * x)


def _silu_grad(x, s):
    return s * (1.0 + x * (1.0 - s))


def _ln_stats(v):
    mu = jnp.mean(v, axis=-1, keepdims=True)
    vc = v - mu
    rstd = lax.rsqrt(jnp.mean(vc * vc, axis=-1, keepdims=True) + LN_EPS)
    return vc * rstd, rstd


def _ln_grad(dy, g, vhat, rstd):
    dvh = dy * g
    return rstd * (dvh - jnp.mean(dvh, axis=-1, keepdims=True)
                   - vhat * jnp.mean(dvh * vhat, axis=-1, keepdims=True))


def _rowsum(v):
    return jnp.sum(v, axis=0, keepdims=True)


def _coords():
    return lax.axis_index("x"), lax.axis_index("y"), lax.axis_index("c")


def _flip(pos, d):
    x, y, c = pos
    return (1 - x if d & 4 else x, 1 - y if d & 2 else y, 1 - c if d & 1 else c)


def _slot(pos):
    return 4 * pos[0] + 2 * pos[1] + pos[2]


CHIP_FLIPS = (4, 2, 6)


class _Gather:
    def __init__(self, arrays, mid_frac=0.8):
        self.arrays = list(arrays)
        self.n = n = len(arrays)
        self.mid_frac = mid_frac
        self.out_shape = [jax.ShapeDtypeStruct((NDEV,) + a.shape, a.dtype) for a in arrays]
        self.scratch = [pltpu.SemaphoreType.DMA((n, 7)), pltpu.SemaphoreType.DMA((n, 7)),
                        pltpu.SemaphoreType.DMA((n,))]

    def _copy(self, refs, i, k, block, to, own=False):
        ins, outs, (send, recv, _) = refs
        slot = outs[i].at[_slot(block)]
        return pltpu.make_async_remote_copy(
            src_ref=ins[i] if own else slot, dst_ref=slot, send_sem=send.at[i, k], recv_sem=recv.at[i, k],
            device_id=to, device_id_type=MESH)

    def _local(self, refs, i):
        ins, outs, (_, _, loc) = refs
        return pltpu.make_async_copy(ins[i], outs[i].at[_slot(_coords())], loc.at[i])

    def _first(self, refs, i, k):
        me = _coords()
        return self._copy(refs, i, k, me, _flip(me, ((1,) + CHIP_FLIPS)[k]), own=True)

    def _passed(self, refs, i, j):
        me = _coords()
        return self._copy(refs, i, 4 + j, _flip(me, CHIP_FLIPS[j]), _flip(me, 1))

    def start(self, refs):
        for i in range(self.n):
            self._local(refs, i).start()
        for k in range(4):
            for i in range(self.n):
                self._first(refs, i, k).start()

    def forward(self, refs):
        me = _coords()
        for j, d in enumerate(CHIP_FLIPS):
            for i in range(self.n):
                self._copy(refs, i, 1 + j, _flip(me, d), me).wait_recv()
                self._passed(refs, i, j).start()

    def finish(self, refs):
        me = _coords()
        sib = _flip(me, 1)
        for i in range(self.n):
            self._copy(refs, i, 0, sib, me).wait_recv()
        for j, d in enumerate(CHIP_FLIPS):
            for i in range(self.n):
                self._copy(refs, i, 4 + j, _flip(sib, d), me).wait_recv()
        for i in range(self.n):
            for k in range(4):
                self._first(refs, i, k).wait_send()
            for j in range(3):
                self._passed(refs, i, j).wait_send()
            self._local(refs, i).wait()


class _Scatter:
    def __init__(self, arrays):
        self.arrays = list(arrays)
        self.n = n = len(arrays)
        self.mid_frac = None
        self.out_shape = [jax.ShapeDtypeStruct(a.shape, a.dtype) for a in arrays]
        self.scratch = [pltpu.SemaphoreType.DMA((n, 7)), pltpu.SemaphoreType.DMA((n, 7)),
                        pltpu.SemaphoreType.DMA((n,))]

    def _copy(self, refs, i, d, landing):
        ins, outs, (send, recv, _) = refs
        me = _coords()
        peer = _flip(me, d)
        return pltpu.make_async_remote_copy(
            src_ref=ins[i].at[_slot(peer)], dst_ref=outs[i].at[_slot(peer) if landing else _slot(me)],
            send_sem=send.at[i, d - 1], recv_sem=recv.at[i, d - 1], device_id=peer, device_id_type=MESH)

    def _local(self, refs, i):
        ins, outs, (_, _, loc) = refs
        me = _slot(_coords())
        return pltpu.make_async_copy(ins[i].at[me], outs[i].at[me], loc.at[i])

    def start(self, refs):
        for i in range(self.n):
            self._local(refs, i).start()
        for d in range(1, NDEV):
            for i in range(self.n):
                self._copy(refs, i, d, False).start()

    def forward(self, refs):
        pass

    def finish(self, refs):
        for d in range(1, NDEV):
            for i in range(self.n):
                self._copy(refs, i, d, True).wait_recv()
        for d in range(1, NDEV):
            for i in range(self.n):
                self._copy(refs, i, d, False).wait_send()
        for i in range(self.n):
            self._local(refs, i).wait()


class _Both:
    def __init__(self, a, b):
        self.parts = (a, b)
        self.arrays = a.arrays + b.arrays
        self.n = a.n + b.n
        self.mid_frac = a.mid_frac if a.mid_frac is not None else b.mid_frac
        self.out_shape = a.out_shape + b.out_shape
        self.scratch = a.scratch + b.scratch

    def _each(self, refs):
        ins, outs, sems = refs
        na, ns = self.parts[0].n, len(self.parts[0].scratch)
        return ((self.parts[0], (ins[:na], outs[:na], sems[:ns])), (self.parts[1], (ins[na:], outs[na:], sems[ns:])))

    def start(self, refs):
        for part, r in self._each(refs):
            part.start(r)

    def forward(self, refs):
        for part, r in self._each(refs):
            part.forward(r)

    def finish(self, refs):
        for part, r in self._each(refs):
            part.finish(r)


def _call(body, name, grid, in_specs, out_specs, out_shape, scratch, args, comm=None, start_frac=0.0):
    params = _params(*(["arbitrary"] * len(grid)))
    if comm is None:
        outs = pl.pallas_call(
            body, name=name, grid=grid, in_specs=in_specs, out_specs=out_specs, out_shape=out_shape,
            scratch_shapes=scratch, compiler_params=params)(*args)
        return list(outs), []
    n_in, n_out, n_scr, k = len(in_specs), len(out_specs), len(scratch), comm.n
    nsteps = math.prod(grid) if grid else 1
    first = int(nsteps * start_frac)
    mid = max(first, min(nsteps - 1, int(nsteps * comm.mid_frac))) if comm.mid_frac is not None else None

    def hosted(*refs):
        ins, refs = refs[:n_in], refs[n_in:]
        cins, refs = refs[:k], refs[k:]
        outs, refs = refs[:n_out], refs[n_out:]
        couts, refs = refs[:k], refs[k:]
        scr, sems = refs[:n_scr], refs[n_scr:]
        crefs = (cins, couts, sems)
        if nsteps == 1:
            comm.start(crefs)
            body(*ins, *outs, *scr)
            comm.forward(crefs)
            comm.finish(crefs)
            return
        step = pl.program_id(0)
        for a in range(1, len(grid)):
            step = step * grid[a] + pl.program_id(a)
        pl.when(step == first)(lambda: comm.start(crefs))
        if mid is not None:
            pl.when(step == mid)(lambda: comm.forward(crefs))
        body(*ins, *outs, *scr)
        pl.when(step == nsteps - 1)(lambda: comm.finish(crefs))

    any_spec = pl.BlockSpec(memory_space=pl.ANY)
    outs = pl.pallas_call(
        hosted, name=name, grid=grid,
        in_specs=list(in_specs) + [any_spec] * k, out_specs=list(out_specs) + [any_spec] * k,
        out_shape=list(out_shape) + comm.out_shape, scratch_shapes=list(scratch) + comm.scratch,
        compiler_params=params)(*args, *comm.arrays)
    return list(outs[:n_out]), list(outs[n_out:])


def _rms_matmul(x, g, w, name, comm=None):
    T = x.shape[0]
    nb = w.shape[2]
    tT = _tile(T, 1024)
    per = 2

    def body(x_ref, g_ref, w_ref, z_ref, h_ref):
        @pl.when(pl.program_id(1) == 0)
        def _():
            xf = x_ref[...]
            r = lax.rsqrt(jnp.mean(xf * xf, axis=-1, keepdims=True) + RMS_EPS)
            h_ref[...] = (xf * r * g_ref[...]).astype(h_ref.dtype)

        for j in range(per):
            z_ref[:, j * nb:(j + 1) * nb] = jnp.dot(h_ref[...], w_ref[j], preferred_element_type=F32
                                                    ).astype(z_ref.dtype)

    return _call(
        body, name, (T // tT, NDEV // per),
        [pl.BlockSpec((tT, D), lambda t, n: (t, 0)),
         pl.BlockSpec((1, D), lambda t, n: (0, 0)),
         pl.BlockSpec((per, D, nb), lambda t, n: (n, 0, 0))],
        [pl.BlockSpec((tT, per * nb), lambda t, n: (t, n)),
         pl.BlockSpec((tT, D), lambda t, n: (t, 0))],
        [jax.ShapeDtypeStruct((T, NDEV * nb), ACT_DTYPE), jax.ShapeDtypeStruct((T, D), MXU_DTYPE)],
        [], (x, g.reshape(1, D), w), comm)


ARRIVAL = (0, 1) + CHIP_FLIPS + tuple(d ^ 1 for d in CHIP_FLIPS)


def _rms_matmul_gathering(x, g, w_shard, name, comm=None):
    T = x.shape[0]
    nb = w_shard.shape[1]
    tT = _tile(T, 1024)
    nT = T // tT

    def body(x_ref, g_ref, wsh_ref, z_ref, h_ref, wfull_ref, h_all, wbuf, zbuf, fetch_sems, z_sems, send_sems,
             recv_sems, own_sem):
        n, t = pl.program_id(0), pl.program_id(1)
        step = n * nT + t
        me = _coords()
        sib = _flip(me, 1)

        def remote(k, block, to, own=False):
            slot = wfull_ref.at[_slot(block)]
            return pltpu.make_async_remote_copy(
                src_ref=wsh_ref if own else slot, dst_ref=slot, send_sem=send_sems.at[k], recv_sem=recv_sems.at[k],
                device_id=to, device_id_type=MESH)

        def first(k):
            return remote(k, me, _flip(me, ARRIVAL[1 + k]), own=True)

        def passed(j):
            return remote(4 + j, _flip(me, CHIP_FLIPS[j]), sib)

        own_copy = pltpu.make_async_copy(wsh_ref, wfull_ref.at[_slot(me)], own_sem)

        def fetch(src, nn):
            return pltpu.make_async_copy(src, wbuf.at[nn % 2], fetch_sems.at[nn % 2])

        @pl.when(step == 0)
        def _():
            own_copy.start()
            for k in range(3):
                first(k).start()
            fetch(wsh_ref, 0).start()

        @pl.when(t == 0)
        def _():
            fetch(wfull_ref.at[0], n).wait()

        @pl.when(n == 0)
        def _():
            xf = x_ref[...]
            r = lax.rsqrt(jnp.mean(xf * xf, axis=-1, keepdims=True) + RMS_EPS)
            h = (xf * r * g_ref[...]).astype(h_ref.dtype)
            h_ref[...] = h
            h_all[t] = h

        def z_copy(s, col):
            return pltpu.make_async_copy(
                zbuf.at[s % 2], z_ref.at[pl.ds(pl.multiple_of(t * tT, tT), tT), pl.ds(col * nb, nb)],
                z_sems.at[s % 2])

        @pl.when(step >= 2)
        def _():
            z_copy(step, 0).wait()

        d = sum(jnp.where(n == nn, ARRIVAL[nn], 0) for nn in range(NDEV))
        col = _slot((me[0] ^ ((d >> 2) & 1), me[1] ^ ((d >> 1) & 1), me[2] ^ (d & 1)))
        zbuf[step % 2] = jnp.dot(h_all[t], wbuf[n % 2], preferred_element_type=F32).astype(zbuf.dtype)
        z_copy(step, col).start()

        for nn in range(1, NDEV):
            @pl.when((n == nn - 1) & (t == nT - 1))
            def _(nn=nn):
                block = _flip(me, ARRIVAL[nn])
                if nn == 1:
                    remote(0, sib, me).wait_recv()
                elif nn < 5:
                    if nn == 2:
                        first(3).start()
                    remote(nn - 1, block, me).wait_recv()
                    passed(nn - 2).start()
                else:
                    remote(nn - 1, block, me).wait_recv()
                fetch(wfull_ref.at[_slot(block)], nn).start()

        @pl.when(step == NDEV * nT - 1)
        def _():
            z_copy(step - 1, 0).wait()
            z_copy(step, 0).wait()
            for k in range(4):
                first(k).wait_send()
            for j in range(3):
                passed(j).wait_send()
            own_copy.wait()

    keep = lambda n, t: (jnp.where(n == 0, t, nT - 1), 0)
    any_spec = pl.BlockSpec(memory_space=pl.ANY)
    return _call(
        body, name, (NDEV, nT),
        [pl.BlockSpec((tT, D), keep), pl.BlockSpec((1, D), lambda n, t: (0, 0)), any_spec],
        [any_spec, pl.BlockSpec((tT, D), keep), any_spec],
        [jax.ShapeDtypeStruct((T, NDEV * nb), ACT_DTYPE), jax.ShapeDtypeStruct((T, D), MXU_DTYPE),
         jax.ShapeDtypeStruct((NDEV,) + w_shard.shape, w_shard.dtype)],
        [pltpu.VMEM((nT, tT, D), MXU_DTYPE), pltpu.VMEM((2, D, nb), w_shard.dtype),
         pltpu.VMEM((2, tT, nb), ACT_DTYPE), pltpu.SemaphoreType.DMA((2,)), pltpu.SemaphoreType.DMA((2,)),
         pltpu.SemaphoreType.DMA((7,)), pltpu.SemaphoreType.DMA((7,)), pltpu.SemaphoreType.DMA],
        (x, g.reshape(1, D), w_shard), comm, start_frac=0.5)


def _rms_matmul_bwd(dz, w, x, g, dxo, name, comm=None):
    T = x.shape[0]
    nb = w.shape[2]
    tT = _tile(T, 512)
    per = 4
    steps = NDEV // per

    def body(dz_ref, w_ref, x_ref, g_ref, dxo_ref, dx_ref, dg_ref, acc):
        t, n = pl.program_id(0), pl.program_id(1)

        @pl.when((n == 0) & (t == 0))
        def _():
            dg_ref[...] = jnp.zeros_like(dg_ref)

        part = None
        for j in range(per):
            d = lax.dot_general(_mx(dz_ref[:, j * nb:(j + 1) * nb]), w_ref[j], NT_DIMS, preferred_element_type=F32)
            part = d if part is None else part + d

        @pl.when(n == 0)
        def _():
            acc[...] = part

        @pl.when(n > 0)
        def _():
            acc[...] += part

        @pl.when(n == steps - 1)
        def _():
            xf = x_ref[...]
            r = lax.rsqrt(jnp.mean(xf * xf, axis=-1, keepdims=True) + RMS_EPS)
            xh = xf * r
            dh = acc[...]
            dxh = dh * g_ref[...]
            dx_ref[...] = dxo_ref[...] + r * (dxh - xh * jnp.mean(dxh * xh, axis=-1, keepdims=True))
            dg_ref[...] += _rowsum(dh * xh)

    return _call(
        body, name, (T // tT, steps),
        [pl.BlockSpec((tT, per * nb), lambda t, n: (t, n)),
         pl.BlockSpec((per, D, nb), lambda t, n: (n, 0, 0)),
         pl.BlockSpec((tT, D), lambda t, n: (t, 0)),
         pl.BlockSpec((1, D), lambda t, n: (0, 0)),
         pl.BlockSpec((tT, D), lambda t, n: (t, 0))],
        [pl.BlockSpec((tT, D), lambda t, n: (t, 0)),
         pl.BlockSpec((1, D), lambda t, n: (0, 0))],
        [jax.ShapeDtypeStruct((T, D), F32), jax.ShapeDtypeStruct((1, D), F32)],
        [pltpu.VMEM((tT, D), F32)], (dz, w, x, g.reshape(1, D), dxo), comm)


def _atb(a, b, nblk, name, comm=None):
    G, T, M = a.shape
    N = b.shape[2]
    nb = N // nblk
    tk = _tile(T, 2048)
    nk = T // tk

    def body(a_ref, b_ref, o_ref, acc):
        k = pl.program_id(1)

        @pl.when(k == 0)
        def _():
            acc[...] = jnp.zeros_like(acc)

        acc[...] += lax.dot_general(_mx(a_ref[...]), _mx(b_ref[...]), TN_DIMS, preferred_element_type=F32)

        @pl.when(k == nk - 1)
        def _():
            o_ref[...] = acc[...].astype(o_ref.dtype)

    outs, couts = _call(
        body, name, (G * nblk, nk),
        [pl.BlockSpec((None, tk, M), lambda n, k: (n // nblk, k, 0)),
         pl.BlockSpec((None, tk, nb), lambda n, k: (n // nblk, k, n % nblk))],
        [pl.BlockSpec((None, M, nb), lambda n, k: (n, 0, 0))],
        [jax.ShapeDtypeStruct((G * nblk, M, nb), GRAD_DTYPE)],
        [pltpu.VMEM((M, nb), F32)], (a, b), comm)
    return outs[0], couts


def _loss_rows(xf, t_ref, g_ref, loss_ref, dx_ref, dg_ref):
    @pl.when(pl.program_id(0) == 0)
    def _():
        loss_ref[...] = jnp.zeros_like(loss_ref)
        dg_ref[...] = jnp.zeros_like(dg_ref)

    r = lax.rsqrt(jnp.mean(xf * xf, axis=-1, keepdims=True) + RMS_EPS)
    xh = xf * r
    err = xh * g_ref[...] - t_ref[...]
    loss_ref[...] += 0.5 * jnp.sum(jnp.mean(err * err, axis=-1, keepdims=True), axis=0, keepdims=True)
    dy = err * (1.0 / D)
    dxh = dy * g_ref[...]
    dx_ref[...] = r * (dxh - xh * jnp.mean(dxh * xh, axis=-1, keepdims=True))
    dg_ref[...] += _rowsum(dy * xh)


def _spatial_gate(wm_ref, bst_ref, vb_ref, sv_ref, n_chunks):
    for c in range(n_chunks):
        rows = slice(c * CHUNK, (c + 1) * CHUNK)
        for g in range(N_GROUPS):
            cols = slice(g * CHUNK, (g + 1) * CHUNK)
            sv_ref[rows, cols] = (jnp.dot(wm_ref[g], vb_ref[rows, cols], preferred_element_type=F32)
                                  + bst_ref[:, g:g + 1])


def _lane_loop(fn):
    def step(i, carry):
        fn(pl.ds(pl.multiple_of(i * LANES, LANES), LANES))
        return carry

    lax.fori_loop(0, D // LANES, step, 0)


def _shifted_copies(buf, sh, n):
    for s in range(1, 8):
        sh[s - 1, 0:n, :] = buf[s:s + n, :]


def _window(buf, sh, base, off, lanes):
    a, s = divmod(off, 8)
    src = buf if s == 0 else sh.at[s - 1]
    return src[base + 8 * a:base + 8 * a + 8, lanes]


def _softmax_rows(s):
    e = jnp.exp(s - jnp.max(s, axis=-1, keepdims=True))
    return e / jnp.sum(e, axis=-1, keepdims=True)


def _branch_fwd(z, kv, wm, bst, ln_a, cw, cvec, name, comm=None):
    T = z.shape[0]
    tT = _tile(T, 256)
    n_chunks = tT // CHUNK

    def body(z_ref, kv_ref, wm_ref, bst_ref, lna_ref, cw_ref, cvec_ref, br_ref, c_ref, gbuf, gsh, vb, ua):
        @pl.when(pl.program_id(0) == 0)
        def _():
            gbuf[0:HALO, :] = jnp.zeros((HALO, D), F32)

        def seg(s):
            return z_ref[:, s * D:(s + 1) * D].astype(F32)

        u, _ = _gelu(seg(SEG_AU))
        zg = seg(SEG_AG)
        ua[...] = u * (zg * jax.nn.sigmoid(zg))
        gv, _ = _gelu(seg(SEG_AV))
        vhat, _ = _ln_stats(gv)
        vb[...] = _mx(vhat * lna_ref[0:1, :] + lna_ref[1:2, :])
        for c in range(n_chunks):
            rows = slice(c * CHUNK, (c + 1) * CHUNK)
            for g in range(N_GROUPS):
                cols = slice(g * CHUNK, (g + 1) * CHUNK)
                sv = jnp.dot(wm_ref[g], vb[rows, cols], preferred_element_type=F32) + bst_ref[:, g:g + 1]
                br_ref[0, rows, cols] = (sv * ua[rows, cols]).astype(br_ref.dtype)

        gbuf[HALO:HALO + tT, :] = seg(SEG_BA) * jax.nn.sigmoid(seg(SEG_BB))
        _shifted_copies(gbuf, gsh, tT + HALO - 8)
        def conv_lanes(lanes):
            taps = [jnp.broadcast_to(cw_ref[k:k + 1, lanes], (8, LANES)) for k in range(CONV_K)]
            bias = jnp.broadcast_to(cvec_ref[0:1, lanes], (8, LANES))
            for base in range(0, tT, 8):
                acc = [bias, None, None, None]
                for k in range(CONV_K):
                    term = taps[k] * _window(gbuf, gsh, base, k + HALO - CONV_K + 1, lanes)
                    acc[k % 4] = term if acc[k % 4] is None else acc[k % 4] + term
                c_ref[base:base + 8, lanes] = (acc[0] + acc[1]) + (acc[2] + acc[3])

        _lane_loop(conv_lanes)
        gbuf[0:HALO, :] = gbuf[tT:tT + HALO, :]
        chat, _ = _ln_stats(c_ref[...])
        cl = chat * cvec_ref[1:2, :] + cvec_ref[2:3, :]
        zg = seg(SEG_BG)
        br_ref[1] = (cl * jax.nn.sigmoid(cl) * (zg * jax.nn.sigmoid(zg))).astype(br_ref.dtype)

        for h in range(HEADS):
            cols = slice(h * HEAD_DIM, (h + 1) * HEAD_DIM)
            q = _mx(z_ref[:, SEG_CQ * D + h * HEAD_DIM:SEG_CQ * D + (h + 1) * HEAD_DIM])
            s = lax.dot_general(q, kv_ref[:, cols], NT_DIMS, preferred_element_type=F32)
            p = _softmax_rows(s * (1.0 / math.sqrt(HEAD_DIM)))
            att = jnp.dot(_mx(p), kv_ref[:, D + h * HEAD_DIM:D + (h + 1) * HEAD_DIM], preferred_element_type=F32)
            zg = z_ref[:, SEG_CG * D + h * HEAD_DIM:SEG_CG * D + (h + 1) * HEAD_DIM].astype(F32)
            br_ref[2, :, cols] = (att * (zg * jax.nn.sigmoid(zg))).astype(br_ref.dtype)

    full = lambda shape: pl.BlockSpec(shape, lambda t: (0,) * len(shape))
    return _call(
        body, name, (T // tT,),
        [pl.BlockSpec((tT, SEG_M * D), lambda t: (t, 0)),
         full(kv.shape), full(wm.shape), full(bst.shape), full(ln_a.shape), full(cw.shape), full(cvec.shape)],
        [pl.BlockSpec((3, tT, D), lambda t: (0, t, 0)), pl.BlockSpec((tT, D), lambda t: (t, 0))],
        [jax.ShapeDtypeStruct((3, T, D), MXU_DTYPE), jax.ShapeDtypeStruct((T, D), F32)],
        [pltpu.VMEM((tT + HALO, D), F32), pltpu.VMEM((7, tT + HALO - 8, D), F32),
         pltpu.VMEM((tT, D), MXU_DTYPE), pltpu.VMEM((tT, D), F32)],
        (z, kv, wm, bst, ln_a, cw, cvec), comm)


def _branch_bwd(z, dbr, c, dzm, kv, wm, wmt, bst, ln_a, cw, cvec, name, comm=None):
    T = z.shape[0]
    M = kv.shape[0]
    tT = _tile(T, 128)
    nT = T // tT
    n_chunks = tT // CHUNK

    def body(z_ref, dbr_ref, c_ref, dzm_ref, kv_ref, wm_ref, wmt_ref, bst_ref, lna_ref,
             cw_ref, cvec_ref, dz_ref, vecg_ref, dbst_ref, dws_ref, dcw_ref, dkv_ref,
             gbuf, dcbuf, vb, dsvb, sv, dvbuf, dcsh, dglu, vh, gq, dcw8, dcw_step):
        i = pl.program_id(0)

        @pl.when(i == 0)
        def _():
            vecg_ref[...] = jnp.zeros_like(vecg_ref)
            dbst_ref[...] = jnp.zeros_like(dbst_ref)
            dws_ref[...] = jnp.zeros_like(dws_ref)
            dcw8[...] = jnp.zeros_like(dcw8)
            dkv_ref[...] = jnp.zeros_like(dkv_ref)
            dcbuf[tT:tT + HALO, :] = jnp.zeros((HALO, D), F32)

        strips = [slice(r0, r0 + STRIP) for r0 in range(0, tT, STRIP)]

        def seg(r, s):
            return z_ref[r, s * D:(s + 1) * D].astype(F32)

        def put(r, s, val):
            dz_ref[r, s * D:(s + 1) * D] = val.astype(dz_ref.dtype)

        for r in strips:
            zv = seg(r, SEG_AV)
            gv, tv = _gelu(zv)
            vhat, rstd = _ln_stats(gv)
            vb[r, :] = _mx(vhat * lna_ref[0:1, :] + lna_ref[1:2, :])
            vh[r, :] = vhat
            gq[r, :] = rstd * _gelu_grad(zv, tv)
        _spatial_gate(wm_ref, bst_ref, vb, sv, n_chunks)
        for r in strips:
            zu, zg = seg(r, SEG_AU), seg(r, SEG_AG)
            u, tu = _gelu(zu)
            sg = jax.nn.sigmoid(zg)
            d_a = dbr_ref[0, r, :].astype(F32)
            put(r, SEG_AU, d_a * sv[r, :] * (zg * sg) * _gelu_grad(zu, tu))
            put(r, SEG_AG, d_a * u * sv[r, :] * _silu_grad(zg, sg))
            dsv = d_a * u * (zg * sg)
            dsvb[r, :] = _mx(dsv)
            in_chunk = slice(r.start % CHUNK, r.start % CHUNK + STRIP)
            for g in range(N_GROUPS):
                dbst_ref[in_chunk, g:g + 1] += jnp.sum(dsv[:, g * CHUNK:(g + 1) * CHUNK], axis=-1, keepdims=True)
        tril = (lax.broadcasted_iota(jnp.int32, (CHUNK, CHUNK), 0)
                >= lax.broadcasted_iota(jnp.int32, (CHUNK, CHUNK), 1))
        for g in range(N_GROUPS):
            cols = slice(g * CHUNK, (g + 1) * CHUNK)
            for cc in range(n_chunks):
                rows = slice(cc * CHUNK, (cc + 1) * CHUNK)
                dws = lax.dot_general(dsvb[rows, cols], vb[rows, cols], NT_DIMS, preferred_element_type=F32)
                dws_ref[g] += jnp.where(tril, dws, 0.0)
                dvbuf[rows, cols] = jnp.dot(wmt_ref[g], dsvb[rows, cols], preferred_element_type=F32)
        for r in strips:
            dv, vhat = dvbuf[r, :], vh[r, :]
            vecg_ref[0:1, :] += _rowsum(dv * vhat)
            vecg_ref[1:2, :] += _rowsum(dv)
            dvh = dv * lna_ref[0:1, :]
            put(r, SEG_AV, (dvh - jnp.mean(dvh, axis=-1, keepdims=True)
                            - vhat * jnp.mean(dvh * vhat, axis=-1, keepdims=True)) * gq[r, :])

        sgb_buf = sv
        for r in strips:
            za, zg = seg(r, SEG_BA), seg(r, SEG_BG)
            sgb = jax.nn.sigmoid(seg(r, SEG_BB))
            sgb_buf[r, :] = sgb
            gbuf[r, :] = za * sgb
            chat, crstd = _ln_stats(c_ref[r, :])
            cl = chat * cvec_ref[1:2, :] + cvec_ref[2:3, :]
            scl = jax.nn.sigmoid(cl)
            sg = jax.nn.sigmoid(zg)
            d_b = dbr_ref[1, r, :].astype(F32)
            put(r, SEG_BG, d_b * (cl * scl) * _silu_grad(zg, sg))
            dcl = d_b * (zg * sg) * _silu_grad(cl, scl)
            vecg_ref[3:4, :] += _rowsum(dcl * chat)
            vecg_ref[4:5, :] += _rowsum(dcl)
            dc = _ln_grad(dcl, cvec_ref[1:2, :], chat, crstd)
            vecg_ref[2:3, :] += _rowsum(dc)
            dcbuf[r, :] = dc
        _shifted_copies(dcbuf, dcsh, tT + HALO - 8)

        def conv_grads(lanes):
            taps = [jnp.broadcast_to(cw_ref[k:k + 1, lanes], (8, LANES)) for k in range(CONV_K)]
            wsum = [None] * CONV_K
            for base in range(0, tT, 8):
                glu = gbuf[base:base + 8, lanes]
                acc = [None] * 4
                for k in range(CONV_K):
                    win = _window(dcbuf, dcsh, base, CONV_K - 1 - k, lanes)
                    term = taps[k] * win
                    acc[k % 4] = term if acc[k % 4] is None else acc[k % 4] + term
                    term = glu * win
                    wsum[k] = term if wsum[k] is None else wsum[k] + term
                dglu[base:base + 8, lanes] = (acc[0] + acc[1]) + (acc[2] + acc[3])
            for k in range(CONV_K):
                dcw_step[8 * k:8 * k + 8, lanes] = wsum[k]

        _lane_loop(conv_grads)
        dcw8[...] += dcw_step[...]

        @pl.when(i == nT - 1)
        def _():
            for k in range(CONV_K):
                dcw_ref[k:k + 1, :] = _rowsum(dcw8[8 * k:8 * k + 8, :])
            dcw_ref[CONV_K:HALO, :] = jnp.zeros((HALO - CONV_K, D), F32)

        dcbuf[tT:tT + HALO, :] = dcbuf[0:HALO, :]
        for r in strips:
            dg, sgb = dglu[r, :], sgb_buf[r, :]
            put(r, SEG_BA, dg * sgb)
            put(r, SEG_BB, dg * seg(r, SEG_BA) * sgb * (1.0 - sgb))

        scale = 1.0 / math.sqrt(HEAD_DIM)
        for h in range(HEADS):
            cols = slice(h * HEAD_DIM, (h + 1) * HEAD_DIM)
            qcols = slice(SEG_CQ * D + h * HEAD_DIM, SEG_CQ * D + (h + 1) * HEAD_DIM)
            gcols = slice(SEG_CG * D + h * HEAD_DIM, SEG_CG * D + (h + 1) * HEAD_DIM)
            vcols = slice(D + h * HEAD_DIM, D + (h + 1) * HEAD_DIM)
            q = _mx(z_ref[:, qcols])
            kh, vh = kv_ref[:, cols], kv_ref[:, vcols]
            p = _softmax_rows(lax.dot_general(q, kh, NT_DIMS, preferred_element_type=F32) * scale)
            pb = _mx(p)
            att = jnp.dot(pb, vh, preferred_element_type=F32)
            zg = z_ref[:, gcols].astype(F32)
            sg = jax.nn.sigmoid(zg)
            d_c = dbr_ref[2, :, cols].astype(F32)
            dz_ref[:, gcols] = (d_c * att * _silu_grad(zg, sg)).astype(dz_ref.dtype)
            datt = _mx(d_c * (zg * sg))
            dp = lax.dot_general(datt, vh, NT_DIMS, preferred_element_type=F32)
            dkv_ref[:, vcols] += lax.dot_general(pb, datt, TN_DIMS, preferred_element_type=F32)
            ds = _mx(p * (dp - jnp.sum(dp * p, axis=-1, keepdims=True)) * scale)
            dz_ref[:, qcols] = jnp.dot(ds, kh, preferred_element_type=F32).astype(dz_ref.dtype)
            dkv_ref[:, cols] += lax.dot_general(ds, q, TN_DIMS, preferred_element_type=F32)

        dz_ref[:, SEG_M * D:] = dzm_ref[...].astype(dz_ref.dtype)

    rev = lambda i: nT - 1 - i
    full = lambda shape: pl.BlockSpec(shape, lambda i: (0,) * len(shape))
    return _call(
        body, name, (nT,),
        [pl.BlockSpec((tT, SEG_M * D), lambda i: (rev(i), 0)),
         pl.BlockSpec((3, tT, D), lambda i: (0, rev(i), 0)),
         pl.BlockSpec((tT, D), lambda i: (rev(i), 0)),
         pl.BlockSpec((tT, 3 * D), lambda i: (rev(i), 0)),
         full(kv.shape), full(wm.shape), full(wmt.shape), full(bst.shape), full(ln_a.shape),
         full(cw.shape), full(cvec.shape)],
        [pl.BlockSpec((tT, N_IN), lambda i: (rev(i), 0)),
         full((8, D)), full((CHUNK, N_GROUPS)), full((N_GROUPS, CHUNK, CHUNK)), full((HALO, D)),
         full((M, 2 * D))],
        [jax.ShapeDtypeStruct((T, N_IN), MXU_DTYPE), jax.ShapeDtypeStruct((8, D), F32),
         jax.ShapeDtypeStruct((CHUNK, N_GROUPS), F32),
         jax.ShapeDtypeStruct((N_GROUPS, CHUNK, CHUNK), F32),
         jax.ShapeDtypeStruct((HALO, D), F32), jax.ShapeDtypeStruct((M, 2 * D), F32)],
        [pltpu.VMEM((tT, D), F32), pltpu.VMEM((tT + HALO, D), F32),
         pltpu.VMEM((tT, D), MXU_DTYPE), pltpu.VMEM((tT, D), MXU_DTYPE),
         pltpu.VMEM((tT, D), F32), pltpu.VMEM((tT, D), F32),
         pltpu.VMEM((7, tT + HALO - 8, D), F32),
         pltpu.VMEM((tT, D), F32), pltpu.VMEM((tT, D), F32), pltpu.VMEM((tT, D), F32),
         pltpu.VMEM((CONV_K * 8, D), F32), pltpu.VMEM((CONV_K * 8, D), F32)],
        (z, dbr, c, dzm, kv, wm, wmt, bst, ln_a, cw, cvec), comm)


def _merge_fwd(br, z, x, wb, wo, name, head=None):
    T = x.shape[0]
    tT = _tile(T, 512)

    def body(br_ref, z0, z1, z2, x_ref, wb_ref, wo_ref, *rest):
        mg_ref, pj_ref = rest[-2:]
        merged = jnp.zeros((tT, D), F32)
        for n, zm in enumerate((z0, z1, z2)):
            proj = jnp.dot(br_ref[n], wb_ref[:, n].reshape(D, D), preferred_element_type=F32)
            pj_ref[n] = proj.astype(pj_ref.dtype)
            merged = merged + jax.nn.sigmoid(zm[...].astype(F32)) * proj
        mg_ref[...] = merged.astype(mg_ref.dtype)
        xn = x_ref[...] + jnp.dot(_mx(merged), wo_ref[...].reshape(D, D), preferred_element_type=F32)
        if head is None:
            rest[0][...] = xn
        else:
            _loss_rows(xn, *rest[:5])

    row = pl.BlockSpec((tT, D), lambda t: (t, 0))
    vec = pl.BlockSpec((1, D), lambda t: (0, 0))
    zspec = lambda n: pl.BlockSpec((tT, D), lambda t: (t, SEG_M + n))
    in_specs = [pl.BlockSpec((3, tT, D), lambda t: (0, t, 0)), zspec(0), zspec(1), zspec(2), row,
                pl.BlockSpec(wb.shape, lambda t: (0, 0, 0, 0)), pl.BlockSpec(wo.shape, lambda t: (0, 0, 0))]
    out_specs = [row, pl.BlockSpec((3, tT, D), lambda t: (0, t, 0))]
    out_shape = [jax.ShapeDtypeStruct((T, D), MXU_DTYPE), jax.ShapeDtypeStruct((3, T, D), ACT_DTYPE)]
    args = (br, z, z, z, x, wb, wo)
    if head is None:
        out_specs = [row] + out_specs
        out_shape = [jax.ShapeDtypeStruct((T, D), F32)] + out_shape
    else:
        in_specs += [row, vec]
        args += (head[0], head[1].reshape(1, D))
        out_specs = [pl.BlockSpec((1, 1), lambda t: (0, 0)), row, vec] + out_specs
        out_shape = [jax.ShapeDtypeStruct((1, 1), F32), jax.ShapeDtypeStruct((T, D), F32),
                     jax.ShapeDtypeStruct((1, D), F32)] + out_shape
    return pl.pallas_call(body, name=name, grid=(T // tT,), in_specs=in_specs, out_specs=out_specs,
                          out_shape=out_shape, compiler_params=_params("arbitrary"))(*args)


def _merge_bwd(dxo, proj, z, wb, wo, name):
    T = dxo.shape[0]
    tT = _tile(T, 512)

    def body(dxo_ref, pj_ref, z0, z1, z2, wb_ref, wo_ref, dpj_ref, dbr_ref, dzm_ref):
        dmerged = lax.dot_general(_mx(dxo_ref[...]), wo_ref[...].reshape(D, D), NT_DIMS,
                                  preferred_element_type=F32)
        for n, zm in enumerate((z0, z1, z2)):
            gate = jax.nn.sigmoid(zm[...].astype(F32))
            dproj = _mx(gate * dmerged)
            dpj_ref[n] = dproj
            dzm_ref[:, n * D:(n + 1) * D] = (pj_ref[n].astype(F32) * dmerged * gate * (1.0 - gate)
                                             ).astype(dzm_ref.dtype)
            dbr_ref[n] = lax.dot_general(dproj, wb_ref[:, n].reshape(D, D), NT_DIMS,
                                         preferred_element_type=F32).astype(dbr_ref.dtype)

    zspec = lambda n: pl.BlockSpec((tT, D), lambda t: (t, SEG_M + n))
    return pl.pallas_call(
        body, name=name, grid=(T // tT,),
        in_specs=[pl.BlockSpec((tT, D), lambda t: (t, 0)),
                  pl.BlockSpec((3, tT, D), lambda t: (0, t, 0)), zspec(0), zspec(1), zspec(2),
                  pl.BlockSpec(wb.shape, lambda t: (0, 0, 0, 0)),
                  pl.BlockSpec(wo.shape, lambda t: (0, 0, 0))],
        out_specs=[pl.BlockSpec((3, tT, D), lambda t: (0, t, 0)),
                   pl.BlockSpec((3, tT, D), lambda t: (0, t, 0)),
                   pl.BlockSpec((tT, 3 * D), lambda t: (t, 0))],
        out_shape=[jax.ShapeDtypeStruct((3, T, D), MXU_DTYPE), jax.ShapeDtypeStruct((3, T, D), ACT_DTYPE),
                   jax.ShapeDtypeStruct((T, 3 * D), MXU_DTYPE)],
        compiler_params=_params("parallel"),
    )(dxo, proj, z, z, z, wb, wo)


def _adamw(parts, w, m, v, name, comm=None):
    G, R, C = w.shape
    tr = 128 if R % 128 == 0 else R
    nr = R // tr
    c1 = 1.0 / (1.0 - ADAM_B1 ** ADAM_STEP)
    c2 = 1.0 / (1.0 - ADAM_B2 ** ADAM_STEP)

    def body(*refs):
        p_refs, (w_ref, m_ref, v_ref, g_out, d_out, m_out, v_out) = refs[:G], refs[G:]
        for i in range(G):
            @pl.when(pl.program_id(0) == i)
            def _(p_ref=p_refs[i]):
                g = p_ref[0].astype(F32)
                for p in range(1, NDEV):
                    g = g + p_ref[p].astype(F32)
                mn = ADAM_B1 * m_ref[...] + (1.0 - ADAM_B1) * g
                vn = ADAM_B2 * v_ref[...] + (1.0 - ADAM_B2) * (g * g)
                g_out[...] = g
                m_out[...] = mn
                v_out[...] = vn
                d_out[...] = -ADAM_LR * ((mn * c1) / (jnp.sqrt(vn * c2) + ADAM_EPS) + ADAM_WD * w_ref[...])

    def parts_spec(i):
        return pl.BlockSpec((NDEV, tr, C), lambda l, r: (0, jnp.where(l == i, r, jnp.where(l > i, nr - 1, 0)), 0))

    spec = pl.BlockSpec((None, tr, C), lambda l, r: (l, r, 0))
    return _call(
        body, name, (G, nr), [parts_spec(i) for i in range(G)] + [spec] * 3,
        [spec] * 4, [jax.ShapeDtypeStruct((G, R, C), F32)] * 4, [], (*parts, w, m, v), comm)


def kernel(x, mem, norm_g, mem_norm_g, w_in, gmlp_ln_g, gmlp_ln_b, w_s, b_s, conv_w, conv_b, conv_ln_g, conv_ln_b, w_kv, w_branch, w_out, final_norm_g, loss_target, m_norm_g, m_mem_norm_g, m_w_in, m_gmlp_ln_g, m_gmlp_ln_b, m_w_s, m_b_s, m_conv_w, m_conv_b, m_conv_ln_g, m_conv_ln_b, m_w_kv, m_w_branch, m_w_out, m_final_norm_g, v_norm_g, v_mem_norm_g, v_w_in, v_gmlp_ln_g, v_gmlp_ln_b, v_w_s, v_b_s, v_conv_w, v_conv_b, v_conv_ln_g, v_conv_ln_b, v_w_kv, v_w_branch, v_w_out, v_final_norm_g):
    L = w_in.shape[0]
    x0, mem0, tgt = x[0], mem[0], loss_target[0]
    T, M = x0.shape[0], mem0.shape[0]
    nbc = conv_w.shape[2]

    def shards(l):
        return [_mx(w_in[l]), _mx(w_kv[l]), _mx(w_branch[l]), _mx(w_out[l]), conv_w[l]]

    gather_rest = _Gather(shards(0)[1:])
    gather_upper = _Gather([a for l in range(1, L) for a in shards(l)], mid_frac=0.85) if L > 1 else None

    tril = jnp.tril(jnp.ones((CHUNK, CHUNK), bool))
    wm = [_mx(jnp.where(tril[None], w_s[l], 0.0)) for l in range(L)]
    wmt = [w.transpose(0, 2, 1) for w in wm]
    bst = [b_s[l].T for l in range(L)]
    ln_a = [jnp.stack([gmlp_ln_g[l], gmlp_ln_b[l]]) for l in range(L)]
    cvec = [jnp.stack([conv_b[l], conv_ln_g[l], conv_ln_b[l]]) for l in range(L)]

    def conv_taps(gathered):
        return jnp.pad(gathered.transpose(1, 0, 2).reshape(CONV_K, D), ((0, HALO - CONV_K), (0, 0)))

    win, wkv, wbr, wou, cwf = [], [], [], [], []
    memn, kvs, xs, saved = [], [], [x0], []
    for l in range(L):
        if l == 0:
            (z, h, w0), full = _rms_matmul_gathering(x0, norm_g[0], shards(0)[0], "inproj_fwd0", gather_rest)
            win, wkv, wbr, wou, cwf = [w0], [full[0]], [full[1]], [full[2]], [conv_taps(full[3])]
        else:
            (z, h), _ = _rms_matmul(xs[l], norm_g[l], win[l], f"inproj_fwd{l}")
        (kv, mn), _ = _rms_matmul(mem0, mem_norm_g[l], wkv[l], f"kv_fwd{l}")
        kvs.append(_mx(kv))
        memn.append(mn)
        (br, cpre), full = _branch_fwd(z, kvs[l], wm[l], bst[l], ln_a[l], cwf[l], cvec[l], f"branch_fwd{l}",
                                       gather_upper if l == 0 else None)
        for k in range(1, L if l == 0 else 0):
            f = full[5 * (k - 1):5 * k]
            win.append(f[0])
            wkv.append(f[1])
            wbr.append(f[2])
            wou.append(f[3])
            cwf.append(conv_taps(f[4]))
        if l < L - 1:
            xn, merged, proj = _merge_fwd(br, z, xs[l], wbr[l], wou[l], f"merge_fwd{l}")
            xs.append(xn)
        else:
            loss_part, dx, dfg, merged, proj = _merge_fwd(br, z, xs[l], wbr[l], wou[l], f"merge_fwd{l}",
                                                          head=(tgt, final_norm_g))
        saved.append((z, h, br, cpre, merged, proj))

    pending, recv = [("final_norm_g", dfg, True)], {}

    def flush():
        scat = [(k, a) for k, a, g in pending if not g]
        gath = [(k, a) for k, a, g in pending if g]
        pending.clear()
        comms = ([_Scatter([a for _, a in scat])] if scat else []) + ([_Gather([a for _, a in gath])] if gath else [])
        return [k for k, _ in scat + gath], comms[0] if len(comms) == 1 else _Both(*comms)

    def landed(keys, arrays):
        recv.update(zip(keys, arrays))

    for l in reversed(range(L)):
        z, h, br, cpre, merged, proj = saved[l]
        dproj, dbr, dzm = _merge_bwd(dx, proj, z, wbr[l], wou[l], f"merge_bwd{l}")
        dwb, _ = _atb(br, dproj, 1, f"dwbranch{l}")
        for n in range(3):
            pending.append((f"w_branch{l}_{n}", dwb[n].reshape(NDEV, D // NDEV, D), False))
        dwo, _ = _atb(merged[None], dx[None], 1, f"dwout{l}")
        pending.append((f"w_out{l}", dwo.reshape(NDEV, D // NDEV, D), False))
        keys, comm = flush()
        (dz, vecg, dbst, dws, dcw, dkv), got = _branch_bwd(
            z, dbr, cpre, dzm, kvs[l], wm[l], wmt[l], bst[l], ln_a[l], cwf[l], cvec[l], f"branch_bwd{l}", comm)
        landed(keys, got)
        dwk, _ = _atb(memn[l][None], dkv[None], NDEV, f"dwkv{l}")
        (_, dmg), _ = _rms_matmul_bwd(dkv, wkv[l], mem0, mem_norm_g[l], jnp.zeros((M, D), F32), f"kv_bwd{l}")
        rest = jnp.concatenate([dmg, vecg[0:2], vecg[2:5], dbst.T.reshape(1, D)], axis=0)
        pending += [(f"w_kv{l}", dwk, False),
                    (f"conv_w{l}", dcw[:CONV_K].reshape(CONV_K, NDEV, nbc).transpose(1, 0, 2), False),
                    (f"small{l}", rest, True), (f"w_s{l}", dws.reshape(N_GROUPS * CHUNK, CHUNK), True)]
        keys, comm = flush()
        dwi, got = _atb(h[None], dz[None], NDEV, f"dwin{l}", comm)
        landed(keys, got)
        pending.append((f"w_in{l}", dwi, False))
        keys, comm = flush()
        (dx, dng), got = _rms_matmul_bwd(dz, win[l], xs[l], norm_g[l], dx, f"inproj_bwd{l}", comm)
        landed(keys, got)
        pending.append((f"norm_g{l}", dng, True))
    grad_x = dx[None]

    def pack(p):
        rows = []
        for l in range(L):
            rows += [p["norm_g"][l], p["mem_norm_g"][l], p["gmlp_ln_g"][l], p["gmlp_ln_b"][l], p["conv_b"][l],
                     p["conv_ln_g"][l], p["conv_ln_b"][l], p["b_s"][l].reshape(D)]
        return jnp.stack(rows + [p["final_norm_g"]])[None]

    names = ["norm_g", "mem_norm_g", "gmlp_ln_g", "gmlp_ln_b", "conv_b", "conv_ln_g", "conv_ln_b", "b_s",
             "final_norm_g"]
    w_small = pack(dict(zip(names, [norm_g, mem_norm_g, gmlp_ln_g, gmlp_ln_b, conv_b, conv_ln_g, conv_ln_b,
                                    b_s, final_norm_g])))
    m_small = pack(dict(zip(names, [m_norm_g, m_mem_norm_g, m_gmlp_ln_g, m_gmlp_ln_b, m_conv_b, m_conv_ln_g,
                                    m_conv_ln_b, m_b_s, m_final_norm_g])))
    v_small = pack(dict(zip(names, [v_norm_g, v_mem_norm_g, v_gmlp_ln_g, v_gmlp_ln_b, v_conv_b, v_conv_ln_g,
                                    v_conv_ln_b, v_b_s, v_final_norm_g])))
    outs = {}

    def run(key, parts, w, m, v, comm=None):
        outs[key], got = _adamw(parts, w, m, v, "adamw_" + key, comm)
        return got

    keys, comm = flush()
    landed(keys, run("w_in", [recv[f"w_in{l}"] for l in range(L)], w_in, m_w_in, v_w_in, comm))
    parts_small = jnp.concatenate([recv[f"{k}{l}"] for l in range(L) for k in ("norm_g", "small")]
                                  + [recv["final_norm_g"]], axis=1)
    parts_ws = jnp.concatenate([recv[f"w_s{l}"] for l in range(L)], axis=1)
    run("w_kv", [recv[f"w_kv{l}"] for l in range(L)], w_kv, m_w_kv, v_w_kv)
    sh = (L * 3, D // NDEV, D)
    run("w_branch", [recv[f"w_branch{l}_{n}"] for l in range(L) for n in range(3)],
        w_branch.reshape(sh), m_w_branch.reshape(sh), v_w_branch.reshape(sh))
    run("w_out", [recv[f"w_out{l}"] for l in range(L)], w_out, m_w_out, v_w_out)
    run("conv_w", [recv[f"conv_w{l}"] for l in range(L)], conv_w, m_conv_w, v_conv_w)
    run("small", [parts_small], w_small, m_small, v_small)
    ws_shape = (1, L * N_GROUPS * CHUNK, CHUNK)
    run("w_s", [parts_ws], w_s.reshape(ws_shape), m_w_s.reshape(ws_shape), v_w_s.reshape(ws_shape))

    def leaf(name, k):
        if name in ("w_in", "w_kv", "w_out", "conv_w"):
            return outs[name][k]
        if name == "w_branch":
            return outs[name][k].reshape(w_branch.shape)
        if name == "w_s":
            return outs["w_s"][k].reshape(L, N_GROUPS, CHUNK, CHUNK)
        sm = outs["small"][k][0]
        if name == "final_norm_g":
            return sm[8 * L]
        j = names.index(name)
        rows = jnp.stack([sm[8 * l + j] for l in range(L)])
        return rows.reshape(L, N_GROUPS, CHUNK) if name == "b_s" else rows

    order = ["norm_g", "mem_norm_g", "w_in", "gmlp_ln_g", "gmlp_ln_b", "w_s", "b_s", "conv_w", "conv_b",
             "conv_ln_g", "conv_ln_b", "w_kv", "w_branch", "w_out", "final_norm_g"]
    loss = lax.psum(loss_part[0, 0], ("x", "y", "c"))
    return (loss, grad_x, *[leaf(nm, k) for k in range(4) for nm in order])
```

```python
import math

import jax
import jax.numpy as jnp
from jax import lax
from jax.experimental import pallas as pl
from jax.experimental.pallas import tpu as pltpu

F32 = jnp.float32
MXU_DTYPE = jnp.bfloat16
ACT_DTYPE = jnp.bfloat16
GRAD_DTYPE = jnp.bfloat16

D = 1024
N_SEG = 11
N_IN = N_SEG * D
NDEV = 8
CHUNK = 128
N_GROUPS = 8
CONV_K = 31
HALO = 32
LANES = 128
STRIP = 32
HEADS = 4
HEAD_DIM = D // HEADS
RMS_EPS = 1e-6
LN_EPS = 1e-5
ADAM_LR, ADAM_B1, ADAM_B2, ADAM_EPS, ADAM_WD, ADAM_STEP = 0.001, 0.9, 0.999, 1e-08, 0.01, 10
SEG_AU, SEG_AV, SEG_AG, SEG_BA, SEG_BB, SEG_BG, SEG_CQ, SEG_CG, SEG_M = 0, 1, 2, 3, 4, 5, 6, 7, 8

VMEM_LIMIT = 60 * 1024 * 1024
MESH = pl.DeviceIdType.MESH
NT_DIMS = (((1,), (1,)), ((), ()))
TN_DIMS = (((0,), (0,)), ((), ()))


def _params(*sem):
    return pltpu.CompilerParams(dimension_semantics=sem, vmem_limit_bytes=VMEM_LIMIT)


def _tile(n, want):
    t = min(n, want)
    assert n % t == 0, (n, want)
    return t


def _mx(v):
    return v.astype(MXU_DTYPE)


def _gelu(x):
    t = jnp.tanh(0.7978845608028654 * (x + 0.044715 * x * x * x))
    return 0.5 * x * (1.0 + t), t


def _gelu_grad(x, t):
    return 0.5 * (1.0 + t) + 0.5 * x * (1.0 - t * t) * 0.7978845608028654 * (1.0 + 3.0 * 0.044715 * x * x)


def _silu_grad(x, s):
    return s * (1.0 + x * (1.0 - s))


def _ln_stats(v):
    mu = jnp.mean(v, axis=-1, keepdims=True)
    vc = v - mu
    rstd = lax.rsqrt(jnp.mean(vc * vc, axis=-1, keepdims=True) + LN_EPS)
    return vc * rstd, rstd


def _ln_grad(dy, g, vhat, rstd):
    dvh = dy * g
    return rstd * (dvh - jnp.mean(dvh, axis=-1, keepdims=True)
                   - vhat * jnp.mean(dvh * vhat, axis=-1, keepdims=True))


def _rowsum(v):
    return jnp.sum(v, axis=0, keepdims=True)


def _coords():
    return lax.axis_index("x"), lax.axis_index("y"), lax.axis_index("c")


def _flip(pos, d):
    x, y, c = pos
    return (1 - x if d & 4 else x, 1 - y if d & 2 else y, 1 - c if d & 1 else c)


def _slot(pos):
    return 4 * pos[0] + 2 * pos[1] + pos[2]


CHIP_FLIPS = (4, 2, 6)


class _Gather:
    def __init__(self, arrays, mid_frac=0.8):
        self.arrays = list(arrays)
        self.n = n = len(arrays)
        self.mid_frac = mid_frac
        self.out_shape = [jax.ShapeDtypeStruct((NDEV,) + a.shape, a.dtype) for a in arrays]
        self.scratch = [pltpu.SemaphoreType.DMA((n, 7)), pltpu.SemaphoreType.DMA((n, 7)),
                        pltpu.SemaphoreType.DMA((n,))]

    def _copy(self, refs, i, k, block, to, own=False):
        ins, outs, (send, recv, _) = refs
        slot = outs[i].at[_slot(block)]
        return pltpu.make_async_remote_copy(
            src_ref=ins[i] if own else slot, dst_ref=slot, send_sem=send.at[i, k], recv_sem=recv.at[i, k],
            device_id=to, device_id_type=MESH)

    def _local(self, refs, i):
        ins, outs, (_, _, loc) = refs
        return pltpu.make_async_copy(ins[i], outs[i].at[_slot(_coords())], loc.at[i])

    def _first(self, refs, i, k):
        me = _coords()
        return self._copy(refs, i, k, me, _flip(me, ((1,) + CHIP_FLIPS)[k]), own=True)

    def _passed(self, refs, i, j):
        me = _coords()
        return self._copy(refs, i, 4 + j, _flip(me, CHIP_FLIPS[j]), _flip(me, 1))

    def start(self, refs):
        for i in range(self.n):
            self._local(refs, i).start()
        for k in range(4):
            for i in range(self.n):
                self._first(refs, i, k).start()

    def forward(self, refs):
        me = _coords()
        for j, d in enumerate(CHIP_FLIPS):
            for i in range(self.n):
                self._copy(refs, i, 1 + j, _flip(me, d), me).wait_recv()
                self._passed(refs, i, j).start()

    def finish(self, refs):
        me = _coords()
        sib = _flip(me, 1)
        for i in range(self.n):
            self._copy(refs, i, 0, sib, me).wait_recv()
        for j, d in enumerate(CHIP_FLIPS):
            for i in range(self.n):
                self._copy(refs, i, 4 + j, _flip(sib, d), me).wait_recv()
        for i in range(self.n):
            for k in range(4):
                self._first(refs, i, k).wait_send()
            for j in range(3):
                self._passed(refs, i, j).wait_send()
            self._local(refs, i).wait()


class _Scatter:
    def __init__(self, arrays):
        self.arrays = list(arrays)
        self.n = n = len(arrays)
        self.mid_frac = None
        self.out_shape = [jax.ShapeDtypeStruct(a.shape, a.dtype) for a in arrays]
        self.scratch = [pltpu.SemaphoreType.DMA((n, 7)), pltpu.SemaphoreType.DMA((n, 7)),
                        pltpu.SemaphoreType.DMA((n,))]

    def _copy(self, refs, i, d, landing):
        ins, outs, (send, recv, _) = refs
        me = _coords()
        peer = _flip(me, d)
        return pltpu.make_async_remote_copy(
            src_ref=ins[i].at[_slot(peer)], dst_ref=outs[i].at[_slot(peer) if landing else _slot(me)],
            send_sem=send.at[i, d - 1], recv_sem=recv.at[i, d - 1], device_id=peer, device_id_type=MESH)

    def _local(self, refs, i):
        ins, outs, (_, _, loc) = refs
        me = _slot(_coords())
        return pltpu.make_async_copy(ins[i].at[me], outs[i].at[me], loc.at[i])

    def start(self, refs):
        for i in range(self.n):
            self._local(refs, i).start()
        for d in range(1, NDEV):
            for i in range(self.n):
                self._copy(refs, i, d, False).start()

    def forward(self, refs):
        pass

    def finish(self, refs):
        for d in range(1, NDEV):
            for i in range(self.n):
                self._copy(refs, i, d, True).wait_recv()
        for d in range(1, NDEV):
            for i in range(self.n):
                self._copy(refs, i, d, False).wait_send()
        for i in range(self.n):
            self._local(refs, i).wait()


class _Both:
    def __init__(self, a, b):
        self.parts = (a, b)
        self.arrays = a.arrays + b.arrays
        self.n = a.n + b.n
        self.mid_frac = a.mid_frac if a.mid_frac is not None else b.mid_frac
        self.out_shape = a.out_shape + b.out_shape
        self.scratch = a.scratch + b.scratch

    def _each(self, refs):
        ins, outs, sems = refs
        na, ns = self.parts[0].n, len(self.parts[0].scratch)
        return ((self.parts[0], (ins[:na], outs[:na], sems[:ns])), (self.parts[1], (ins[na:], outs[na:], sems[ns:])))

    def start(self, refs):
        for part, r in self._each(refs):
            part.start(r)

    def forward(self, refs):
        for part, r in self._each(refs):
            part.forward(r)

    def finish(self, refs):
        for part, r in self._each(refs):
            part.finish(r)


def _call(body, name, grid, in_specs, out_specs, out_shape, scratch, args, comm=None, start_frac=0.0):
    params = _params(*(["arbitrary"] * len(grid)))
    if comm is None:
        outs = pl.pallas_call(
            body, name=name, grid=grid, in_specs=in_specs, out_specs=out_specs, out_shape=out_shape,
            scratch_shapes=scratch, compiler_params=params)(*args)
        return list(outs), []
    n_in, n_out, n_scr, k = len(in_specs), len(out_specs), len(scratch), comm.n
    nsteps = math.prod(grid) if grid else 1
    first = int(nsteps * start_frac)
    mid = max(first, min(nsteps - 1, int(nsteps * comm.mid_frac))) if comm.mid_frac is not None else None

    def hosted(*refs):
        ins, refs = refs[:n_in], refs[n_in:]
        cins, refs = refs[:k], refs[k:]
        outs, refs = refs[:n_out], refs[n_out:]
        couts, refs = refs[:k], refs[k:]
        scr, sems = refs[:n_scr], refs[n_scr:]
        crefs = (cins, couts, sems)
        if nsteps == 1:
            comm.start(crefs)
            body(*ins, *outs, *scr)
            comm.forward(crefs)
            comm.finish(crefs)
            return
        step = pl.program_id(0)
        for a in range(1, len(grid)):
            step = step * grid[a] + pl.program_id(a)
        pl.when(step == first)(lambda: comm.start(crefs))
        if mid is not None:
            pl.when(step == mid)(lambda: comm.forward(crefs))
        body(*ins, *outs, *scr)
        pl.when(step == nsteps - 1)(lambda: comm.finish(crefs))

    any_spec = pl.BlockSpec(memory_space=pl.ANY)
    outs = pl.pallas_call(
        hosted, name=name, grid=grid,
        in_specs=list(in_specs) + [any_spec] * k, out_specs=list(out_specs) + [any_spec] * k,
        out_shape=list(out_shape) + comm.out_shape, scratch_shapes=list(scratch) + comm.scratch,
        compiler_params=params)(*args, *comm.arrays)
    return list(outs[:n_out]), list(outs[n_out:])


def _rms_matmul(x, g, w, name, comm=None):
    T = x.shape[0]
    nb = w.shape[2]
    tT = _tile(T, 1024)
    per = 2

    def body(x_ref, g_ref, w_ref, z_ref, h_ref):
        @pl.when(pl.program_id(1) == 0)
        def _():
            xf = x_ref[...]
            r = lax.rsqrt(jnp.mean(xf * xf, axis=-1, keepdims=True) + RMS_EPS)
            h_ref[...] = (xf * r * g_ref[...]).astype(h_ref.dtype)

        for j in range(per):
            z_ref[:, j * nb:(j + 1) * nb] = jnp.dot(h_ref[...], w_ref[j], preferred_element_type=F32
                                                    ).astype(z_ref.dtype)

    return _call(
        body, name, (T // tT, NDEV // per),
        [pl.BlockSpec((tT, D), lambda t, n: (t, 0)),
         pl.BlockSpec((1, D), lambda t, n: (0, 0)),
         pl.BlockSpec((per, D, nb), lambda t, n: (n, 0, 0))],
        [pl.BlockSpec((tT, per * nb), lambda t, n: (t, n)),
         pl.BlockSpec((tT, D), lambda t, n: (t, 0))],
        [jax.ShapeDtypeStruct((T, NDEV * nb), ACT_DTYPE), jax.ShapeDtypeStruct((T, D), MXU_DTYPE)],
        [], (x, g.reshape(1, D), w), comm)


ARRIVAL = (0, 1) + CHIP_FLIPS + tuple(d ^ 1 for d in CHIP_FLIPS)


def _rms_matmul_gathering(x, g, w_shard, name, comm=None):
    T = x.shape[0]
    nb = w_shard.shape[1]
    tT = _tile(T, 1024)
    nT = T // tT

    def body(x_ref, g_ref, wsh_ref, z_ref, h_ref, wfull_ref, h_all, wbuf, zbuf, fetch_sems, z_sems, send_sems,
             recv_sems, own_sem):
        n, t = pl.program_id(0), pl.program_id(1)
        step = n * nT + t
        me = _coords()
        sib = _flip(me, 1)

        def remote(k, block, to, own=False):
            slot = wfull_ref.at[_slot(block)]
            return pltpu.make_async_remote_copy(
                src_ref=wsh_ref if own else slot, dst_ref=slot, send_sem=send_sems.at[k], recv_sem=recv_sems.at[k],
                device_id=to, device_id_type=MESH)

        def first(k):
            return remote(k, me, _flip(me, ARRIVAL[1 + k]), own=True)

        def passed(j):
            return remote(4 + j, _flip(me, CHIP_FLIPS[j]), sib)

        own_copy = pltpu.make_async_copy(wsh_ref, wfull_ref.at[_slot(me)], own_sem)

        def fetch(src, nn):
            return pltpu.make_async_copy(src, wbuf.at[nn % 2], fetch_sems.at[nn % 2])

        @pl.when(step == 0)
        def _():
            own_copy.start()
            for k in range(3):
                first(k).start()
            fetch(wsh_ref, 0).start()

        @pl.when(t == 0)
        def _():
            fetch(wfull_ref.at[0], n).wait()

        @pl.when(n == 0)
        def _():
            xf = x_ref[...]
            r = lax.rsqrt(jnp.mean(xf * xf, axis=-1, keepdims=True) + RMS_EPS)
            h = (xf * r * g_ref[...]).astype(h_ref.dtype)
            h_ref[...] = h
            h_all[t] = h

        def z_copy(s, col):
            return pltpu.make_async_copy(
                zbuf.at[s % 2], z_ref.at[pl.ds(pl.multiple_of(t * tT, tT), tT), pl.ds(col * nb, nb)],
                z_sems.at[s % 2])

        @pl.when(step >= 2)
        def _():
            z_copy(step, 0).wait()

        d = sum(jnp.where(n == nn, ARRIVAL[nn], 0) for nn in range(NDEV))
        col = _slot((me[0] ^ ((d >> 2) & 1), me[1] ^ ((d >> 1) & 1), me[2] ^ (d & 1)))
        zbuf[step % 2] = jnp.dot(h_all[t], wbuf[n % 2], preferred_element_type=F32).astype(zbuf.dtype)
        z_copy(step, col).start()

        for nn in range(1, NDEV):
            @pl.when((n == nn - 1) & (t == nT - 1))
            def _(nn=nn):
                block = _flip(me, ARRIVAL[nn])
                if nn == 1:
                    remote(0, sib, me).wait_recv()
                elif nn < 5:
                    if nn == 2:
                        first(3).start()
                    remote(nn - 1, block, me).wait_recv()
                    passed(nn - 2).start()
                else:
                    remote(nn - 1, block, me).wait_recv()
                fetch(wfull_ref.at[_slot(block)], nn).start()

        @pl.when(step == NDEV * nT - 1)
        def _():
            z_copy(step - 1, 0).wait()
            z_copy(step, 0).wait()
            for k in range(4):
                first(k).wait_send()
            for j in range(3):
                passed(j).wait_send()
            own_copy.wait()

    keep = lambda n, t: (jnp.where(n == 0, t, nT - 1), 0)
    any_spec = pl.BlockSpec(memory_space=pl.ANY)
    return _call(
        body, name, (NDEV, nT),
        [pl.BlockSpec((tT, D), keep), pl.BlockSpec((1, D), lambda n, t: (0, 0)), any_spec],
        [any_spec, pl.BlockSpec((tT, D), keep), any_spec],
        [jax.ShapeDtypeStruct((T, NDEV * nb), ACT_DTYPE), jax.ShapeDtypeStruct((T, D), MXU_DTYPE),
         jax.ShapeDtypeStruct((NDEV,) + w_shard.shape, w_shard.dtype)],
        [pltpu.VMEM((nT, tT, D), MXU_DTYPE), pltpu.VMEM((2, D, nb), w_shard.dtype),
         pltpu.VMEM((2, tT, nb), ACT_DTYPE), pltpu.SemaphoreType.DMA((2,)), pltpu.SemaphoreType.DMA((2,)),
         pltpu.SemaphoreType.DMA((7,)), pltpu.SemaphoreType.DMA((7,)), pltpu.SemaphoreType.DMA],
        (x, g.reshape(1, D), w_shard), comm, start_frac=0.5)


def _rms_matmul_bwd(dz, w, x, g, dxo, name, comm=None):
    T = x.shape[0]
    nb = w.shape[2]
    tT = _tile(T, 256)
    per = 8
    steps = NDEV // per

    def body(dz_ref, w_ref, x_ref, g_ref, dxo_ref, dx_ref, dg_ref, acc):
        t, n = pl.program_id(0), pl.program_id(1)

        @pl.when((n == 0) & (t == 0))
        def _():
            dg_ref[...] = jnp.zeros_like(dg_ref)

        part = None
        for j in range(per):
            d = lax.dot_general(_mx(dz_ref[:, j * nb:(j + 1) * nb]), w_ref[j], NT_DIMS, preferred_element_type=F32)
            part = d if part is None else part + d

        @pl.when(n == 0)
        def _():
            acc[...] = part

        @pl.when(n > 0)
        def _():
            acc[...] += part

        @pl.when(n == steps - 1)
        def _():
            xf = x_ref[...]
            r = lax.rsqrt(jnp.mean(xf * xf, axis=-1, keepdims=True) + RMS_EPS)
            xh = xf * r
            dh = acc[...]
            dxh = dh * g_ref[...]
            dx_ref[...] = dxo_ref[...] + r * (dxh - xh * jnp.mean(dxh * xh, axis=-1, keepdims=True))
            dg_ref[...] += _rowsum(dh * xh)

    return _call(
        body, name, (T // tT, steps),
        [pl.BlockSpec((tT, per * nb), lambda t, n: (t, n)),
         pl.BlockSpec((per, D, nb), lambda t, n: (n, 0, 0)),
         pl.BlockSpec((tT, D), lambda t, n: (t, 0)),
         pl.BlockSpec((1, D), lambda t, n: (0, 0)),
         pl.BlockSpec((tT, D), lambda t, n: (t, 0))],
        [pl.BlockSpec((tT, D), lambda t, n: (t, 0)),
         pl.BlockSpec((1, D), lambda t, n: (0, 0))],
        [jax.ShapeDtypeStruct((T, D), F32), jax.ShapeDtypeStruct((1, D), F32)],
        [pltpu.VMEM((tT, D), F32)], (dz, w, x, g.reshape(1, D), dxo), comm)


def _atb(a, b, nblk, name, comm=None):
    G, T, M = a.shape
    N = b.shape[2]
    nb = N // nblk
    tk = _tile(T, 2048)
    nk = T // tk

    def body(a_ref, b_ref, o_ref, acc):
        k = pl.program_id(1)

        @pl.when(k == 0)
        def _():
            acc[...] = jnp.zeros_like(acc)

        acc[...] += lax.dot_general(_mx(a_ref[...]), _mx(b_ref[...]), TN_DIMS, preferred_element_type=F32)

        @pl.when(k == nk - 1)
        def _():
            o_ref[...] = acc[...].astype(o_ref.dtype)

    outs, couts = _call(
        body, name, (G * nblk, nk),
        [pl.BlockSpec((None, tk, M), lambda n, k: (n // nblk, k, 0)),
         pl.BlockSpec((None, tk, nb), lambda n, k: (n // nblk, k, n % nblk))],
        [pl.BlockSpec((None, M, nb), lambda n, k: (n, 0, 0))],
        [jax.ShapeDtypeStruct((G * nblk, M, nb), GRAD_DTYPE)],
        [pltpu.VMEM((M, nb), F32)], (a, b), comm)
    return outs[0], couts


def _loss_rows(xf, t_ref, g_ref, loss_ref, dx_ref, dg_ref):
    @pl.when(pl.program_id(0) == 0)
    def _():
        loss_ref[...] = jnp.zeros_like(loss_ref)
        dg_ref[...] = jnp.zeros_like(dg_ref)

    r = lax.rsqrt(jnp.mean(xf * xf, axis=-1, keepdims=True) + RMS_EPS)
    xh = xf * r
    err = xh * g_ref[...] - t_ref[...]
    loss_ref[...] += 0.5 * jnp.sum(jnp.mean(err * err, axis=-1, keepdims=True), axis=0, keepdims=True)
    dy = err * (1.0 / D)
    dxh = dy * g_ref[...]
    dx_ref[...] = r * (dxh - xh * jnp.mean(dxh * xh, axis=-1, keepdims=True))
    dg_ref[...] += _rowsum(dy * xh)


def _spatial_gate(wm_ref, bst_ref, vb_ref, sv_ref, n_chunks):
    for c in range(n_chunks):
        rows = slice(c * CHUNK, (c + 1) * CHUNK)
        for g in range(N_GROUPS):
            cols = slice(g * CHUNK, (g + 1) * CHUNK)
            sv_ref[rows, cols] = (jnp.dot(wm_ref[g], vb_ref[rows, cols], preferred_element_type=F32)
                                  + bst_ref[:, g:g + 1])


def _lane_loop(fn):
    def step(i, carry):
        fn(pl.ds(pl.multiple_of(i * LANES, LANES), LANES))
        return carry

    lax.fori_loop(0, D // LANES, step, 0)


def _shifted_copies(buf, sh, n):
    for s in range(1, 8):
        sh[s - 1, 0:n, :] = buf[s:s + n, :]


def _window(buf, sh, base, off, lanes):
    a, s = divmod(off, 8)
    src = buf if s == 0 else sh.at[s - 1]
    return src[base + 8 * a:base + 8 * a + 8, lanes]


def _softmax_rows(s):
    e = jnp.exp(s - jnp.max(s, axis=-1, keepdims=True))
    return e / jnp.sum(e, axis=-1, keepdims=True)


def _branch_fwd(z, kv, wm, bst, ln_a, cw, cvec, name, comm=None):
    T = z.shape[0]
    tT = _tile(T, 256)
    n_chunks = tT // CHUNK

    def body(z_ref, kv_ref, wm_ref, bst_ref, lna_ref, cw_ref, cvec_ref, br_ref, c_ref, gbuf, gsh, vb, ua):
        @pl.when(pl.program_id(0) == 0)
        def _():
            gbuf[0:HALO, :] = jnp.zeros((HALO, D), F32)

        def seg(s):
            return z_ref[:, s * D:(s + 1) * D].astype(F32)

        u, _ = _gelu(seg(SEG_AU))
        zg = seg(SEG_AG)
        ua[...] = u * (zg * jax.nn.sigmoid(zg))
        gv, _ = _gelu(seg(SEG_AV))
        vhat, _ = _ln_stats(gv)
        vb[...] = _mx(vhat * lna_ref[0:1, :] + lna_ref[1:2, :])
        for c in range(n_chunks):
            rows = slice(c * CHUNK, (c + 1) * CHUNK)
            for g in range(N_GROUPS):
                cols = slice(g * CHUNK, (g + 1) * CHUNK)
                sv = jnp.dot(wm_ref[g], vb[rows, cols], preferred_element_type=F32) + bst_ref[:, g:g + 1]
                br_ref[0, rows, cols] = (sv * ua[rows, cols]).astype(br_ref.dtype)

        gbuf[HALO:HALO + tT, :] = seg(SEG_BA) * jax.nn.sigmoid(seg(SEG_BB))
        _shifted_copies(gbuf, gsh, tT + HALO - 8)
        def conv_lanes(lanes):
            taps = [jnp.broadcast_to(cw_ref[k:k + 1, lanes], (8, LANES)) for k in range(CONV_K)]
            bias = jnp.broadcast_to(cvec_ref[0:1, lanes], (8, LANES))
            for base in range(0, tT, 8):
                acc = [bias, None, None, None]
                for k in range(CONV_K):
                    term = taps[k] * _window(gbuf, gsh, base, k + HALO - CONV_K + 1, lanes)
                    acc[k % 4] = term if acc[k % 4] is None else acc[k % 4] + term
                c_ref[base:base + 8, lanes] = (acc[0] + acc[1]) + (acc[2] + acc[3])

        _lane_loop(conv_lanes)
        gbuf[0:HALO, :] = gbuf[tT:tT + HALO, :]
        chat, _ = _ln_stats(c_ref[...])
        cl = chat * cvec_ref[1:2, :] + cvec_ref[2:3, :]
        zg = seg(SEG_BG)
        br_ref[1] = (cl * jax.nn.sigmoid(cl) * (zg * jax.nn.sigmoid(zg))).astype(br_ref.dtype)

        for h in range(HEADS):
            cols = slice(h * HEAD_DIM, (h + 1) * HEAD_DIM)
            q = _mx(z_ref[:, SEG_CQ * D + h * HEAD_DIM:SEG_CQ * D + (h + 1) * HEAD_DIM])
            s = lax.dot_general(q, kv_ref[:, cols], NT_DIMS, preferred_element_type=F32)
            p = _softmax_rows(s * (1.0 / math.sqrt(HEAD_DIM)))
            att = jnp.dot(_mx(p), kv_ref[:, D + h * HEAD_DIM:D + (h + 1) * HEAD_DIM], preferred_element_type=F32)
            zg = z_ref[:, SEG_CG * D + h * HEAD_DIM:SEG_CG * D + (h + 1) * HEAD_DIM].astype(F32)
            br_ref[2, :, cols] = (att * (zg * jax.nn.sigmoid(zg))).astype(br_ref.dtype)

    full = lambda shape: pl.BlockSpec(shape, lambda t: (0,) * len(shape))
    return _call(
        body, name, (T // tT,),
        [pl.BlockSpec((tT, SEG_M * D), lambda t: (t, 0)),
         full(kv.shape), full(wm.shape), full(bst.shape), full(ln_a.shape), full(cw.shape), full(cvec.shape)],
        [pl.BlockSpec((3, tT, D), lambda t: (0, t, 0)), pl.BlockSpec((tT, D), lambda t: (t, 0))],
        [jax.ShapeDtypeStruct((3, T, D), MXU_DTYPE), jax.ShapeDtypeStruct((T, D), F32)],
        [pltpu.VMEM((tT + HALO, D), F32), pltpu.VMEM((7, tT + HALO - 8, D), F32),
         pltpu.VMEM((tT, D), MXU_DTYPE), pltpu.VMEM((tT, D), F32)],
        (z, kv, wm, bst, ln_a, cw, cvec), comm)


def _branch_bwd(z, dbr, c, dzm, kv, wm, wmt, bst, ln_a, cw, cvec, name, comm=None):
    T = z.shape[0]
    M = kv.shape[0]
    tT = _tile(T, 128)
    nT = T // tT
    n_chunks = tT // CHUNK

    def body(z_ref, dbr_ref, c_ref, dzm_ref, kv_ref, wm_ref, wmt_ref, bst_ref, lna_ref,
             cw_ref, cvec_ref, dz_ref, vecg_ref, dbst_ref, dws_ref, dcw_ref, dkv_ref,
             gbuf, dcbuf, vb, dsvb, sv, dvbuf, dcsh, dglu, vh, gq, dcw8, dcw_step):
        i = pl.program_id(0)

        @pl.when(i == 0)
        def _():
            vecg_ref[...] = jnp.zeros_like(vecg_ref)
            dbst_ref[...] = jnp.zeros_like(dbst_ref)
            dws_ref[...] = jnp.zeros_like(dws_ref)
            dcw8[...] = jnp.zeros_like(dcw8)
            dkv_ref[...] = jnp.zeros_like(dkv_ref)
            dcbuf[tT:tT + HALO, :] = jnp.zeros((HALO, D), F32)

        strips = [slice(r0, r0 + STRIP) for r0 in range(0, tT, STRIP)]

        def seg(r, s):
            return z_ref[r, s * D:(s + 1) * D].astype(F32)

        def put(r, s, val):
            dz_ref[r, s * D:(s + 1) * D] = val.astype(dz_ref.dtype)

        for r in strips:
            zv = seg(r, SEG_AV)
            gv, tv = _gelu(zv)
            vhat, rstd = _ln_stats(gv)
            vb[r, :] = _mx(vhat * lna_ref[0:1, :] + lna_ref[1:2, :])
            vh[r, :] = vhat
            gq[r, :] = rstd * _gelu_grad(zv, tv)
        _spatial_gate(wm_ref, bst_ref, vb, sv, n_chunks)
        for r in strips:
            zu, zg = seg(r, SEG_AU), seg(r, SEG_AG)
            u, tu = _gelu(zu)
            sg = jax.nn.sigmoid(zg)
            d_a = dbr_ref[0, r, :].astype(F32)
            put(r, SEG_AU, d_a * sv[r, :] * (zg * sg) * _gelu_grad(zu, tu))
            put(r, SEG_AG, d_a * u * sv[r, :] * _silu_grad(zg, sg))
            dsv = d_a * u * (zg * sg)
            dsvb[r, :] = _mx(dsv)
            in_chunk = slice(r.start % CHUNK, r.start % CHUNK + STRIP)
            for g in range(N_GROUPS):
                dbst_ref[in_chunk, g:g + 1] += jnp.sum(dsv[:, g * CHUNK:(g + 1) * CHUNK], axis=-1, keepdims=True)
        tril = (lax.broadcasted_iota(jnp.int32, (CHUNK, CHUNK), 0)
                >= lax.broadcasted_iota(jnp.int32, (CHUNK, CHUNK), 1))
        for g in range(N_GROUPS):
            cols = slice(g * CHUNK, (g + 1) * CHUNK)
            for cc in range(n_chunks):
                rows = slice(cc * CHUNK, (cc + 1) * CHUNK)
                dws = lax.dot_general(dsvb[rows, cols], vb[rows, cols], NT_DIMS, preferred_element_type=F32)
                dws_ref[g] += jnp.where(tril, dws, 0.0)
                dvbuf[rows, cols] = jnp.dot(wmt_ref[g], dsvb[rows, cols], preferred_element_type=F32)
        for r in strips:
            dv, vhat = dvbuf[r, :], vh[r, :]
            vecg_ref[0:1, :] += _rowsum(dv * vhat)
            vecg_ref[1:2, :] += _rowsum(dv)
            dvh = dv * lna_ref[0:1, :]
            put(r, SEG_AV, (dvh - jnp.mean(dvh, axis=-1, keepdims=True)
                            - vhat * jnp.mean(dvh * vhat, axis=-1, keepdims=True)) * gq[r, :])

        sgb_buf = sv
        for r in strips:
            za, zg = seg(r, SEG_BA), seg(r, SEG_BG)
            sgb = jax.nn.sigmoid(seg(r, SEG_BB))
            sgb_buf[r, :] = sgb
            gbuf[r, :] = za * sgb
            chat, crstd = _ln_stats(c_ref[r, :])
            cl = chat * cvec_ref[1:2, :] + cvec_ref[2:3, :]
            scl = jax.nn.sigmoid(cl)
            sg = jax.nn.sigmoid(zg)
            d_b = dbr_ref[1, r, :].astype(F32)
            put(r, SEG_BG, d_b * (cl * scl) * _silu_grad(zg, sg))
            dcl = d_b * (zg * sg) * _silu_grad(cl, scl)
            vecg_ref[3:4, :] += _rowsum(dcl * chat)
            vecg_ref[4:5, :] += _rowsum(dcl)
            dc = _ln_grad(dcl, cvec_ref[1:2, :], chat, crstd)
            vecg_ref[2:3, :] += _rowsum(dc)
            dcbuf[r, :] = dc
        _shifted_copies(dcbuf, dcsh, tT + HALO - 8)

        def conv_grads(lanes):
            taps = [jnp.broadcast_to(cw_ref[k:k + 1, lanes], (8, LANES)) for k in range(CONV_K)]
            wsum = [None] * CONV_K
            for base in range(0, tT, 8):
                glu = gbuf[base:base + 8, lanes]
                acc = [None] * 4
                for k in range(CONV_K):
                    win = _window(dcbuf, dcsh, base, CONV_K - 1 - k, lanes)
                    term = taps[k] * win
                    acc[k % 4] = term if acc[k % 4] is None else acc[k % 4] + term
                    term = glu * win
                    wsum[k] = term if wsum[k] is None else wsum[k] + term
                dglu[base:base + 8, lanes] = (acc[0] + acc[1]) + (acc[2] + acc[3])
            for k in range(CONV_K):
                dcw_step[8 * k:8 * k + 8, lanes] = wsum[k]

        _lane_loop(conv_grads)
        dcw8[...] += dcw_step[...]

        @pl.when(i == nT - 1)
        def _():
            for k in range(CONV_K):
                dcw_ref[k:k + 1, :] = _rowsum(dcw8[8 * k:8 * k + 8, :])
            dcw_ref[CONV_K:HALO, :] = jnp.zeros((HALO - CONV_K, D), F32)

        dcbuf[tT:tT + HALO, :] = dcbuf[0:HALO, :]
        for r in strips:
            dg, sgb = dglu[r, :], sgb_buf[r, :]
            put(r, SEG_BA, dg * sgb)
            put(r, SEG_BB, dg * seg(r, SEG_BA) * sgb * (1.0 - sgb))

        scale = 1.0 / math.sqrt(HEAD_DIM)
        for h in range(HEADS):
            cols = slice(h * HEAD_DIM, (h + 1) * HEAD_DIM)
            qcols = slice(SEG_CQ * D + h * HEAD_DIM, SEG_CQ * D + (h + 1) * HEAD_DIM)
            gcols = slice(SEG_CG * D + h * HEAD_DIM, SEG_CG * D + (h + 1) * HEAD_DIM)
            vcols = slice(D + h * HEAD_DIM, D + (h + 1) * HEAD_DIM)
            q = _mx(z_ref[:, qcols])
            kh, vh = kv_ref[:, cols], kv_ref[:, vcols]
            p = _softmax_rows(lax.dot_general(q, kh, NT_DIMS, preferred_element_type=F32) * scale)
            pb = _mx(p)
            att = jnp.dot(pb, vh, preferred_element_type=F32)
            zg = z_ref[:, gcols].astype(F32)
            sg = jax.nn.sigmoid(zg)
            d_c = dbr_ref[2, :, cols].astype(F32)
            dz_ref[:, gcols] = (d_c * att * _silu_grad(zg, sg)).astype(dz_ref.dtype)
            datt = _mx(d_c * (zg * sg))
            dp = lax.dot_general(datt, vh, NT_DIMS, preferred_element_type=F32)
            dkv_ref[:, vcols] += lax.dot_general(pb, datt, TN_DIMS, preferred_element_type=F32)
            ds = _mx(p * (dp - jnp.sum(dp * p, axis=-1, keepdims=True)) * scale)
            dz_ref[:, qcols] = jnp.dot(ds, kh, preferred_element_type=F32).astype(dz_ref.dtype)
            dkv_ref[:, cols] += lax.dot_general(ds, q, TN_DIMS, preferred_element_type=F32)

        dz_ref[:, SEG_M * D:] = dzm_ref[...].astype(dz_ref.dtype)

    rev = lambda i: nT - 1 - i
    full = lambda shape: pl.BlockSpec(shape, lambda i: (0,) * len(shape))
    return _call(
        body, name, (nT,),
        [pl.BlockSpec((tT, SEG_M * D), lambda i: (rev(i), 0)),
         pl.BlockSpec((3, tT, D), lambda i: (0, rev(i), 0)),
         pl.BlockSpec((tT, D), lambda i: (rev(i), 0)),
         pl.BlockSpec((tT, 3 * D), lambda i: (rev(i), 0)),
         full(kv.shape), full(wm.shape), full(wmt.shape), full(bst.shape), full(ln_a.shape),
         full(cw.shape), full(cvec.shape)],
        [pl.BlockSpec((tT, N_IN), lambda i: (rev(i), 0)),
         full((8, D)), full((CHUNK, N_GROUPS)), full((N_GROUPS, CHUNK, CHUNK)), full((HALO, D)),
         full((M, 2 * D))],
        [jax.ShapeDtypeStruct((T, N_IN), MXU_DTYPE), jax.ShapeDtypeStruct((8, D), F32),
         jax.ShapeDtypeStruct((CHUNK, N_GROUPS), F32),
         jax.ShapeDtypeStruct((N_GROUPS, CHUNK, CHUNK), F32),
         jax.ShapeDtypeStruct((HALO, D), F32), jax.ShapeDtypeStruct((M, 2 * D), F32)],
        [pltpu.VMEM((tT, D), F32), pltpu.VMEM((tT + HALO, D), F32),
         pltpu.VMEM((tT, D), MXU_DTYPE), pltpu.VMEM((tT, D), MXU_DTYPE),
         pltpu.VMEM((tT, D), F32), pltpu.VMEM((tT, D), F32),
         pltpu.VMEM((7, tT + HALO - 8, D), F32),
         pltpu.VMEM((tT, D), F32), pltpu.VMEM((tT, D), F32), pltpu.VMEM((tT, D), F32),
         pltpu.VMEM((CONV_K * 8, D), F32), pltpu.VMEM((CONV_K * 8, D), F32)],
        (z, dbr, c, dzm, kv, wm, wmt, bst, ln_a, cw, cvec), comm)


def _merge_fwd(br, z, x, wb, wo, name, head=None):
    T = x.shape[0]
    tT = _tile(T, 512)

    def body(br_ref, z0, z1, z2, x_ref, wb_ref, wo_ref, *rest):
        mg_ref, pj_ref = rest[-2:]
        merged = jnp.zeros((tT, D), F32)
        for n, zm in enumerate((z0, z1, z2)):
            proj = jnp.dot(br_ref[n], wb_ref[:, n].reshape(D, D), preferred_element_type=F32)
            pj_ref[n] = proj.astype(pj_ref.dtype)
            merged = merged + jax.nn.sigmoid(zm[...].astype(F32)) * proj
        mg_ref[...] = merged.astype(mg_ref.dtype)
        xn = x_ref[...] + jnp.dot(_mx(merged), wo_ref[...].reshape(D, D), preferred_element_type=F32)
        if head is None:
            rest[0][...] = xn
        else:
            _loss_rows(xn, *rest[:5])

    row = pl.BlockSpec((tT, D), lambda t: (t, 0))
    vec = pl.BlockSpec((1, D), lambda t: (0, 0))
    zspec = lambda n: pl.BlockSpec((tT, D), lambda t: (t, SEG_M + n))
    in_specs = [pl.BlockSpec((3, tT, D), lambda t: (0, t, 0)), zspec(0), zspec(1), zspec(2), row,
                pl.BlockSpec(wb.shape, lambda t: (0, 0, 0, 0)), pl.BlockSpec(wo.shape, lambda t: (0, 0, 0))]
    out_specs = [row, pl.BlockSpec((3, tT, D), lambda t: (0, t, 0))]
    out_shape = [jax.ShapeDtypeStruct((T, D), MXU_DTYPE), jax.ShapeDtypeStruct((3, T, D), ACT_DTYPE)]
    args = (br, z, z, z, x, wb, wo)
    if head is None:
        out_specs = [row] + out_specs
        out_shape = [jax.ShapeDtypeStruct((T, D), F32)] + out_shape
    else:
        in_specs += [row, vec]
        args += (head[0], head[1].reshape(1, D))
        out_specs = [pl.BlockSpec((1, 1), lambda t: (0, 0)), row, vec] + out_specs
        out_shape = [jax.ShapeDtypeStruct((1, 1), F32), jax.ShapeDtypeStruct((T, D), F32),
                     jax.ShapeDtypeStruct((1, D), F32)] + out_shape
    return pl.pallas_call(body, name=name, grid=(T // tT,), in_specs=in_specs, out_specs=out_specs,
                          out_shape=out_shape, compiler_params=_params("arbitrary"))(*args)


def _merge_bwd(dxo, proj, z, wb, wo, name):
    T = dxo.shape[0]
    tT = _tile(T, 512)

    def body(dxo_ref, pj_ref, z0, z1, z2, wb_ref, wo_ref, dpj_ref, dbr_ref, dzm_ref):
        dmerged = lax.dot_general(_mx(dxo_ref[...]), wo_ref[...].reshape(D, D), NT_DIMS,
                                  preferred_element_type=F32)
        for n, zm in enumerate((z0, z1, z2)):
            gate = jax.nn.sigmoid(zm[...].astype(F32))
            dproj = _mx(gate * dmerged)
            dpj_ref[n] = dproj
            dzm_ref[:, n * D:(n + 1) * D] = (pj_ref[n].astype(F32) * dmerged * gate * (1.0 - gate)
                                             ).astype(dzm_ref.dtype)
            dbr_ref[n] = lax.dot_general(dproj, wb_ref[:, n].reshape(D, D), NT_DIMS,
                                         preferred_element_type=F32).astype(dbr_ref.dtype)

    zspec = lambda n: pl.BlockSpec((tT, D), lambda t: (t, SEG_M + n))
    return pl.pallas_call(
        body, name=name, grid=(T // tT,),
        in_specs=[pl.BlockSpec((tT, D), lambda t: (t, 0)),
                  pl.BlockSpec((3, tT, D), lambda t: (0, t, 0)), zspec(0), zspec(1), zspec(2),
                  pl.BlockSpec(wb.shape, lambda t: (0, 0, 0, 0)),
                  pl.BlockSpec(wo.shape, lambda t: (0, 0, 0))],
        out_specs=[pl.BlockSpec((3, tT, D), lambda t: (0, t, 0)),
                   pl.BlockSpec((3, tT, D), lambda t: (0, t, 0)),
                   pl.BlockSpec((tT, 3 * D), lambda t: (t, 0))],
        out_shape=[jax.ShapeDtypeStruct((3, T, D), MXU_DTYPE), jax.ShapeDtypeStruct((3, T, D), ACT_DTYPE),
                   jax.ShapeDtypeStruct((T, 3 * D), MXU_DTYPE)],
        compiler_params=_params("parallel"),
    )(dxo, proj, z, z, z, wb, wo)


def _adamw(parts, w, m, v, name, comm=None):
    G, R, C = w.shape
    tr = 128 if R % 128 == 0 else R
    nr = R // tr
    c1 = 1.0 / (1.0 - ADAM_B1 ** ADAM_STEP)
    c2 = 1.0 / (1.0 - ADAM_B2 ** ADAM_STEP)

    def body(*refs):
        p_refs, (w_ref, m_ref, v_ref, g_out, d_out, m_out, v_out) = refs[:G], refs[G:]
        for i in range(G):
            @pl.when(pl.program_id(0) == i)
            def _(p_ref=p_refs[i]):
                g = p_ref[0].astype(F32)
                for p in range(1, NDEV):
                    g = g + p_ref[p].astype(F32)
                mn = ADAM_B1 * m_ref[...] + (1.0 - ADAM_B1) * g
                vn = ADAM_B2 * v_ref[...] + (1.0 - ADAM_B2) * (g * g)
                g_out[...] = g
                m_out[...] = mn
                v_out[...] = vn
                d_out[...] = -ADAM_LR * ((mn * c1) / (jnp.sqrt(vn * c2) + ADAM_EPS) + ADAM_WD * w_ref[...])

    def parts_spec(i):
        return pl.BlockSpec((NDEV, tr, C), lambda l, r: (0, jnp.where(l == i, r, jnp.where(l > i, nr - 1, 0)), 0))

    spec = pl.BlockSpec((None, tr, C), lambda l, r: (l, r, 0))
    return _call(
        body, name, (G, nr), [parts_spec(i) for i in range(G)] + [spec] * 3,
        [spec] * 4, [jax.ShapeDtypeStruct((G, R, C), F32)] * 4, [], (*parts, w, m, v), comm)


def kernel(x, mem, norm_g, mem_norm_g, w_in, gmlp_ln_g, gmlp_ln_b, w_s, b_s, conv_w, conv_b, conv_ln_g, conv_ln_b, w_kv, w_branch, w_out, final_norm_g, loss_target, m_norm_g, m_mem_norm_g, m_w_in, m_gmlp_ln_g, m_gmlp_ln_b, m_w_s, m_b_s, m_conv_w, m_conv_b, m_conv_ln_g, m_conv_ln_b, m_w_kv, m_w_branch, m_w_out, m_final_norm_g, v_norm_g, v_mem_norm_g, v_w_in, v_gmlp_ln_g, v_gmlp_ln_b, v_w_s, v_b_s, v_conv_w, v_conv_b, v_conv_ln_g, v_conv_ln_b, v_w_kv, v_w_branch, v_w_out, v_final_norm_g):
    L = w_in.shape[0]
    x0, mem0, tgt = x[0], mem[0], loss_target[0]
    T, M = x0.shape[0], mem0.shape[0]
    nbc = conv_w.shape[2]

    def shards(l):
        return [_mx(w_in[l]), _mx(w_kv[l]), _mx(w_branch[l]), _mx(w_out[l]), conv_w[l]]

    gather_rest = _Gather(shards(0)[1:])
    gather_upper = _Gather([a for l in range(1, L) for a in shards(l)], mid_frac=0.85) if L > 1 else None

    tril = jnp.tril(jnp.ones((CHUNK, CHUNK), bool))
    wm = [_mx(jnp.where(tril[None], w_s[l], 0.0)) for l in range(L)]
    wmt = [w.transpose(0, 2, 1) for w in wm]
    bst = [b_s[l].T for l in range(L)]
    ln_a = [jnp.stack([gmlp_ln_g[l], gmlp_ln_b[l]]) for l in range(L)]
    cvec = [jnp.stack([conv_b[l], conv_ln_g[l], conv_ln_b[l]]) for l in range(L)]

    def conv_taps(gathered):
        return jnp.pad(gathered.transpose(1, 0, 2).reshape(CONV_K, D), ((0, HALO - CONV_K), (0, 0)))

    win, wkv, wbr, wou, cwf = [], [], [], [], []
    memn, kvs, xs, saved = [], [], [x0], []
    for l in range(L):
        if l == 0:
            (z, h, w0), full = _rms_matmul_gathering(x0, norm_g[0], shards(0)[0], "inproj_fwd0", gather_rest)
            win, wkv, wbr, wou, cwf = [w0], [full[0]], [full[1]], [full[2]], [conv_taps(full[3])]
        else:
            (z, h), _ = _rms_matmul(xs[l], norm_g[l], win[l], f"inproj_fwd{l}")
        (kv, mn), _ = _rms_matmul(mem0, mem_norm_g[l], wkv[l], f"kv_fwd{l}")
        kvs.append(_mx(kv))
        memn.append(mn)
        (br, cpre), full = _branch_fwd(z, kvs[l], wm[l], bst[l], ln_a[l], cwf[l], cvec[l], f"branch_fwd{l}",
                                       gather_upper if l == 0 else None)
        for k in range(1, L if l == 0 else 0):
            f = full[5 * (k - 1):5 * k]
            win.append(f[0])
            wkv.append(f[1])
            wbr.append(f[2])
            wou.append(f[3])
            cwf.append(conv_taps(f[4]))
        if l < L - 1:
            xn, merged, proj = _merge_fwd(br, z, xs[l], wbr[l], wou[l], f"merge_fwd{l}")
            xs.append(xn)
        else:
            loss_part, dx, dfg, merged, proj = _merge_fwd(br, z, xs[l], wbr[l], wou[l], f"merge_fwd{l}",
                                                          head=(tgt, final_norm_g))
        saved.append((z, h, br, cpre, merged, proj))

    pending, recv = [("final_norm_g", dfg, True)], {}

    def flush():
        scat = [(k, a) for k, a, g in pending if not g]
        gath = [(k, a) for k, a, g in pending if g]
        pending.clear()
        comms = ([_Scatter([a for _, a in scat])] if scat else []) + ([_Gather([a for _, a in gath])] if gath else [])
        return [k for k, _ in scat + gath], comms[0] if len(comms) == 1 else _Both(*comms)

    def landed(keys, arrays):
        recv.update(zip(keys, arrays))

    for l in reversed(range(L)):
        z, h, br, cpre, merged, proj = saved[l]
        dproj, dbr, dzm = _merge_bwd(dx, proj, z, wbr[l], wou[l], f"merge_bwd{l}")
        dwb, _ = _atb(br, dproj, 1, f"dwbranch{l}")
        for n in range(3):
            pending.append((f"w_branch{l}_{n}", dwb[n].reshape(NDEV, D // NDEV, D), False))
        dwo, _ = _atb(merged[None], dx[None], 1, f"dwout{l}")
        pending.append((f"w_out{l}", dwo.reshape(NDEV, D // NDEV, D), False))
        keys, comm = flush()
        (dz, vecg, dbst, dws, dcw, dkv), got = _branch_bwd(
            z, dbr, cpre, dzm, kvs[l], wm[l], wmt[l], bst[l], ln_a[l], cwf[l], cvec[l], f"branch_bwd{l}", comm)
        landed(keys, got)
        dwk, _ = _atb(memn[l][None], dkv[None], NDEV, f"dwkv{l}")
        (_, dmg), _ = _rms_matmul_bwd(dkv, wkv[l], mem0, mem_norm_g[l], jnp.zeros((M, D), F32), f"kv_bwd{l}")
        rest = jnp.concatenate([dmg, vecg[0:2], vecg[2:5], dbst.T.reshape(1, D)], axis=0)
        pending += [(f"w_kv{l}", dwk, False),
                    (f"conv_w{l}", dcw[:CONV_K].reshape(CONV_K, NDEV, nbc).transpose(1, 0, 2), False),
                    (f"small{l}", rest, True), (f"w_s{l}", dws.reshape(N_GROUPS * CHUNK, CHUNK), True)]
        keys, comm = flush()
        dwi, got = _atb(h[None], dz[None], NDEV, f"dwin{l}", comm)
        landed(keys, got)
        pending.append((f"w_in{l}", dwi, False))
        keys, comm = flush()
        (dx, dng), got = _rms_matmul_bwd(dz, win[l], xs[l], norm_g[l], dx, f"inproj_bwd{l}", comm)
        landed(keys, got)
        pending.append((f"norm_g{l}", dng, True))
    grad_x = dx[None]

    def pack(p):
        rows = []
        for l in range(L):
            rows += [p["norm_g"][l], p["mem_norm_g"][l], p["gmlp_ln_g"][l], p["gmlp_ln_b"][l], p["conv_b"][l],
                     p["conv_ln_g"][l], p["conv_ln_b"][l], p["b_s"][l].reshape(D)]
        return jnp.stack(rows + [p["final_norm_g"]])[None]

    names = ["norm_g", "mem_norm_g", "gmlp_ln_g", "gmlp_ln_b", "conv_b", "conv_ln_g", "conv_ln_b", "b_s",
             "final_norm_g"]
    w_small = pack(dict(zip(names, [norm_g, mem_norm_g, gmlp_ln_g, gmlp_ln_b, conv_b, conv_ln_g, conv_ln_b,
                                    b_s, final_norm_g])))
    m_small = pack(dict(zip(names, [m_norm_g, m_mem_norm_g, m_gmlp_ln_g, m_gmlp_ln_b, m_conv_b, m_conv_ln_g,
                                    m_conv_ln_b, m_b_s, m_final_norm_g])))
    v_small = pack(dict(zip(names, [v_norm_g, v_mem_norm_g, v_gmlp_ln_g, v_gmlp_ln_b, v_conv_b, v_conv_ln_g,
                                    v_conv_ln_b, v_b_s, v_final_norm_g])))
    outs = {}

    def run(key, parts, w, m, v, comm=None):
        outs[key], got = _adamw(parts, w, m, v, "adamw_" + key, comm)
        return got

    keys, comm = flush()
    landed(keys, run("w_in", [recv[f"w_in{l}"] for l in range(L)], w_in, m_w_in, v_w_in, comm))
    parts_small = jnp.concatenate([recv[f"{k}{l}"] for l in range(L) for k in ("norm_g", "small")]
                                  + [recv["final_norm_g"]], axis=1)
    parts_ws = jnp.concatenate([recv[f"w_s{l}"] for l in range(L)], axis=1)
    run("w_kv", [recv[f"w_kv{l}"] for l in range(L)], w_kv, m_w_kv, v_w_kv)
    sh = (L * 3, D // NDEV, D)
    run("w_branch", [recv[f"w_branch{l}_{n}"] for l in range(L) for n in range(3)],
        w_branch.reshape(sh), m_w_branch.reshape(sh), v_w_branch.reshape(sh))
    run("w_out", [recv[f"w_out{l}"] for l in range(L)], w_out, m_w_out, v_w_out)
    run("conv_w", [recv[f"conv_w{l}"] for l in range(L)], conv_w, m_conv_w, v_conv_w)
    run("small", [parts_small], w_small, m_small, v_small)
    ws_shape = (1, L * N_GROUPS * CHUNK, CHUNK)
    run("w_s", [parts_ws], w_s.reshape(ws_shape), m_w_s.reshape(ws_shape), v_w_s.reshape(ws_shape))

    def leaf(name, k):
        if name in ("w_in", "w_kv", "w_out", "conv_w"):
            return outs[name][k]
        if name == "w_branch":
            return outs[name][k].reshape(w_branch.shape)
        if name == "w_s":
            return outs["w_s"][k].reshape(L, N_GROUPS, CHUNK, CHUNK)
        sm = outs["small"][k][0]
        if name == "final_norm_g":
            return sm[8 * L]
        j = names.index(name)
        rows = jnp.stack([sm[8 * l + j] for l in range(L)])
        return rows.reshape(L, N_GROUPS, CHUNK) if name == "b_s" else rows

    order = ["norm_g", "mem_norm_g", "w_in", "gmlp_ln_g", "gmlp_ln_b", "w_s", "b_s", "conv_w", "conv_b",
             "conv_ln_g", "conv_ln_b", "w_kv", "w_branch", "w_out", "final_norm_g"]
    loss = lax.psum(loss_part[0, 0], ("x", "y", "c"))
    return (loss, grad_x, *[leaf(nm, k) for k in range(4) for nm in order])
```

```python
import math

import jax
import jax.numpy as jnp
from jax import lax
from jax.experimental import pallas as pl
from jax.experimental.pallas import tpu as pltpu

F32 = jnp.float32
MXU_DTYPE = jnp.bfloat16
ACT_DTYPE = jnp.bfloat16
GRAD_DTYPE = jnp.bfloat16

D = 1024
N_SEG = 11
N_IN = N_SEG * D
NDEV = 8
CHUNK = 128
N_GROUPS = 8
CONV_K = 31
HALO = 32
LANES = 128
STRIP = 32
HEADS = 4
HEAD_DIM = D // HEADS
RMS_EPS = 1e-6
LN_EPS = 1e-5
ADAM_LR, ADAM_B1, ADAM_B2, ADAM_EPS, ADAM_WD, ADAM_STEP = 0.001, 0.9, 0.999, 1e-08, 0.01, 10
SEG_AU, SEG_AV, SEG_AG, SEG_BA, SEG_BB, SEG_BG, SEG_CQ, SEG_CG, SEG_M = 0, 1, 2, 3, 4, 5, 6, 7, 8

VMEM_LIMIT = 60 * 1024 * 1024
MESH = pl.DeviceIdType.MESH
NT_DIMS = (((1,), (1,)), ((), ()))
TN_DIMS = (((0,), (0,)), ((), ()))


def _params(*sem):
    return pltpu.CompilerParams(dimension_semantics=sem, vmem_limit_bytes=VMEM_LIMIT)


def _tile(n, want):
    t = min(n, want)
    assert n % t == 0, (n, want)
    return t


def _mx(v):
    return v.astype(MXU_DTYPE)


def _gelu(x):
    t = jnp.tanh(0.7978845608028654 * (x + 0.044715 * x * x * x))
    return 0.5 * x * (1.0 + t), t


def _gelu_grad(x, t):
    return 0.5 * (1.0 + t) + 0.5 * x * (1.0 - t * t) * 0.7978845608028654 * (1.0 + 3.0 * 0.044715 * x * x)


def _silu_grad(x, s):
    return s * (1.0 + x * (1.0 - s))


def _ln_stats(v):
    mu = jnp.mean(v, axis=-1, keepdims=True)
    vc = v - mu
    rstd = lax.rsqrt(jnp.mean(vc * vc, axis=-1, keepdims=True) + LN_EPS)
    return vc * rstd, rstd


def _ln_grad(dy, g, vhat, rstd):
    dvh = dy * g
    return rstd * (dvh - jnp.mean(dvh, axis=-1, keepdims=True)
                   - vhat * jnp.mean(dvh * vhat, axis=-1, keepdims=True))


def _rowsum(v):
    return jnp.sum(v, axis=0, keepdims=True)


def _coords():
    return lax.axis_index("x"), lax.axis_index("y"), lax.axis_index("c")


def _flip(pos, d):
    x, y, c = pos
    return (1 - x if d & 4 else x, 1 - y if d & 2 else y, 1 - c if d & 1 else c)


def _slot(pos):
    return 4 * pos[0] + 2 * pos[1] + pos[2]


CHIP_FLIPS = (4, 2, 6)


class _Gather:
    def __init__(self, arrays, mid_frac=0.8):
        self.arrays = list(arrays)
        self.n = n = len(arrays)
        self.mid_frac = mid_frac
        self.out_shape = [jax.ShapeDtypeStruct((NDEV,) + a.shape, a.dtype) for a in arrays]
        self.scratch = [pltpu.SemaphoreType.DMA((n, 7)), pltpu.SemaphoreType.DMA((n, 7)),
                        pltpu.SemaphoreType.DMA((n,))]

    def _copy(self, refs, i, k, block, to, own=False):
        ins, outs, (send, recv, _) = refs
        slot = outs[i].at[_slot(block)]
        return pltpu.make_async_remote_copy(
            src_ref=ins[i] if own else slot, dst_ref=slot, send_sem=send.at[i, k], recv_sem=recv.at[i, k],
            device_id=to, device_id_type=MESH)

    def _local(self, refs, i):
        ins, outs, (_, _, loc) = refs
        return pltpu.make_async_copy(ins[i], outs[i].at[_slot(_coords())], loc.at[i])

    def _first(self, refs, i, k):
        me = _coords()
        return self._copy(refs, i, k, me, _flip(me, ((1,) + CHIP_FLIPS)[k]), own=True)

    def _passed(self, refs, i, j):
        me = _coords()
        return self._copy(refs, i, 4 + j, _flip(me, CHIP_FLIPS[j]), _flip(me, 1))

    def start(self, refs):
        for i in range(self.n):
            self._local(refs, i).start()
        for k in range(4):
            for i in range(self.n):
                self._first(refs, i, k).start()

    def forward(self, refs):
        me = _coords()
        for j, d in enumerate(CHIP_FLIPS):
            for i in range(self.n):
                self._copy(refs, i, 1 + j, _flip(me, d), me).wait_recv()
                self._passed(refs, i, j).start()

    def finish(self, refs):
        me = _coords()
        sib = _flip(me, 1)
        for i in range(self.n):
            self._copy(refs, i, 0, sib, me).wait_recv()
        for j, d in enumerate(CHIP_FLIPS):
            for i in range(self.n):
                self._copy(refs, i, 4 + j, _flip(sib, d), me).wait_recv()
        for i in range(self.n):
            for k in range(4):
                self._first(refs, i, k).wait_send()
            for j in range(3):
                self._passed(refs, i, j).wait_send()
            self._local(refs, i).wait()


class _Scatter:
    def __init__(self, arrays):
        self.arrays = list(arrays)
        self.n = n = len(arrays)
        self.mid_frac = None
        self.out_shape = [jax.ShapeDtypeStruct(a.shape, a.dtype) for a in arrays]
        self.scratch = [pltpu.SemaphoreType.DMA((n, 7)), pltpu.SemaphoreType.DMA((n, 7)),
                        pltpu.SemaphoreType.DMA((n,))]

    def _copy(self, refs, i, d, landing):
        ins, outs, (send, recv, _) = refs
        me = _coords()
        peer = _flip(me, d)
        return pltpu.make_async_remote_copy(
            src_ref=ins[i].at[_slot(peer)], dst_ref=outs[i].at[_slot(peer) if landing else _slot(me)],
            send_sem=send.at[i, d - 1], recv_sem=recv.at[i, d - 1], device_id=peer, device_id_type=MESH)

    def _local(self, refs, i):
        ins, outs, (_, _, loc) = refs
        me = _slot(_coords())
        return pltpu.make_async_copy(ins[i].at[me], outs[i].at[me], loc.at[i])

    def start(self, refs):
        for i in range(self.n):
            self._local(refs, i).start()
        for d in range(1, NDEV):
            for i in range(self.n):
                self._copy(refs, i, d, False).start()

    def forward(self, refs):
        pass

    def finish(self, refs):
        for d in range(1, NDEV):
            for i in range(self.n):
                self._copy(refs, i, d, True).wait_recv()
        for d in range(1, NDEV):
            for i in range(self.n):
                self._copy(refs, i, d, False).wait_send()
        for i in range(self.n):
            self._local(refs, i).wait()


class _Both:
    def __init__(self, a, b):
        self.parts = (a, b)
        self.arrays = a.arrays + b.arrays
        self.n = a.n + b.n
        self.mid_frac = a.mid_frac if a.mid_frac is not None else b.mid_frac
        self.out_shape = a.out_shape + b.out_shape
        self.scratch = a.scratch + b.scratch

    def _each(self, refs):
        ins, outs, sems = refs
        na, ns = self.parts[0].n, len(self.parts[0].scratch)
        return ((self.parts[0], (ins[:na], outs[:na], sems[:ns])), (self.parts[1], (ins[na:], outs[na:], sems[ns:])))

    def start(self, refs):
        for part, r in self._each(refs):
            part.start(r)

    def forward(self, refs):
        for part, r in self._each(refs):
            part.forward(r)

    def finish(self, refs):
        for part, r in self._each(refs):
            part.finish(r)


def _call(body, name, grid, in_specs, out_specs, out_shape, scratch, args, comm=None, start_frac=0.0):
    params = _params(*(["arbitrary"] * len(grid)))
    if comm is None:
        outs = pl.pallas_call(
            body, name=name, grid=grid, in_specs=in_specs, out_specs=out_specs, out_shape=out_shape,
            scratch_shapes=scratch, compiler_params=params)(*args)
        return list(outs), []
    n_in, n_out, n_scr, k = len(in_specs), len(out_specs), len(scratch), comm.n
    nsteps = math.prod(grid) if grid else 1
    first = int(nsteps * start_frac)
    mid = max(first, min(nsteps - 1, int(nsteps * comm.mid_frac))) if comm.mid_frac is not None else None

    def hosted(*refs):
        ins, refs = refs[:n_in], refs[n_in:]
        cins, refs = refs[:k], refs[k:]
        outs, refs = refs[:n_out], refs[n_out:]
        couts, refs = refs[:k], refs[k:]
        scr, sems = refs[:n_scr], refs[n_scr:]
        crefs = (cins, couts, sems)
        if nsteps == 1:
            comm.start(crefs)
            body(*ins, *outs, *scr)
            comm.forward(crefs)
            comm.finish(crefs)
            return
        step = pl.program_id(0)
        for a in range(1, len(grid)):
            step = step * grid[a] + pl.program_id(a)
        pl.when(step == first)(lambda: comm.start(crefs))
        if mid is not None:
            pl.when(step == mid)(lambda: comm.forward(crefs))
        body(*ins, *outs, *scr)
        pl.when(step == nsteps - 1)(lambda: comm.finish(crefs))

    any_spec = pl.BlockSpec(memory_space=pl.ANY)
    outs = pl.pallas_call(
        hosted, name=name, grid=grid,
        in_specs=list(in_specs) + [any_spec] * k, out_specs=list(out_specs) + [any_spec] * k,
        out_shape=list(out_shape) + comm.out_shape, scratch_shapes=list(scratch) + comm.scratch,
        compiler_params=params)(*args, *comm.arrays)
    return list(outs[:n_out]), list(outs[n_out:])


def _rms_matmul(x, g, w, name, comm=None):
    T = x.shape[0]
    nb = w.shape[2]
    tT = _tile(T, 1024)
    per = 2

    def body(x_ref, g_ref, w_ref, z_ref, h_ref):
        @pl.when(pl.program_id(1) == 0)
        def _():
            xf = x_ref[...]
            r = lax.rsqrt(jnp.mean(xf * xf, axis=-1, keepdims=True) + RMS_EPS)
            h_ref[...] = (xf * r * g_ref[...]).astype(h_ref.dtype)

        for j in range(per):
            z_ref[:, j * nb:(j + 1) * nb] = jnp.dot(h_ref[...], w_ref[j], preferred_element_type=F32
                                                    ).astype(z_ref.dtype)

    return _call(
        body, name, (T // tT, NDEV // per),
        [pl.BlockSpec((tT, D), lambda t, n: (t, 0)),
         pl.BlockSpec((1, D), lambda t, n: (0, 0)),
         pl.BlockSpec((per, D, nb), lambda t, n: (n, 0, 0))],
        [pl.BlockSpec((tT, per * nb), lambda t, n: (t, n)),
         pl.BlockSpec((tT, D), lambda t, n: (t, 0))],
        [jax.ShapeDtypeStruct((T, NDEV * nb), ACT_DTYPE), jax.ShapeDtypeStruct((T, D), MXU_DTYPE)],
        [], (x, g.reshape(1, D), w), comm)


ARRIVAL = (0, 1) + CHIP_FLIPS + tuple(d ^ 1 for d in CHIP_FLIPS)


def _rms_matmul_gathering(x, g, w_shard, name, comm=None):
    T = x.shape[0]
    nb = w_shard.shape[1]
    tT = _tile(T, 1024)
    nT = T // tT

    def body(x_ref, g_ref, wsh_ref, z_ref, h_ref, wfull_ref, h_all, wbuf, zbuf, fetch_sems, z_sems, send_sems,
             recv_sems, own_sem):
        n, t = pl.program_id(0), pl.program_id(1)
        step = n * nT + t
        me = _coords()
        sib = _flip(me, 1)

        def remote(k, block, to, own=False):
            slot = wfull_ref.at[_slot(block)]
            return pltpu.make_async_remote_copy(
                src_ref=wsh_ref if own else slot, dst_ref=slot, send_sem=send_sems.at[k], recv_sem=recv_sems.at[k],
                device_id=to, device_id_type=MESH)

        def first(k):
            return remote(k, me, _flip(me, ARRIVAL[1 + k]), own=True)

        def passed(j):
            return remote(4 + j, _flip(me, CHIP_FLIPS[j]), sib)

        own_copy = pltpu.make_async_copy(wsh_ref, wfull_ref.at[_slot(me)], own_sem)

        def fetch(src, nn):
            return pltpu.make_async_copy(src, wbuf.at[nn % 2], fetch_sems.at[nn % 2])

        @pl.when(step == 0)
        def _():
            own_copy.start()
            for k in range(3):
                first(k).start()
            fetch(wsh_ref, 0).start()

        @pl.when(t == 0)
        def _():
            fetch(wfull_ref.at[0], n).wait()

        @pl.when(n == 0)
        def _():
            xf = x_ref[...]
            r = lax.rsqrt(jnp.mean(xf * xf, axis=-1, keepdims=True) + RMS_EPS)
            h = (xf * r * g_ref[...]).astype(h_ref.dtype)
            h_ref[...] = h
            h_all[t] = h

        def z_copy(s, col):
            return pltpu.make_async_copy(
                zbuf.at[s % 2], z_ref.at[pl.ds(pl.multiple_of(t * tT, tT), tT), pl.ds(col * nb, nb)],
                z_sems.at[s % 2])

        @pl.when(step >= 2)
        def _():
            z_copy(step, 0).wait()

        d = sum(jnp.where(n == nn, ARRIVAL[nn], 0) for nn in range(NDEV))
        col = _slot((me[0] ^ ((d >> 2) & 1), me[1] ^ ((d >> 1) & 1), me[2] ^ (d & 1)))
        zbuf[step % 2] = jnp.dot(h_all[t], wbuf[n % 2], preferred_element_type=F32).astype(zbuf.dtype)
        z_copy(step, col).start()

        for nn in range(1, NDEV):
            @pl.when((n == nn - 1) & (t == nT - 1))
            def _(nn=nn):
                block = _flip(me, ARRIVAL[nn])
                if nn == 1:
                    remote(0, sib, me).wait_recv()
                elif nn < 5:
                    if nn == 2:
                        first(3).start()
                    remote(nn - 1, block, me).wait_recv()
                    passed(nn - 2).start()
                else:
                    remote(nn - 1, block, me).wait_recv()
                fetch(wfull_ref.at[_slot(block)], nn).start()

        @pl.when(step == NDEV * nT - 1)
        def _():
            z_copy(step - 1, 0).wait()
            z_copy(step, 0).wait()
            for k in range(4):
                first(k).wait_send()
            for j in range(3):
                passed(j).wait_send()
            own_copy.wait()

    keep = lambda n, t: (jnp.where(n == 0, t, nT - 1), 0)
    any_spec = pl.BlockSpec(memory_space=pl.ANY)
    return _call(
        body, name, (NDEV, nT),
        [pl.BlockSpec((tT, D), keep), pl.BlockSpec((1, D), lambda n, t: (0, 0)), any_spec],
        [any_spec, pl.BlockSpec((tT, D), keep), any_spec],
        [jax.ShapeDtypeStruct((T, NDEV * nb), ACT_DTYPE), jax.ShapeDtypeStruct((T, D), MXU_DTYPE),
         jax.ShapeDtypeStruct((NDEV,) + w_shard.shape, w_shard.dtype)],
        [pltpu.VMEM((nT, tT, D), MXU_DTYPE), pltpu.VMEM((2, D, nb), w_shard.dtype),
         pltpu.VMEM((2, tT, nb), ACT_DTYPE), pltpu.SemaphoreType.DMA((2,)), pltpu.SemaphoreType.DMA((2,)),
         pltpu.SemaphoreType.DMA((7,)), pltpu.SemaphoreType.DMA((7,)), pltpu.SemaphoreType.DMA],
        (x, g.reshape(1, D), w_shard), comm, start_frac=0.5)


def _rms_matmul_bwd(dz, w, x, g, dxo, name, comm=None):
    T = x.shape[0]
    nb = w.shape[2]
    tT = _tile(T, 256)
    per = 8
    steps = NDEV // per

    def body(dz_ref, w_ref, x_ref, g_ref, dxo_ref, dx_ref, dg_ref, acc):
        t, n = pl.program_id(0), pl.program_id(1)

        @pl.when((n == 0) & (t == 0))
        def _():
            dg_ref[...] = jnp.zeros_like(dg_ref)

        part = None
        for j in range(per):
            d = lax.dot_general(_mx(dz_ref[:, j * nb:(j + 1) * nb]), w_ref[j], NT_DIMS, preferred_element_type=F32)
            part = d if part is None else part + d

        @pl.when(n == 0)
        def _():
            acc[...] = part

        @pl.when(n > 0)
        def _():
            acc[...] += part

        @pl.when(n == steps - 1)
        def _():
            xf = x_ref[...]
            r = lax.rsqrt(jnp.mean(xf * xf, axis=-1, keepdims=True) + RMS_EPS)
            xh = xf * r
            dh = acc[...]
            dxh = dh * g_ref[...]
            dx_ref[...] = dxo_ref[...] + r * (dxh - xh * jnp.mean(dxh * xh, axis=-1, keepdims=True))
            dg_ref[...] += _rowsum(dh * xh)

    return _call(
        body, name, (T // tT, steps),
        [pl.BlockSpec((tT, per * nb), lambda t, n: (t, n)),
         pl.BlockSpec((per, D, nb), lambda t, n: (n, 0, 0)),
         pl.BlockSpec((tT, D), lambda t, n: (t, 0)),
         pl.BlockSpec((1, D), lambda t, n: (0, 0)),
         pl.BlockSpec((tT, D), lambda t, n: (t, 0))],
        [pl.BlockSpec((tT, D), lambda t, n: (t, 0)),
         pl.BlockSpec((1, D), lambda t, n: (0, 0))],
        [jax.ShapeDtypeStruct((T, D), F32), jax.ShapeDtypeStruct((1, D), F32)],
        [pltpu.VMEM((tT, D), F32)], (dz, w, x, g.reshape(1, D), dxo), comm)


def _atb(a, b, nblk, name, comm=None):
    G, T, M = a.shape
    N = b.shape[2]
    nb = N // nblk
    tk = _tile(T, 2048)
    nk = T // tk

    def body(a_ref, b_ref, o_ref, acc):
        k = pl.program_id(1)

        @pl.when(k == 0)
        def _():
            acc[...] = jnp.zeros_like(acc)

        acc[...] += lax.dot_general(_mx(a_ref[...]), _mx(b_ref[...]), TN_DIMS, preferred_element_type=F32)

        @pl.when(k == nk - 1)
        def _():
            o_ref[...] = acc[...].astype(o_ref.dtype)

    outs, couts = _call(
        body, name, (G * nblk, nk),
        [pl.BlockSpec((None, tk, M), lambda n, k: (n // nblk, k, 0)),
         pl.BlockSpec((None, tk, nb), lambda n, k: (n // nblk, k, n % nblk))],
        [pl.BlockSpec((None, M, nb), lambda n, k: (n, 0, 0))],
        [jax.ShapeDtypeStruct((G * nblk, M, nb), GRAD_DTYPE)],
        [pltpu.VMEM((M, nb), F32)], (a, b), comm)
    return outs[0], couts


def _loss_rows(xf, t_ref, g_ref, loss_ref, dx_ref, dg_ref):
    @pl.when(pl.program_id(0) == 0)
    def _():
        loss_ref[...] = jnp.zeros_like(loss_ref)
        dg_ref[...] = jnp.zeros_like(dg_ref)

    r = lax.rsqrt(jnp.mean(xf * xf, axis=-1, keepdims=True) + RMS_EPS)
    xh = xf * r
    err = xh * g_ref[...] - t_ref[...]
    loss_ref[...] += 0.5 * jnp.sum(jnp.mean(err * err, axis=-1, keepdims=True), axis=0, keepdims=True)
    dy = err * (1.0 / D)
    dxh = dy * g_ref[...]
    dx_ref[...] = r * (dxh - xh * jnp.mean(dxh * xh, axis=-1, keepdims=True))
    dg_ref[...] += _rowsum(dy * xh)


def _spatial_gate(wm_ref, bst_ref, vb_ref, sv_ref, n_chunks):
    for c in range(n_chunks):
        rows = slice(c * CHUNK, (c + 1) * CHUNK)
        for g in range(N_GROUPS):
            cols = slice(g * CHUNK, (g + 1) * CHUNK)
            sv_ref[rows, cols] = (jnp.dot(wm_ref[g], vb_ref[rows, cols], preferred_element_type=F32)
                                  + bst_ref[:, g:g + 1])


def _lane_loop(fn):
    def step(i, carry):
        fn(pl.ds(pl.multiple_of(i * LANES, LANES), LANES))
        return carry

    lax.fori_loop(0, D // LANES, step, 0)


def _shifted_copies(buf, sh, n):
    for s in range(1, 8):
        sh[s - 1, 0:n, :] = buf[s:s + n, :]


def _window(buf, sh, base, off, lanes):
    a, s = divmod(off, 8)
    src = buf if s == 0 else sh.at[s - 1]
    return src[base + 8 * a:base + 8 * a + 8, lanes]


def _softmax_rows(s):
    e = jnp.exp(s - jnp.max(s, axis=-1, keepdims=True))
    return e / jnp.sum(e, axis=-1, keepdims=True)


def _branch_fwd(z, kv, wm, bst, ln_a, cw, cvec, name, comm=None):
    T = z.shape[0]
    tT = _tile(T, 256)
    n_chunks = tT // CHUNK

    def body(z_ref, kv_ref, wm_ref, bst_ref, lna_ref, cw_ref, cvec_ref, br_ref, c_ref, gbuf, gsh, vb, ua):
        @pl.when(pl.program_id(0) == 0)
        def _():
            gbuf[0:HALO, :] = jnp.zeros((HALO, D), F32)

        def seg(s):
            return z_ref[:, s * D:(s + 1) * D].astype(F32)

        u, _ = _gelu(seg(SEG_AU))
        zg = seg(SEG_AG)
        ua[...] = u * (zg * jax.nn.sigmoid(zg))
        gv, _ = _gelu(seg(SEG_AV))
        vhat, _ = _ln_stats(gv)
        vb[...] = _mx(vhat * lna_ref[0:1, :] + lna_ref[1:2, :])
        for c in range(n_chunks):
            rows = slice(c * CHUNK, (c + 1) * CHUNK)
            for g in range(N_GROUPS):
                cols = slice(g * CHUNK, (g + 1) * CHUNK)
                sv = jnp.dot(wm_ref[g], vb[rows, cols], preferred_element_type=F32) + bst_ref[:, g:g + 1]
                br_ref[0, rows, cols] = (sv * ua[rows, cols]).astype(br_ref.dtype)

        gbuf[HALO:HALO + tT, :] = seg(SEG_BA) * jax.nn.sigmoid(seg(SEG_BB))
        _shifted_copies(gbuf, gsh, tT + HALO - 8)
        def conv_lanes(lanes):
            taps = [jnp.broadcast_to(cw_ref[k:k + 1, lanes], (8, LANES)) for k in range(CONV_K)]
            bias = jnp.broadcast_to(cvec_ref[0:1, lanes], (8, LANES))
            for base in range(0, tT, 8):
                acc = [bias, None, None, None]
                for k in range(CONV_K):
                    term = taps[k] * _window(gbuf, gsh, base, k + HALO - CONV_K + 1, lanes)
                    acc[k % 4] = term if acc[k % 4] is None else acc[k % 4] + term
                c_ref[base:base + 8, lanes] = (acc[0] + acc[1]) + (acc[2] + acc[3])

        _lane_loop(conv_lanes)
        gbuf[0:HALO, :] = gbuf[tT:tT + HALO, :]
        chat, _ = _ln_stats(c_ref[...])
        cl = chat * cvec_ref[1:2, :] + cvec_ref[2:3, :]
        zg = seg(SEG_BG)
        br_ref[1] = (cl * jax.nn.sigmoid(cl) * (zg * jax.nn.sigmoid(zg))).astype(br_ref.dtype)

        for h in range(HEADS):
            cols = slice(h * HEAD_DIM, (h + 1) * HEAD_DIM)
            q = _mx(z_ref[:, SEG_CQ * D + h * HEAD_DIM:SEG_CQ * D + (h + 1) * HEAD_DIM])
            s = lax.dot_general(q, kv_ref[:, cols], NT_DIMS, preferred_element_type=F32)
            p = _softmax_rows(s * (1.0 / math.sqrt(HEAD_DIM)))
            att = jnp.dot(_mx(p), kv_ref[:, D + h * HEAD_DIM:D + (h + 1) * HEAD_DIM], preferred_element_type=F32)
            zg = z_ref[:, SEG_CG * D + h * HEAD_DIM:SEG_CG * D + (h + 1) * HEAD_DIM].astype(F32)
            br_ref[2, :, cols] = (att * (zg * jax.nn.sigmoid(zg))).astype(br_ref.dtype)

    full = lambda shape: pl.BlockSpec(shape, lambda t: (0,) * len(shape))
    return _call(
        body, name, (T // tT,),
        [pl.BlockSpec((tT, SEG_M * D), lambda t: (t, 0)),
         full(kv.shape), full(wm.shape), full(bst.shape), full(ln_a.shape), full(cw.shape), full(cvec.shape)],
        [pl.BlockSpec((3, tT, D), lambda t: (0, t, 0)), pl.BlockSpec((tT, D), lambda t: (t, 0))],
        [jax.ShapeDtypeStruct((3, T, D), MXU_DTYPE), jax.ShapeDtypeStruct((T, D), F32)],
        [pltpu.VMEM((tT + HALO, D), F32), pltpu.VMEM((7, tT + HALO - 8, D), F32),
         pltpu.VMEM((tT, D), MXU_DTYPE), pltpu.VMEM((tT, D), F32)],
        (z, kv, wm, bst, ln_a, cw, cvec), comm)


def _branch_bwd(z, dbr, c, dzm, kv, wm, wmt, bst, ln_a, cw, cvec, name, comm=None):
    T = z.shape[0]
    M = kv.shape[0]
    tT = _tile(T, 128)
    nT = T // tT
    n_chunks = tT // CHUNK

    def body(z_ref, dbr_ref, c_ref, dzm_ref, kv_ref, wm_ref, wmt_ref, bst_ref, lna_ref,
             cw_ref, cvec_ref, dz_ref, vecg_ref, dbst_ref, dws_ref, dcw_ref, dkv_ref,
             gbuf, dcbuf, vb, dsvb, sv, dvbuf, dcsh, dglu, vh, gq, dcw8, dcw_step):
        i = pl.program_id(0)

        @pl.when(i == 0)
        def _():
            vecg_ref[...] = jnp.zeros_like(vecg_ref)
            dbst_ref[...] = jnp.zeros_like(dbst_ref)
            dws_ref[...] = jnp.zeros_like(dws_ref)
            dcw8[...] = jnp.zeros_like(dcw8)
            dkv_ref[...] = jnp.zeros_like(dkv_ref)
            dcbuf[tT:tT + HALO, :] = jnp.zeros((HALO, D), F32)

        strips = [slice(r0, r0 + STRIP) for r0 in range(0, tT, STRIP)]

        def seg(r, s):
            return z_ref[r, s * D:(s + 1) * D].astype(F32)

        def put(r, s, val):
            dz_ref[r, s * D:(s + 1) * D] = val.astype(dz_ref.dtype)

        for r in strips:
            zv = seg(r, SEG_AV)
            gv, tv = _gelu(zv)
            vhat, rstd = _ln_stats(gv)
            vb[r, :] = _mx(vhat * lna_ref[0:1, :] + lna_ref[1:2, :])
            vh[r, :] = vhat
            gq[r, :] = rstd * _gelu_grad(zv, tv)
        _spatial_gate(wm_ref, bst_ref, vb, sv, n_chunks)
        for r in strips:
            zu, zg = seg(r, SEG_AU), seg(r, SEG_AG)
            u, tu = _gelu(zu)
            sg = jax.nn.sigmoid(zg)
            d_a = dbr_ref[0, r, :].astype(F32)
            put(r, SEG_AU, d_a * sv[r, :] * (zg * sg) * _gelu_grad(zu, tu))
            put(r, SEG_AG, d_a * u * sv[r, :] * _silu_grad(zg, sg))
            dsv = d_a * u * (zg * sg)
            dsvb[r, :] = _mx(dsv)
            in_chunk = slice(r.start % CHUNK, r.start % CHUNK + STRIP)
            for g in range(N_GROUPS):
                dbst_ref[in_chunk, g:g + 1] += jnp.sum(dsv[:, g * CHUNK:(g + 1) * CHUNK], axis=-1, keepdims=True)
        tril = (lax.broadcasted_iota(jnp.int32, (CHUNK, CHUNK), 0)
                >= lax.broadcasted_iota(jnp.int32, (CHUNK, CHUNK), 1))
        for g in range(N_GROUPS):
            cols = slice(g * CHUNK, (g + 1) * CHUNK)
            for cc in range(n_chunks):
                rows = slice(cc * CHUNK, (cc + 1) * CHUNK)
                dws = lax.dot_general(dsvb[rows, cols], vb[rows, cols], NT_DIMS, preferred_element_type=F32)
                dws_ref[g] += jnp.where(tril, dws, 0.0)
                dvbuf[rows, cols] = jnp.dot(wmt_ref[g], dsvb[rows, cols], preferred_element_type=F32)
        for r in strips:
            dv, vhat = dvbuf[r, :], vh[r, :]
            vecg_ref[0:1, :] += _rowsum(dv * vhat)
            vecg_ref[1:2, :] += _rowsum(dv)
            dvh = dv * lna_ref[0:1, :]
            put(r, SEG_AV, (dvh - jnp.mean(dvh, axis=-1, keepdims=True)
                            - vhat * jnp.mean(dvh * vhat, axis=-1, keepdims=True)) * gq[r, :])

        sgb_buf = sv
        for r in strips:
            za, zg = seg(r, SEG_BA), seg(r, SEG_BG)
            sgb = jax.nn.sigmoid(seg(r, SEG_BB))
            sgb_buf[r, :] = sgb
            gbuf[r, :] = za * sgb
            chat, crstd = _ln_stats(c_ref[r, :])
            cl = chat * cvec_ref[1:2, :] + cvec_ref[2:3, :]
            scl = jax.nn.sigmoid(cl)
            sg = jax.nn.sigmoid(zg)
            d_b = dbr_ref[1, r, :].astype(F32)
            put(r, SEG_BG, d_b * (cl * scl) * _silu_grad(zg, sg))
            dcl = d_b * (zg * sg) * _silu_grad(cl, scl)
            vecg_ref[3:4, :] += _rowsum(dcl * chat)
            vecg_ref[4:5, :] += _rowsum(dcl)
            dc = _ln_grad(dcl, cvec_ref[1:2, :], chat, crstd)
            vecg_ref[2:3, :] += _rowsum(dc)
            dcbuf[r, :] = dc
        _shifted_copies(dcbuf, dcsh, tT + HALO - 8)

        def conv_grads(lanes):
            taps = [jnp.broadcast_to(cw_ref[k:k + 1, lanes], (8, LANES)) for k in range(CONV_K)]
            wsum = [None] * CONV_K
            for base in range(0, tT, 8):
                glu = gbuf[base:base + 8, lanes]
                acc = [None] * 4
                for k in range(CONV_K):
                    win = _window(dcbuf, dcsh, base, CONV_K - 1 - k, lanes)
                    term = taps[k] * win
                    acc[k % 4] = term if acc[k % 4] is None else acc[k % 4] + term
                    term = glu * win
                    wsum[k] = term if wsum[k] is None else wsum[k] + term
                dglu[base:base + 8, lanes] = (acc[0] + acc[1]) + (acc[2] + acc[3])
            for k in range(CONV_K):
                dcw_step[8 * k:8 * k + 8, lanes] = wsum[k]

        _lane_loop(conv_grads)
        dcw8[...] += dcw_step[...]

        @pl.when(i == nT - 1)
        def _():
            for k in range(CONV_K):
                dcw_ref[k:k + 1, :] = _rowsum(dcw8[8 * k:8 * k + 8, :])
            dcw_ref[CONV_K:HALO, :] = jnp.zeros((HALO - CONV_K, D), F32)

        dcbuf[tT:tT + HALO, :] = dcbuf[0:HALO, :]
        for r in strips:
            dg, sgb = dglu[r, :], sgb_buf[r, :]
            put(r, SEG_BA, dg * sgb)
            put(r, SEG_BB, dg * seg(r, SEG_BA) * sgb * (1.0 - sgb))

        scale = 1.0 / math.sqrt(HEAD_DIM)
        for h in range(HEADS):
            cols = slice(h * HEAD_DIM, (h + 1) * HEAD_DIM)
            qcols = slice(SEG_CQ * D + h * HEAD_DIM, SEG_CQ * D + (h + 1) * HEAD_DIM)
            gcols = slice(SEG_CG * D + h * HEAD_DIM, SEG_CG * D + (h + 1) * HEAD_DIM)
            vcols = slice(D + h * HEAD_DIM, D + (h + 1) * HEAD_DIM)
            q = _mx(z_ref[:, qcols])
            kh, vh = kv_ref[:, cols], kv_ref[:, vcols]
            p = _softmax_rows(lax.dot_general(q, kh, NT_DIMS, preferred_element_type=F32) * scale)
            pb = _mx(p)
            att = jnp.dot(pb, vh, preferred_element_type=F32)
            zg = z_ref[:, gcols].astype(F32)
            sg = jax.nn.sigmoid(zg)
            d_c = dbr_ref[2, :, cols].astype(F32)
            dz_ref[:, gcols] = (d_c * att * _silu_grad(zg, sg)).astype(dz_ref.dtype)
            datt = _mx(d_c * (zg * sg))
            dp = lax.dot_general(datt, vh, NT_DIMS, preferred_element_type=F32)
            dkv_ref[:, vcols] += lax.dot_general(pb, datt, TN_DIMS, preferred_element_type=F32)
            ds = _mx(p * (dp - jnp.sum(dp * p, axis=-1, keepdims=True)) * scale)
            dz_ref[:, qcols] = jnp.dot(ds, kh, preferred_element_type=F32).astype(dz_ref.dtype)
            dkv_ref[:, cols] += lax.dot_general(ds, q, TN_DIMS, preferred_element_type=F32)

        dz_ref[:, SEG_M * D:] = dzm_ref[...].astype(dz_ref.dtype)

    rev = lambda i: nT - 1 - i
    full = lambda shape: pl.BlockSpec(shape, lambda i: (0,) * len(shape))
    return _call(
        body, name, (nT,),
        [pl.BlockSpec((tT, SEG_M * D), lambda i: (rev(i), 0)),
         pl.BlockSpec((3, tT, D), lambda i: (0, rev(i), 0)),
         pl.BlockSpec((tT, D), lambda i: (rev(i), 0)),
         pl.BlockSpec((tT, 3 * D), lambda i: (rev(i), 0)),
         full(kv.shape), full(wm.shape), full(wmt.shape), full(bst.shape), full(ln_a.shape),
         full(cw.shape), full(cvec.shape)],
        [pl.BlockSpec((tT, N_IN), lambda i: (rev(i), 0)),
         full((8, D)), full((CHUNK, N_GROUPS)), full((N_GROUPS, CHUNK, CHUNK)), full((HALO, D)),
         full((M, 2 * D))],
        [jax.ShapeDtypeStruct((T, N_IN), MXU_DTYPE), jax.ShapeDtypeStruct((8, D), F32),
         jax.ShapeDtypeStruct((CHUNK, N_GROUPS), F32),
         jax.ShapeDtypeStruct((N_GROUPS, CHUNK, CHUNK), F32),
         jax.ShapeDtypeStruct((HALO, D), F32), jax.ShapeDtypeStruct((M, 2 * D), F32)],
        [pltpu.VMEM((tT, D), F32), pltpu.VMEM((tT + HALO, D), F32),
         pltpu.VMEM((tT, D), MXU_DTYPE), pltpu.VMEM((tT, D), MXU_DTYPE),
         pltpu.VMEM((tT, D), F32), pltpu.VMEM((tT, D), F32),
         pltpu.VMEM((7, tT + HALO - 8, D), F32),
         pltpu.VMEM((tT, D), F32), pltpu.VMEM((tT, D), F32), pltpu.VMEM((tT, D), F32),
         pltpu.VMEM((CONV_K * 8, D), F32), pltpu.VMEM((CONV_K * 8, D), F32)],
        (z, dbr, c, dzm, kv, wm, wmt, bst, ln_a, cw, cvec), comm)


def _merge_fwd(br, z, x, wb, wo, name, head=None):
    T = x.shape[0]
    tT = _tile(T, 512)

    def body(br_ref, z0, z1, z2, x_ref, wb_ref, wo_ref, *rest):
        mg_ref, pj_ref = rest[-2:]
        merged = jnp.zeros((tT, D), F32)
        for n, zm in enumerate((z0, z1, z2)):
            proj = jnp.dot(br_ref[n], wb_ref[:, n].reshape(D, D), preferred_element_type=F32)
            pj_ref[n] = proj.astype(pj_ref.dtype)
            merged = merged + jax.nn.sigmoid(zm[...].astype(F32)) * proj
        mg_ref[...] = merged.astype(mg_ref.dtype)
        xn = x_ref[...] + jnp.dot(_mx(merged), wo_ref[...].reshape(D, D), preferred_element_type=F32)
        if head is None:
            rest[0][...] = xn
        else:
            _loss_rows(xn, *rest[:5])

    row = pl.BlockSpec((tT, D), lambda t: (t, 0))
    vec = pl.BlockSpec((1, D), lambda t: (0, 0))
    zspec = lambda n: pl.BlockSpec((tT, D), lambda t: (t, SEG_M + n))
    in_specs = [pl.BlockSpec((3, tT, D), lambda t: (0, t, 0)), zspec(0), zspec(1), zspec(2), row,
                pl.BlockSpec(wb.shape, lambda t: (0, 0, 0, 0)), pl.BlockSpec(wo.shape, lambda t: (0, 0, 0))]
    out_specs = [row, pl.BlockSpec((3, tT, D), lambda t: (0, t, 0))]
    out_shape = [jax.ShapeDtypeStruct((T, D), MXU_DTYPE), jax.ShapeDtypeStruct((3, T, D), ACT_DTYPE)]
    args = (br, z, z, z, x, wb, wo)
    if head is None:
        out_specs = [row] + out_specs
        out_shape = [jax.ShapeDtypeStruct((T, D), F32)] + out_shape
    else:
        in_specs += [row, vec]
        args += (head[0], head[1].reshape(1, D))
        out_specs = [pl.BlockSpec((1, 1), lambda t: (0, 0)), row, vec] + out_specs
        out_shape = [jax.ShapeDtypeStruct((1, 1), F32), jax.ShapeDtypeStruct((T, D), F32),
                     jax.ShapeDtypeStruct((1, D), F32)] + out_shape
    return pl.pallas_call(body, name=name, grid=(T // tT,), in_specs=in_specs, out_specs=out_specs,
                          out_shape=out_shape, compiler_params=_params("arbitrary"))(*args)


def _merge_bwd(dxo, proj, z, wb, wo, name):
    T = dxo.shape[0]
    tT = _tile(T, 512)

    def body(dxo_ref, pj_ref, z0, z1, z2, wb_ref, wo_ref, dpj_ref, dbr_ref, dzm_ref):
        dmerged = lax.dot_general(_mx(dxo_ref[...]), wo_ref[...].reshape(D, D), NT_DIMS,
                                  preferred_element_type=F32)
        for n, zm in enumerate((z0, z1, z2)):
            gate = jax.nn.sigmoid(zm[...].astype(F32))
            dproj = _mx(gate * dmerged)
            dpj_ref[n] = dproj
            dzm_ref[:, n * D:(n + 1) * D] = (pj_ref[n].astype(F32) * dmerged * gate * (1.0 - gate)
                                             ).astype(dzm_ref.dtype)
            dbr_ref[n] = lax.dot_general(dproj, wb_ref[:, n].reshape(D, D), NT_DIMS,
                                         preferred_element_type=F32).astype(dbr_ref.dtype)

    zspec = lambda n: pl.BlockSpec((tT, D), lambda t: (t, SEG_M + n))
    return pl.pallas_call(
        body, name=name, grid=(T // tT,),
        in_specs=[pl.BlockSpec((tT, D), lambda t: (t, 0)),
                  pl.BlockSpec((3, tT, D), lambda t: (0, t, 0)), zspec(0), zspec(1), zspec(2),
                  pl.BlockSpec(wb.shape, lambda t: (0, 0, 0, 0)),
                  pl.BlockSpec(wo.shape, lambda t: (0, 0, 0))],
        out_specs=[pl.BlockSpec((3, tT, D), lambda t: (0, t, 0)),
                   pl.BlockSpec((3, tT, D), lambda t: (0, t, 0)),
                   pl.BlockSpec((tT, 3 * D), lambda t: (t, 0))],
        out_shape=[jax.ShapeDtypeStruct((3, T, D), MXU_DTYPE), jax.ShapeDtypeStruct((3, T, D), ACT_DTYPE),
                   jax.ShapeDtypeStruct((T, 3 * D), MXU_DTYPE)],
        compiler_params=_params("parallel"),
    )(dxo, proj, z, z, z, wb, wo)


def _adamw(parts, w, m, v, name, comm=None):
    G, R, C = w.shape
    tr = 128 if R % 128 == 0 else R
    nr = R // tr
    c1 = 1.0 / (1.0 - ADAM_B1 ** ADAM_STEP)
    c2 = 1.0 / (1.0 - ADAM_B2 ** ADAM_STEP)

    def body(*refs):
        p_refs, (w_ref, m_ref, v_ref, g_out, d_out, m_out, v_out) = refs[:G], refs[G:]
        for i in range(G):
            @pl.when(pl.program_id(0) == i)
            def _(p_ref=p_refs[i]):
                g = p_ref[0].astype(F32)
                for p in range(1, NDEV):
                    g = g + p_ref[p].astype(F32)
                mn = ADAM_B1 * m_ref[...] + (1.0 - ADAM_B1) * g
                vn = ADAM_B2 * v_ref[...] + (1.0 - ADAM_B2) * (g * g)
                g_out[...] = g
                m_out[...] = mn
                v_out[...] = vn
                d_out[...] = -ADAM_LR * ((mn * c1) / (jnp.sqrt(vn * c2) + ADAM_EPS) + ADAM_WD * w_ref[...])

    def parts_spec(i):
        return pl.BlockSpec((NDEV, tr, C), lambda l, r: (0, jnp.where(l == i, r, jnp.where(l > i, nr - 1, 0)), 0))

    spec = pl.BlockSpec((None, tr, C), lambda l, r: (l, r, 0))
    return _call(
        body, name, (G, nr), [parts_spec(i) for i in range(G)] + [spec] * 3,
        [spec] * 4, [jax.ShapeDtypeStruct((G, R, C), F32)] * 4, [], (*parts, w, m, v), comm)


def kernel(x, mem, norm_g, mem_norm_g, w_in, gmlp_ln_g, gmlp_ln_b, w_s, b_s, conv_w, conv_b, conv_ln_g, conv_ln_b, w_kv, w_branch, w_out, final_norm_g, loss_target, m_norm_g, m_mem_norm_g, m_w_in, m_gmlp_ln_g, m_gmlp_ln_b, m_w_s, m_b_s, m_conv_w, m_conv_b, m_conv_ln_g, m_conv_ln_b, m_w_kv, m_w_branch, m_w_out, m_final_norm_g, v_norm_g, v_mem_norm_g, v_w_in, v_gmlp_ln_g, v_gmlp_ln_b, v_w_s, v_b_s, v_conv_w, v_conv_b, v_conv_ln_g, v_conv_ln_b, v_w_kv, v_w_branch, v_w_out, v_final_norm_g):
    L = w_in.shape[0]
    x0, mem0, tgt = x[0], mem[0], loss_target[0]
    T, M = x0.shape[0], mem0.shape[0]
    nbc = conv_w.shape[2]

    def shards(l):
        return [_mx(w_in[l]), _mx(w_kv[l]), _mx(w_branch[l]), _mx(w_out[l]), conv_w[l]]

    gather_rest = _Gather(shards(0)[1:])
    gather_upper = _Gather([a for l in range(1, L) for a in shards(l)], mid_frac=0.85) if L > 1 else None

    tril = jnp.tril(jnp.ones((CHUNK, CHUNK), bool))
    wm = [_mx(jnp.where(tril[None], w_s[l], 0.0)) for l in range(L)]
    wmt = [w.transpose(0, 2, 1) for w in wm]
    bst = [b_s[l].T for l in range(L)]
    ln_a = [jnp.stack([gmlp_ln_g[l], gmlp_ln_b[l]]) for l in range(L)]
    cvec = [jnp.stack([conv_b[l], conv_ln_g[l], conv_ln_b[l]]) for l in range(L)]

    def conv_taps(gathered):
        return jnp.pad(gathered.transpose(1, 0, 2).reshape(CONV_K, D), ((0, HALO - CONV_K), (0, 0)))

    win, wkv, wbr, wou, cwf = [], [], [], [], []
    memn, kvs, xs, saved = [], [], [x0], []
    for l in range(L):
        if l == 0:
            (z, h, w0), full = _rms_matmul_gathering(x0, norm_g[0], shards(0)[0], "inproj_fwd0", gather_rest)
            win, wkv, wbr, wou, cwf = [w0], [full[0]], [full[1]], [full[2]], [conv_taps(full[3])]
        else:
            (z, h), _ = _rms_matmul(xs[l], norm_g[l], win[l], f"inproj_fwd{l}")
        (kv, mn), _ = _rms_matmul(mem0, mem_norm_g[l], wkv[l], f"kv_fwd{l}")
        kvs.append(_mx(kv))
        memn.append(mn)
        (br, cpre), full = _branch_fwd(z, kvs[l], wm[l], bst[l], ln_a[l], cwf[l], cvec[l], f"branch_fwd{l}",
                                       gather_upper if l == 0 else None)
        for k in range(1, L if l == 0 else 0):
            f = full[5 * (k - 1):5 * k]
            win.append(f[0])
            wkv.append(f[1])
            wbr.append(f[2])
            wou.append(f[3])
            cwf.append(conv_taps(f[4]))
        if l < L - 1:
            xn, merged, proj = _merge_fwd(br, z, xs[l], wbr[l], wou[l], f"merge_fwd{l}")
            xs.append(xn)
        else:
            loss_part, dx, dfg, merged, proj = _merge_fwd(br, z, xs[l], wbr[l], wou[l], f"merge_fwd{l}",
                                                          head=(tgt, final_norm_g))
        saved.append((z, h, br, cpre, merged, proj))

    pending, recv = [("final_norm_g", dfg, True)], {}

    def flush():
        scat = [(k, a) for k, a, g in pending if not g]
        gath = [(k, a) for k, a, g in pending if g]
        pending.clear()
        comms = ([_Scatter([a for _, a in scat])] if scat else []) + ([_Gather([a for _, a in gath])] if gath else [])
        return [k for k, _ in scat + gath], comms[0] if len(comms) == 1 else _Both(*comms)

    def landed(keys, arrays):
        recv.update(zip(keys, arrays))

    for l in reversed(range(L)):
        z, h, br, cpre, merged, proj = saved[l]
        dproj, dbr, dzm = _merge_bwd(dx, proj, z, wbr[l], wou[l], f"merge_bwd{l}")
        dwb, _ = _atb(br, dproj, 1, f"dwbranch{l}")
        for n in range(3):
            pending.append((f"w_branch{l}_{n}", dwb[n].reshape(NDEV, D // NDEV, D), False))
        dwo, _ = _atb(merged[None], dx[None], 1, f"dwout{l}")
        pending.append((f"w_out{l}", dwo.reshape(NDEV, D // NDEV, D), False))
        keys, comm = flush()
        (dz, vecg, dbst, dws, dcw, dkv), got = _branch_bwd(
            z, dbr, cpre, dzm, kvs[l], wm[l], wmt[l], bst[l], ln_a[l], cwf[l], cvec[l], f"branch_bwd{l}", comm)
        landed(keys, got)
        dwk, _ = _atb(memn[l][None], dkv[None], NDEV, f"dwkv{l}")
        (_, dmg), _ = _rms_matmul_bwd(dkv, wkv[l], mem0, mem_norm_g[l], jnp.zeros((M, D), F32), f"kv_bwd{l}")
        rest = jnp.concatenate([dmg, vecg[0:2], vecg[2:5], dbst.T.reshape(1, D)], axis=0)
        pending += [(f"w_kv{l}", dwk, False),
                    (f"conv_w{l}", dcw[:CONV_K].reshape(CONV_K, NDEV, nbc).transpose(1, 0, 2), False),
                    (f"small{l}", rest, True), (f"w_s{l}", dws.reshape(N_GROUPS * CHUNK, CHUNK), True)]
        keys, comm = flush()
        dwi, got = _atb(h[None], dz[None], NDEV, f"dwin{l}", comm)
        landed(keys, got)
        pending.append((f"w_in{l}", dwi, False))
        keys, comm = flush() if l == 0 else ([], None)
        (dx, dng), got = _rms_matmul_bwd(dz, win[l], xs[l], norm_g[l], dx, f"inproj_bwd{l}", comm)
        landed(keys, got)
        pending.append((f"norm_g{l}", dng, True))
    grad_x = dx[None]

    def pack(p):
        rows = []
        for l in range(L):
            rows += [p["norm_g"][l], p["mem_norm_g"][l], p["gmlp_ln_g"][l], p["gmlp_ln_b"][l], p["conv_b"][l],
                     p["conv_ln_g"][l], p["conv_ln_b"][l], p["b_s"][l].reshape(D)]
        return jnp.stack(rows + [p["final_norm_g"]])[None]

    names = ["norm_g", "mem_norm_g", "gmlp_ln_g", "gmlp_ln_b", "conv_b", "conv_ln_g", "conv_ln_b", "b_s",
             "final_norm_g"]
    w_small = pack(dict(zip(names, [norm_g, mem_norm_g, gmlp_ln_g, gmlp_ln_b, conv_b, conv_ln_g, conv_ln_b,
                                    b_s, final_norm_g])))
    m_small = pack(dict(zip(names, [m_norm_g, m_mem_norm_g, m_gmlp_ln_g, m_gmlp_ln_b, m_conv_b, m_conv_ln_g,
                                    m_conv_ln_b, m_b_s, m_final_norm_g])))
    v_small = pack(dict(zip(names, [v_norm_g, v_mem_norm_g, v_gmlp_ln_g, v_gmlp_ln_b, v_conv_b, v_conv_ln_g,
                                    v_conv_ln_b, v_b_s, v_final_norm_g])))
    outs = {}

    def run(key, parts, w, m, v, comm=None):
        outs[key], got = _adamw(parts, w, m, v, "adamw_" + key, comm)
        return got

    keys, comm = flush()
    landed(keys, run("w_in", [recv[f"w_in{l}"] for l in range(L)], w_in, m_w_in, v_w_in, comm))
    parts_small = jnp.concatenate([recv[f"{k}{l}"] for l in range(L) for k in ("norm_g", "small")]
                                  + [recv["final_norm_g"]], axis=1)
    parts_ws = jnp.concatenate([recv[f"w_s{l}"] for l in range(L)], axis=1)
    run("w_kv", [recv[f"w_kv{l}"] for l in range(L)], w_kv, m_w_kv, v_w_kv)
    sh = (L * 3, D // NDEV, D)
    run("w_branch", [recv[f"w_branch{l}_{n}"] for l in range(L) for n in range(3)],
        w_branch.reshape(sh), m_w_branch.reshape(sh), v_w_branch.reshape(sh))
    run("w_out", [recv[f"w_out{l}"] for l in range(L)], w_out, m_w_out, v_w_out)
    run("conv_w", [recv[f"conv_w{l}"] for l in range(L)], conv_w, m_conv_w, v_conv_w)
    run("small", [parts_small], w_small, m_small, v_small)
    ws_shape = (1, L * N_GROUPS * CHUNK, CHUNK)
    run("w_s", [parts_ws], w_s.reshape(ws_shape), m_w_s.reshape(ws_shape), v_w_s.reshape(ws_shape))

    def leaf(name, k):
        if name in ("w_in", "w_kv", "w_out", "conv_w"):
            return outs[name][k]
        if name == "w_branch":
            return outs[name][k].reshape(w_branch.shape)
        if name == "w_s":
            return outs["w_s"][k].reshape(L, N_GROUPS, CHUNK, CHUNK)
        sm = outs["small"][k][0]
        if name == "final_norm_g":
            return sm[8 * L]
        j = names.index(name)
        rows = jnp.stack([sm[8 * l + j] for l in range(L)])
        return rows.reshape(L, N_GROUPS, CHUNK) if name == "b_s" else rows

    order = ["norm_g", "mem_norm_g", "w_in", "gmlp_ln_g", "gmlp_ln_b", "w_s", "b_s", "conv_w", "conv_b",
             "conv_ln_g", "conv_ln_b", "w_kv", "w_branch", "w_out", "final_norm_g"]
    loss = lax.psum(loss_part[0, 0], ("x", "y", "c"))
    return (loss, grad_x, *[leaf(nm, k) for k in range(4) for nm in order])
```

```python
import math

import jax
import jax.numpy as jnp
from jax import lax
from jax.experimental import pallas as pl
from jax.experimental.pallas import tpu as pltpu

F32 = jnp.float32
MXU_DTYPE = jnp.bfloat16
ACT_DTYPE = jnp.bfloat16
GRAD_DTYPE = jnp.bfloat16

D = 1024
N_SEG = 11
N_IN = N_SEG * D
NDEV = 8
CHUNK = 128
N_GROUPS = 8
CONV_K = 31
HALO = 32
LANES = 128
STRIP = 32
HEADS = 4
HEAD_DIM = D // HEADS
RMS_EPS = 1e-6
LN_EPS = 1e-5
ADAM_LR, ADAM_B1, ADAM_B2, ADAM_EPS, ADAM_WD, ADAM_STEP = 0.001, 0.9, 0.999, 1e-08, 0.01, 10
SEG_AU, SEG_AV, SEG_AG, SEG_BA, SEG_BB, SEG_BG, SEG_CQ, SEG_CG, SEG_M = 0, 1, 2, 3, 4, 5, 6, 7, 8

VMEM_LIMIT = 60 * 1024 * 1024
MESH = pl.DeviceIdType.MESH
NT_DIMS = (((1,), (1,)), ((), ()))
TN_DIMS = (((0,), (0,)), ((), ()))


def _params(*sem):
    return pltpu.CompilerParams(dimension_semantics=sem, vmem_limit_bytes=VMEM_LIMIT)


def _tile(n, want):
    t = min(n, want)
    assert n % t == 0, (n, want)
    return t


def _mx(v):
    return v.astype(MXU_DTYPE)


def _gelu(x):
    t = jnp.tanh(0.7978845608028654 * (x + 0.044715 * x * x * x))
    return 0.5 * x * (1.0 + t), t


def _gelu_grad(x, t):
    return 0.5 * (1.0 + t) + 0.5 * x * (1.0 - t * t) * 0.7978845608028654 * (1.0 + 3.0 * 0.044715 * x * x)


def _silu_grad(x, s):
    return s * (1.0 + x * (1.0 - s))


def _ln_stats(v):
    mu = jnp.mean(v, axis=-1, keepdims=True)
    vc = v - mu
    rstd = lax.rsqrt(jnp.mean(vc * vc, axis=-1, keepdims=True) + LN_EPS)
    return vc * rstd, rstd


def _ln_grad(dy, g, vhat, rstd):
    dvh = dy * g
    return rstd * (dvh - jnp.mean(dvh, axis=-1, keepdims=True)
                   - vhat * jnp.mean(dvh * vhat, axis=-1, keepdims=True))


def _rowsum(v):
    return jnp.sum(v, axis=0, keepdims=True)


def _coords():
    return lax.axis_index("x"), lax.axis_index("y"), lax.axis_index("c")


def _flip(pos, d):
    x, y, c = pos
    return (1 - x if d & 4 else x, 1 - y if d & 2 else y, 1 - c if d & 1 else c)


def _slot(pos):
    return 4 * pos[0] + 2 * pos[1] + pos[2]


CHIP_FLIPS = (4, 2, 6)


class _Gather:
    def __init__(self, arrays, mid_frac=0.8):
        self.arrays = list(arrays)
        self.n = n = len(arrays)
        self.mid_frac = mid_frac
        self.out_shape = [jax.ShapeDtypeStruct((NDEV,) + a.shape, a.dtype) for a in arrays]
        self.scratch = [pltpu.SemaphoreType.DMA((n, 7)), pltpu.SemaphoreType.DMA((n, 7)),
                        pltpu.SemaphoreType.DMA((n,))]

    def _copy(self, refs, i, k, block, to, own=False):
        ins, outs, (send, recv, _) = refs
        slot = outs[i].at[_slot(block)]
        return pltpu.make_async_remote_copy(
            src_ref=ins[i] if own else slot, dst_ref=slot, send_sem=send.at[i, k], recv_sem=recv.at[i, k],
            device_id=to, device_id_type=MESH)

    def _local(self, refs, i):
        ins, outs, (_, _, loc) = refs
        return pltpu.make_async_copy(ins[i], outs[i].at[_slot(_coords())], loc.at[i])

    def _first(self, refs, i, k):
        me = _coords()
        return self._copy(refs, i, k, me, _flip(me, ((1,) + CHIP_FLIPS)[k]), own=True)

    def _passed(self, refs, i, j):
        me = _coords()
        return self._copy(refs, i, 4 + j, _flip(me, CHIP_FLIPS[j]), _flip(me, 1))

    def start(self, refs):
        for i in range(self.n):
            self._local(refs, i).start()
        for k in range(4):
            for i in range(self.n):
                self._first(refs, i, k).start()

    def forward(self, refs):
        me = _coords()
        for j, d in enumerate(CHIP_FLIPS):
            for i in range(self.n):
                self._copy(refs, i, 1 + j, _flip(me, d), me).wait_recv()
                self._passed(refs, i, j).start()

    def finish(self, refs):
        me = _coords()
        sib = _flip(me, 1)
        for i in range(self.n):
            self._copy(refs, i, 0, sib, me).wait_recv()
        for j, d in enumerate(CHIP_FLIPS):
            for i in range(self.n):
                self._copy(refs, i, 4 + j, _flip(sib, d), me).wait_recv()
        for i in range(self.n):
            for k in range(4):
                self._first(refs, i, k).wait_send()
            for j in range(3):
                self._passed(refs, i, j).wait_send()
            self._local(refs, i).wait()


class _Scatter:
    def __init__(self, arrays):
        self.arrays = list(arrays)
        self.n = n = len(arrays)
        self.mid_frac = None
        self.out_shape = [jax.ShapeDtypeStruct(a.shape, a.dtype) for a in arrays]
        self.scratch = [pltpu.SemaphoreType.DMA((n, 7)), pltpu.SemaphoreType.DMA((n, 7)),
                        pltpu.SemaphoreType.DMA((n,))]

    def _copy(self, refs, i, d, landing):
        ins, outs, (send, recv, _) = refs
        me = _coords()
        peer = _flip(me, d)
        return pltpu.make_async_remote_copy(
            src_ref=ins[i].at[_slot(peer)], dst_ref=outs[i].at[_slot(peer) if landing else _slot(me)],
            send_sem=send.at[i, d - 1], recv_sem=recv.at[i, d - 1], device_id=peer, device_id_type=MESH)

    def _local(self, refs, i):
        ins, outs, (_, _, loc) = refs
        me = _slot(_coords())
        return pltpu.make_async_copy(ins[i].at[me], outs[i].at[me], loc.at[i])

    def start(self, refs):
        for i in range(self.n):
            self._local(refs, i).start()
        for d in range(1, NDEV):
            for i in range(self.n):
                self._copy(refs, i, d, False).start()

    def forward(self, refs):
        pass

    def finish(self, refs):
        for d in range(1, NDEV):
            for i in range(self.n):
                self._copy(refs, i, d, True).wait_recv()
        for d in range(1, NDEV):
            for i in range(self.n):
                self._copy(refs, i, d, False).wait_send()
        for i in range(self.n):
            self._local(refs, i).wait()


class _Both:
    def __init__(self, a, b):
        self.parts = (a, b)
        self.arrays = a.arrays + b.arrays
        self.n = a.n + b.n
        self.mid_frac = a.mid_frac if a.mid_frac is not None else b.mid_frac
        self.out_shape = a.out_shape + b.out_shape
        self.scratch = a.scratch + b.scratch

    def _each(self, refs):
        ins, outs, sems = refs
        na, ns = self.parts[0].n, len(self.parts[0].scratch)
        return ((self.parts[0], (ins[:na], outs[:na], sems[:ns])), (self.parts[1], (ins[na:], outs[na:], sems[ns:])))

    def start(self, refs):
        for part, r in self._each(refs):
            part.start(r)

    def forward(self, refs):
        for part, r in self._each(refs):
            part.forward(r)

    def finish(self, refs):
        for part, r in self._each(refs):
            part.finish(r)


def _call(body, name, grid, in_specs, out_specs, out_shape, scratch, args, comm=None, start_frac=0.0):
    params = _params(*(["arbitrary"] * len(grid)))
    if comm is None:
        outs = pl.pallas_call(
            body, name=name, grid=grid, in_specs=in_specs, out_specs=out_specs, out_shape=out_shape,
            scratch_shapes=scratch, compiler_params=params)(*args)
        return list(outs), []
    n_in, n_out, n_scr, k = len(in_specs), len(out_specs), len(scratch), comm.n
    nsteps = math.prod(grid) if grid else 1
    first = int(nsteps * start_frac)
    mid = max(first, min(nsteps - 1, int(nsteps * comm.mid_frac))) if comm.mid_frac is not None else None

    def hosted(*refs):
        ins, refs = refs[:n_in], refs[n_in:]
        cins, refs = refs[:k], refs[k:]
        outs, refs = refs[:n_out], refs[n_out:]
        couts, refs = refs[:k], refs[k:]
        scr, sems = refs[:n_scr], refs[n_scr:]
        crefs = (cins, couts, sems)
        if nsteps == 1:
            comm.start(crefs)
            body(*ins, *outs, *scr)
            comm.forward(crefs)
            comm.finish(crefs)
            return
        step = pl.program_id(0)
        for a in range(1, len(grid)):
            step = step * grid[a] + pl.program_id(a)
        pl.when(step == first)(lambda: comm.start(crefs))
        if mid is not None:
            pl.when(step == mid)(lambda: comm.forward(crefs))
        body(*ins, *outs, *scr)
        pl.when(step == nsteps - 1)(lambda: comm.finish(crefs))

    any_spec = pl.BlockSpec(memory_space=pl.ANY)
    outs = pl.pallas_call(
        hosted, name=name, grid=grid,
        in_specs=list(in_specs) + [any_spec] * k, out_specs=list(out_specs) + [any_spec] * k,
        out_shape=list(out_shape) + comm.out_shape, scratch_shapes=list(scratch) + comm.scratch,
        compiler_params=params)(*args, *comm.arrays)
    return list(outs[:n_out]), list(outs[n_out:])


def _rms_matmul(x, g, w, name, comm=None):
    T = x.shape[0]
    nb = w.shape[2]
    tT = _tile(T, 1024)
    per = 2

    def body(x_ref, g_ref, w_ref, z_ref, h_ref):
        @pl.when(pl.program_id(1) == 0)
        def _():
            xf = x_ref[...]
            r = lax.rsqrt(jnp.mean(xf * xf, axis=-1, keepdims=True) + RMS_EPS)
            h_ref[...] = (xf * r * g_ref[...]).astype(h_ref.dtype)

        for j in range(per):
            z_ref[:, j * nb:(j + 1) * nb] = jnp.dot(h_ref[...], w_ref[j], preferred_element_type=F32
                                                    ).astype(z_ref.dtype)

    return _call(
        body, name, (T // tT, NDEV // per),
        [pl.BlockSpec((tT, D), lambda t, n: (t, 0)),
         pl.BlockSpec((1, D), lambda t, n: (0, 0)),
         pl.BlockSpec((per, D, nb), lambda t, n: (n, 0, 0))],
        [pl.BlockSpec((tT, per * nb), lambda t, n: (t, n)),
         pl.BlockSpec((tT, D), lambda t, n: (t, 0))],
        [jax.ShapeDtypeStruct((T, NDEV * nb), ACT_DTYPE), jax.ShapeDtypeStruct((T, D), MXU_DTYPE)],
        [], (x, g.reshape(1, D), w), comm)


ARRIVAL = (0, 1) + CHIP_FLIPS + tuple(d ^ 1 for d in CHIP_FLIPS)


def _rms_matmul_gathering(x, g, w_shard, name, comm=None):
    T = x.shape[0]
    nb = w_shard.shape[1]
    tT = _tile(T, 1024)
    nT = T // tT

    def body(x_ref, g_ref, wsh_ref, z_ref, h_ref, wfull_ref, h_all, wbuf, zbuf, fetch_sems, z_sems, send_sems,
             recv_sems, own_sem):
        n, t = pl.program_id(0), pl.program_id(1)
        step = n * nT + t
        me = _coords()
        sib = _flip(me, 1)

        def remote(k, block, to, own=False):
            slot = wfull_ref.at[_slot(block)]
            return pltpu.make_async_remote_copy(
                src_ref=wsh_ref if own else slot, dst_ref=slot, send_sem=send_sems.at[k], recv_sem=recv_sems.at[k],
                device_id=to, device_id_type=MESH)

        def first(k):
            return remote(k, me, _flip(me, ARRIVAL[1 + k]), own=True)

        def passed(j):
            return remote(4 + j, _flip(me, CHIP_FLIPS[j]), sib)

        own_copy = pltpu.make_async_copy(wsh_ref, wfull_ref.at[_slot(me)], own_sem)

        def fetch(src, nn):
            return pltpu.make_async_copy(src, wbuf.at[nn % 2], fetch_sems.at[nn % 2])

        @pl.when(step == 0)
        def _():
            own_copy.start()
            for k in range(3):
                first(k).start()
            fetch(wsh_ref, 0).start()

        @pl.when(t == 0)
        def _():
            fetch(wfull_ref.at[0], n).wait()

        @pl.when(n == 0)
        def _():
            xf = x_ref[...]
            r = lax.rsqrt(jnp.mean(xf * xf, axis=-1, keepdims=True) + RMS_EPS)
            h = (xf * r * g_ref[...]).astype(h_ref.dtype)
            h_ref[...] = h
            h_all[t] = h

        def z_copy(s, col):
            return pltpu.make_async_copy(
                zbuf.at[s % 2], z_ref.at[pl.ds(pl.multiple_of(t * tT, tT), tT), pl.ds(col * nb, nb)],
                z_sems.at[s % 2])

        @pl.when(step >= 2)
        def _():
            z_copy(step, 0).wait()

        d = sum(jnp.where(n == nn, ARRIVAL[nn], 0) for nn in range(NDEV))
        col = _slot((me[0] ^ ((d >> 2) & 1), me[1] ^ ((d >> 1) & 1), me[2] ^ (d & 1)))
        zbuf[step % 2] = jnp.dot(h_all[t], wbuf[n % 2], preferred_element_type=F32).astype(zbuf.dtype)
        z_copy(step, col).start()

        for nn in range(1, NDEV):
            @pl.when((n == nn - 1) & (t == nT - 1))
            def _(nn=nn):
                block = _flip(me, ARRIVAL[nn])
                if nn == 1:
                    remote(0, sib, me).wait_recv()
                elif nn < 5:
                    if nn == 2:
                        first(3).start()
                    remote(nn - 1, block, me).wait_recv()
                    passed(nn - 2).start()
                else:
                    remote(nn - 1, block, me).wait_recv()
                fetch(wfull_ref.at[_slot(block)], nn).start()

        @pl.when(step == NDEV * nT - 1)
        def _():
            z_copy(step - 1, 0).wait()
            z_copy(step, 0).wait()
            for k in range(4):
                first(k).wait_send()
            for j in range(3):
                passed(j).wait_send()
            own_copy.wait()

    keep = lambda n, t: (jnp.where(n == 0, t, nT - 1), 0)
    any_spec = pl.BlockSpec(memory_space=pl.ANY)
    return _call(
        body, name, (NDEV, nT),
        [pl.BlockSpec((tT, D), keep), pl.BlockSpec((1, D), lambda n, t: (0, 0)), any_spec],
        [any_spec, pl.BlockSpec((tT, D), keep), any_spec],
        [jax.ShapeDtypeStruct((T, NDEV * nb), ACT_DTYPE), jax.ShapeDtypeStruct((T, D), MXU_DTYPE),
         jax.ShapeDtypeStruct((NDEV,) + w_shard.shape, w_shard.dtype)],
        [pltpu.VMEM((nT, tT, D), MXU_DTYPE), pltpu.VMEM((2, D, nb), w_shard.dtype),
         pltpu.VMEM((2, tT, nb), ACT_DTYPE), pltpu.SemaphoreType.DMA((2,)), pltpu.SemaphoreType.DMA((2,)),
         pltpu.SemaphoreType.DMA((7,)), pltpu.SemaphoreType.DMA((7,)), pltpu.SemaphoreType.DMA],
        (x, g.reshape(1, D), w_shard), comm, start_frac=0.5)


def _rms_matmul_bwd(dz, w, x, g, dxo, name, comm=None):
    T = x.shape[0]
    nb = w.shape[2]
    tT = _tile(T, 256)
    per = 8
    steps = NDEV // per

    def body(dz_ref, w_ref, x_ref, g_ref, dxo_ref, dx_ref, dg_ref, acc):
        t, n = pl.program_id(0), pl.program_id(1)

        @pl.when((n == 0) & (t == 0))
        def _():
            dg_ref[...] = jnp.zeros_like(dg_ref)

        part = None
        for j in range(per):
            d = lax.dot_general(_mx(dz_ref[:, j * nb:(j + 1) * nb]), w_ref[j], NT_DIMS, preferred_element_type=F32)
            part = d if part is None else part + d

        @pl.when(n == 0)
        def _():
            acc[...] = part

        @pl.when(n > 0)
        def _():
            acc[...] += part

        @pl.when(n == steps - 1)
        def _():
            xf = x_ref[...]
            r = lax.rsqrt(jnp.mean(xf * xf, axis=-1, keepdims=True) + RMS_EPS)
            xh = xf * r
            dh = acc[...]
            dxh = dh * g_ref[...]
            dx_ref[...] = dxo_ref[...] + r * (dxh - xh * jnp.mean(dxh * xh, axis=-1, keepdims=True))
            dg_ref[...] += _rowsum(dh * xh)

    return _call(
        body, name, (T // tT, steps),
        [pl.BlockSpec((tT, per * nb), lambda t, n: (t, n)),
         pl.BlockSpec((per, D, nb), lambda t, n: (n, 0, 0)),
         pl.BlockSpec((tT, D), lambda t, n: (t, 0)),
         pl.BlockSpec((1, D), lambda t, n: (0, 0)),
         pl.BlockSpec((tT, D), lambda t, n: (t, 0))],
        [pl.BlockSpec((tT, D), lambda t, n: (t, 0)),
         pl.BlockSpec((1, D), lambda t, n: (0, 0))],
        [jax.ShapeDtypeStruct((T, D), F32), jax.ShapeDtypeStruct((1, D), F32)],
        [pltpu.VMEM((tT, D), F32)], (dz, w, x, g.reshape(1, D), dxo), comm)


def _atb(a, b, nblk, name, comm=None):
    G, T, M = a.shape
    N = b.shape[2]
    nb = N // nblk
    tk = _tile(T, 2048)
    nk = T // tk

    def body(a_ref, b_ref, o_ref, acc):
        k = pl.program_id(1)

        @pl.when(k == 0)
        def _():
            acc[...] = jnp.zeros_like(acc)

        acc[...] += lax.dot_general(_mx(a_ref[...]), _mx(b_ref[...]), TN_DIMS, preferred_element_type=F32)

        @pl.when(k == nk - 1)
        def _():
            o_ref[...] = acc[...].astype(o_ref.dtype)

    outs, couts = _call(
        body, name, (G * nblk, nk),
        [pl.BlockSpec((None, tk, M), lambda n, k: (n // nblk, k, 0)),
         pl.BlockSpec((None, tk, nb), lambda n, k: (n // nblk, k, n % nblk))],
        [pl.BlockSpec((None, M, nb), lambda n, k: (n, 0, 0))],
        [jax.ShapeDtypeStruct((G * nblk, M, nb), GRAD_DTYPE)],
        [pltpu.VMEM((M, nb), F32)], (a, b), comm)
    return outs[0], couts


def _loss_rows(xf, t_ref, g_ref, loss_ref, dx_ref, dg_ref):
    @pl.when(pl.program_id(0) == 0)
    def _():
        loss_ref[...] = jnp.zeros_like(loss_ref)
        dg_ref[...] = jnp.zeros_like(dg_ref)

    r = lax.rsqrt(jnp.mean(xf * xf, axis=-1, keepdims=True) + RMS_EPS)
    xh = xf * r
    err = xh * g_ref[...] - t_ref[...]
    loss_ref[...] += 0.5 * jnp.sum(jnp.mean(err * err, axis=-1, keepdims=True), axis=0, keepdims=True)
    dy = err * (1.0 / D)
    dxh = dy * g_ref[...]
    dx_ref[...] = r * (dxh - xh * jnp.mean(dxh * xh, axis=-1, keepdims=True))
    dg_ref[...] += _rowsum(dy * xh)


def _spatial_gate(wm_ref, bst_ref, vb_ref, sv_ref, n_chunks):
    for c in range(n_chunks):
        rows = slice(c * CHUNK, (c + 1) * CHUNK)
        for g in range(N_GROUPS):
            cols = slice(g * CHUNK, (g + 1) * CHUNK)
            sv_ref[rows, cols] = (jnp.dot(wm_ref[g], vb_ref[rows, cols], preferred_element_type=F32)
                                  + bst_ref[:, g:g + 1])


def _lane_loop(fn):
    def step(i, carry):
        fn(pl.ds(pl.multiple_of(i * LANES, LANES), LANES))
        return carry

    lax.fori_loop(0, D // LANES, step, 0)


def _shifted_copies(buf, sh, n):
    for s in range(1, 8):
        sh[s - 1, 0:n, :] = buf[s:s + n, :]


def _window(buf, sh, base, off, lanes):
    a, s = divmod(off, 8)
    src = buf if s == 0 else sh.at[s - 1]
    return src[base + 8 * a:base + 8 * a + 8, lanes]


def _softmax_rows(s):
    e = jnp.exp(s - jnp.max(s, axis=-1, keepdims=True))
    return e / jnp.sum(e, axis=-1, keepdims=True)


def _branch_fwd(z, kv, wm, bst, ln_a, cw, cvec, name, comm=None):
    T = z.shape[0]
    tT = _tile(T, 256)
    n_chunks = tT // CHUNK

    def body(z_ref, kv_ref, wm_ref, bst_ref, lna_ref, cw_ref, cvec_ref, br_ref, c_ref, gbuf, gsh, vb, ua):
        @pl.when(pl.program_id(0) == 0)
        def _():
            gbuf[0:HALO, :] = jnp.zeros((HALO, D), F32)

        halves = [slice(r0, r0 + tT // 2) for r0 in range(0, tT, tT // 2)]

        def seg(s, r=slice(None)):
            return z_ref[r, s * D:(s + 1) * D].astype(F32)

        for r in halves:
            u, _ = _gelu(seg(SEG_AU, r))
            zg = seg(SEG_AG, r)
            ua[r, :] = u * (zg * jax.nn.sigmoid(zg))
            gv, _ = _gelu(seg(SEG_AV, r))
            vhat, _ = _ln_stats(gv)
            vb[r, :] = _mx(vhat * lna_ref[0:1, :] + lna_ref[1:2, :])
        for c in range(n_chunks):
            rows = slice(c * CHUNK, (c + 1) * CHUNK)
            for g in range(N_GROUPS):
                cols = slice(g * CHUNK, (g + 1) * CHUNK)
                sv = jnp.dot(wm_ref[g], vb[rows, cols], preferred_element_type=F32) + bst_ref[:, g:g + 1]
                br_ref[0, rows, cols] = (sv * ua[rows, cols]).astype(br_ref.dtype)

        gbuf[HALO:HALO + tT, :] = seg(SEG_BA) * jax.nn.sigmoid(seg(SEG_BB))
        _shifted_copies(gbuf, gsh, tT + HALO - 8)
        def conv_lanes(lanes):
            taps = [jnp.broadcast_to(cw_ref[k:k + 1, lanes], (8, LANES)) for k in range(CONV_K)]
            bias = jnp.broadcast_to(cvec_ref[0:1, lanes], (8, LANES))
            for base in range(0, tT, 8):
                acc = [bias, None, None, None]
                for k in range(CONV_K):
                    term = taps[k] * _window(gbuf, gsh, base, k + HALO - CONV_K + 1, lanes)
                    acc[k % 4] = term if acc[k % 4] is None else acc[k % 4] + term
                c_ref[base:base + 8, lanes] = (acc[0] + acc[1]) + (acc[2] + acc[3])

        _lane_loop(conv_lanes)
        gbuf[0:HALO, :] = gbuf[tT:tT + HALO, :]
        for r in halves:
            chat, _ = _ln_stats(c_ref[r, :])
            cl = chat * cvec_ref[1:2, :] + cvec_ref[2:3, :]
            zg = seg(SEG_BG, r)
            br_ref[1, r, :] = (cl * jax.nn.sigmoid(cl) * (zg * jax.nn.sigmoid(zg))).astype(br_ref.dtype)

        for h in range(HEADS):
            cols = slice(h * HEAD_DIM, (h + 1) * HEAD_DIM)
            q = _mx(z_ref[:, SEG_CQ * D + h * HEAD_DIM:SEG_CQ * D + (h + 1) * HEAD_DIM])
            s = lax.dot_general(q, kv_ref[:, cols], NT_DIMS, preferred_element_type=F32)
            p = _softmax_rows(s * (1.0 / math.sqrt(HEAD_DIM)))
            att = jnp.dot(_mx(p), kv_ref[:, D + h * HEAD_DIM:D + (h + 1) * HEAD_DIM], preferred_element_type=F32)
            zg = z_ref[:, SEG_CG * D + h * HEAD_DIM:SEG_CG * D + (h + 1) * HEAD_DIM].astype(F32)
            br_ref[2, :, cols] = (att * (zg * jax.nn.sigmoid(zg))).astype(br_ref.dtype)

    full = lambda shape: pl.BlockSpec(shape, lambda t: (0,) * len(shape))
    return _call(
        body, name, (T // tT,),
        [pl.BlockSpec((tT, SEG_M * D), lambda t: (t, 0)),
         full(kv.shape), full(wm.shape), full(bst.shape), full(ln_a.shape), full(cw.shape), full(cvec.shape)],
        [pl.BlockSpec((3, tT, D), lambda t: (0, t, 0)), pl.BlockSpec((tT, D), lambda t: (t, 0))],
        [jax.ShapeDtypeStruct((3, T, D), MXU_DTYPE), jax.ShapeDtypeStruct((T, D), F32)],
        [pltpu.VMEM((tT + HALO, D), F32), pltpu.VMEM((7, tT + HALO - 8, D), F32),
         pltpu.VMEM((tT, D), MXU_DTYPE), pltpu.VMEM((tT, D), F32)],
        (z, kv, wm, bst, ln_a, cw, cvec), comm)


def _branch_bwd(z, dbr, c, dzm, kv, wm, wmt, bst, ln_a, cw, cvec, name, comm=None):
    T = z.shape[0]
    M = kv.shape[0]
    tT = _tile(T, 128)
    nT = T // tT
    n_chunks = tT // CHUNK

    def body(z_ref, dbr_ref, c_ref, dzm_ref, kv_ref, wm_ref, wmt_ref, bst_ref, lna_ref,
             cw_ref, cvec_ref, dz_ref, vecg_ref, dbst_ref, dws_ref, dcw_ref, dkv_ref,
             gbuf, dcbuf, vb, dsvb, sv, dvbuf, dcsh, dglu, vh, gq, dcw8, dcw_step):
        i = pl.program_id(0)

        @pl.when(i == 0)
        def _():
            vecg_ref[...] = jnp.zeros_like(vecg_ref)
            dbst_ref[...] = jnp.zeros_like(dbst_ref)
            dws_ref[...] = jnp.zeros_like(dws_ref)
            dcw8[...] = jnp.zeros_like(dcw8)
            dkv_ref[...] = jnp.zeros_like(dkv_ref)
            dcbuf[tT:tT + HALO, :] = jnp.zeros((HALO, D), F32)

        strips = [slice(r0, r0 + STRIP) for r0 in range(0, tT, STRIP)]

        def seg(r, s):
            return z_ref[r, s * D:(s + 1) * D].astype(F32)

        def put(r, s, val):
            dz_ref[r, s * D:(s + 1) * D] = val.astype(dz_ref.dtype)

        for r in strips:
            zv = seg(r, SEG_AV)
            gv, tv = _gelu(zv)
            vhat, rstd = _ln_stats(gv)
            vb[r, :] = _mx(vhat * lna_ref[0:1, :] + lna_ref[1:2, :])
            vh[r, :] = vhat
            gq[r, :] = rstd * _gelu_grad(zv, tv)
        _spatial_gate(wm_ref, bst_ref, vb, sv, n_chunks)
        for r in strips:
            zu, zg = seg(r, SEG_AU), seg(r, SEG_AG)
            u, tu = _gelu(zu)
            sg = jax.nn.sigmoid(zg)
            d_a = dbr_ref[0, r, :].astype(F32)
            put(r, SEG_AU, d_a * sv[r, :] * (zg * sg) * _gelu_grad(zu, tu))
            put(r, SEG_AG, d_a * u * sv[r, :] * _silu_grad(zg, sg))
            dsv = d_a * u * (zg * sg)
            dsvb[r, :] = _mx(dsv)
            in_chunk = slice(r.start % CHUNK, r.start % CHUNK + STRIP)
            for g in range(N_GROUPS):
                dbst_ref[in_chunk, g:g + 1] += jnp.sum(dsv[:, g * CHUNK:(g + 1) * CHUNK], axis=-1, keepdims=True)
        tril = (lax.broadcasted_iota(jnp.int32, (CHUNK, CHUNK), 0)
                >= lax.broadcasted_iota(jnp.int32, (CHUNK, CHUNK), 1))
        for g in range(N_GROUPS):
            cols = slice(g * CHUNK, (g + 1) * CHUNK)
            for cc in range(n_chunks):
                rows = slice(cc * CHUNK, (cc + 1) * CHUNK)
                dws = lax.dot_general(dsvb[rows, cols], vb[rows, cols], NT_DIMS, preferred_element_type=F32)
                dws_ref[g] += jnp.where(tril, dws, 0.0)
                dvbuf[rows, cols] = jnp.dot(wmt_ref[g], dsvb[rows, cols], preferred_element_type=F32)
        for r in strips:
            dv, vhat = dvbuf[r, :], vh[r, :]
            vecg_ref[0:1, :] += _rowsum(dv * vhat)
            vecg_ref[1:2, :] += _rowsum(dv)
            dvh = dv * lna_ref[0:1, :]
            put(r, SEG_AV, (dvh - jnp.mean(dvh, axis=-1, keepdims=True)
                            - vhat * jnp.mean(dvh * vhat, axis=-1, keepdims=True)) * gq[r, :])

        sgb_buf = sv
        for r in strips:
            za, zg = seg(r, SEG_BA), seg(r, SEG_BG)
            sgb = jax.nn.sigmoid(seg(r, SEG_BB))
            sgb_buf[r, :] = sgb
            gbuf[r, :] = za * sgb
            chat, crstd = _ln_stats(c_ref[r, :])
            cl = chat * cvec_ref[1:2, :] + cvec_ref[2:3, :]
            scl = jax.nn.sigmoid(cl)
            sg = jax.nn.sigmoid(zg)
            d_b = dbr_ref[1, r, :].astype(F32)
            put(r, SEG_BG, d_b * (cl * scl) * _silu_grad(zg, sg))
            dcl = d_b * (zg * sg) * _silu_grad(cl, scl)
            vecg_ref[3:4, :] += _rowsum(dcl * chat)
            vecg_ref[4:5, :] += _rowsum(dcl)
            dc = _ln_grad(dcl, cvec_ref[1:2, :], chat, crstd)
            vecg_ref[2:3, :] += _rowsum(dc)
            dcbuf[r, :] = dc
        _shifted_copies(dcbuf, dcsh, tT + HALO - 8)

        def conv_grads(lanes):
            taps = [jnp.broadcast_to(cw_ref[k:k + 1, lanes], (8, LANES)) for k in range(CONV_K)]
            wsum = [None] * CONV_K
            for base in range(0, tT, 8):
                glu = gbuf[base:base + 8, lanes]
                acc = [None] * 4
                for k in range(CONV_K):
                    win = _window(dcbuf, dcsh, base, CONV_K - 1 - k, lanes)
                    term = taps[k] * win
                    acc[k % 4] = term if acc[k % 4] is None else acc[k % 4] + term
                    term = glu * win
                    wsum[k] = term if wsum[k] is None else wsum[k] + term
                dglu[base:base + 8, lanes] = (acc[0] + acc[1]) + (acc[2] + acc[3])
            for k in range(CONV_K):
                dcw_step[8 * k:8 * k + 8, lanes] = wsum[k]

        _lane_loop(conv_grads)
        dcw8[...] += dcw_step[...]

        @pl.when(i == nT - 1)
        def _():
            for k in range(CONV_K):
                dcw_ref[k:k + 1, :] = _rowsum(dcw8[8 * k:8 * k + 8, :])
            dcw_ref[CONV_K:HALO, :] = jnp.zeros((HALO - CONV_K, D), F32)

        dcbuf[tT:tT + HALO, :] = dcbuf[0:HALO, :]
        for r in strips:
            dg, sgb = dglu[r, :], sgb_buf[r, :]
            put(r, SEG_BA, dg * sgb)
            put(r, SEG_BB, dg * seg(r, SEG_BA) * sgb * (1.0 - sgb))

        scale = 1.0 / math.sqrt(HEAD_DIM)
        for h in range(HEADS):
            cols = slice(h * HEAD_DIM, (h + 1) * HEAD_DIM)
            qcols = slice(SEG_CQ * D + h * HEAD_DIM, SEG_CQ * D + (h + 1) * HEAD_DIM)
            gcols = slice(SEG_CG * D + h * HEAD_DIM, SEG_CG * D + (h + 1) * HEAD_DIM)
            vcols = slice(D + h * HEAD_DIM, D + (h + 1) * HEAD_DIM)
            q = _mx(z_ref[:, qcols])
            kh, vh = kv_ref[:, cols], kv_ref[:, vcols]
            p = _softmax_rows(lax.dot_general(q, kh, NT_DIMS, preferred_element_type=F32) * scale)
            pb = _mx(p)
            att = jnp.dot(pb, vh, preferred_element_type=F32)
            zg = z_ref[:, gcols].astype(F32)
            sg = jax.nn.sigmoid(zg)
            d_c = dbr_ref[2, :, cols].astype(F32)
            dz_ref[:, gcols] = (d_c * att * _silu_grad(zg, sg)).astype(dz_ref.dtype)
            datt = _mx(d_c * (zg * sg))
            dp = lax.dot_general(datt, vh, NT_DIMS, preferred_element_type=F32)
            dkv_ref[:, vcols] += lax.dot_general(pb, datt, TN_DIMS, preferred_element_type=F32)
            ds = _mx(p * (dp - jnp.sum(dp * p, axis=-1, keepdims=True)) * scale)
            dz_ref[:, qcols] = jnp.dot(ds, kh, preferred_element_type=F32).astype(dz_ref.dtype)
            dkv_ref[:, cols] += lax.dot_general(ds, q, TN_DIMS, preferred_element_type=F32)

        dz_ref[:, SEG_M * D:] = dzm_ref[...].astype(dz_ref.dtype)

    rev = lambda i: nT - 1 - i
    full = lambda shape: pl.BlockSpec(shape, lambda i: (0,) * len(shape))
    return _call(
        body, name, (nT,),
        [pl.BlockSpec((tT, SEG_M * D), lambda i: (rev(i), 0)),
         pl.BlockSpec((3, tT, D), lambda i: (0, rev(i), 0)),
         pl.BlockSpec((tT, D), lambda i: (rev(i), 0)),
         pl.BlockSpec((tT, 3 * D), lambda i: (rev(i), 0)),
         full(kv.shape), full(wm.shape), full(wmt.shape), full(bst.shape), full(ln_a.shape),
         full(cw.shape), full(cvec.shape)],
        [pl.BlockSpec((tT, N_IN), lambda i: (rev(i), 0)),
         full((8, D)), full((CHUNK, N_GROUPS)), full((N_GROUPS, CHUNK, CHUNK)), full((HALO, D)),
         full((M, 2 * D))],
        [jax.ShapeDtypeStruct((T, N_IN), MXU_DTYPE), jax.ShapeDtypeStruct((8, D), F32),
         jax.ShapeDtypeStruct((CHUNK, N_GROUPS), F32),
         jax.ShapeDtypeStruct((N_GROUPS, CHUNK, CHUNK), F32),
         jax.ShapeDtypeStruct((HALO, D), F32), jax.ShapeDtypeStruct((M, 2 * D), F32)],
        [pltpu.VMEM((tT, D), F32), pltpu.VMEM((tT + HALO, D), F32),
         pltpu.VMEM((tT, D), MXU_DTYPE), pltpu.VMEM((tT, D), MXU_DTYPE),
         pltpu.VMEM((tT, D), F32), pltpu.VMEM((tT, D), F32),
         pltpu.VMEM((7, tT + HALO - 8, D), F32),
         pltpu.VMEM((tT, D), F32), pltpu.VMEM((tT, D), F32), pltpu.VMEM((tT, D), F32),
         pltpu.VMEM((CONV_K * 8, D), F32), pltpu.VMEM((CONV_K * 8, D), F32)],
        (z, dbr, c, dzm, kv, wm, wmt, bst, ln_a, cw, cvec), comm)


def _merge_fwd(br, z, x, wb, wo, name, head=None):
    T = x.shape[0]
    tT = _tile(T, 512)

    def body(br_ref, z0, z1, z2, x_ref, wb_ref, wo_ref, *rest):
        mg_ref, pj_ref = rest[-2:]
        merged = jnp.zeros((tT, D), F32)
        for n, zm in enumerate((z0, z1, z2)):
            proj = jnp.dot(br_ref[n], wb_ref[:, n].reshape(D, D), preferred_element_type=F32)
            pj_ref[n] = proj.astype(pj_ref.dtype)
            merged = merged + jax.nn.sigmoid(zm[...].astype(F32)) * proj
        mg_ref[...] = merged.astype(mg_ref.dtype)
        xn = x_ref[...] + jnp.dot(_mx(merged), wo_ref[...].reshape(D, D), preferred_element_type=F32)
        if head is None:
            rest[0][...] = xn
        else:
            _loss_rows(xn, *rest[:5])

    row = pl.BlockSpec((tT, D), lambda t: (t, 0))
    vec = pl.BlockSpec((1, D), lambda t: (0, 0))
    zspec = lambda n: pl.BlockSpec((tT, D), lambda t: (t, SEG_M + n))
    in_specs = [pl.BlockSpec((3, tT, D), lambda t: (0, t, 0)), zspec(0), zspec(1), zspec(2), row,
                pl.BlockSpec(wb.shape, lambda t: (0, 0, 0, 0)), pl.BlockSpec(wo.shape, lambda t: (0, 0, 0))]
    out_specs = [row, pl.BlockSpec((3, tT, D), lambda t: (0, t, 0))]
    out_shape = [jax.ShapeDtypeStruct((T, D), MXU_DTYPE), jax.ShapeDtypeStruct((3, T, D), ACT_DTYPE)]
    args = (br, z, z, z, x, wb, wo)
    if head is None:
        out_specs = [row] + out_specs
        out_shape = [jax.ShapeDtypeStruct((T, D), F32)] + out_shape
    else:
        in_specs += [row, vec]
        args += (head[0], head[1].reshape(1, D))
        out_specs = [pl.BlockSpec((1, 1), lambda t: (0, 0)), row, vec] + out_specs
        out_shape = [jax.ShapeDtypeStruct((1, 1), F32), jax.ShapeDtypeStruct((T, D), F32),
                     jax.ShapeDtypeStruct((1, D), F32)] + out_shape
    return pl.pallas_call(body, name=name, grid=(T // tT,), in_specs=in_specs, out_specs=out_specs,
                          out_shape=out_shape, compiler_params=_params("arbitrary"))(*args)


def _merge_bwd(dxo, proj, z, wb, wo, name):
    T = dxo.shape[0]
    tT = _tile(T, 512)

    def body(dxo_ref, pj_ref, z0, z1, z2, wb_ref, wo_ref, dpj_ref, dbr_ref, dzm_ref):
        dmerged = lax.dot_general(_mx(dxo_ref[...]), wo_ref[...].reshape(D, D), NT_DIMS,
                                  preferred_element_type=F32)
        for n, zm in enumerate((z0, z1, z2)):
            gate = jax.nn.sigmoid(zm[...].astype(F32))
            dproj = _mx(gate * dmerged)
            dpj_ref[n] = dproj
            dzm_ref[:, n * D:(n + 1) * D] = (pj_ref[n].astype(F32) * dmerged * gate * (1.0 - gate)
                                             ).astype(dzm_ref.dtype)
            dbr_ref[n] = lax.dot_general(dproj, wb_ref[:, n].reshape(D, D), NT_DIMS,
                                         preferred_element_type=F32).astype(dbr_ref.dtype)

    zspec = lambda n: pl.BlockSpec((tT, D), lambda t: (t, SEG_M + n))
    return pl.pallas_call(
        body, name=name, grid=(T // tT,),
        in_specs=[pl.BlockSpec((tT, D), lambda t: (t, 0)),
                  pl.BlockSpec((3, tT, D), lambda t: (0, t, 0)), zspec(0), zspec(1), zspec(2),
                  pl.BlockSpec(wb.shape, lambda t: (0, 0, 0, 0)),
                  pl.BlockSpec(wo.shape, lambda t: (0, 0, 0))],
        out_specs=[pl.BlockSpec((3, tT, D), lambda t: (0, t, 0)),
                   pl.BlockSpec((3, tT, D), lambda t: (0, t, 0)),
                   pl.BlockSpec((tT, 3 * D), lambda t: (t, 0))],
        out_shape=[jax.ShapeDtypeStruct((3, T, D), MXU_DTYPE), jax.ShapeDtypeStruct((3, T, D), ACT_DTYPE),
                   jax.ShapeDtypeStruct((T, 3 * D), MXU_DTYPE)],
        compiler_params=_params("parallel"),
    )(dxo, proj, z, z, z, wb, wo)


def _adamw(parts, w, m, v, name, comm=None):
    G, R, C = w.shape
    tr = 128 if R % 128 == 0 else R
    nr = R // tr
    c1 = 1.0 / (1.0 - ADAM_B1 ** ADAM_STEP)
    c2 = 1.0 / (1.0 - ADAM_B2 ** ADAM_STEP)

    def body(*refs):
        p_refs, (w_ref, m_ref, v_ref, g_out, d_out, m_out, v_out) = refs[:G], refs[G:]
        for i in range(G):
            @pl.when(pl.program_id(0) == i)
            def _(p_ref=p_refs[i]):
                g = p_ref[0].astype(F32)
                for p in range(1, NDEV):
                    g = g + p_ref[p].astype(F32)
                mn = ADAM_B1 * m_ref[...] + (1.0 - ADAM_B1) * g
                vn = ADAM_B2 * v_ref[...] + (1.0 - ADAM_B2) * (g * g)
                g_out[...] = g
                m_out[...] = mn
                v_out[...] = vn
                d_out[...] = -ADAM_LR * ((mn * c1) / (jnp.sqrt(vn * c2) + ADAM_EPS) + ADAM_WD * w_ref[...])

    def parts_spec(i):
        return pl.BlockSpec((NDEV, tr, C), lambda l, r: (0, jnp.where(l == i, r, jnp.where(l > i, nr - 1, 0)), 0))

    spec = pl.BlockSpec((None, tr, C), lambda l, r: (l, r, 0))
    return _call(
        body, name, (G, nr), [parts_spec(i) for i in range(G)] + [spec] * 3,
        [spec] * 4, [jax.ShapeDtypeStruct((G, R, C), F32)] * 4, [], (*parts, w, m, v), comm)


def kernel(x, mem, norm_g, mem_norm_g, w_in, gmlp_ln_g, gmlp_ln_b, w_s, b_s, conv_w, conv_b, conv_ln_g, conv_ln_b, w_kv, w_branch, w_out, final_norm_g, loss_target, m_norm_g, m_mem_norm_g, m_w_in, m_gmlp_ln_g, m_gmlp_ln_b, m_w_s, m_b_s, m_conv_w, m_conv_b, m_conv_ln_g, m_conv_ln_b, m_w_kv, m_w_branch, m_w_out, m_final_norm_g, v_norm_g, v_mem_norm_g, v_w_in, v_gmlp_ln_g, v_gmlp_ln_b, v_w_s, v_b_s, v_conv_w, v_conv_b, v_conv_ln_g, v_conv_ln_b, v_w_kv, v_w_branch, v_w_out, v_final_norm_g):
    L = w_in.shape[0]
    x0, mem0, tgt = x[0], mem[0], loss_target[0]
    T, M = x0.shape[0], mem0.shape[0]
    nbc = conv_w.shape[2]

    def shards(l):
        return [_mx(w_in[l]), _mx(w_kv[l]), _mx(w_branch[l]), _mx(w_out[l]), conv_w[l]]

    gather_rest = _Gather(shards(0)[1:])
    gather_upper = _Gather([a for l in range(1, L) for a in shards(l)], mid_frac=0.85) if L > 1 else None

    tril = jnp.tril(jnp.ones((CHUNK, CHUNK), bool))
    wm = [_mx(jnp.where(tril[None], w_s[l], 0.0)) for l in range(L)]
    wmt = [w.transpose(0, 2, 1) for w in wm]
    bst = [b_s[l].T for l in range(L)]
    ln_a = [jnp.stack([gmlp_ln_g[l], gmlp_ln_b[l]]) for l in range(L)]
    cvec = [jnp.stack([conv_b[l], conv_ln_g[l], conv_ln_b[l]]) for l in range(L)]

    def conv_taps(gathered):
        return jnp.pad(gathered.transpose(1, 0, 2).reshape(CONV_K, D), ((0, HALO - CONV_K), (0, 0)))

    win, wkv, wbr, wou, cwf = [], [], [], [], []
    memn, kvs, xs, saved = [], [], [x0], []
    for l in range(L):
        if l == 0:
            (z, h, w0), full = _rms_matmul_gathering(x0, norm_g[0], shards(0)[0], "inproj_fwd0", gather_rest)
            win, wkv, wbr, wou, cwf = [w0], [full[0]], [full[1]], [full[2]], [conv_taps(full[3])]
        else:
            (z, h), _ = _rms_matmul(xs[l], norm_g[l], win[l], f"inproj_fwd{l}")
        (kv, mn), _ = _rms_matmul(mem0, mem_norm_g[l], wkv[l], f"kv_fwd{l}")
        kvs.append(_mx(kv))
        memn.append(mn)
        (br, cpre), full = _branch_fwd(z, kvs[l], wm[l], bst[l], ln_a[l], cwf[l], cvec[l], f"branch_fwd{l}",
                                       gather_upper if l == 0 else None)
        for k in range(1, L if l == 0 else 0):
            f = full[5 * (k - 1):5 * k]
            win.append(f[0])
            wkv.append(f[1])
            wbr.append(f[2])
            wou.append(f[3])
            cwf.append(conv_taps(f[4]))
        if l < L - 1:
            xn, merged, proj = _merge_fwd(br, z, xs[l], wbr[l], wou[l], f"merge_fwd{l}")
            xs.append(xn)
        else:
            loss_part, dx, dfg, merged, proj = _merge_fwd(br, z, xs[l], wbr[l], wou[l], f"merge_fwd{l}",
                                                          head=(tgt, final_norm_g))
        saved.append((z, h, br, cpre, merged, proj))

    pending, recv = [("final_norm_g", dfg, True)], {}

    def flush():
        scat = [(k, a) for k, a, g in pending if not g]
        gath = [(k, a) for k, a, g in pending if g]
        pending.clear()
        comms = ([_Scatter([a for _, a in scat])] if scat else []) + ([_Gather([a for _, a in gath])] if gath else [])
        return [k for k, _ in scat + gath], comms[0] if len(comms) == 1 else _Both(*comms)

    def landed(keys, arrays):
        recv.update(zip(keys, arrays))

    for l in reversed(range(L)):
        z, h, br, cpre, merged, proj = saved[l]
        dproj, dbr, dzm = _merge_bwd(dx, proj, z, wbr[l], wou[l], f"merge_bwd{l}")
        dwb, _ = _atb(br, dproj, 1, f"dwbranch{l}")
        for n in range(3):
            pending.append((f"w_branch{l}_{n}", dwb[n].reshape(NDEV, D // NDEV, D), False))
        dwo, _ = _atb(merged[None], dx[None], 1, f"dwout{l}")
        pending.append((f"w_out{l}", dwo.reshape(NDEV, D // NDEV, D), False))
        keys, comm = flush()
        (dz, vecg, dbst, dws, dcw, dkv), got = _branch_bwd(
            z, dbr, cpre, dzm, kvs[l], wm[l], wmt[l], bst[l], ln_a[l], cwf[l], cvec[l], f"branch_bwd{l}", comm)
        landed(keys, got)
        dwk, _ = _atb(memn[l][None], dkv[None], NDEV, f"dwkv{l}")
        (_, dmg), _ = _rms_matmul_bwd(dkv, wkv[l], mem0, mem_norm_g[l], jnp.zeros((M, D), F32), f"kv_bwd{l}")
        rest = jnp.concatenate([dmg, vecg[0:2], vecg[2:5], dbst.T.reshape(1, D)], axis=0)
        pending += [(f"w_kv{l}", dwk, False),
                    (f"conv_w{l}", dcw[:CONV_K].reshape(CONV_K, NDEV, nbc).transpose(1, 0, 2), False),
                    (f"small{l}", rest, True), (f"w_s{l}", dws.reshape(N_GROUPS * CHUNK, CHUNK), True)]
        keys, comm = flush()
        dwi, got = _atb(h[None], dz[None], NDEV, f"dwin{l}", comm)
        landed(keys, got)
        pending.append((f"w_in{l}", dwi, False))
        keys, comm = flush() if l == 0 else ([], None)
        (dx, dng), got = _rms_matmul_bwd(dz, win[l], xs[l], norm_g[l], dx, f"inproj_bwd{l}", comm)
        landed(keys, got)
        pending.append((f"norm_g{l}", dng, True))
    grad_x = dx[None]

    def pack(p):
        rows = []
        for l in range(L):
            rows += [p["norm_g"][l], p["mem_norm_g"][l], p["gmlp_ln_g"][l], p["gmlp_ln_b"][l], p["conv_b"][l],
                     p["conv_ln_g"][l], p["conv_ln_b"][l], p["b_s"][l].reshape(D)]
        return jnp.stack(rows + [p["final_norm_g"]])[None]

    names = ["norm_g", "mem_norm_g", "gmlp_ln_g", "gmlp_ln_b", "conv_b", "conv_ln_g", "conv_ln_b", "b_s",
             "final_norm_g"]
    w_small = pack(dict(zip(names, [norm_g, mem_norm_g, gmlp_ln_g, gmlp_ln_b, conv_b, conv_ln_g, conv_ln_b,
                                    b_s, final_norm_g])))
    m_small = pack(dict(zip(names, [m_norm_g, m_mem_norm_g, m_gmlp_ln_g, m_gmlp_ln_b, m_conv_b, m_conv_ln_g,
                                    m_conv_ln_b, m_b_s, m_final_norm_g])))
    v_small = pack(dict(zip(names, [v_norm_g, v_mem_norm_g, v_gmlp_ln_g, v_gmlp_ln_b, v_conv_b, v_conv_ln_g,
                                    v_conv_ln_b, v_b_s, v_final_norm_g])))
    outs = {}

    def run(key, parts, w, m, v, comm=None):
        outs[key], got = _adamw(parts, w, m, v, "adamw_" + key, comm)
        return got

    keys, comm = flush()
    landed(keys, run("w_in", [recv[f"w_in{l}"] for l in range(L)], w_in, m_w_in, v_w_in, comm))
    parts_small = jnp.concatenate([recv[f"{k}{l}"] for l in range(L) for k in ("norm_g", "small")]
                                  + [recv["final_norm_g"]], axis=1)
    parts_ws = jnp.concatenate([recv[f"w_s{l}"] for l in range(L)], axis=1)
    run("w_kv", [recv[f"w_kv{l}"] for l in range(L)], w_kv, m_w_kv, v_w_kv)
    sh = (L * 3, D // NDEV, D)
    run("w_branch", [recv[f"w_branch{l}_{n}"] for l in range(L) for n in range(3)],
        w_branch.reshape(sh), m_w_branch.reshape(sh), v_w_branch.reshape(sh))
    run("w_out", [recv[f"w_out{l}"] for l in range(L)], w_out, m_w_out, v_w_out)
    run("conv_w", [recv[f"conv_w{l}"] for l in range(L)], conv_w, m_conv_w, v_conv_w)
    run("small", [parts_small], w_small, m_small, v_small)
    ws_shape = (1, L * N_GROUPS * CHUNK, CHUNK)
    run("w_s", [parts_ws], w_s.reshape(ws_shape), m_w_s.reshape(ws_shape), v_w_s.reshape(ws_shape))

    def leaf(name, k):
        if name in ("w_in", "w_kv", "w_out", "conv_w"):
            return outs[name][k]
        if name == "w_branch":
            return outs[name][k].reshape(w_branch.shape)
        if name == "w_s":
            return outs["w_s"][k].reshape(L, N_GROUPS, CHUNK, CHUNK)
        sm = outs["small"][k][0]
        if name == "final_norm_g":
            return sm[8 * L]
        j = names.index(name)
        rows = jnp.stack([sm[8 * l + j] for l in range(L)])
        return rows.reshape(L, N_GROUPS, CHUNK) if name == "b_s" else rows

    order = ["norm_g", "mem_norm_g", "w_in", "gmlp_ln_g", "gmlp_ln_b", "w_s", "b_s", "conv_w", "conv_b",
             "conv_ln_g", "conv_ln_b", "w_kv", "w_branch", "w_out", "final_norm_g"]
    loss = lax.psum(loss_part[0, 0], ("x", "y", "c"))
    return (loss, grad_x, *[leaf(nm, k) for k in range(4) for nm in order])
```
